```python
import jax, jax.numpy as jnp
from jax import lax
import numpy as np

D_MODEL = 1024
BATCH = 8
SEQ = 16384
DEPTH = 4

MLA_HEADS = 8
QK_NOPE_DIM = 64
QK_ROPE_DIM = 32
V_HEAD_DIM = 64
Q_LORA_RANK = 384
KV_LORA_RANK = 256
ROPE_THETA = 10000.0
Q_BLOCK = 128
SG_GROUPS = 8
SG_GROUP_DIM = 64
SG_WIDTH = SG_GROUPS * SG_GROUP_DIM
SG_CHUNK = 128
CONV_WIDTH = D_MODEL
CONV_K = 3
D_FF = 2816
NORM_EPS = 1e-6

MLA_OUT = MLA_HEADS * V_HEAD_DIM
MIX_WIDTH = MLA_OUT + SG_WIDTH
QK_HEAD_DIM = QK_NOPE_DIM + QK_ROPE_DIM
EVEN_IN = Q_LORA_RANK + KV_LORA_RANK + QK_ROPE_DIM + 2 * SG_WIDTH
N_EVEN = (DEPTH + 1) // 2
N_ODD = DEPTH // 2

kernel_name = "macaron_mla_sgu_shortconv_hybrid"


def rms_norm(x, g):
    x32 = x.astype(jnp.float32)
    y = x32 * lax.rsqrt(jnp.mean(x32 * x32, axis=-1, keepdims=True) + NORM_EPS)
    return (y * g.astype(jnp.float32)).astype(x.dtype)


def swiglu(h, w_gate, w_up, w_down):
    return (jax.nn.silu(h @ w_gate) * (h @ w_up)) @ w_down


def rope_tables(positions):
    inv_freq = ROPE_THETA ** (-jnp.arange(0, QK_ROPE_DIM, 2, dtype=jnp.float32) / QK_ROPE_DIM)
    ang = positions.astype(jnp.float32)[..., None] * inv_freq
    return jnp.cos(ang), jnp.sin(ang)


def apply_rope(t, cos, sin):
    t32 = t.astype(jnp.float32)
    t1, t2 = jnp.split(t32, 2, axis=-1)
    out = jnp.concatenate([t1 * cos - t2 * sin, t1 * sin + t2 * cos], axis=-1)
    return out.astype(t.dtype)


def mla_attention(q_nope, q_rope, k_nope, k_rope, v):
    B, S, H, _ = q_nope.shape
    nb = S // Q_BLOCK
    scale = QK_HEAD_DIM ** -0.5
    k_idx = jnp.arange(S)

    def to_blocks(t):
        return jnp.moveaxis(t.reshape(B, nb, Q_BLOCK, *t.shape[2:]), 1, 0)

    def one_block(args):
        qn, qr, i = args
        s = (jnp.einsum('bqhd,bkhd->bhqk', qn, k_nope, preferred_element_type=jnp.float32)
             + jnp.einsum('bqhr,bkr->bhqk', qr, k_rope, preferred_element_type=jnp.float32)) * scale
        q_idx = i * Q_BLOCK + jnp.arange(Q_BLOCK)
        s = jnp.where(k_idx[None, :] <= q_idx[:, None], s, -jnp.inf)
        p = jax.nn.softmax(s, axis=-1).astype(v.dtype)
        return jnp.einsum('bhqk,bkhd->bqhd', p, v)

    out = lax.map(one_block, (to_blocks(q_nope), to_blocks(q_rope), jnp.arange(nb)))
    return jnp.moveaxis(out, 0, 1).reshape(B, S, H * V_HEAD_DIM)


def spatial_gating(z, sg_norm, sg_w, sg_b):
    u, v = jnp.split(z, 2, axis=-1)
    v = rms_norm(v, sg_norm)
    B, S, _ = v.shape
    nc = S // SG_CHUNK
    v = v.reshape(B, nc, SG_CHUNK, SG_GROUPS, SG_GROUP_DIM)
    w = sg_w * jnp.tril(jnp.ones((SG_CHUNK, SG_CHUNK), dtype=sg_w.dtype))
    mixed = jnp.einsum('gts,bnsgc->bntgc', w, v) + sg_b.T[None, None, :, :, None]
    return u * mixed.reshape(B, S, SG_WIDTH)


def mla_sgu_mixer(h, cos, sin, w_in, q_norm, w_uq, kv_norm, w_ukv, sg_norm, sg_w, sg_b, w_out):
    B, S, _ = h.shape
    proj = h @ w_in
    c_q, c_kv, k_rope, z = jnp.split(
        proj, [Q_LORA_RANK, Q_LORA_RANK + KV_LORA_RANK, Q_LORA_RANK + KV_LORA_RANK + QK_ROPE_DIM], axis=-1)
    q = (rms_norm(c_q, q_norm) @ w_uq).reshape(B, S, MLA_HEADS, QK_HEAD_DIM)
    q_nope = q[..., :QK_NOPE_DIM]
    q_rope = apply_rope(q[..., QK_NOPE_DIM:], cos[:, :, None, :], sin[:, :, None, :])
    k_rope = apply_rope(k_rope, cos, sin)
    kv = (rms_norm(c_kv, kv_norm) @ w_ukv).reshape(B, S, MLA_HEADS, QK_NOPE_DIM + V_HEAD_DIM)
    k_nope, v = kv[..., :QK_NOPE_DIM], kv[..., QK_NOPE_DIM:]
    attn = mla_attention(q_nope, q_rope, k_nope, k_rope, v)
    sg = spatial_gating(jax.nn.gelu(z, approximate=False), sg_norm, sg_w, sg_b)
    return jnp.concatenate([attn, sg], axis=-1) @ w_out


def short_conv_mixer(h, w_in, conv_w, w_out):
    b_gate, c_gate, z = jnp.split(h @ w_in, 3, axis=-1)
    y = lax.conv_general_dilated(
        c_gate * z, conv_w[:, None, :], window_strides=(1,), padding=[(CONV_K - 1, 0)],
        dimension_numbers=('NWC', 'WIO', 'NWC'), feature_group_count=CONV_WIDTH)
    return (b_gate * y) @ w_out


def _fwd_setup_inputs(seed: int = 0) -> dict:
    key = jax.random.key(seed)
    keys = iter(jax.random.split(key, 32))
    f32 = jnp.float32

    def dense(shape, fan_in):
        return jax.random.normal(next(keys), shape, f32) * (fan_in ** -0.5)

    def gain(shape):
        return 1.0 + 0.1 * jax.random.normal(next(keys), shape, f32)

    x = jax.random.normal(next(keys), (BATCH, SEQ, D_MODEL), f32)
    offset = jax.random.randint(next(keys), (BATCH, 1), 0, 1024, dtype=jnp.int32)
    positions = offset + jnp.arange(SEQ, dtype=jnp.int32)[None, :]
    return {
        "x": x,
        "positions": positions,
        "ffn_pre_norm": gain((DEPTH, D_MODEL)),
        "ffn_pre_w_gate": dense((DEPTH, D_MODEL, D_FF), D_MODEL),
        "ffn_pre_w_up": dense((DEPTH, D_MODEL, D_FF), D_MODEL),
        "ffn_pre_w_down": dense((DEPTH, D_FF, D_MODEL), D_FF),
        "mix_norm": gain((DEPTH, D_MODEL)),
        "ffn_post_norm": gain((DEPTH, D_MODEL)),
        "ffn_post_w_gate": dense((DEPTH, D_MODEL, D_FF), D_MODEL),
        "ffn_post_w_up": dense((DEPTH, D_MODEL, D_FF), D_MODEL),
        "ffn_post_w_down": dense((DEPTH, D_FF, D_MODEL), D_FF),
        "even_w_in": dense((N_EVEN, D_MODEL, EVEN_IN), D_MODEL),
        "q_norm": gain((N_EVEN, Q_LORA_RANK)),
        "w_uq": dense((N_EVEN, Q_LORA_RANK, MLA_HEADS * QK_HEAD_DIM), Q_LORA_RANK),
        "kv_norm": gain((N_EVEN, KV_LORA_RANK)),
        "w_ukv": dense((N_EVEN, KV_LORA_RANK, MLA_HEADS * (QK_NOPE_DIM + V_HEAD_DIM)), KV_LORA_RANK),
        "sg_norm": gain((N_EVEN, SG_WIDTH)),
        "sg_w": dense((N_EVEN, SG_GROUPS, SG_CHUNK, SG_CHUNK), SG_CHUNK),
        "sg_b": gain((N_EVEN, SG_GROUPS, SG_CHUNK)),
        "even_w_out": dense((N_EVEN, MIX_WIDTH, D_MODEL), MIX_WIDTH),
        "conv_w_in": dense((N_ODD, D_MODEL, 3 * CONV_WIDTH), D_MODEL),
        "conv_w": dense((N_ODD, CONV_K, CONV_WIDTH), CONV_K),
        "conv_w_out": dense((N_ODD, CONV_WIDTH, D_MODEL), CONV_WIDTH),
        "final_norm": gain((D_MODEL,)),
    }


def _fwd_reference(x, positions, ffn_pre_norm, ffn_pre_w_gate, ffn_pre_w_up, ffn_pre_w_down,
              mix_norm, ffn_post_norm, ffn_post_w_gate, ffn_post_w_up, ffn_post_w_down,
              even_w_in, q_norm, w_uq, kv_norm, w_ukv, sg_norm, sg_w, sg_b, even_w_out,
              conv_w_in, conv_w, conv_w_out, final_norm):
    cos, sin = rope_tables(positions)
    for layer in range(DEPTH):
        x = x + 0.5 * swiglu(rms_norm(x, ffn_pre_norm[layer]),
                             ffn_pre_w_gate[layer], ffn_pre_w_up[layer], ffn_pre_w_down[layer])
        h = rms_norm(x, mix_norm[layer])
        if layer % 2 == 0:
            e = layer // 2
            x = x + mla_sgu_mixer(h, cos, sin, even_w_in[e], q_norm[e], w_uq[e], kv_norm[e],
                                  w_ukv[e], sg_norm[e], sg_w[e], sg_b[e], even_w_out[e])
        else:
            o = layer // 2
            x = x + short_conv_mixer(h, conv_w_in[o], conv_w[o], conv_w_out[o])
        x = x + 0.5 * swiglu(rms_norm(x, ffn_post_norm[layer]),
                             ffn_post_w_gate[layer], ffn_post_w_up[layer], ffn_post_w_down[layer])
    return rms_norm(x, final_norm)


import jax as _jax
import jax.numpy as _jnp

TWIN_FORMAT = 'train_step'
FWD_PARAMS = ['x', 'positions', 'ffn_pre_norm', 'ffn_pre_w_gate', 'ffn_pre_w_up', 'ffn_pre_w_down', 'mix_norm', 'ffn_post_norm', 'ffn_post_w_gate', 'ffn_post_w_up', 'ffn_post_w_down', 'even_w_in', 'q_norm', 'w_uq', 'kv_norm', 'w_ukv', 'sg_norm', 'sg_w', 'sg_b', 'even_w_out', 'conv_w_in', 'conv_w', 'conv_w_out', 'final_norm']
TWIN_WEIGHTS = ['ffn_pre_norm', 'ffn_pre_w_gate', 'ffn_pre_w_up', 'ffn_pre_w_down', 'mix_norm', 'ffn_post_norm', 'ffn_post_w_gate', 'ffn_post_w_up', 'ffn_post_w_down', 'even_w_in', 'q_norm', 'w_uq', 'kv_norm', 'w_ukv', 'sg_norm', 'sg_w', 'sg_b', 'even_w_out', 'conv_w_in', 'conv_w', 'conv_w_out', 'final_norm']
TWIN_DIFF_INPUT = 'x'
TWIN_INPUTS = ['x', 'positions', 'ffn_pre_norm', 'ffn_pre_w_gate', 'ffn_pre_w_up', 'ffn_pre_w_down', 'mix_norm', 'ffn_post_norm', 'ffn_post_w_gate', 'ffn_post_w_up', 'ffn_post_w_down', 'even_w_in', 'q_norm', 'w_uq', 'kv_norm', 'w_ukv', 'sg_norm', 'sg_w', 'sg_b', 'even_w_out', 'conv_w_in', 'conv_w', 'conv_w_out', 'final_norm', 'loss_target', 'm_ffn_pre_norm', 'm_ffn_pre_w_gate', 'm_ffn_pre_w_up', 'm_ffn_pre_w_down', 'm_mix_norm', 'm_ffn_post_norm', 'm_ffn_post_w_gate', 'm_ffn_post_w_up', 'm_ffn_post_w_down', 'm_even_w_in', 'm_q_norm', 'm_w_uq', 'm_kv_norm', 'm_w_ukv', 'm_sg_norm', 'm_sg_w', 'm_sg_b', 'm_even_w_out', 'm_conv_w_in', 'm_conv_w', 'm_conv_w_out', 'm_final_norm', 'v_ffn_pre_norm', 'v_ffn_pre_w_gate', 'v_ffn_pre_w_up', 'v_ffn_pre_w_down', 'v_mix_norm', 'v_ffn_post_norm', 'v_ffn_post_w_gate', 'v_ffn_post_w_up', 'v_ffn_post_w_down', 'v_even_w_in', 'v_q_norm', 'v_w_uq', 'v_kv_norm', 'v_w_ukv', 'v_sg_norm', 'v_sg_w', 'v_sg_b', 'v_even_w_out', 'v_conv_w_in', 'v_conv_w', 'v_conv_w_out', 'v_final_norm']
TWIN_OUTPUTS = ['loss', 'grad_x', 'grad_ffn_pre_norm', 'grad_ffn_pre_w_gate', 'grad_ffn_pre_w_up', 'grad_ffn_pre_w_down', 'grad_mix_norm', 'grad_ffn_post_norm', 'grad_ffn_post_w_gate', 'grad_ffn_post_w_up', 'grad_ffn_post_w_down', 'grad_even_w_in', 'grad_q_norm', 'grad_w_uq', 'grad_kv_norm', 'grad_w_ukv', 'grad_sg_norm', 'grad_sg_w', 'grad_sg_b', 'grad_even_w_out', 'grad_conv_w_in', 'grad_conv_w', 'grad_conv_w_out', 'grad_final_norm', 'delta_ffn_pre_norm', 'delta_ffn_pre_w_gate', 'delta_ffn_pre_w_up', 'delta_ffn_pre_w_down', 'delta_mix_norm', 'delta_ffn_post_norm', 'delta_ffn_post_w_gate', 'delta_ffn_post_w_up', 'delta_ffn_post_w_down', 'delta_even_w_in', 'delta_q_norm', 'delta_w_uq', 'delta_kv_norm', 'delta_w_ukv', 'delta_sg_norm', 'delta_sg_w', 'delta_sg_b', 'delta_even_w_out', 'delta_conv_w_in', 'delta_conv_w', 'delta_conv_w_out', 'delta_final_norm', 'new_m_ffn_pre_norm', 'new_m_ffn_pre_w_gate', 'new_m_ffn_pre_w_up', 'new_m_ffn_pre_w_down', 'new_m_mix_norm', 'new_m_ffn_post_norm', 'new_m_ffn_post_w_gate', 'new_m_ffn_post_w_up', 'new_m_ffn_post_w_down', 'new_m_even_w_in', 'new_m_q_norm', 'new_m_w_uq', 'new_m_kv_norm', 'new_m_w_ukv', 'new_m_sg_norm', 'new_m_sg_w', 'new_m_sg_b', 'new_m_even_w_out', 'new_m_conv_w_in', 'new_m_conv_w', 'new_m_conv_w_out', 'new_m_final_norm', 'new_v_ffn_pre_norm', 'new_v_ffn_pre_w_gate', 'new_v_ffn_pre_w_up', 'new_v_ffn_pre_w_down', 'new_v_mix_norm', 'new_v_ffn_post_norm', 'new_v_ffn_post_w_gate', 'new_v_ffn_post_w_up', 'new_v_ffn_post_w_down', 'new_v_even_w_in', 'new_v_q_norm', 'new_v_w_uq', 'new_v_kv_norm', 'new_v_w_ukv', 'new_v_sg_norm', 'new_v_sg_w', 'new_v_sg_b', 'new_v_even_w_out', 'new_v_conv_w_in', 'new_v_conv_w', 'new_v_conv_w_out', 'new_v_final_norm']
TWIN_LEAF_KINDS = {'loss': 'loss', 'grad_x': 'grad_x', 'grad_ffn_pre_norm': 'grad_w', 'grad_ffn_pre_w_gate': 'grad_w', 'grad_ffn_pre_w_up': 'grad_w', 'grad_ffn_pre_w_down': 'grad_w', 'grad_mix_norm': 'grad_w', 'grad_ffn_post_norm': 'grad_w', 'grad_ffn_post_w_gate': 'grad_w', 'grad_ffn_post_w_up': 'grad_w', 'grad_ffn_post_w_down': 'grad_w', 'grad_even_w_in': 'grad_w', 'grad_q_norm': 'grad_w', 'grad_w_uq': 'grad_w', 'grad_kv_norm': 'grad_w', 'grad_w_ukv': 'grad_w', 'grad_sg_norm': 'grad_w', 'grad_sg_w': 'grad_w', 'grad_sg_b': 'grad_w', 'grad_even_w_out': 'grad_w', 'grad_conv_w_in': 'grad_w', 'grad_conv_w': 'grad_w', 'grad_conv_w_out': 'grad_w', 'grad_final_norm': 'grad_w', 'delta_ffn_pre_norm': 'delta_w', 'delta_ffn_pre_w_gate': 'delta_w', 'delta_ffn_pre_w_up': 'delta_w', 'delta_ffn_pre_w_down': 'delta_w', 'delta_mix_norm': 'delta_w', 'delta_ffn_post_norm': 'delta_w', 'delta_ffn_post_w_gate': 'delta_w', 'delta_ffn_post_w_up': 'delta_w', 'delta_ffn_post_w_down': 'delta_w', 'delta_even_w_in': 'delta_w', 'delta_q_norm': 'delta_w', 'delta_w_uq': 'delta_w', 'delta_kv_norm': 'delta_w', 'delta_w_ukv': 'delta_w', 'delta_sg_norm': 'delta_w', 'delta_sg_w': 'delta_w', 'delta_sg_b': 'delta_w', 'delta_even_w_out': 'delta_w', 'delta_conv_w_in': 'delta_w', 'delta_conv_w': 'delta_w', 'delta_conv_w_out': 'delta_w', 'delta_final_norm': 'delta_w', 'new_m_ffn_pre_norm': 'new_m', 'new_m_ffn_pre_w_gate': 'new_m', 'new_m_ffn_pre_w_up': 'new_m', 'new_m_ffn_pre_w_down': 'new_m', 'new_m_mix_norm': 'new_m', 'new_m_ffn_post_norm': 'new_m', 'new_m_ffn_post_w_gate': 'new_m', 'new_m_ffn_post_w_up': 'new_m', 'new_m_ffn_post_w_down': 'new_m', 'new_m_even_w_in': 'new_m', 'new_m_q_norm': 'new_m', 'new_m_w_uq': 'new_m', 'new_m_kv_norm': 'new_m', 'new_m_w_ukv': 'new_m', 'new_m_sg_norm': 'new_m', 'new_m_sg_w': 'new_m', 'new_m_sg_b': 'new_m', 'new_m_even_w_out': 'new_m', 'new_m_conv_w_in': 'new_m', 'new_m_conv_w': 'new_m', 'new_m_conv_w_out': 'new_m', 'new_m_final_norm': 'new_m', 'new_v_ffn_pre_norm': 'new_v', 'new_v_ffn_pre_w_gate': 'new_v', 'new_v_ffn_pre_w_up': 'new_v', 'new_v_ffn_pre_w_down': 'new_v', 'new_v_mix_norm': 'new_v', 'new_v_ffn_post_norm': 'new_v', 'new_v_ffn_post_w_gate': 'new_v', 'new_v_ffn_post_w_up': 'new_v', 'new_v_ffn_post_w_down': 'new_v', 'new_v_even_w_in': 'new_v', 'new_v_q_norm': 'new_v', 'new_v_w_uq': 'new_v', 'new_v_kv_norm': 'new_v', 'new_v_w_ukv': 'new_v', 'new_v_sg_norm': 'new_v', 'new_v_sg_w': 'new_v', 'new_v_sg_b': 'new_v', 'new_v_even_w_out': 'new_v', 'new_v_conv_w_in': 'new_v', 'new_v_conv_w': 'new_v', 'new_v_conv_w_out': 'new_v', 'new_v_final_norm': 'new_v'}


def _forward(args):
    return _fwd_reference(*[args[k] for k in FWD_PARAMS])


def _output_shape():
    def fwd():
        inp = _fwd_setup_inputs(0)
        return _fwd_reference(*[inp[k] for k in FWD_PARAMS])
    out = _jax.eval_shape(fwd)
    return out.shape, out.dtype

N_MICROBATCH = 1
ADAM_LR = 0.001
ADAM_B1 = 0.9
ADAM_B2 = 0.999
ADAM_EPS = 1e-08
ADAM_WD = 0.01
ADAM_STEP = 10
PER_EXAMPLE_BATCH_AXIS = {'x': 0, 'positions': 0, 'loss_target': 0}
SHARED_INPUTS = []
_WEIGHT_DTYPES = {'ffn_pre_norm': _jnp.float32, 'ffn_pre_w_gate': _jnp.float32, 'ffn_pre_w_up': _jnp.float32, 'ffn_pre_w_down': _jnp.float32, 'mix_norm': _jnp.float32, 'ffn_post_norm': _jnp.float32, 'ffn_post_w_gate': _jnp.float32, 'ffn_post_w_up': _jnp.float32, 'ffn_post_w_down': _jnp.float32, 'even_w_in': _jnp.float32, 'q_norm': _jnp.float32, 'w_uq': _jnp.float32, 'kv_norm': _jnp.float32, 'w_ukv': _jnp.float32, 'sg_norm': _jnp.float32, 'sg_w': _jnp.float32, 'sg_b': _jnp.float32, 'even_w_out': _jnp.float32, 'conv_w_in': _jnp.float32, 'conv_w': _jnp.float32, 'conv_w_out': _jnp.float32, 'final_norm': _jnp.float32}
MOMENT_SCALE = {'ffn_pre_norm': 2.147187e-01, 'ffn_pre_w_gate': 8.806556e-02, 'ffn_pre_w_up': 8.573039e-02, 'ffn_pre_w_down': 1.420003e-01, 'mix_norm': 3.858595e-01, 'ffn_post_norm': 1.588970e-01, 'ffn_post_w_gate': 6.492632e-02, 'ffn_post_w_up': 6.380738e-02, 'ffn_post_w_down': 1.061128e-01, 'even_w_in': 2.498283e-01, 'q_norm': 9.697520e-02, 'w_uq': 6.954832e-02, 'kv_norm': 1.768371e-01, 'w_ukv': 9.022990e-02, 'sg_norm': 2.015846e-01, 'sg_w': 1.444500e-01, 'sg_b': 2.090296e-01, 'even_w_out': 3.037458e-01, 'conv_w_in': 2.510908e-01, 'conv_w': 2.647472e-01, 'conv_w_out': 2.720563e-01, 'final_norm': 1.280790e+02}


def _to_microbatches(a, axis):
    t = _jnp.moveaxis(a, axis, 0)
    t = t.reshape((N_MICROBATCH, t.shape[0] // N_MICROBATCH) + t.shape[1:])
    return _jnp.moveaxis(t, 1, axis + 1)


def setup_inputs(seed: int = 0) -> dict:
    inp = _fwd_setup_inputs(seed)
    key = _jax.random.fold_in(_jax.random.key(seed), 7919)
    shape, _ = _output_shape()
    out = dict(inp)
    out["loss_target"] = _jax.random.normal(_jax.random.fold_in(key, 0), shape, _jnp.float32)
    for i, name in enumerate(TWIN_WEIGHTS):
        w = inp[name].astype(_jnp.float32)
        if MOMENT_SCALE is None:
            s = _jnp.sqrt(_jnp.mean(_jnp.square(w)) + 1e-30)
        else:
            s = MOMENT_SCALE[name]
        km, kv = _jax.random.split(_jax.random.fold_in(key, i + 1))
        out[name] = w
        out["m_" + name] = s * _jax.random.normal(km, w.shape, _jnp.float32)
        out["v_" + name] = (s * s) * _jax.random.uniform(kv, w.shape, _jnp.float32, 0.5, 1.5)
    if N_MICROBATCH > 1:
        for name, axis in PER_EXAMPLE_BATCH_AXIS.items():
            out[name] = _to_microbatches(out[name], axis)
    return {'x': out['x'], 'positions': out['positions'], 'ffn_pre_norm': out['ffn_pre_norm'], 'ffn_pre_w_gate': out['ffn_pre_w_gate'], 'ffn_pre_w_up': out['ffn_pre_w_up'], 'ffn_pre_w_down': out['ffn_pre_w_down'], 'mix_norm': out['mix_norm'], 'ffn_post_norm': out['ffn_post_norm'], 'ffn_post_w_gate': out['ffn_post_w_gate'], 'ffn_post_w_up': out['ffn_post_w_up'], 'ffn_post_w_down': out['ffn_post_w_down'], 'even_w_in': out['even_w_in'], 'q_norm': out['q_norm'], 'w_uq': out['w_uq'], 'kv_norm': out['kv_norm'], 'w_ukv': out['w_ukv'], 'sg_norm': out['sg_norm'], 'sg_w': out['sg_w'], 'sg_b': out['sg_b'], 'even_w_out': out['even_w_out'], 'conv_w_in': out['conv_w_in'], 'conv_w': out['conv_w'], 'conv_w_out': out['conv_w_out'], 'final_norm': out['final_norm'], 'loss_target': out['loss_target'], 'm_ffn_pre_norm': out['m_ffn_pre_norm'], 'm_ffn_pre_w_gate': out['m_ffn_pre_w_gate'], 'm_ffn_pre_w_up': out['m_ffn_pre_w_up'], 'm_ffn_pre_w_down': out['m_ffn_pre_w_down'], 'm_mix_norm': out['m_mix_norm'], 'm_ffn_post_norm': out['m_ffn_post_norm'], 'm_ffn_post_w_gate': out['m_ffn_post_w_gate'], 'm_ffn_post_w_up': out['m_ffn_post_w_up'], 'm_ffn_post_w_down': out['m_ffn_post_w_down'], 'm_even_w_in': out['m_even_w_in'], 'm_q_norm': out['m_q_norm'], 'm_w_uq': out['m_w_uq'], 'm_kv_norm': out['m_kv_norm'], 'm_w_ukv': out['m_w_ukv'], 'm_sg_norm': out['m_sg_norm'], 'm_sg_w': out['m_sg_w'], 'm_sg_b': out['m_sg_b'], 'm_even_w_out': out['m_even_w_out'], 'm_conv_w_in': out['m_conv_w_in'], 'm_conv_w': out['m_conv_w'], 'm_conv_w_out': out['m_conv_w_out'], 'm_final_norm': out['m_final_norm'], 'v_ffn_pre_norm': out['v_ffn_pre_norm'], 'v_ffn_pre_w_gate': out['v_ffn_pre_w_gate'], 'v_ffn_pre_w_up': out['v_ffn_pre_w_up'], 'v_ffn_pre_w_down': out['v_ffn_pre_w_down'], 'v_mix_norm': out['v_mix_norm'], 'v_ffn_post_norm': out['v_ffn_post_norm'], 'v_ffn_post_w_gate': out['v_ffn_post_w_gate'], 'v_ffn_post_w_up': out['v_ffn_post_w_up'], 'v_ffn_post_w_down': out['v_ffn_post_w_down'], 'v_even_w_in': out['v_even_w_in'], 'v_q_norm': out['v_q_norm'], 'v_w_uq': out['v_w_uq'], 'v_kv_norm': out['v_kv_norm'], 'v_w_ukv': out['v_w_ukv'], 'v_sg_norm': out['v_sg_norm'], 'v_sg_w': out['v_sg_w'], 'v_sg_b': out['v_sg_b'], 'v_even_w_out': out['v_even_w_out'], 'v_conv_w_in': out['v_conv_w_in'], 'v_conv_w': out['v_conv_w'], 'v_conv_w_out': out['v_conv_w_out'], 'v_final_norm': out['v_final_norm']}


def _loss(weights, diff, rest, loss_target):
    with _jax.named_scope("forward"):
        args = {**rest, TWIN_DIFF_INPUT: diff, **{k: w.astype(_WEIGHT_DTYPES[k]) for k, w in weights.items()}}
        y = _forward(args)
    with _jax.named_scope("loss_head"):
        err = _jnp.square(y.astype(_jnp.float32) - loss_target)
        return 0.5 * _jnp.sum(_jnp.mean(err, axis=-1)) if err.ndim else 0.5 * err


def _adamw(w, g, m, v):
    m = ADAM_B1 * m + (1.0 - ADAM_B1) * g
    v = ADAM_B2 * v + (1.0 - ADAM_B2) * _jnp.square(g)
    m_hat = m / (1.0 - ADAM_B1 ** ADAM_STEP)
    v_hat = v / (1.0 - ADAM_B2 ** ADAM_STEP)
    delta = -ADAM_LR * (m_hat / (_jnp.sqrt(v_hat) + ADAM_EPS) + ADAM_WD * w)
    return delta, m, v


def reference(x, positions, ffn_pre_norm, ffn_pre_w_gate, ffn_pre_w_up, ffn_pre_w_down, mix_norm, ffn_post_norm, ffn_post_w_gate, ffn_post_w_up, ffn_post_w_down, even_w_in, q_norm, w_uq, kv_norm, w_ukv, sg_norm, sg_w, sg_b, even_w_out, conv_w_in, conv_w, conv_w_out, final_norm, loss_target, m_ffn_pre_norm, m_ffn_pre_w_gate, m_ffn_pre_w_up, m_ffn_pre_w_down, m_mix_norm, m_ffn_post_norm, m_ffn_post_w_gate, m_ffn_post_w_up, m_ffn_post_w_down, m_even_w_in, m_q_norm, m_w_uq, m_kv_norm, m_w_ukv, m_sg_norm, m_sg_w, m_sg_b, m_even_w_out, m_conv_w_in, m_conv_w, m_conv_w_out, m_final_norm, v_ffn_pre_norm, v_ffn_pre_w_gate, v_ffn_pre_w_up, v_ffn_pre_w_down, v_mix_norm, v_ffn_post_norm, v_ffn_post_w_gate, v_ffn_post_w_up, v_ffn_post_w_down, v_even_w_in, v_q_norm, v_w_uq, v_kv_norm, v_w_ukv, v_sg_norm, v_sg_w, v_sg_b, v_even_w_out, v_conv_w_in, v_conv_w, v_conv_w_out, v_final_norm):
    given = dict(x=x, positions=positions, ffn_pre_norm=ffn_pre_norm, ffn_pre_w_gate=ffn_pre_w_gate, ffn_pre_w_up=ffn_pre_w_up, ffn_pre_w_down=ffn_pre_w_down, mix_norm=mix_norm, ffn_post_norm=ffn_post_norm, ffn_post_w_gate=ffn_post_w_gate, ffn_post_w_up=ffn_post_w_up, ffn_post_w_down=ffn_post_w_down, even_w_in=even_w_in, q_norm=q_norm, w_uq=w_uq, kv_norm=kv_norm, w_ukv=w_ukv, sg_norm=sg_norm, sg_w=sg_w, sg_b=sg_b, even_w_out=even_w_out, conv_w_in=conv_w_in, conv_w=conv_w, conv_w_out=conv_w_out, final_norm=final_norm, loss_target=loss_target, m_ffn_pre_norm=m_ffn_pre_norm, m_ffn_pre_w_gate=m_ffn_pre_w_gate, m_ffn_pre_w_up=m_ffn_pre_w_up, m_ffn_pre_w_down=m_ffn_pre_w_down, m_mix_norm=m_mix_norm, m_ffn_post_norm=m_ffn_post_norm, m_ffn_post_w_gate=m_ffn_post_w_gate, m_ffn_post_w_up=m_ffn_post_w_up, m_ffn_post_w_down=m_ffn_post_w_down, m_even_w_in=m_even_w_in, m_q_norm=m_q_norm, m_w_uq=m_w_uq, m_kv_norm=m_kv_norm, m_w_ukv=m_w_ukv, m_sg_norm=m_sg_norm, m_sg_w=m_sg_w, m_sg_b=m_sg_b, m_even_w_out=m_even_w_out, m_conv_w_in=m_conv_w_in, m_conv_w=m_conv_w, m_conv_w_out=m_conv_w_out, m_final_norm=m_final_norm, v_ffn_pre_norm=v_ffn_pre_norm, v_ffn_pre_w_gate=v_ffn_pre_w_gate, v_ffn_pre_w_up=v_ffn_pre_w_up, v_ffn_pre_w_down=v_ffn_pre_w_down, v_mix_norm=v_mix_norm, v_ffn_post_norm=v_ffn_post_norm, v_ffn_post_w_gate=v_ffn_post_w_gate, v_ffn_post_w_up=v_ffn_post_w_up, v_ffn_post_w_down=v_ffn_post_w_down, v_even_w_in=v_even_w_in, v_q_norm=v_q_norm, v_w_uq=v_w_uq, v_kv_norm=v_kv_norm, v_w_ukv=v_w_ukv, v_sg_norm=v_sg_norm, v_sg_w=v_sg_w, v_sg_b=v_sg_b, v_even_w_out=v_even_w_out, v_conv_w_in=v_conv_w_in, v_conv_w=v_conv_w, v_conv_w_out=v_conv_w_out, v_final_norm=v_final_norm)
    weights = {n: given[n] for n in TWIN_WEIGHTS}
    shared = {n: given[n] for n in SHARED_INPUTS}
    per_example = {n: given[n] for n in ['x', 'positions']}
    grad_fn = _jax.value_and_grad(_loss, argnums=(0, 1))

    def one_microbatch(ex, loss_target):
        ex = dict(ex)
        diff = ex.pop(TWIN_DIFF_INPUT)
        return grad_fn(weights, diff, {**shared, **ex}, loss_target)

    if N_MICROBATCH == 1:
        loss, (grad_w, grad_x) = one_microbatch(per_example, given["loss_target"])
    else:
        def body(carry, xs):
            loss_sum, grad_sum = carry
            l_k, (gw_k, gx_k) = one_microbatch(xs[0], xs[1])
            with _jax.named_scope("update"):
                return (loss_sum + l_k, _jax.tree.map(_jnp.add, grad_sum, gw_k)), gx_k

        init = (_jnp.zeros((), _jnp.float32), _jax.tree.map(_jnp.zeros_like, weights))
        (loss, grad_w), grad_x = _jax.lax.scan(body, init, (per_example, given["loss_target"]))
    with _jax.named_scope("update"):
        delta_w, new_m, new_v = {}, {}, {}
        for n in TWIN_WEIGHTS:
            delta_w[n], new_m[n], new_v[n] = _adamw(weights[n], grad_w[n], given["m_" + n], given["v_" + n])
    return (loss, grad_x, *[grad_w[n] for n in TWIN_WEIGHTS], *[delta_w[n] for n in TWIN_WEIGHTS],
            *[new_m[n] for n in TWIN_WEIGHTS], *[new_v[n] for n in TWIN_WEIGHTS])
```

```python
import functools
import math

import numpy as np
import jax
import jax.numpy as jnp
from jax import lax
from jax.experimental import pallas as pl
from jax.experimental.pallas import tpu as pltpu

F32 = jnp.float32
BF16 = jnp.bfloat16

N_DEV = 8
NORM_EPS = 1e-6
HEADS = 8
NOPE = 64
ROPE = 32
VDIM = 64
HEAD_PAD = 128
QK_DIM = NOPE + ROPE
ROPE_THETA = 10000.0
SG_GROUPS = 8
SG_GROUP_DIM = 64
SG_WIDTH = SG_GROUPS * SG_GROUP_DIM
SG_CHUNK = 128
CONV_K = 3
ADAM_LR, ADAM_B1, ADAM_B2, ADAM_EPS, ADAM_WD, ADAM_STEP = 0.001, 0.9, 0.999, 1e-08, 0.01, 10

LANE = 128
PACK_W = 1024
VMEM_LIMIT = 60 * 1024 * 1024

TILES = dict(ffn_fwd=512, ffn_bwd=256, ffn_dw=1024, mm=512, mm_tn=1024, ew=512, attn=1024, sgu=512, adam=512)

NT = (((1,), (1,)), ((), ()))
NN = (((1,), (0,)), ((), ()))
TN = (((0,), (0,)), ((), ()))


def _dot(a, b, dims):
    return lax.dot_general(a, b, dims, preferred_element_type=F32)


def _cparams(*sem):
    return pltpu.CompilerParams(dimension_semantics=sem if sem else None, vmem_limit_bytes=VMEM_LIMIT)


def _tile(n, want):
    t = min(want, n)
    while n % t:
        t //= 2
    return t


def _lane_tile(n, cap):
    best = None
    for k in range(1, n // LANE + 1):
        t = k * LANE
        if n % t == 0 and t <= cap:
            best = t
    return best or n


def _row(v):
    return v.reshape(1, -1).astype(F32)


def _all_gather(block, name):
    R, W = block.shape

    def body(x_ref, out_ref, send_sems, recv_sems, local_sem):
        x, y, c = lax.axis_index("x"), lax.axis_index("y"), lax.axis_index("c")
        me, sibling = (x, y, c), (x, y, 1 - c)
        chips = [(1 - x, y), (x, 1 - y), (1 - x, 1 - y)]

        def slot(px, py, pc):
            return out_ref.at[4 * px + 2 * py + pc]

        def copy(k, blk, to, src=None):
            return pltpu.make_async_remote_copy(
                src_ref=slot(*blk) if src is None else src, dst_ref=slot(*blk),
                send_sem=send_sems.at[k], recv_sem=recv_sems.at[k],
                device_id=to, device_id_type=pl.DeviceIdType.MESH)

        mine = pltpu.make_async_copy(x_ref, slot(*me), local_sem)
        mine.start()
        first = [copy(0, me, sibling, src=x_ref)]
        first += [copy(1 + j, me, (*chip, c), src=x_ref) for j, chip in enumerate(chips)]
        for cp in first:
            cp.start()
        passed = [copy(4 + j, (*chip, c), sibling) for j, chip in enumerate(chips)]
        for j, chip in enumerate(chips):
            copy(1 + j, (*chip, c), me).wait_recv()
            passed[j].start()
        copy(0, sibling, me).wait_recv()
        for j, chip in enumerate(chips):
            copy(4 + j, (*chip, 1 - c), me).wait_recv()
        for cp in first + passed:
            cp.wait_send()
        mine.wait()

    return pl.pallas_call(
        body, name=name,
        out_shape=jax.ShapeDtypeStruct((N_DEV, R, W), block.dtype),
        in_specs=[pl.BlockSpec(memory_space=pl.ANY)],
        out_specs=pl.BlockSpec(memory_space=pl.ANY),
        scratch_shapes=[pltpu.SemaphoreType.DMA((7,)), pltpu.SemaphoreType.DMA((7,)), pltpu.SemaphoreType.DMA],
    )(block)


def _all_to_all(send, name):
    _, R, W = send.shape

    def body(s_ref, r_ref, send_sems, recv_sems, local_sem):
        x, y, c = lax.axis_index("x"), lax.axis_index("y"), lax.axis_index("c")
        me = 4 * x + 2 * y + c
        mine = pltpu.make_async_copy(s_ref.at[me], r_ref.at[me], local_sem)
        mine.start()
        copies = []
        for k in range(1, N_DEV):
            px, py, pc = x ^ (k >> 2), y ^ ((k >> 1) & 1), c ^ (k & 1)
            peer = 4 * px + 2 * py + pc
            copies.append(pltpu.make_async_remote_copy(
                src_ref=s_ref.at[peer], dst_ref=r_ref.at[me],
                send_sem=send_sems.at[k - 1], recv_sem=recv_sems.at[k - 1],
                device_id=(px, py, pc), device_id_type=pl.DeviceIdType.MESH))
        for cp in copies:
            cp.start()
        for cp in copies:
            cp.wait_recv()
        for cp in copies:
            cp.wait_send()
        mine.wait()

    return pl.pallas_call(
        body, name=name,
        out_shape=jax.ShapeDtypeStruct(send.shape, send.dtype),
        in_specs=[pl.BlockSpec(memory_space=pl.ANY)],
        out_specs=pl.BlockSpec(memory_space=pl.ANY),
        scratch_shapes=[pltpu.SemaphoreType.DMA((7,)), pltpu.SemaphoreType.DMA((7,)), pltpu.SemaphoreType.DMA],
    )(send)


def _sum_slots(parts, name):
    _, R, W = parts.shape
    tr = _tile(R, TILES["adam"])

    def body(p_ref, o_ref):
        acc = p_ref[0].astype(F32)
        for s in range(1, N_DEV):
            acc = acc + p_ref[s].astype(F32)
        o_ref[...] = acc

    return pl.pallas_call(
        body, name=name, grid=(R // tr,),
        in_specs=[pl.BlockSpec((N_DEV, tr, W), lambda i: (0, i, 0))],
        out_specs=pl.BlockSpec((tr, W), lambda i: (i, 0)),
        out_shape=jax.ShapeDtypeStruct((R, W), F32),
        compiler_params=_cparams("parallel"),
    )(parts)


def _adamw(w, g, m, v, name):
    R, W = w.shape
    tr = _tile(R, TILES["adam"])
    c1 = 1.0 - ADAM_B1 ** ADAM_STEP
    c2 = 1.0 - ADAM_B2 ** ADAM_STEP

    def body(w_ref, g_ref, m_ref, v_ref, d_ref, nm_ref, nv_ref):
        g = g_ref[...]
        nm = ADAM_B1 * m_ref[...] + (1.0 - ADAM_B1) * g
        nv = ADAM_B2 * v_ref[...] + (1.0 - ADAM_B2) * (g * g)
        d_ref[...] = -ADAM_LR * ((nm / c1) / (jnp.sqrt(nv / c2) + ADAM_EPS) + ADAM_WD * w_ref[...])
        nm_ref[...] = nm
        nv_ref[...] = nv

    spec = pl.BlockSpec((tr, W), lambda i: (i, 0))
    return pl.pallas_call(
        body, name=name, grid=(R // tr,),
        in_specs=[spec] * 4, out_specs=[spec] * 3,
        out_shape=[jax.ShapeDtypeStruct((R, W), F32)] * 3,
        compiler_params=_cparams("parallel"),
    )(w, g, m, v)


def _mm(a, b, mode, name, out_dtype=BF16, res=None, scale=1.0, acol=None, kdim=None):
    if mode == "tn":
        S, M = a.shape
        N = b.shape[1]
        ts = _tile(S, TILES["mm_tn"])
        tmo = _lane_tile(M, 1024)

        def body(a_ref, b_ref, o_ref, acc):
            s = pl.program_id(1)

            @pl.when(s == 0)
            def _():
                acc[...] = jnp.zeros_like(acc)

            acc[...] += _dot(a_ref[...].astype(BF16), b_ref[...].astype(BF16), TN)

            @pl.when(s == pl.num_programs(1) - 1)
            def _():
                o_ref[...] = acc[...].astype(out_dtype)

        return pl.pallas_call(
            functools.partial(body), name=name, grid=(M // tmo, S // ts),
            in_specs=[pl.BlockSpec((ts, tmo), lambda i, s: (s, i)), pl.BlockSpec((ts, N), lambda i, s: (s, 0))],
            out_specs=pl.BlockSpec((tmo, N), lambda i, s: (i, 0)),
            out_shape=jax.ShapeDtypeStruct((M, N), out_dtype),
            scratch_shapes=[pltpu.VMEM((tmo, N), F32)],
            compiler_params=_cparams("parallel", "arbitrary"),
        )(a, b)

    M = a.shape[0]
    K = kdim if kdim is not None else a.shape[1]
    ac = 0 if acol is None else acol
    N = b.shape[1] if mode == "nn" else b.shape[0]
    tm = _tile(M, TILES["mm"])
    dims = NN if mode == "nn" else NT

    def body(*refs):
        if res is None:
            a_ref, b_ref, o_ref = refs
        else:
            a_ref, b_ref, r_ref, o_ref = refs
        acc = _dot(a_ref[...].astype(BF16), b_ref[...].astype(BF16), dims)
        if res is not None:
            acc = r_ref[...] + scale * acc
        o_ref[...] = acc.astype(out_dtype)

    in_specs = [pl.BlockSpec((tm, K), lambda i: (i, ac)), pl.BlockSpec(b.shape, lambda i: (0, 0))]
    args = [a, b]
    if res is not None:
        in_specs.append(pl.BlockSpec((tm, N), lambda i: (i, 0)))
        args.append(res)
    return pl.pallas_call(
        body, name=name, grid=(M // tm,),
        in_specs=in_specs, out_specs=pl.BlockSpec((tm, N), lambda i: (i, 0)),
        out_shape=jax.ShapeDtypeStruct((M, N), out_dtype),
        compiler_params=_cparams("parallel"),
    )(*args)


def _rms_fwd(x, gain, name, col=0, width=None):
    S = x.shape[0]
    W = width if width is not None else x.shape[1]
    tm = _tile(S, TILES["ew"])

    def body(x_ref, g_ref, o_ref):
        xv = x_ref[...].astype(F32)
        r = lax.rsqrt(jnp.mean(xv * xv, axis=-1, keepdims=True) + NORM_EPS)
        o_ref[...] = (xv * r * g_ref[...]).astype(BF16)

    return pl.pallas_call(
        body, name=name, grid=(S // tm,),
        in_specs=[pl.BlockSpec((tm, W), lambda i: (i, col)), pl.BlockSpec((1, W), lambda i: (0, 0))],
        out_specs=pl.BlockSpec((tm, W), lambda i: (i, 0)),
        out_shape=jax.ShapeDtypeStruct((S, W), BF16),
        compiler_params=_cparams("parallel"),
    )(x, _row(gain))


def _rms_bwd(dy, x, gain, name, col=0, res=None, out_dtype=F32):
    S, W = dy.shape
    tm = _tile(S, TILES["ew"])

    def body(*refs):
        if res is None:
            dy_ref, x_ref, g_ref, dx_ref, dg_ref = refs
        else:
            dy_ref, x_ref, g_ref, r_ref, dx_ref, dg_ref = refs

        @pl.when(pl.program_id(0) == 0)
        def _():
            dg_ref[...] = jnp.zeros_like(dg_ref)

        xv = x_ref[...].astype(F32)
        d = dy_ref[...].astype(F32)
        r = lax.rsqrt(jnp.mean(xv * xv, axis=-1, keepdims=True) + NORM_EPS)
        xhat = xv * r
        dg_ref[...] += jnp.sum(d * xhat, axis=0, keepdims=True)
        dxhat = d * g_ref[...]
        dx = r * (dxhat - xhat * jnp.mean(dxhat * xhat, axis=-1, keepdims=True))
        if res is not None:
            dx = dx + r_ref[...]
        dx_ref[...] = dx.astype(out_dtype)

    in_specs = [pl.BlockSpec((tm, W), lambda i: (i, 0)), pl.BlockSpec((tm, W), lambda i: (i, col)),
                pl.BlockSpec((1, W), lambda i: (0, 0))]
    args = [dy, x, _row(gain)]
    if res is not None:
        in_specs.append(pl.BlockSpec((tm, W), lambda i: (i, 0)))
        args.append(res)
    return pl.pallas_call(
        body, name=name, grid=(S // tm,),
        in_specs=in_specs,
        out_specs=[pl.BlockSpec((tm, W), lambda i: (i, 0)), pl.BlockSpec((1, W), lambda i: (0, 0))],
        out_shape=[jax.ShapeDtypeStruct((S, W), out_dtype), jax.ShapeDtypeStruct((1, W), F32)],
        compiler_params=_cparams("arbitrary"),
    )(*args)


def _silu_parts(a):
    s = jax.nn.sigmoid(a)
    return a * s, s * (1.0 + a * (1.0 - s))


def _ffn_fwd(x, gain, wg_t, wu_t, wd, name):
    S, D = x.shape
    Fd = wd.shape[0]
    tm = _tile(S, TILES["ffn_fwd"])
    fc = _lane_tile(Fd, 512)

    def body(x_ref, g_ref, wg_ref, wu_ref, wd_ref, o_ref, a_ref, b_ref):
        xv = x_ref[...]
        r = lax.rsqrt(jnp.mean(xv * xv, axis=-1, keepdims=True) + NORM_EPS)
        h = (xv * r * g_ref[...]).astype(BF16)
        acc = jnp.zeros((tm, D), F32)
        for c in range(Fd // fc):
            sl = slice(c * fc, (c + 1) * fc)
            a = _dot(h, wg_ref[sl, :], NT)
            b = _dot(h, wu_ref[sl, :], NT)
            a_ref[:, sl] = a.astype(BF16)
            b_ref[:, sl] = b.astype(BF16)
            z = (a * jax.nn.sigmoid(a) * b).astype(BF16)
            acc = acc + _dot(z, wd_ref[sl, :], NN)
        o_ref[...] = xv + 0.5 * acc

    wspec = pl.BlockSpec((Fd, D), lambda i: (0, 0), pipeline_mode=pl.Buffered(1))
    return pl.pallas_call(
        body, name=name, grid=(S // tm,),
        in_specs=[pl.BlockSpec((tm, D), lambda i: (i, 0)), pl.BlockSpec((1, D), lambda i: (0, 0)), wspec, wspec, wspec],
        out_specs=[pl.BlockSpec((tm, D), lambda i: (i, 0)), pl.BlockSpec((tm, Fd), lambda i: (i, 0)),
                   pl.BlockSpec((tm, Fd), lambda i: (i, 0))],
        out_shape=[jax.ShapeDtypeStruct((S, D), F32), jax.ShapeDtypeStruct((S, Fd), BF16),
                   jax.ShapeDtypeStruct((S, Fd), BF16)],
        compiler_params=_cparams("parallel"),
    )(x, _row(gain), wg_t, wu_t, wd)


def _ffn_bwd(g, x, gain, a, b, wg_t, wu_t, wd, name):
    S, D = x.shape
    Fd = wd.shape[0]
    tm = _tile(S, TILES["ffn_bwd"])
    fc = _lane_tile(Fd, 512)

    def body(g_ref, x_ref, gain_ref, a_ref, b_ref, wg_ref, wu_ref, wd_ref, dx_ref, dz_ref, h_ref, dy_ref, dg_ref):
        @pl.when(pl.program_id(0) == 0)
        def _():
            dg_ref[...] = jnp.zeros_like(dg_ref)

        gv = g_ref[...]
        xv = x_ref[...]
        r = lax.rsqrt(jnp.mean(xv * xv, axis=-1, keepdims=True) + NORM_EPS)
        xhat = xv * r
        h_ref[...] = (xhat * gain_ref[...]).astype(BF16)
        dy = (0.5 * gv).astype(BF16)
        dy_ref[...] = dy
        dh = jnp.zeros((tm, D), F32)
        for c in range(Fd // fc):
            sl = slice(c * fc, (c + 1) * fc)
            av = a_ref[:, sl].astype(F32)
            bv = b_ref[:, sl].astype(F32)
            dz = _dot(dy, wd_ref[sl, :], NT).astype(BF16)
            dz_ref[:, sl] = dz
            dzf = dz.astype(F32)
            silu, dsilu = _silu_parts(av)
            da = (dzf * bv * dsilu).astype(BF16)
            db = (dzf * silu).astype(BF16)
            dh = dh + _dot(da, wg_ref[sl, :], NN) + _dot(db, wu_ref[sl, :], NN)
        dg_ref[...] += jnp.sum(dh * xhat, axis=0, keepdims=True)
        dxhat = dh * gain_ref[...]
        dx_ref[...] = gv + r * (dxhat - xhat * jnp.mean(dxhat * xhat, axis=-1, keepdims=True))

    wspec = pl.BlockSpec((Fd, D), lambda i: (0, 0), pipeline_mode=pl.Buffered(1))
    row = pl.BlockSpec((tm, D), lambda i: (i, 0))
    wide = pl.BlockSpec((tm, Fd), lambda i: (i, 0))
    return pl.pallas_call(
        body, name=name, grid=(S // tm,),
        in_specs=[row, row, pl.BlockSpec((1, D), lambda i: (0, 0)), wide, wide, wspec, wspec, wspec],
        out_specs=[row, wide, row, row, pl.BlockSpec((1, D), lambda i: (0, 0))],
        out_shape=[jax.ShapeDtypeStruct((S, D), F32), jax.ShapeDtypeStruct((S, Fd), BF16),
                   jax.ShapeDtypeStruct((S, D), BF16), jax.ShapeDtypeStruct((S, D), BF16),
                   jax.ShapeDtypeStruct((1, D), F32)],
        compiler_params=_cparams("arbitrary"),
    )(g, x, _row(gain), a, b, wg_t, wu_t, wd)


def _ffn_dw(a, b, dz, h, dy, name):
    S, Fd = a.shape
    D = h.shape[1]
    ts = _tile(S, TILES["ffn_dw"])
    tf = _lane_tile(Fd, 256)

    def body(a_ref, b_ref, dz_ref, h_ref, dy_ref, og_ref, ou_ref, od_ref, accg, accu, accd):
        s = pl.program_id(1)

        @pl.when(s == 0)
        def _():
            accg[...] = jnp.zeros_like(accg)
            accu[...] = jnp.zeros_like(accu)
            accd[...] = jnp.zeros_like(accd)

        av = a_ref[...].astype(F32)
        bv = b_ref[...].astype(F32)
        dzf = dz_ref[...].astype(F32)
        silu, dsilu = _silu_parts(av)
        da = (dzf * bv * dsilu).astype(BF16)
        db = (dzf * silu).astype(BF16)
        z = (silu * bv).astype(BF16)
        hv = h_ref[...]
        accg[...] += _dot(da, hv, TN)
        accu[...] += _dot(db, hv, TN)
        accd[...] += _dot(z, dy_ref[...], TN)

        @pl.when(s == pl.num_programs(1) - 1)
        def _():
            og_ref[...] = accg[...]
            ou_ref[...] = accu[...]
            od_ref[...] = accd[...]

    wide = pl.BlockSpec((ts, tf), lambda f, s: (s, f))
    row = pl.BlockSpec((ts, D), lambda f, s: (s, 0))
    out = pl.BlockSpec((tf, D), lambda f, s: (f, 0))
    return pl.pallas_call(
        body, name=name, grid=(Fd // tf, S // ts),
        in_specs=[wide, wide, wide, row, row], out_specs=[out, out, out],
        out_shape=[jax.ShapeDtypeStruct((Fd, D), F32)] * 3,
        scratch_shapes=[pltpu.VMEM((tf, D), F32)] * 3,
        compiler_params=_cparams("parallel", "arbitrary"),
    )(a, b, dz, h, dy)


def _loss_head(x, target, gain, name):
    S, D = x.shape
    tm = _tile(S, TILES["ew"])

    def body(x_ref, t_ref, g_ref, dx_ref, dg_ref, loss_ref):
        @pl.when(pl.program_id(0) == 0)
        def _():
            dg_ref[...] = jnp.zeros_like(dg_ref)
            loss_ref[...] = jnp.zeros_like(loss_ref)

        xv = x_ref[...]
        r = lax.rsqrt(jnp.mean(xv * xv, axis=-1, keepdims=True) + NORM_EPS)
        xhat = xv * r
        e = xhat * g_ref[...] - t_ref[...]
        per_tok = jnp.mean(e * e, axis=-1, keepdims=True)
        loss_ref[...] += jnp.broadcast_to(0.5 * jnp.sum(per_tok, axis=0, keepdims=True), (1, LANE))
        dy = e * (1.0 / D)
        dg_ref[...] += jnp.sum(dy * xhat, axis=0, keepdims=True)
        dxhat = dy * g_ref[...]
        dx_ref[...] = r * (dxhat - xhat * jnp.mean(dxhat * xhat, axis=-1, keepdims=True))

    row = pl.BlockSpec((tm, D), lambda i: (i, 0))
    return pl.pallas_call(
        body, name=name, grid=(S // tm,),
        in_specs=[row, row, pl.BlockSpec((1, D), lambda i: (0, 0))],
        out_specs=[row, pl.BlockSpec((1, D), lambda i: (0, 0)), pl.BlockSpec((1, LANE), lambda i: (0, 0))],
        out_shape=[jax.ShapeDtypeStruct((S, D), F32), jax.ShapeDtypeStruct((1, D), F32),
                   jax.ShapeDtypeStruct((1, LANE), F32)],
        compiler_params=_cparams("arbitrary"),
    )(x, target, _row(gain))


def _shift_down(u, halo, k, rows):
    out = pltpu.roll(u, k, 0)
    for j in range(k):
        out = jnp.where(rows == j, halo[8 - k + j:8 - k + j + 1, :], out)
    return out


def _shift_up(u, halo, k, rows, n):
    out = pltpu.roll(u, n - k, 0)
    for j in range(k):
        out = jnp.where(rows == n - k + j, halo[j:j + 1, :], out)
    return out


def _conv_fwd(p, cw, name):
    S, W3 = p.shape
    W = W3 // 3
    tm = _tile(S, TILES["ew"])
    hb = tm // 8

    def body(p_ref, ph_ref, w_ref, v_ref):
        i = pl.program_id(0)
        bg = p_ref[:, 0:W].astype(F32)
        u = p_ref[:, W:2 * W].astype(F32) * p_ref[:, 2 * W:3 * W].astype(F32)
        uh = ph_ref[:, W:2 * W].astype(F32) * ph_ref[:, 2 * W:3 * W].astype(F32)
        uh = jnp.where(i > 0, uh, 0.0)
        rows = lax.broadcasted_iota(jnp.int32, (tm, 1), 0)
        u1 = _shift_down(u, uh, 1, rows)
        u2 = _shift_down(u, uh, 2, rows)
        y = w_ref[0:1, :] * u2 + w_ref[1:2, :] * u1 + w_ref[2:3, :] * u
        v_ref[...] = (bg * y).astype(BF16)

    return pl.pallas_call(
        body, name=name, grid=(S // tm,),
        in_specs=[pl.BlockSpec((tm, W3), lambda i: (i, 0)),
                  pl.BlockSpec((8, W3), lambda i: (jnp.maximum(i * hb - 1, 0), 0)),
                  pl.BlockSpec((8, W), lambda i: (0, 0))],
        out_specs=pl.BlockSpec((tm, W), lambda i: (i, 0)),
        out_shape=jax.ShapeDtypeStruct((S, W), BF16),
        compiler_params=_cparams("parallel"),
    )(p, p, cw)


def _conv_bwd(dv, p, cw, name):
    S, W3 = p.shape
    W = W3 // 3
    tm = _tile(S, TILES["ew"])
    hb = tm // 8
    last = S // 8 - 1

    def body(dv_ref, dvn_ref, p_ref, pp_ref, pn_ref, w_ref, dp_ref, dw_ref):
        i = pl.program_id(0)
        n = pl.num_programs(0)

        @pl.when(i == 0)
        def _():
            dw_ref[...] = jnp.zeros_like(dw_ref)

        bg = p_ref[:, 0:W].astype(F32)
        cg = p_ref[:, W:2 * W].astype(F32)
        zz = p_ref[:, 2 * W:3 * W].astype(F32)
        u = cg * zz
        uh = pp_ref[:, W:2 * W].astype(F32) * pp_ref[:, 2 * W:3 * W].astype(F32)
        uh = jnp.where(i > 0, uh, 0.0)
        rows = lax.broadcasted_iota(jnp.int32, (tm, 1), 0)
        u1 = _shift_down(u, uh, 1, rows)
        u2 = _shift_down(u, uh, 2, rows)
        w0, w1, w2 = w_ref[0:1, :], w_ref[1:2, :], w_ref[2:3, :]
        y = w0 * u2 + w1 * u1 + w2 * u
        dvv = dv_ref[...].astype(F32)
        dy = dvv * bg
        dyh = dvn_ref[...].astype(F32) * pn_ref[:, 0:W].astype(F32)
        dyh = jnp.where(i < n - 1, dyh, 0.0)
        d1 = _shift_up(dy, dyh, 1, rows, tm)
        d2 = _shift_up(dy, dyh, 2, rows, tm)
        du = w2 * dy + w1 * d1 + w0 * d2
        dp_ref[:, 0:W] = (dvv * y).astype(BF16)
        dp_ref[:, W:2 * W] = (du * zz).astype(BF16)
        dp_ref[:, 2 * W:3 * W] = (du * cg).astype(BF16)
        dw_ref[0:1, :] += jnp.sum(dy * u2, axis=0, keepdims=True)
        dw_ref[1:2, :] += jnp.sum(dy * u1, axis=0, keepdims=True)
        dw_ref[2:3, :] += jnp.sum(dy * u, axis=0, keepdims=True)

    return pl.pallas_call(
        body, name=name, grid=(S // tm,),
        in_specs=[pl.BlockSpec((tm, W), lambda i: (i, 0)),
                  pl.BlockSpec((8, W), lambda i: (jnp.minimum((i + 1) * hb, last), 0)),
                  pl.BlockSpec((tm, W3), lambda i: (i, 0)),
                  pl.BlockSpec((8, W3), lambda i: (jnp.maximum(i * hb - 1, 0), 0)),
                  pl.BlockSpec((8, W3), lambda i: (jnp.minimum((i + 1) * hb, last), 0)),
                  pl.BlockSpec((8, W), lambda i: (0, 0))],
        out_specs=[pl.BlockSpec((tm, W3), lambda i: (i, 0)), pl.BlockSpec((8, W), lambda i: (0, 0))],
        out_shape=[jax.ShapeDtypeStruct((S, W3), BF16), jax.ShapeDtypeStruct((8, W), F32)],
        compiler_params=_cparams("arbitrary"),
    )(dv, dv, p, p, p, cw)


def _rope_swap(r, lane):
    mid = NOPE + ROPE // 2
    first = (lane >= NOPE) & (lane < mid)
    second = (lane >= mid) & (lane < QK_DIM)
    return jnp.where(first, pltpu.roll(r, HEAD_PAD - ROPE // 2, 1), jnp.where(second, pltpu.roll(r, ROPE // 2, 1), 0.0))


def _rope_fwd(q_big, kv_big, proj, kr_col, ct, st, name):
    S = q_big.shape[0]
    HW = HEADS * HEAD_PAD
    tm = _tile(S, TILES["ew"])

    def body(q_ref, k_ref, kr_ref, ct_ref, st_ref, qo_ref, ko_ref):
        lane = lax.broadcasted_iota(jnp.int32, (1, HEAD_PAD), 1)
        ctv, stv = ct_ref[...], st_ref[...]
        krr = pltpu.roll(kr_ref[...].astype(F32), NOPE, 1)
        kro = krr * ctv + _rope_swap(krr, lane) * stv
        for h in range(HEADS):
            sl = slice(h * HEAD_PAD, (h + 1) * HEAD_PAD)
            qh = q_ref[:, sl].astype(F32)
            qo_ref[:, sl] = (qh * ctv + _rope_swap(qh, lane) * stv).astype(BF16)
            ko_ref[:, sl] = (k_ref[:, sl].astype(F32) + kro).astype(BF16)

    wide = pl.BlockSpec((tm, HW), lambda i: (i, 0))
    narrow = pl.BlockSpec((tm, HEAD_PAD), lambda i: (i, 0))
    return pl.pallas_call(
        body, name=name, grid=(S // tm,),
        in_specs=[wide, wide, pl.BlockSpec((tm, HEAD_PAD), lambda i: (i, kr_col)), narrow, narrow],
        out_specs=[wide, wide],
        out_shape=[jax.ShapeDtypeStruct((S, HW), BF16)] * 2,
        compiler_params=_cparams("parallel"),
    )(q_big, kv_big, proj, ct, st)


def _rope_bwd(dq, dk, dv, ct, st, name):
    S = dq.shape[0]
    HW = HEADS * HEAD_PAD
    tm = _tile(S, TILES["ew"])

    def body(dq_ref, dk_ref, dv_ref, ct_ref, st_ref, oq_ref, okv_ref, okr_ref):
        lane = lax.broadcasted_iota(jnp.int32, (1, HEAD_PAD), 1)
        ctv, stv = ct_ref[...], st_ref[...]
        acc = jnp.zeros((tm, HEAD_PAD), F32)
        for h in range(HEADS):
            sl = slice(h * HEAD_PAD, (h + 1) * HEAD_PAD)
            d = dq_ref[:, sl].astype(F32)
            oq_ref[:, sl] = (d * ctv + _rope_swap(d * stv, lane)).astype(BF16)
            d = dk_ref[:, sl].astype(F32)
            okv_ref[:, sl] = jnp.where(lane < NOPE, d, 0.0).astype(BF16)
            acc = acc + jnp.where(lane >= NOPE, d * ctv + _rope_swap(d * stv, lane), 0.0)
        okv_ref[:, HW:2 * HW] = dv_ref[...].astype(BF16)
        okr_ref[...] = pltpu.roll(acc, HEAD_PAD - NOPE, 1).astype(BF16)

    wide = pl.BlockSpec((tm, HW), lambda i: (i, 0))
    narrow = pl.BlockSpec((tm, HEAD_PAD), lambda i: (i, 0))
    return pl.pallas_call(
        body, name=name, grid=(S // tm,),
        in_specs=[wide, wide, wide, narrow, narrow],
        out_specs=[wide, pl.BlockSpec((tm, 2 * HW), lambda i: (i, 0)), narrow],
        out_shape=[jax.ShapeDtypeStruct((S, HW), BF16), jax.ShapeDtypeStruct((S, 2 * HW), BF16),
                   jax.ShapeDtypeStruct((S, HEAD_PAD), BF16)],
        compiler_params=_cparams("parallel"),
    )(dq, dk, dv, ct, st)


def _pairs(n, by_key):
    if by_key:
        pr = [(i, j) for j in range(n) for i in range(j, n)]
    else:
        pr = [(i, j) for i in range(n) for j in range(i + 1)]
    qi = np.array([p[0] for p in pr], np.int32)
    kj = np.array([p[1] for p in pr], np.int32)
    return jnp.asarray(qi), jnp.asarray(kj)


def _causal_mask(i, j, t):
    rows = lax.broadcasted_iota(jnp.int32, (t, t), 0) + i * t
    cols = lax.broadcasted_iota(jnp.int32, (t, t), 1) + j * t
    return cols <= rows


def _attn_fwd(q, k, kv_big, name):
    S = q.shape[0]
    HW = HEADS * HEAD_PAD
    t = _tile(S, TILES["attn"])
    n = S // t
    qi, kj = _pairs(n, by_key=False)
    scale = QK_DIM ** -0.5

    def body(qi_ref, kj_ref, q_ref, k_ref, v_ref, o_ref, lse_ref, m_s, l_s, acc_s):
        p_id = pl.program_id(1)
        i, j = qi_ref[p_id], kj_ref[p_id]

        @pl.when(j == 0)
        def _():
            m_s[...] = jnp.full_like(m_s, -jnp.inf)
            l_s[...] = jnp.zeros_like(l_s)
            acc_s[...] = jnp.zeros_like(acc_s)

        s = _dot(q_ref[...], k_ref[...], NT) * scale
        s = jnp.where(_causal_mask(i, j, t), s, -jnp.inf)
        m_old = m_s[...]
        m_new = jnp.maximum(m_old, jnp.max(s, axis=-1, keepdims=True))
        alpha = jnp.exp(m_old - m_new)
        p = jnp.exp(s - m_new)
        l_s[...] = alpha * l_s[...] + jnp.sum(p, axis=-1, keepdims=True)
        acc_s[...] = alpha * acc_s[...] + _dot(p.astype(BF16), v_ref[...], NN)
        m_s[...] = m_new

        @pl.when(j == i)
        def _():
            l = l_s[...]
            o_ref[...] = (acc_s[...] / l).astype(BF16)
            lse_ref[...] = jnp.broadcast_to(m_s[...] + jnp.log(l), (t, HEAD_PAD))

    grid_spec = pltpu.PrefetchScalarGridSpec(
        num_scalar_prefetch=2, grid=(HEADS, int(qi.shape[0])),
        in_specs=[pl.BlockSpec((t, HEAD_PAD), lambda h, p, qi, kj: (qi[p], h)),
                  pl.BlockSpec((t, HEAD_PAD), lambda h, p, qi, kj: (kj[p], h)),
                  pl.BlockSpec((t, HEAD_PAD), lambda h, p, qi, kj: (kj[p], HEADS + h))],
        out_specs=[pl.BlockSpec((t, HEAD_PAD), lambda h, p, qi, kj: (qi[p], h)),
                   pl.BlockSpec((t, HEAD_PAD), lambda h, p, qi, kj: (qi[p], h))],
        scratch_shapes=[pltpu.VMEM((t, 1), F32), pltpu.VMEM((t, 1), F32), pltpu.VMEM((t, HEAD_PAD), F32)])
    return pl.pallas_call(
        body, name=name, grid_spec=grid_spec,
        out_shape=[jax.ShapeDtypeStruct((S, HW), BF16), jax.ShapeDtypeStruct((S, HW), F32)],
        compiler_params=_cparams("parallel", "arbitrary"),
    )(qi, kj, q, k, kv_big)


def _attn_bwd(q, k, kv_big, o, do, lse, name):
    S = q.shape[0]
    HW = HEADS * HEAD_PAD
    t = _tile(S, TILES["attn"])
    n = S // t
    qi, kj = _pairs(n, by_key=True)
    scale = QK_DIM ** -0.5

    def body(qi_ref, kj_ref, q_ref, k_ref, v_ref, o_ref, do_ref, lse_ref, dq_ref, dk_ref, dv_ref, dk_s, dv_s):
        p_id = pl.program_id(1)
        i, j = qi_ref[p_id], kj_ref[p_id]

        @pl.when(p_id == 0)
        def _():
            dq_ref[...] = jnp.zeros_like(dq_ref)

        @pl.when(i == j)
        def _():
            dk_s[...] = jnp.zeros_like(dk_s)
            dv_s[...] = jnp.zeros_like(dv_s)

        qv, kv, vv = q_ref[...], k_ref[...], v_ref[...]
        dov = do_ref[...]
        s = _dot(qv, kv, NT) * scale
        p = jnp.where(_causal_mask(i, j, t), jnp.exp(s - lse_ref[:, 0:1]), 0.0)
        delta = jnp.sum(dov.astype(F32) * o_ref[...].astype(F32), axis=-1, keepdims=True)
        dv_s[...] += _dot(p.astype(BF16), dov, TN)
        dp = _dot(dov, vv, NT)
        ds = (p * (dp - delta) * scale).astype(BF16)
        dk_s[...] += _dot(ds, qv, TN)
        rows = pl.ds(pl.multiple_of(i * t, t), t)
        dq_ref[rows, :] += _dot(ds, kv, NN)

        @pl.when(i == n - 1)
        def _():
            dk_ref[...] = dk_s[...]
            dv_ref[...] = dv_s[...]

    qspec = pl.BlockSpec((t, HEAD_PAD), lambda h, p, qi, kj: (qi[p], h))
    kspec = pl.BlockSpec((t, HEAD_PAD), lambda h, p, qi, kj: (kj[p], h))
    grid_spec = pltpu.PrefetchScalarGridSpec(
        num_scalar_prefetch=2, grid=(HEADS, int(qi.shape[0])),
        in_specs=[qspec, kspec, pl.BlockSpec((t, HEAD_PAD), lambda h, p, qi, kj: (kj[p], HEADS + h)),
                  qspec, qspec, qspec],
        out_specs=[pl.BlockSpec((S, HEAD_PAD), lambda h, p, qi, kj: (0, h)), kspec, kspec],
        scratch_shapes=[pltpu.VMEM((t, HEAD_PAD), F32), pltpu.VMEM((t, HEAD_PAD), F32)])
    return pl.pallas_call(
        body, name=name, grid_spec=grid_spec,
        out_shape=[jax.ShapeDtypeStruct((S, HW), F32)] * 3,
        compiler_params=_cparams("parallel", "arbitrary"),
    )(qi, kj, q, k, kv_big, o, do, lse)


_SQRT_HALF = 0.7071067811865476
_INV_SQRT_2PI = 0.3989422804014327


def _sg_select(r, grp):
    out = jnp.where(grp == 0, r[0:SG_CHUNK, :], 0.0)
    for g in range(1, SG_GROUPS):
        out = out + jnp.where(grp == g, r[g * SG_CHUNK:(g + 1) * SG_CHUNK, :], 0.0)
    return out


def _sgu_fwd(proj, gain, wstack, bmat, name):
    S = proj.shape[0]
    W = SG_WIDTH
    tm = _tile(S, TILES["sgu"])

    def body(z_ref, g_ref, w_ref, b_ref, o_ref):
        z = z_ref[...].astype(F32)
        zg = 0.5 * z * (1.0 + lax.erf(z * _SQRT_HALF))
        u, vv = zg[:, 0:W], zg[:, W:2 * W]
        r = lax.rsqrt(jnp.mean(vv * vv, axis=-1, keepdims=True) + NORM_EPS)
        vn = (vv * r * g_ref[...]).astype(BF16)
        grp = lax.broadcasted_iota(jnp.int32, (1, W), 1) // SG_GROUP_DIM
        for c in range(tm // SG_CHUNK):
            sl = slice(c * SG_CHUNK, (c + 1) * SG_CHUNK)
            mixed = _sg_select(_dot(w_ref[...], vn[sl, :], NN), grp) + b_ref[...]
            o_ref[sl, :] = (u[sl, :] * mixed).astype(BF16)

    return pl.pallas_call(
        body, name=name, grid=(S // tm,),
        in_specs=[pl.BlockSpec((tm, 2 * W), lambda i: (i, 0)), pl.BlockSpec((1, W), lambda i: (0, 0)),
                  pl.BlockSpec(wstack.shape, lambda i: (0, 0)), pl.BlockSpec(bmat.shape, lambda i: (0, 0))],
        out_specs=pl.BlockSpec((tm, W), lambda i: (i, 0)),
        out_shape=jax.ShapeDtypeStruct((S, W), BF16),
        compiler_params=_cparams("parallel"),
    )(proj, _row(gain), wstack, bmat)


def _sgu_bwd(dsg, proj, gain, wstack, wtstack, bmat, gsum, name):
    S = proj.shape[0]
    W = SG_WIDTH
    tm = _tile(S, TILES["sgu"])
    GS = SG_GROUPS * SG_CHUNK

    def body(d_ref, z_ref, g_ref, w_ref, wt_ref, b_ref, e_ref, dz_ref, dw_ref, db_ref, dg_ref, dw_s, db_s):
        i = pl.program_id(0)

        @pl.when(i == 0)
        def _():
            dw_s[...] = jnp.zeros_like(dw_s)
            db_s[...] = jnp.zeros_like(db_s)
            dg_ref[...] = jnp.zeros_like(dg_ref)

        z = z_ref[...].astype(F32)
        cdf = 0.5 * (1.0 + lax.erf(z * _SQRT_HALF))
        zg = z * cdf
        u, vv = zg[:, 0:W], zg[:, W:2 * W]
        r = lax.rsqrt(jnp.mean(vv * vv, axis=-1, keepdims=True) + NORM_EPS)
        vhat = vv * r
        vn = (vhat * g_ref[...]).astype(BF16)
        grp = lax.broadcasted_iota(jnp.int32, (1, W), 1) // SG_GROUP_DIM
        d = d_ref[...].astype(F32)
        du_parts, dvn_parts = [], []
        for c in range(tm // SG_CHUNK):
            sl = slice(c * SG_CHUNK, (c + 1) * SG_CHUNK)
            vc = vn[sl, :]
            mixed = _sg_select(_dot(w_ref[...], vc, NN), grp) + b_ref[...]
            dc = d[sl, :]
            du_parts.append(dc * mixed)
            dmix = dc * u[sl, :]
            db_s[...] += dmix
            dmb = dmix.astype(BF16)
            dvn_parts.append(_sg_select(_dot(wt_ref[...], dmb, NN), grp))
            astack = jnp.concatenate([jnp.where(grp == g, dmb, jnp.zeros_like(dmb)) for g in range(SG_GROUPS)], axis=0)
            dw_s[...] += _dot(astack, vc, NT)
        du = jnp.concatenate(du_parts, axis=0)
        dvn = jnp.concatenate(dvn_parts, axis=0)
        dg_ref[...] += jnp.sum(dvn * vhat, axis=0, keepdims=True)
        dvhat = dvn * g_ref[...]
        dvv = r * (dvhat - vhat * jnp.mean(dvhat * vhat, axis=-1, keepdims=True))
        dgelu = cdf + z * (_INV_SQRT_2PI * jnp.exp(-0.5 * z * z))
        dz_ref[:, 0:W] = (du * dgelu[:, 0:W]).astype(BF16)
        dz_ref[:, W:2 * W] = (dvv * dgelu[:, W:2 * W]).astype(BF16)

        @pl.when(i == pl.num_programs(0) - 1)
        def _():
            dw_ref[...] = dw_s[...]
            db_ref[...] = lax.dot_general(db_s[...], e_ref[...], NN, precision=lax.Precision.HIGHEST,
                                          preferred_element_type=F32)

    full = lambda a: pl.BlockSpec(a.shape, lambda i: (0, 0))
    return pl.pallas_call(
        body, name=name, grid=(S // tm,),
        in_specs=[pl.BlockSpec((tm, W), lambda i: (i, 0)), pl.BlockSpec((tm, 2 * W), lambda i: (i, 0)),
                  pl.BlockSpec((1, W), lambda i: (0, 0)), full(wstack), full(wtstack), full(bmat), full(gsum)],
        out_specs=[pl.BlockSpec((tm, 2 * W), lambda i: (i, 0)), pl.BlockSpec((GS, SG_CHUNK), lambda i: (0, 0)),
                   pl.BlockSpec((SG_CHUNK, LANE), lambda i: (0, 0)), pl.BlockSpec((1, W), lambda i: (0, 0))],
        out_shape=[jax.ShapeDtypeStruct((S, 2 * W), BF16), jax.ShapeDtypeStruct((GS, SG_CHUNK), F32),
                   jax.ShapeDtypeStruct((SG_CHUNK, LANE), F32), jax.ShapeDtypeStruct((1, W), F32)],
        scratch_shapes=[pltpu.VMEM((GS, SG_CHUNK), F32), pltpu.VMEM((SG_CHUNK, W), F32)],
        compiler_params=_cparams("arbitrary"),
    )(dsg, proj, _row(gain), wstack, wtstack, bmat, gsum)


WEIGHTS = ['ffn_pre_norm', 'ffn_pre_w_gate', 'ffn_pre_w_up', 'ffn_pre_w_down', 'mix_norm', 'ffn_post_norm',
           'ffn_post_w_gate', 'ffn_post_w_up', 'ffn_post_w_down', 'even_w_in', 'q_norm', 'w_uq', 'kv_norm', 'w_ukv',
           'sg_norm', 'sg_w', 'sg_b', 'even_w_out', 'conv_w_in', 'conv_w', 'conv_w_out', 'final_norm']
SHARD_AXIS = dict(ffn_pre_w_gate=2, ffn_pre_w_up=2, ffn_pre_w_down=1, ffn_post_w_gate=2, ffn_post_w_up=2,
                  ffn_post_w_down=1, even_w_in=2, w_uq=2, w_ukv=2, even_w_out=1, conv_w_in=2, conv_w=2, conv_w_out=1)
SHARDED = [n for n in WEIGHTS if n in SHARD_AXIS]
REPLICATED = [n for n in WEIGHTS if n not in SHARD_AXIS]


def _to_t(name, w):
    return jnp.swapaxes(w, 1, 2) if SHARD_AXIS[name] == 2 else w


def _rows_of(n):
    return -(-n // PACK_W)


def _pad_rows(a, mult, axis):
    r = a.shape[axis]
    extra = (-r) % mult
    if extra == 0:
        return a
    pad = [(0, 0)] * a.ndim
    pad[axis] = (0, extra)
    return jnp.pad(a, pad)


def _flat_rows(a, lead):
    flat = a.reshape(a.shape[:lead] + (-1,))
    n = flat.shape[-1]
    flat = _pad_rows(flat, PACK_W, lead)
    return flat.reshape(a.shape[:lead] + (_rows_of(n), PACK_W))


def _pack(pieces, lead, mult):
    rows, offs, off = [], [], 0
    for p in pieces:
        r = _flat_rows(p, lead)
        rows.append(r)
        offs.append(off)
        off += r.shape[lead]
    return _pad_rows(jnp.concatenate(rows, axis=lead), mult, lead), offs


def _unpack(buf, off, shape, lead):
    n = math.prod(shape)
    r = _rows_of(n)
    piece = lax.slice_in_dim(buf, off, off + r, axis=lead)
    piece = piece.reshape(buf.shape[:lead] + (r * PACK_W,))
    piece = lax.slice_in_dim(piece, 0, n, axis=lead)
    return piece.reshape(buf.shape[:lead] + tuple(shape))


def _head_pad(w, per_head, keep):
    k = w.shape[-1]
    w = w.reshape(HEADS, per_head, k)[:, keep[0]:keep[1]]
    w = jnp.pad(w, ((0, 0), (0, HEAD_PAD - (keep[1] - keep[0])), (0, 0)))
    return w.reshape(HEADS * HEAD_PAD, k)


def _head_unpad(w, n):
    return w.reshape(HEADS, HEAD_PAD, w.shape[-1])[:, :n]


def kernel(x, positions, ffn_pre_norm, ffn_pre_w_gate, ffn_pre_w_up, ffn_pre_w_down, mix_norm, ffn_post_norm, ffn_post_w_gate, ffn_post_w_up, ffn_post_w_down, even_w_in, q_norm, w_uq, kv_norm, w_ukv, sg_norm, sg_w, sg_b, even_w_out, conv_w_in, conv_w, conv_w_out, final_norm, loss_target, m_ffn_pre_norm, m_ffn_pre_w_gate, m_ffn_pre_w_up, m_ffn_pre_w_down, m_mix_norm, m_ffn_post_norm, m_ffn_post_w_gate, m_ffn_post_w_up, m_ffn_post_w_down, m_even_w_in, m_q_norm, m_w_uq, m_kv_norm, m_w_ukv, m_sg_norm, m_sg_w, m_sg_b, m_even_w_out, m_conv_w_in, m_conv_w, m_conv_w_out, m_final_norm, v_ffn_pre_norm, v_ffn_pre_w_gate, v_ffn_pre_w_up, v_ffn_pre_w_down, v_mix_norm, v_ffn_post_norm, v_ffn_post_w_gate, v_ffn_post_w_up, v_ffn_post_w_down, v_even_w_in, v_q_norm, v_w_uq, v_kv_norm, v_w_ukv, v_sg_norm, v_sg_w, v_sg_b, v_even_w_out, v_conv_w_in, v_conv_w, v_conv_w_out, v_final_norm):
    given = dict(locals())
    w_loc = {n: given[n] for n in WEIGHTS}
    m_loc = {n: given["m_" + n] for n in WEIGHTS}
    v_loc = {n: given["v_" + n] for n in WEIGHTS}

    S, D = x.shape[1], x.shape[2]
    depth = ffn_pre_norm.shape[0]
    QL, KVL = q_norm.shape[1], kv_norm.shape[1]
    ZW = 2 * SG_WIDTH
    assert x.shape[0] == 1 and ZW % KVL == 0 and (ZW + KVL) % HEAD_PAD == 0 and (ZW + KVL + 2 * HEAD_PAD) % QL == 0
    col_ckv = ZW // KVL
    col_kr = (ZW + KVL) // HEAD_PAD
    col_cq = (ZW + KVL + 2 * HEAD_PAD) // QL

    t_loc = {n: _to_t(n, w_loc[n]) for n in SHARDED}
    wpack, woffs = _pack([t_loc[n].astype(BF16) for n in SHARDED], 0, 16)
    gathered = _all_gather(wpack, "gather_weights")
    full = {}
    for n, off in zip(SHARDED, woffs):
        piece = _unpack(gathered, off, t_loc[n].shape, 1)
        piece = jnp.moveaxis(piece, 0, 1)
        full[n] = piece.reshape(piece.shape[0], N_DEV * piece.shape[2], piece.shape[3])

    n_even = even_w_in.shape[0]
    win_pad, wq_big, wkv_big, wo_attn, wo_sg, wstack, wtstack, bmat = [], [], [], [], [], [], [], []
    tril = jnp.tril(jnp.ones((SG_CHUNK, SG_CHUNK), F32))
    for e in range(n_even):
        wi = full["even_w_in"][e]
        zrow = lambda k: jnp.zeros((k, D), BF16)
        win_pad.append(jnp.concatenate(
            [wi[QL + KVL + ROPE:], wi[QL:QL + KVL], wi[QL + KVL:QL + KVL + ROPE], zrow(HEAD_PAD - ROPE),
             zrow(HEAD_PAD), wi[:QL]], axis=0))
        wq_big.append(_head_pad(full["w_uq"][e], QK_DIM, (0, QK_DIM)))
        wkv = full["w_ukv"][e]
        wkv_big.append(jnp.concatenate([_head_pad(wkv, NOPE + VDIM, (0, NOPE)),
                                        _head_pad(wkv, NOPE + VDIM, (NOPE, NOPE + VDIM))], axis=0))
        wo = full["even_w_out"][e]
        wo_attn.append(_head_pad(wo[:HEADS * VDIM], VDIM, (0, VDIM)))
        wo_sg.append(wo[HEADS * VDIM:])
        wt = sg_w[e] * tril
        wstack.append(wt.reshape(SG_GROUPS * SG_CHUNK, SG_CHUNK).astype(BF16))
        wtstack.append(jnp.swapaxes(wt, 1, 2).reshape(SG_GROUPS * SG_CHUNK, SG_CHUNK).astype(BF16))
        bmat.append(jnp.repeat(sg_b[e].T, SG_GROUP_DIM, axis=1))
    gsum = (jnp.arange(SG_WIDTH)[:, None] // SG_GROUP_DIM == jnp.arange(LANE)[None, :]).astype(F32)
    cw8 = [jnp.pad(jnp.swapaxes(full["conv_w"][o], 0, 1).astype(F32), ((0, 8 - CONV_K), (0, 0)))
           for o in range(conv_w.shape[0])]

    inv_freq = ROPE_THETA ** (-jnp.arange(0, ROPE, 2, dtype=F32) / ROPE)
    ang = positions[0].astype(F32)[:, None] * inv_freq
    cos, sin = jnp.cos(ang), jnp.sin(ang)
    ones, zeros = jnp.ones((S, NOPE), F32), jnp.zeros((S, HEAD_PAD - QK_DIM), F32)
    ct = jnp.concatenate([ones, cos, cos, zeros], axis=1)
    st = jnp.concatenate([0.0 * ones, -sin, sin, zeros], axis=1)

    xs = x[0]
    saved = []
    for l in range(depth):
        sv = dict(x0=xs)
        x1, sv["a1"], sv["b1"] = _ffn_fwd(xs, ffn_pre_norm[l], full["ffn_pre_w_gate"][l], full["ffn_pre_w_up"][l],
                                          full["ffn_pre_w_down"][l], "ffn_fwd")
        h = _rms_fwd(x1, mix_norm[l], "mix_norm_fwd")
        sv.update(x1=x1, h=h)
        if l % 2 == 0:
            e = l // 2
            proj = _mm(h, win_pad[e], "nt", "even_in_proj", out_dtype=F32)
            qn = _rms_fwd(proj, q_norm[e], "q_norm_fwd", col=col_cq, width=QL)
            kvn = _rms_fwd(proj, kv_norm[e], "kv_norm_fwd", col=col_ckv, width=KVL)
            q_big = _mm(qn, wq_big[e], "nt", "q_up_proj", out_dtype=F32)
            kv_big = _mm(kvn, wkv_big[e], "nt", "kv_up_proj", out_dtype=BF16)
            q_r, k_r = _rope_fwd(q_big, kv_big, proj, col_kr, ct, st, "rope_fwd")
            o_att, lse = _attn_fwd(q_r, k_r, kv_big, "attn_fwd")
            sg = _sgu_fwd(proj, sg_norm[e], wstack[e], bmat[e], "sgu_fwd")
            tmp = _mm(o_att, wo_attn[e], "nn", "even_out_attn", out_dtype=F32, res=x1)
            x2 = _mm(sg, wo_sg[e], "nn", "even_out_sg", out_dtype=F32, res=tmp)
            sv.update(proj=proj, qn=qn, kvn=kvn, kv_big=kv_big, q=q_r, k=k_r, o=o_att, lse=lse, sg=sg)
        else:
            o = l // 2
            p = _mm(h, full["conv_w_in"][o], "nt", "conv_in_proj", out_dtype=BF16)
            cv = _conv_fwd(p, cw8[o], "conv_fwd")
            x2 = _mm(cv, full["conv_w_out"][o], "nn", "conv_out_proj", out_dtype=F32, res=x1)
            sv.update(p=p, cv=cv)
        sv["x2"] = x2
        xs, sv["a2"], sv["b2"] = _ffn_fwd(x2, ffn_post_norm[l], full["ffn_post_w_gate"][l], full["ffn_post_w_up"][l],
                                          full["ffn_post_w_down"][l], "ffn_fwd")
        saved.append(sv)

    gt = {n: [None] * w_loc[n].shape[0] for n in SHARDED}
    gr = {n: [None] * w_loc[n].shape[0] for n in REPLICATED if n != "final_norm"}
    dx, g_final, loss_part = _loss_head(xs, loss_target[0], final_norm, "loss_head")
    for l in reversed(range(depth)):
        sv = saved[l]
        dx, dz, hh, dy, dgain = _ffn_bwd(dx, sv["x2"], ffn_post_norm[l], sv["a2"], sv["b2"], full["ffn_post_w_gate"][l],
                                         full["ffn_post_w_up"][l], full["ffn_post_w_down"][l], "ffn_bwd")
        gr["ffn_post_norm"][l] = dgain[0]
        gt["ffn_post_w_gate"][l], gt["ffn_post_w_up"][l], gt["ffn_post_w_down"][l] = _ffn_dw(
            sv["a2"], sv["b2"], dz, hh, dy, "ffn_dw")
        h = sv["h"]
        if l % 2 == 0:
            e = l // 2
            d_o = _mm(dx, wo_attn[e], "nt", "even_out_attn_bwd", out_dtype=BF16)
            d_sg = _mm(dx, wo_sg[e], "nt", "even_out_sg_bwd", out_dtype=BF16)
            g_wo_attn = _mm(sv["o"], dx, "tn", "even_out_attn_dw", out_dtype=F32)
            g_wo_sg = _mm(sv["sg"], dx, "tn", "even_out_sg_dw", out_dtype=F32)
            dq, dk, dv = _attn_bwd(sv["q"], sv["k"], sv["kv_big"], sv["o"], d_o, sv["lse"], "attn_bwd")
            dq_big, dkv_big, dkr = _rope_bwd(dq, dk, dv, ct, st, "rope_bwd")
            dz_sg, g_wstack, g_bias, g_sgn = _sgu_bwd(d_sg, sv["proj"], sg_norm[e], wstack[e], wtstack[e], bmat[e],
                                                      gsum, "sgu_bwd")
            dqn = _mm(dq_big, wq_big[e], "nn", "q_up_proj_bwd", out_dtype=F32)
            g_wq_big = _mm(dq_big, sv["qn"], "tn", "q_up_proj_dw", out_dtype=F32)
            dkvn = _mm(dkv_big, wkv_big[e], "nn", "kv_up_proj_bwd", out_dtype=F32)
            g_wkv_big = _mm(dkv_big, sv["kvn"], "tn", "kv_up_proj_dw", out_dtype=F32)
            dcq, g_qn = _rms_bwd(dqn, sv["proj"], q_norm[e], "q_norm_bwd", col=col_cq, out_dtype=BF16)
            dckv, g_kvn = _rms_bwd(dkvn, sv["proj"], kv_norm[e], "kv_norm_bwd", col=col_ckv, out_dtype=BF16)
            dproj = jnp.concatenate([dz_sg, dckv, dkr, jnp.zeros((S, HEAD_PAD), BF16), dcq], axis=1)
            dh = _mm(dproj, win_pad[e], "nn", "even_in_proj_bwd", out_dtype=F32)
            g_win = _mm(dproj, h, "tn", "even_in_proj_dw", out_dtype=F32)
            o_cq, o_ckv, o_kr = col_cq * QL, col_ckv * KVL, col_kr * HEAD_PAD
            gt["even_w_in"][e] = jnp.concatenate(
                [g_win[o_cq:o_cq + QL], g_win[o_ckv:o_ckv + KVL], g_win[o_kr:o_kr + ROPE], g_win[:ZW]], axis=0)
            gt["w_uq"][e] = _head_unpad(g_wq_big, QK_DIM).reshape(HEADS * QK_DIM, QL)
            hw = HEADS * HEAD_PAD
            gt["w_ukv"][e] = jnp.concatenate([_head_unpad(g_wkv_big[:hw], NOPE), _head_unpad(g_wkv_big[hw:], VDIM)],
                                             axis=1).reshape(HEADS * (NOPE + VDIM), KVL)
            gt["even_w_out"][e] = jnp.concatenate([_head_unpad(g_wo_attn, VDIM).reshape(HEADS * VDIM, D), g_wo_sg], axis=0)
            gr["q_norm"][e], gr["kv_norm"][e], gr["sg_norm"][e] = g_qn[0], g_kvn[0], g_sgn[0]
            gr["sg_w"][e] = g_wstack.reshape(SG_GROUPS, SG_CHUNK, SG_CHUNK) * tril
            gr["sg_b"][e] = g_bias[:, :SG_GROUPS].T
        else:
            o = l // 2
            dcv = _mm(dx, full["conv_w_out"][o], "nt", "conv_out_proj_bwd", out_dtype=BF16)
            gt["conv_w_out"][o] = _mm(sv["cv"], dx, "tn", "conv_out_proj_dw", out_dtype=F32)
            dp, dcw = _conv_bwd(dcv, sv["p"], cw8[o], "conv_bwd")
            dh = _mm(dp, full["conv_w_in"][o], "nn", "conv_in_proj_bwd", out_dtype=F32)
            gt["conv_w_in"][o] = _mm(dp, h, "tn", "conv_in_proj_dw", out_dtype=F32)
            gt["conv_w"][o] = jnp.swapaxes(dcw[:CONV_K], 0, 1)
        dx, dgain = _rms_bwd(dh, sv["x1"], mix_norm[l], "mix_norm_bwd", res=dx)
        gr["mix_norm"][l] = dgain[0]
        dx, dz, hh, dy, dgain = _ffn_bwd(dx, sv["x0"], ffn_pre_norm[l], sv["a1"], sv["b1"], full["ffn_pre_w_gate"][l],
                                         full["ffn_pre_w_up"][l], full["ffn_pre_w_down"][l], "ffn_bwd")
        gr["ffn_pre_norm"][l] = dgain[0]
        gt["ffn_pre_w_gate"][l], gt["ffn_pre_w_up"][l], gt["ffn_pre_w_down"][l] = _ffn_dw(
            sv["a1"], sv["b1"], dz, hh, dy, "ffn_dw")
    grad_x = dx[None]

    pieces = []
    for n in SHARDED:
        g = jnp.stack(gt[n])
        g = g.reshape(g.shape[0], N_DEV, g.shape[1] // N_DEV, g.shape[2])
        pieces.append(jnp.moveaxis(g, 1, 0).astype(BF16))
    gpack, goffs = _pack(pieces, 1, 16)
    received = _all_to_all(gpack, "scatter_grads")
    gsum_t = _sum_slots(received, "sum_grad_shards")
    grads = {}
    for n, off in zip(SHARDED, goffs):
        g = _unpack(gsum_t, off, t_loc[n].shape, 0)
        grads[n] = _to_t(n, g)

    small = [jnp.stack(gr[n]) for n in REPLICATED if n != "final_norm"] + [g_final[0], loss_part[0, :1]]
    spack, soffs = _pack(small, 0, 8)
    sgath = _all_gather(spack, "gather_small_grads")
    ssum = _sum_slots(sgath, "sum_small_grads")
    names_small = [n for n in REPLICATED if n != "final_norm"] + ["final_norm", "loss"]
    for n, off, piece in zip(names_small, soffs, small):
        val = _unpack(ssum, off, piece.shape, 0)
        if n == "loss":
            loss = val[0]
        else:
            grads[n] = val

    flat = lambda d: _pack([d[n] for n in WEIGHTS], 0, 8)
    (wf, aoffs), (gf, _), (mf, _), (vf, _) = flat(w_loc), flat(grads), flat(m_loc), flat(v_loc)
    delta_f, newm_f, newv_f = _adamw(wf, gf, mf, vf, "adamw")
    outs = [loss, grad_x] + [grads[n] for n in WEIGHTS]
    for buf in (delta_f, newm_f, newv_f):
        outs += [_unpack(buf, off, w_loc[n].shape, 0) for n, off in zip(WEIGHTS, aoffs)]
    return tuple(outs)
```

```python
import functools
import math

import numpy as np
import jax
import jax.numpy as jnp
from jax import lax
from jax.experimental import pallas as pl
from jax.experimental.pallas import tpu as pltpu

F32 = jnp.float32
BF16 = jnp.bfloat16

N_DEV = 8
NORM_EPS = 1e-6
HEADS = 8
NOPE = 64
ROPE = 32
VDIM = 64
HEAD_PAD = 128
QK_DIM = NOPE + ROPE
ROPE_THETA = 10000.0
SG_GROUPS = 8
SG_GROUP_DIM = 64
SG_WIDTH = SG_GROUPS * SG_GROUP_DIM
SG_CHUNK = 128
CONV_K = 3
ADAM_LR, ADAM_B1, ADAM_B2, ADAM_EPS, ADAM_WD, ADAM_STEP = 0.001, 0.9, 0.999, 1e-08, 0.01, 10

LANE = 128
PACK_W = 1024
GRAD_ROWS_MULT = 512
SMALL_ROWS_MULT = 64
VMEM_LIMIT = 60 * 1024 * 1024

TILES = dict(ffn_fwd=512, ffn_bwd=256, ffn_dw=2048, mm=512, mm_tn=1024, ew=512, attn=1024, sgu=512, adam=512)

NT = (((1,), (1,)), ((), ()))
NN = (((1,), (0,)), ((), ()))
TN = (((0,), (0,)), ((), ()))


def _dot(a, b, dims):
    return lax.dot_general(a, b, dims, preferred_element_type=F32)


def _cparams(*sem):
    return pltpu.CompilerParams(dimension_semantics=sem if sem else None, vmem_limit_bytes=VMEM_LIMIT)


def _tile(n, want):
    t = min(want, n)
    while n % t:
        t //= 2
    return t if t % 8 == 0 else n


def _lane_tile(n, cap):
    best = None
    for k in range(1, n // LANE + 1):
        t = k * LANE
        if n % t == 0 and t <= cap:
            best = t
    return best or n


def _row(v):
    return v.reshape(1, -1).astype(F32)


def _all_gather(block, name):
    R, W = block.shape

    def body(x_ref, out_ref, send_sems, recv_sems, local_sem):
        x, y, c = lax.axis_index("x"), lax.axis_index("y"), lax.axis_index("c")
        me, sibling = (x, y, c), (x, y, 1 - c)
        chips = [(1 - x, y), (x, 1 - y), (1 - x, 1 - y)]

        def slot(px, py, pc):
            return out_ref.at[4 * px + 2 * py + pc]

        def copy(k, blk, to, src=None):
            return pltpu.make_async_remote_copy(
                src_ref=slot(*blk) if src is None else src, dst_ref=slot(*blk),
                send_sem=send_sems.at[k], recv_sem=recv_sems.at[k],
                device_id=to, device_id_type=pl.DeviceIdType.MESH)

        mine = pltpu.make_async_copy(x_ref, slot(*me), local_sem)
        mine.start()
        first = [copy(0, me, sibling, src=x_ref)]
        first += [copy(1 + j, me, (*chip, c), src=x_ref) for j, chip in enumerate(chips)]
        for cp in first:
            cp.start()
        passed = [copy(4 + j, (*chip, c), sibling) for j, chip in enumerate(chips)]
        for j, chip in enumerate(chips):
            copy(1 + j, (*chip, c), me).wait_recv()
            passed[j].start()
        copy(0, sibling, me).wait_recv()
        for j, chip in enumerate(chips):
            copy(4 + j, (*chip, 1 - c), me).wait_recv()
        for cp in first + passed:
            cp.wait_send()
        mine.wait()

    return pl.pallas_call(
        body, name=name,
        out_shape=jax.ShapeDtypeStruct((N_DEV, R, W), block.dtype),
        in_specs=[pl.BlockSpec(memory_space=pl.ANY)],
        out_specs=pl.BlockSpec(memory_space=pl.ANY),
        scratch_shapes=[pltpu.SemaphoreType.DMA((7,)), pltpu.SemaphoreType.DMA((7,)), pltpu.SemaphoreType.DMA],
    )(block)


def _all_to_all(pieces, name):
    W = pieces[0].shape[2]
    rows = [p.shape[1] for p in pieces]
    offs = [sum(rows[:i]) for i in range(len(rows))]
    R = sum(rows)
    npc = len(pieces)

    def body(*refs):
        s_refs, r_ref = refs[:npc], refs[npc]
        send_sems, recv_sems, local_sem = refs[npc + 1:]
        x, y, c = lax.axis_index("x"), lax.axis_index("y"), lax.axis_index("c")
        me = 4 * x + 2 * y + c
        for s_ref, off, r in zip(s_refs, offs, rows):
            pltpu.make_async_copy(s_ref.at[me], r_ref.at[me, pl.ds(off, r)], local_sem).start()
        totals = []
        for k in range(1, N_DEV):
            px, py, pc = x ^ (k >> 2), y ^ ((k >> 1) & 1), c ^ (k & 1)
            peer = 4 * px + 2 * py + pc

            def copy(src, dst, k=k, to=(px, py, pc)):
                return pltpu.make_async_remote_copy(
                    src_ref=src, dst_ref=dst, send_sem=send_sems.at[k - 1], recv_sem=recv_sems.at[k - 1],
                    device_id=to, device_id_type=pl.DeviceIdType.MESH)

            for s_ref, off, r in zip(s_refs, offs, rows):
                copy(s_ref.at[peer], r_ref.at[me, pl.ds(off, r)]).start()
            totals.append(copy(r_ref.at[me], r_ref.at[me]))
        for cp in totals:
            cp.wait_recv()
        for cp in totals:
            cp.wait_send()
        pltpu.make_async_copy(r_ref.at[me], r_ref.at[me], local_sem).wait()

    return pl.pallas_call(
        body, name=name,
        out_shape=jax.ShapeDtypeStruct((N_DEV, R, W), pieces[0].dtype),
        in_specs=[pl.BlockSpec(memory_space=pl.ANY)] * npc,
        out_specs=pl.BlockSpec(memory_space=pl.ANY),
        scratch_shapes=[pltpu.SemaphoreType.DMA((7,)), pltpu.SemaphoreType.DMA((7,)), pltpu.SemaphoreType.DMA],
    )(*pieces)


def _sum_slots(parts, name):
    _, R, W = parts.shape
    tr = _tile(R, TILES["adam"])

    def body(p_ref, o_ref):
        acc = p_ref[0].astype(F32)
        for s in range(1, N_DEV):
            acc = acc + p_ref[s].astype(F32)
        o_ref[...] = acc

    return pl.pallas_call(
        body, name=name, grid=(R // tr,),
        in_specs=[pl.BlockSpec((N_DEV, tr, W), lambda i: (0, i, 0))],
        out_specs=pl.BlockSpec((tr, W), lambda i: (i, 0)),
        out_shape=jax.ShapeDtypeStruct((R, W), F32),
        compiler_params=_cparams("parallel"),
    )(parts)


def _adamw(w, g, m, v, name):
    R, W = w.shape
    tr = _tile(R, TILES["adam"])
    c1 = 1.0 - ADAM_B1 ** ADAM_STEP
    c2 = 1.0 - ADAM_B2 ** ADAM_STEP

    def body(w_ref, g_ref, m_ref, v_ref, d_ref, nm_ref, nv_ref):
        g = g_ref[...]
        nm = ADAM_B1 * m_ref[...] + (1.0 - ADAM_B1) * g
        nv = ADAM_B2 * v_ref[...] + (1.0 - ADAM_B2) * (g * g)
        d_ref[...] = -ADAM_LR * ((nm / c1) / (jnp.sqrt(nv / c2) + ADAM_EPS) + ADAM_WD * w_ref[...])
        nm_ref[...] = nm
        nv_ref[...] = nv

    spec = pl.BlockSpec((tr, W), lambda i: (i, 0))
    return pl.pallas_call(
        body, name=name, grid=(R // tr,),
        in_specs=[spec] * 4, out_specs=[spec] * 3,
        out_shape=[jax.ShapeDtypeStruct((R, W), F32)] * 3,
        compiler_params=_cparams("parallel"),
    )(w, g, m, v)


def _mm(a, b, mode, name, out_dtype=BF16, res=None, scale=1.0, acol=None, kdim=None):
    if mode == "tn":
        S, M = a.shape
        N = b.shape[1]
        ts = _tile(S, TILES["mm_tn"])
        tmo = _lane_tile(M, 1024)

        def body(a_ref, b_ref, o_ref, acc):
            s = pl.program_id(1)

            @pl.when(s == 0)
            def _():
                acc[...] = jnp.zeros_like(acc)

            acc[...] += _dot(a_ref[...].astype(BF16), b_ref[...].astype(BF16), TN)

            @pl.when(s == pl.num_programs(1) - 1)
            def _():
                o_ref[...] = acc[...].astype(out_dtype)

        return pl.pallas_call(
            functools.partial(body), name=name, grid=(M // tmo, S // ts),
            in_specs=[pl.BlockSpec((ts, tmo), lambda i, s: (s, i)), pl.BlockSpec((ts, N), lambda i, s: (s, 0))],
            out_specs=pl.BlockSpec((tmo, N), lambda i, s: (i, 0)),
            out_shape=jax.ShapeDtypeStruct((M, N), out_dtype),
            scratch_shapes=[pltpu.VMEM((tmo, N), F32)],
            compiler_params=_cparams("parallel", "arbitrary"),
        )(a, b)

    M = a.shape[0]
    K = kdim if kdim is not None else a.shape[1]
    ac = 0 if acol is None else acol
    N = b.shape[1] if mode == "nn" else b.shape[0]
    tm = _tile(M, TILES["mm"])
    dims = NN if mode == "nn" else NT

    def body(*refs):
        if res is None:
            a_ref, b_ref, o_ref = refs
        else:
            a_ref, b_ref, r_ref, o_ref = refs
        acc = _dot(a_ref[...].astype(BF16), b_ref[...].astype(BF16), dims)
        if res is not None:
            acc = r_ref[...] + scale * acc
        o_ref[...] = acc.astype(out_dtype)

    in_specs = [pl.BlockSpec((tm, K), lambda i: (i, ac)), pl.BlockSpec(b.shape, lambda i: (0, 0))]
    args = [a, b]
    if res is not None:
        in_specs.append(pl.BlockSpec((tm, N), lambda i: (i, 0)))
        args.append(res)
    return pl.pallas_call(
        body, name=name, grid=(M // tm,),
        in_specs=in_specs, out_specs=pl.BlockSpec((tm, N), lambda i: (i, 0)),
        out_shape=jax.ShapeDtypeStruct((M, N), out_dtype),
        compiler_params=_cparams("parallel"),
    )(*args)


def _rms_fwd(x, gain, name, col=0, width=None):
    S = x.shape[0]
    W = width if width is not None else x.shape[1]
    tm = _tile(S, TILES["ew"])

    def body(x_ref, g_ref, o_ref):
        xv = x_ref[...].astype(F32)
        r = lax.rsqrt(jnp.mean(xv * xv, axis=-1, keepdims=True) + NORM_EPS)
        o_ref[...] = (xv * r * g_ref[...]).astype(BF16)

    return pl.pallas_call(
        body, name=name, grid=(S // tm,),
        in_specs=[pl.BlockSpec((tm, W), lambda i: (i, col)), pl.BlockSpec((1, W), lambda i: (0, 0))],
        out_specs=pl.BlockSpec((tm, W), lambda i: (i, 0)),
        out_shape=jax.ShapeDtypeStruct((S, W), BF16),
        compiler_params=_cparams("parallel"),
    )(x, _row(gain))


def _rms_bwd(dy, x, gain, name, col=0, res=None, out_dtype=F32):
    S, W = dy.shape
    tm = _tile(S, TILES["ew"])

    def body(*refs):
        if res is None:
            dy_ref, x_ref, g_ref, dx_ref, dg_ref = refs
        else:
            dy_ref, x_ref, g_ref, r_ref, dx_ref, dg_ref = refs

        @pl.when(pl.program_id(0) == 0)
        def _():
            dg_ref[...] = jnp.zeros_like(dg_ref)

        xv = x_ref[...].astype(F32)
        d = dy_ref[...].astype(F32)
        r = lax.rsqrt(jnp.mean(xv * xv, axis=-1, keepdims=True) + NORM_EPS)
        xhat = xv * r
        dg_ref[...] += jnp.sum(d * xhat, axis=0, keepdims=True)
        dxhat = d * g_ref[...]
        dx = r * (dxhat - xhat * jnp.mean(dxhat * xhat, axis=-1, keepdims=True))
        if res is not None:
            dx = dx + r_ref[...]
        dx_ref[...] = dx.astype(out_dtype)

    in_specs = [pl.BlockSpec((tm, W), lambda i: (i, 0)), pl.BlockSpec((tm, W), lambda i: (i, col)),
                pl.BlockSpec((1, W), lambda i: (0, 0))]
    args = [dy, x, _row(gain)]
    if res is not None:
        in_specs.append(pl.BlockSpec((tm, W), lambda i: (i, 0)))
        args.append(res)
    return pl.pallas_call(
        body, name=name, grid=(S // tm,),
        in_specs=in_specs,
        out_specs=[pl.BlockSpec((tm, W), lambda i: (i, 0)), pl.BlockSpec((1, W), lambda i: (0, 0))],
        out_shape=[jax.ShapeDtypeStruct((S, W), out_dtype), jax.ShapeDtypeStruct((1, W), F32)],
        compiler_params=_cparams("arbitrary"),
    )(*args)


def _silu_parts(a):
    s = jax.nn.sigmoid(a)
    return a * s, s * (1.0 + a * (1.0 - s))


def _ffn_fwd(x, gain, wg_t, wu_t, wd, name):
    S, D = x.shape
    Fd = wd.shape[0]
    tm = _tile(S, TILES["ffn_fwd"])
    fc = _lane_tile(Fd, 512)

    def body(x_ref, g_ref, wg_ref, wu_ref, wd_ref, o_ref, a_ref, b_ref):
        xv = x_ref[...]
        r = lax.rsqrt(jnp.mean(xv * xv, axis=-1, keepdims=True) + NORM_EPS)
        h = (xv * r * g_ref[...]).astype(BF16)
        acc = jnp.zeros((tm, D), F32)
        for c in range(Fd // fc):
            sl = slice(c * fc, (c + 1) * fc)
            a = _dot(h, wg_ref[sl, :], NT)
            b = _dot(h, wu_ref[sl, :], NT)
            a_ref[:, sl] = a.astype(BF16)
            b_ref[:, sl] = b.astype(BF16)
            z = (a * jax.nn.sigmoid(a) * b).astype(BF16)
            acc = acc + _dot(z, wd_ref[sl, :], NN)
        o_ref[...] = xv + 0.5 * acc

    wspec = pl.BlockSpec((Fd, D), lambda i: (0, 0), pipeline_mode=pl.Buffered(1))
    return pl.pallas_call(
        body, name=name, grid=(S // tm,),
        in_specs=[pl.BlockSpec((tm, D), lambda i: (i, 0)), pl.BlockSpec((1, D), lambda i: (0, 0)), wspec, wspec, wspec],
        out_specs=[pl.BlockSpec((tm, D), lambda i: (i, 0)), pl.BlockSpec((tm, Fd), lambda i: (i, 0)),
                   pl.BlockSpec((tm, Fd), lambda i: (i, 0))],
        out_shape=[jax.ShapeDtypeStruct((S, D), F32), jax.ShapeDtypeStruct((S, Fd), BF16),
                   jax.ShapeDtypeStruct((S, Fd), BF16)],
        compiler_params=_cparams("parallel"),
    )(x, _row(gain), wg_t, wu_t, wd)


def _ffn_bwd(g, x, gain, a, b, wg_t, wu_t, wd, name):
    S, D = x.shape
    Fd = wd.shape[0]
    tm = _tile(S, TILES["ffn_bwd"])
    fc = Fd

    def body(g_ref, x_ref, gain_ref, a_ref, b_ref, wg_ref, wu_ref, wd_ref, dx_ref, dz_ref, h_ref, dy_ref, dg_ref):
        @pl.when(pl.program_id(0) == 0)
        def _():
            dg_ref[...] = jnp.zeros_like(dg_ref)

        gv = g_ref[...]
        xv = x_ref[...]
        r = lax.rsqrt(jnp.mean(xv * xv, axis=-1, keepdims=True) + NORM_EPS)
        xhat = xv * r
        h_ref[...] = (xhat * gain_ref[...]).astype(BF16)
        dy = (0.5 * gv).astype(BF16)
        dy_ref[...] = dy
        dh = jnp.zeros((tm, D), F32)
        for c in range(Fd // fc):
            sl = slice(c * fc, (c + 1) * fc)
            av = a_ref[:, sl].astype(F32)
            bv = b_ref[:, sl].astype(F32)
            dz = _dot(dy, wd_ref[sl, :], NT).astype(BF16)
            dz_ref[:, sl] = dz
            dzf = dz.astype(F32)
            silu, dsilu = _silu_parts(av)
            da = (dzf * bv * dsilu).astype(BF16)
            db = (dzf * silu).astype(BF16)
            dh = dh + _dot(da, wg_ref[sl, :], NN) + _dot(db, wu_ref[sl, :], NN)
        dg_ref[...] += jnp.sum(dh * xhat, axis=0, keepdims=True)
        dxhat = dh * gain_ref[...]
        dx_ref[...] = gv + r * (dxhat - xhat * jnp.mean(dxhat * xhat, axis=-1, keepdims=True))

    wspec = pl.BlockSpec((Fd, D), lambda i: (0, 0), pipeline_mode=pl.Buffered(1))
    row = pl.BlockSpec((tm, D), lambda i: (i, 0))
    wide = pl.BlockSpec((tm, Fd), lambda i: (i, 0))
    return pl.pallas_call(
        body, name=name, grid=(S // tm,),
        in_specs=[row, row, pl.BlockSpec((1, D), lambda i: (0, 0)), wide, wide, wspec, wspec, wspec],
        out_specs=[row, wide, row, row, pl.BlockSpec((1, D), lambda i: (0, 0))],
        out_shape=[jax.ShapeDtypeStruct((S, D), F32), jax.ShapeDtypeStruct((S, Fd), BF16),
                   jax.ShapeDtypeStruct((S, D), BF16), jax.ShapeDtypeStruct((S, D), BF16),
                   jax.ShapeDtypeStruct((1, D), F32)],
        compiler_params=_cparams("arbitrary"),
    )(g, x, _row(gain), a, b, wg_t, wu_t, wd)


def _ffn_dw(a, b, dz, h, dy, name):
    S, Fd = a.shape
    D = h.shape[1]
    ts = _tile(S, TILES["ffn_dw"])
    tf = _lane_tile(Fd, 256)

    def body(a_ref, b_ref, dz_ref, h_ref, dy_ref, og_ref, ou_ref, od_ref, accg, accu, accd):
        s = pl.program_id(1)

        @pl.when(s == 0)
        def _():
            accg[...] = jnp.zeros_like(accg)
            accu[...] = jnp.zeros_like(accu)
            accd[...] = jnp.zeros_like(accd)

        av = a_ref[...].astype(F32)
        bv = b_ref[...].astype(F32)
        dzf = dz_ref[...].astype(F32)
        silu, dsilu = _silu_parts(av)
        da = (dzf * bv * dsilu).astype(BF16)
        db = (dzf * silu).astype(BF16)
        z = (silu * bv).astype(BF16)
        hv = h_ref[...]
        accg[...] += _dot(da, hv, TN)
        accu[...] += _dot(db, hv, TN)
        accd[...] += _dot(z, dy_ref[...], TN)

        @pl.when(s == pl.num_programs(1) - 1)
        def _():
            og_ref[...] = accg[...].astype(BF16)
            ou_ref[...] = accu[...].astype(BF16)
            od_ref[...] = accd[...].astype(BF16)

    wide = pl.BlockSpec((ts, tf), lambda f, s: (s, f))
    row = pl.BlockSpec((ts, D), lambda f, s: (s, 0))
    out = pl.BlockSpec((tf, D), lambda f, s: (f, 0))
    return pl.pallas_call(
        body, name=name, grid=(Fd // tf, S // ts),
        in_specs=[wide, wide, wide, row, row], out_specs=[out, out, out],
        out_shape=[jax.ShapeDtypeStruct((Fd, D), BF16)] * 3,
        scratch_shapes=[pltpu.VMEM((tf, D), F32)] * 3,
        compiler_params=_cparams("parallel", "arbitrary"),
    )(a, b, dz, h, dy)


def _loss_head(x, target, gain, name):
    S, D = x.shape
    tm = _tile(S, TILES["ew"])

    def body(x_ref, t_ref, g_ref, dx_ref, dg_ref, loss_ref):
        @pl.when(pl.program_id(0) == 0)
        def _():
            dg_ref[...] = jnp.zeros_like(dg_ref)
            loss_ref[...] = jnp.zeros_like(loss_ref)

        xv = x_ref[...]
        r = lax.rsqrt(jnp.mean(xv * xv, axis=-1, keepdims=True) + NORM_EPS)
        xhat = xv * r
        e = xhat * g_ref[...] - t_ref[...]
        per_tok = jnp.mean(e * e, axis=-1, keepdims=True)
        loss_ref[...] += jnp.broadcast_to(0.5 * jnp.sum(per_tok, axis=0, keepdims=True), (1, LANE))
        dy = e * (1.0 / D)
        dg_ref[...] += jnp.sum(dy * xhat, axis=0, keepdims=True)
        dxhat = dy * g_ref[...]
        dx_ref[...] = r * (dxhat - xhat * jnp.mean(dxhat * xhat, axis=-1, keepdims=True))

    row = pl.BlockSpec((tm, D), lambda i: (i, 0))
    return pl.pallas_call(
        body, name=name, grid=(S // tm,),
        in_specs=[row, row, pl.BlockSpec((1, D), lambda i: (0, 0))],
        out_specs=[row, pl.BlockSpec((1, D), lambda i: (0, 0)), pl.BlockSpec((1, LANE), lambda i: (0, 0))],
        out_shape=[jax.ShapeDtypeStruct((S, D), F32), jax.ShapeDtypeStruct((1, D), F32),
                   jax.ShapeDtypeStruct((1, LANE), F32)],
        compiler_params=_cparams("arbitrary"),
    )(x, target, _row(gain))


def _shift_down(u, halo, k, rows):
    out = pltpu.roll(u, k, 0)
    for j in range(k):
        out = jnp.where(rows == j, halo[8 - k + j:8 - k + j + 1, :], out)
    return out


def _shift_up(u, halo, k, rows, n):
    out = pltpu.roll(u, n - k, 0)
    for j in range(k):
        out = jnp.where(rows == n - k + j, halo[j:j + 1, :], out)
    return out


def _conv_fwd(p, cw, name):
    S, W3 = p.shape
    W = W3 // 3
    tm = _tile(S, TILES["ew"])
    hb = tm // 8

    def body(p_ref, ph_ref, w_ref, v_ref):
        i = pl.program_id(0)
        bg = p_ref[:, 0:W].astype(F32)
        u = p_ref[:, W:2 * W].astype(F32) * p_ref[:, 2 * W:3 * W].astype(F32)
        uh = ph_ref[:, W:2 * W].astype(F32) * ph_ref[:, 2 * W:3 * W].astype(F32)
        uh = jnp.where(i > 0, uh, 0.0)
        rows = lax.broadcasted_iota(jnp.int32, (tm, 1), 0)
        u1 = _shift_down(u, uh, 1, rows)
        u2 = _shift_down(u, uh, 2, rows)
        y = w_ref[0:1, :] * u2 + w_ref[1:2, :] * u1 + w_ref[2:3, :] * u
        v_ref[...] = (bg * y).astype(BF16)

    return pl.pallas_call(
        body, name=name, grid=(S // tm,),
        in_specs=[pl.BlockSpec((tm, W3), lambda i: (i, 0)),
                  pl.BlockSpec((8, W3), lambda i: (jnp.maximum(i * hb - 1, 0), 0)),
                  pl.BlockSpec((8, W), lambda i: (0, 0))],
        out_specs=pl.BlockSpec((tm, W), lambda i: (i, 0)),
        out_shape=jax.ShapeDtypeStruct((S, W), BF16),
        compiler_params=_cparams("parallel"),
    )(p, p, cw)


def _conv_bwd(dv, p, cw, name):
    S, W3 = p.shape
    W = W3 // 3
    tm = _tile(S, TILES["ew"])
    hb = tm // 8
    last = S // 8 - 1

    def body(dv_ref, dvn_ref, p_ref, pp_ref, pn_ref, w_ref, dp_ref, dw_ref):
        i = pl.program_id(0)
        n = pl.num_programs(0)

        @pl.when(i == 0)
        def _():
            dw_ref[...] = jnp.zeros_like(dw_ref)

        bg = p_ref[:, 0:W].astype(F32)
        cg = p_ref[:, W:2 * W].astype(F32)
        zz = p_ref[:, 2 * W:3 * W].astype(F32)
        u = cg * zz
        uh = pp_ref[:, W:2 * W].astype(F32) * pp_ref[:, 2 * W:3 * W].astype(F32)
        uh = jnp.where(i > 0, uh, 0.0)
        rows = lax.broadcasted_iota(jnp.int32, (tm, 1), 0)
        u1 = _shift_down(u, uh, 1, rows)
        u2 = _shift_down(u, uh, 2, rows)
        w0, w1, w2 = w_ref[0:1, :], w_ref[1:2, :], w_ref[2:3, :]
        y = w0 * u2 + w1 * u1 + w2 * u
        dvv = dv_ref[...].astype(F32)
        dy = dvv * bg
        dyh = dvn_ref[...].astype(F32) * pn_ref[:, 0:W].astype(F32)
        dyh = jnp.where(i < n - 1, dyh, 0.0)
        d1 = _shift_up(dy, dyh, 1, rows, tm)
        d2 = _shift_up(dy, dyh, 2, rows, tm)
        du = w2 * dy + w1 * d1 + w0 * d2
        dp_ref[:, 0:W] = (dvv * y).astype(BF16)
        dp_ref[:, W:2 * W] = (du * zz).astype(BF16)
        dp_ref[:, 2 * W:3 * W] = (du * cg).astype(BF16)
        dw_ref[0:1, :] += jnp.sum(dy * u2, axis=0, keepdims=True)
        dw_ref[1:2, :] += jnp.sum(dy * u1, axis=0, keepdims=True)
        dw_ref[2:3, :] += jnp.sum(dy * u, axis=0, keepdims=True)

    return pl.pallas_call(
        body, name=name, grid=(S // tm,),
        in_specs=[pl.BlockSpec((tm, W), lambda i: (i, 0)),
                  pl.BlockSpec((8, W), lambda i: (jnp.minimum((i + 1) * hb, last), 0)),
                  pl.BlockSpec((tm, W3), lambda i: (i, 0)),
                  pl.BlockSpec((8, W3), lambda i: (jnp.maximum(i * hb - 1, 0), 0)),
                  pl.BlockSpec((8, W3), lambda i: (jnp.minimum((i + 1) * hb, last), 0)),
                  pl.BlockSpec((8, W), lambda i: (0, 0))],
        out_specs=[pl.BlockSpec((tm, W3), lambda i: (i, 0)), pl.BlockSpec((8, W), lambda i: (0, 0))],
        out_shape=[jax.ShapeDtypeStruct((S, W3), BF16), jax.ShapeDtypeStruct((8, W), F32)],
        compiler_params=_cparams("arbitrary"),
    )(dv, dv, p, p, p, cw)


def _rope_swap(r, lane):
    mid = NOPE + ROPE // 2
    first = (lane >= NOPE) & (lane < mid)
    second = (lane >= mid) & (lane < QK_DIM)
    return jnp.where(first, pltpu.roll(r, HEAD_PAD - ROPE // 2, 1), jnp.where(second, pltpu.roll(r, ROPE // 2, 1), 0.0))


def _rope_fwd(q_big, kv_big, proj, kr_col, ct, st, name):
    S = q_big.shape[0]
    HW = HEADS * HEAD_PAD
    tm = _tile(S, TILES["ew"])

    def body(q_ref, k_ref, v_ref, kr_ref, ct_ref, st_ref, qo_ref, ko_ref, vo_ref):
        lane = lax.broadcasted_iota(jnp.int32, (1, HEAD_PAD), 1)
        ctv, stv = ct_ref[...], st_ref[...]
        krr = pltpu.roll(kr_ref[...].astype(F32), NOPE, 1)
        kro = krr * ctv + _rope_swap(krr, lane) * stv
        for h in range(HEADS):
            sl = slice(h * HEAD_PAD, (h + 1) * HEAD_PAD)
            qh = q_ref[:, sl].astype(F32)
            qo_ref[:, sl] = (qh * ctv + _rope_swap(qh, lane) * stv).astype(BF16)
            ko_ref[:, sl] = (k_ref[:, sl].astype(F32) + kro).astype(BF16)
            vo_ref[:, sl] = jnp.where(lane == VDIM, 1.0, v_ref[:, sl].astype(F32)).astype(BF16)

    wide = pl.BlockSpec((tm, HW), lambda i: (i, 0))
    narrow = pl.BlockSpec((tm, HEAD_PAD), lambda i: (i, 0))
    return pl.pallas_call(
        body, name=name, grid=(S // tm,),
        in_specs=[wide, wide, pl.BlockSpec((tm, HW), lambda i: (i, 1)),
                  pl.BlockSpec((tm, HEAD_PAD), lambda i: (i, kr_col)), narrow, narrow],
        out_specs=[wide, wide, wide],
        out_shape=[jax.ShapeDtypeStruct((S, HW), BF16)] * 3,
        compiler_params=_cparams("parallel"),
    )(q_big, kv_big, kv_big, proj, ct, st)


def _rope_bwd(dq, dk, dv, ct, st, name):
    S = dq.shape[0]
    HW = HEADS * HEAD_PAD
    tm = _tile(S, TILES["ew"])

    def body(dq_ref, dk_ref, dv_ref, ct_ref, st_ref, oq_ref, okv_ref, okr_ref):
        lane = lax.broadcasted_iota(jnp.int32, (1, HEAD_PAD), 1)
        ctv, stv = ct_ref[...], st_ref[...]
        acc = jnp.zeros((tm, HEAD_PAD), F32)
        for h in range(HEADS):
            sl = slice(h * HEAD_PAD, (h + 1) * HEAD_PAD)
            d = dq_ref[:, sl].astype(F32)
            oq_ref[:, sl] = (d * ctv + _rope_swap(d * stv, lane)).astype(BF16)
            d = dk_ref[:, sl].astype(F32)
            okv_ref[:, sl] = jnp.where(lane < NOPE, d, 0.0).astype(BF16)
            acc = acc + jnp.where(lane >= NOPE, d * ctv + _rope_swap(d * stv, lane), 0.0)
        okv_ref[:, HW:2 * HW] = dv_ref[...].astype(BF16)
        okr_ref[...] = pltpu.roll(acc, HEAD_PAD - NOPE, 1).astype(BF16)

    wide = pl.BlockSpec((tm, HW), lambda i: (i, 0))
    narrow = pl.BlockSpec((tm, HEAD_PAD), lambda i: (i, 0))
    return pl.pallas_call(
        body, name=name, grid=(S // tm,),
        in_specs=[wide, wide, wide, narrow, narrow],
        out_specs=[wide, pl.BlockSpec((tm, 2 * HW), lambda i: (i, 0)), narrow],
        out_shape=[jax.ShapeDtypeStruct((S, HW), BF16), jax.ShapeDtypeStruct((S, 2 * HW), BF16),
                   jax.ShapeDtypeStruct((S, HEAD_PAD), BF16)],
        compiler_params=_cparams("parallel"),
    )(dq, dk, dv, ct, st)


def _pairs(n, by_key):
    if by_key:
        pr = [(i, j) for j in range(n) for i in range(j, n)]
    else:
        pr = [(i, j) for i in range(n) for j in range(i + 1)]
    qi = np.array([p[0] for p in pr], np.int32)
    kj = np.array([p[1] for p in pr], np.int32)
    return jnp.asarray(qi), jnp.asarray(kj)


_LOG2E = 1.4426950408889634


def _tile_mask(t):
    return lax.broadcasted_iota(jnp.int32, (t, t), 1) <= lax.broadcasted_iota(jnp.int32, (t, t), 0)


def _attn_fwd(q, k, v, name):
    S = q.shape[0]
    HW = HEADS * HEAD_PAD
    t = _tile(S, TILES["attn"])
    n = S // t
    qi, kj = _pairs(n, by_key=False)
    c = (QK_DIM ** -0.5) * _LOG2E

    def body(qi_ref, kj_ref, q_ref, k_ref, v_ref, o_ref, lse_ref, m_s, acc_s):
        p_id = pl.program_id(1)
        i, j = qi_ref[p_id], kj_ref[p_id]

        @pl.when(j == 0)
        def _():
            m_s[...] = jnp.full_like(m_s, -jnp.inf)
            acc_s[...] = jnp.zeros_like(acc_s)

        def step(on_diagonal):
            s = _dot(q_ref[...], k_ref[...], NT)
            if on_diagonal:
                s = jnp.where(_tile_mask(t), s, -jnp.inf)
            m_old = m_s[...]
            m_new = jnp.maximum(m_old, jnp.max(s, axis=-1, keepdims=True))
            p = jnp.exp2((s - m_new) * c).astype(BF16)
            acc_s[...] = jnp.exp2((m_old - m_new) * c) * acc_s[...] + _dot(p, v_ref[...], NN)
            m_s[...] = m_new

        @pl.when(i == j)
        def _():
            step(True)

        @pl.when(i != j)
        def _():
            step(False)

        @pl.when(j == i)
        def _():
            acc = acc_s[...]
            l = acc[:, VDIM:VDIM + 1]
            o_ref[...] = (acc * (1.0 / l)).astype(BF16)
            lse_ref[...] = jnp.broadcast_to(m_s[...] * c + jnp.log2(l), (t, HEAD_PAD))

    qspec = pl.BlockSpec((t, HEAD_PAD), lambda h, p, qi, kj: (qi[p], h))
    kspec = pl.BlockSpec((t, HEAD_PAD), lambda h, p, qi, kj: (kj[p], h))
    grid_spec = pltpu.PrefetchScalarGridSpec(
        num_scalar_prefetch=2, grid=(HEADS, int(qi.shape[0])),
        in_specs=[qspec, kspec, kspec], out_specs=[qspec, qspec],
        scratch_shapes=[pltpu.VMEM((t, 1), F32), pltpu.VMEM((t, HEAD_PAD), F32)])
    return pl.pallas_call(
        body, name=name, grid_spec=grid_spec,
        out_shape=[jax.ShapeDtypeStruct((S, HW), BF16), jax.ShapeDtypeStruct((S, HW), F32)],
        compiler_params=_cparams("parallel", "arbitrary"),
    )(qi, kj, q, k, v)


def _attn_bwd(q, k, v, o, do, lse2, name):
    S = q.shape[0]
    HW = HEADS * HEAD_PAD
    t = _tile(S, TILES["attn"])
    n = S // t
    qi, kj = _pairs(n, by_key=True)
    scale = QK_DIM ** -0.5
    c = scale * _LOG2E

    def body(qi_ref, kj_ref, q_ref, k_ref, v_ref, o_ref, do_ref, lse_ref, dq_ref, dk_ref, dv_ref, dk_s, dv_s):
        p_id = pl.program_id(1)
        i, j = qi_ref[p_id], kj_ref[p_id]

        @pl.when(p_id == 0)
        def _():
            dq_ref[...] = jnp.zeros_like(dq_ref)

        @pl.when(i == j)
        def _():
            dk_s[...] = jnp.zeros_like(dk_s)
            dv_s[...] = jnp.zeros_like(dv_s)

        def step(on_diagonal):
            qv, kv, vv = q_ref[...], k_ref[...], v_ref[...]
            dov = do_ref[...]
            p = jnp.exp2(_dot(qv, kv, NT) * c - lse_ref[:, 0:1])
            if on_diagonal:
                p = jnp.where(_tile_mask(t), p, 0.0)
            delta = jnp.sum(dov.astype(F32) * o_ref[...].astype(F32), axis=-1, keepdims=True)
            dv_s[...] += _dot(p.astype(BF16), dov, TN)
            ds = (p * (_dot(dov, vv, NT) - delta)).astype(BF16)
            dk_s[...] += _dot(ds, qv, TN)
            rows = pl.ds(pl.multiple_of(i * t, t), t)
            dq_ref[rows, :] += scale * _dot(ds, kv, NN)

        @pl.when(i == j)
        def _():
            step(True)

        @pl.when(i != j)
        def _():
            step(False)

        @pl.when(i == n - 1)
        def _():
            dk_ref[...] = dk_s[...] * scale
            dv_ref[...] = dv_s[...]

    qspec = pl.BlockSpec((t, HEAD_PAD), lambda h, p, qi, kj: (qi[p], h))
    kspec = pl.BlockSpec((t, HEAD_PAD), lambda h, p, qi, kj: (kj[p], h))
    grid_spec = pltpu.PrefetchScalarGridSpec(
        num_scalar_prefetch=2, grid=(HEADS, int(qi.shape[0])),
        in_specs=[qspec, kspec, kspec, qspec, qspec, qspec],
        out_specs=[pl.BlockSpec((S, HEAD_PAD), lambda h, p, qi, kj: (0, h)), kspec, kspec],
        scratch_shapes=[pltpu.VMEM((t, HEAD_PAD), F32), pltpu.VMEM((t, HEAD_PAD), F32)])
    return pl.pallas_call(
        body, name=name, grid_spec=grid_spec,
        out_shape=[jax.ShapeDtypeStruct((S, HW), F32)] * 3,
        compiler_params=_cparams("parallel", "arbitrary"),
    )(qi, kj, q, k, v, o, do, lse2)


_SQRT_HALF = 0.7071067811865476
_INV_SQRT_2PI = 0.3989422804014327


def _sg_select(r, grp):
    out = jnp.where(grp == 0, r[0:SG_CHUNK, :], 0.0)
    for g in range(1, SG_GROUPS):
        out = out + jnp.where(grp == g, r[g * SG_CHUNK:(g + 1) * SG_CHUNK, :], 0.0)
    return out


def _sgu_fwd(proj, gain, wstack, bmat, name):
    S = proj.shape[0]
    W = SG_WIDTH
    tm = _tile(S, TILES["sgu"])

    def body(z_ref, g_ref, w_ref, b_ref, o_ref):
        z = z_ref[...].astype(F32)
        zg = 0.5 * z * (1.0 + lax.erf(z * _SQRT_HALF))
        u, vv = zg[:, 0:W], zg[:, W:2 * W]
        r = lax.rsqrt(jnp.mean(vv * vv, axis=-1, keepdims=True) + NORM_EPS)
        vn = (vv * r * g_ref[...]).astype(BF16)
        grp = lax.broadcasted_iota(jnp.int32, (1, W), 1) // SG_GROUP_DIM
        for c in range(tm // SG_CHUNK):
            sl = slice(c * SG_CHUNK, (c + 1) * SG_CHUNK)
            mixed = _sg_select(_dot(w_ref[...], vn[sl, :], NN), grp) + b_ref[...]
            o_ref[sl, :] = (u[sl, :] * mixed).astype(BF16)

    return pl.pallas_call(
        body, name=name, grid=(S // tm,),
        in_specs=[pl.BlockSpec((tm, 2 * W), lambda i: (i, 0)), pl.BlockSpec((1, W), lambda i: (0, 0)),
                  pl.BlockSpec(wstack.shape, lambda i: (0, 0)), pl.BlockSpec(bmat.shape, lambda i: (0, 0))],
        out_specs=pl.BlockSpec((tm, W), lambda i: (i, 0)),
        out_shape=jax.ShapeDtypeStruct((S, W), BF16),
        compiler_params=_cparams("parallel"),
    )(proj, _row(gain), wstack, bmat)


def _sgu_bwd(dsg, proj, gain, wstack, wtstack, bmat, gsum, name):
    S = proj.shape[0]
    W = SG_WIDTH
    tm = _tile(S, TILES["sgu"])
    GS = SG_GROUPS * SG_CHUNK

    def body(d_ref, z_ref, g_ref, w_ref, wt_ref, b_ref, e_ref, dz_ref, dw_ref, db_ref, dg_ref, dw_s, db_s):
        i = pl.program_id(0)

        @pl.when(i == 0)
        def _():
            dw_s[...] = jnp.zeros_like(dw_s)
            db_s[...] = jnp.zeros_like(db_s)
            dg_ref[...] = jnp.zeros_like(dg_ref)

        z = z_ref[...].astype(F32)
        cdf = 0.5 * (1.0 + lax.erf(z * _SQRT_HALF))
        zg = z * cdf
        u, vv = zg[:, 0:W], zg[:, W:2 * W]
        r = lax.rsqrt(jnp.mean(vv * vv, axis=-1, keepdims=True) + NORM_EPS)
        vhat = vv * r
        vn = (vhat * g_ref[...]).astype(BF16)
        grp = lax.broadcasted_iota(jnp.int32, (1, W), 1) // SG_GROUP_DIM
        d = d_ref[...].astype(F32)
        du_parts, dvn_parts = [], []
        for c in range(tm // SG_CHUNK):
            sl = slice(c * SG_CHUNK, (c + 1) * SG_CHUNK)
            vc = vn[sl, :]
            mixed = _sg_select(_dot(w_ref[...], vc, NN), grp) + b_ref[...]
            dc = d[sl, :]
            du_parts.append(dc * mixed)
            dmix = dc * u[sl, :]
            db_s[...] += dmix
            dmb = dmix.astype(BF16)
            dvn_parts.append(_sg_select(_dot(wt_ref[...], dmb, NN), grp))
            astack = jnp.concatenate([jnp.where(grp == g, dmb, jnp.zeros_like(dmb)) for g in range(SG_GROUPS)], axis=0)
            dw_s[...] += _dot(astack, vc, NT)
        du = jnp.concatenate(du_parts, axis=0)
        dvn = jnp.concatenate(dvn_parts, axis=0)
        dg_ref[...] += jnp.sum(dvn * vhat, axis=0, keepdims=True)
        dvhat = dvn * g_ref[...]
        dvv = r * (dvhat - vhat * jnp.mean(dvhat * vhat, axis=-1, keepdims=True))
        dgelu = cdf + z * (_INV_SQRT_2PI * jnp.exp(-0.5 * z * z))
        dz_ref[:, 0:W] = (du * dgelu[:, 0:W]).astype(BF16)
        dz_ref[:, W:2 * W] = (dvv * dgelu[:, W:2 * W]).astype(BF16)

        @pl.when(i == pl.num_programs(0) - 1)
        def _():
            dw_ref[...] = dw_s[...]
            db_ref[...] = lax.dot_general(db_s[...], e_ref[...], NN, precision=lax.Precision.HIGHEST,
                                          preferred_element_type=F32)

    full = lambda a: pl.BlockSpec(a.shape, lambda i: (0, 0))
    return pl.pallas_call(
        body, name=name, grid=(S // tm,),
        in_specs=[pl.BlockSpec((tm, W), lambda i: (i, 0)), pl.BlockSpec((tm, 2 * W), lambda i: (i, 0)),
                  pl.BlockSpec((1, W), lambda i: (0, 0)), full(wstack), full(wtstack), full(bmat), full(gsum)],
        out_specs=[pl.BlockSpec((tm, 2 * W), lambda i: (i, 0)), pl.BlockSpec((GS, SG_CHUNK), lambda i: (0, 0)),
                   pl.BlockSpec((SG_CHUNK, LANE), lambda i: (0, 0)), pl.BlockSpec((1, W), lambda i: (0, 0))],
        out_shape=[jax.ShapeDtypeStruct((S, 2 * W), BF16), jax.ShapeDtypeStruct((GS, SG_CHUNK), F32),
                   jax.ShapeDtypeStruct((SG_CHUNK, LANE), F32), jax.ShapeDtypeStruct((1, W), F32)],
        scratch_shapes=[pltpu.VMEM((GS, SG_CHUNK), F32), pltpu.VMEM((SG_CHUNK, W), F32)],
        compiler_params=_cparams("arbitrary"),
    )(dsg, proj, _row(gain), wstack, wtstack, bmat, gsum)


WEIGHTS = ['ffn_pre_norm', 'ffn_pre_w_gate', 'ffn_pre_w_up', 'ffn_pre_w_down', 'mix_norm', 'ffn_post_norm',
           'ffn_post_w_gate', 'ffn_post_w_up', 'ffn_post_w_down', 'even_w_in', 'q_norm', 'w_uq', 'kv_norm', 'w_ukv',
           'sg_norm', 'sg_w', 'sg_b', 'even_w_out', 'conv_w_in', 'conv_w', 'conv_w_out', 'final_norm']
SHARD_AXIS = dict(ffn_pre_w_gate=2, ffn_pre_w_up=2, ffn_pre_w_down=1, ffn_post_w_gate=2, ffn_post_w_up=2,
                  ffn_post_w_down=1, even_w_in=2, w_uq=2, w_ukv=2, even_w_out=1, conv_w_in=2, conv_w=2, conv_w_out=1)
SHARDED = [n for n in WEIGHTS if n in SHARD_AXIS]
REPLICATED = [n for n in WEIGHTS if n not in SHARD_AXIS]


def _to_t(name, w):
    return jnp.swapaxes(w, 1, 2) if SHARD_AXIS[name] == 2 else w


def _rows_of(n):
    return -(-n // PACK_W)


def _pad_rows(a, mult, axis):
    r = a.shape[axis]
    extra = (-r) % mult
    if extra == 0:
        return a
    pad = [(0, 0)] * a.ndim
    pad[axis] = (0, extra)
    return jnp.pad(a, pad)


def _flat_rows(a, lead):
    flat = a.reshape(a.shape[:lead] + (-1,))
    n = flat.shape[-1]
    flat = _pad_rows(flat, PACK_W, lead)
    return flat.reshape(a.shape[:lead] + (_rows_of(n), PACK_W))


def _pack(pieces, lead, mult):
    rows, offs, off = [], [], 0
    for p in pieces:
        r = _flat_rows(p, lead)
        rows.append(r)
        offs.append(off)
        off += r.shape[lead]
    return _pad_rows(jnp.concatenate(rows, axis=lead), mult, lead), offs


def _unpack(buf, off, shape, lead):
    n = math.prod(shape)
    r = _rows_of(n)
    piece = lax.slice_in_dim(buf, off, off + r, axis=lead)
    piece = piece.reshape(buf.shape[:lead] + (r * PACK_W,))
    piece = lax.slice_in_dim(piece, 0, n, axis=lead)
    return piece.reshape(buf.shape[:lead] + tuple(shape))


def _head_pad(w, per_head, keep):
    k = w.shape[-1]
    w = w.reshape(HEADS, per_head, k)[:, keep[0]:keep[1]]
    w = jnp.pad(w, ((0, 0), (0, HEAD_PAD - (keep[1] - keep[0])), (0, 0)))
    return w.reshape(HEADS * HEAD_PAD, k)


def _head_unpad(w, n):
    return w.reshape(HEADS, HEAD_PAD, w.shape[-1])[:, :n]


def kernel(x, positions, ffn_pre_norm, ffn_pre_w_gate, ffn_pre_w_up, ffn_pre_w_down, mix_norm, ffn_post_norm, ffn_post_w_gate, ffn_post_w_up, ffn_post_w_down, even_w_in, q_norm, w_uq, kv_norm, w_ukv, sg_norm, sg_w, sg_b, even_w_out, conv_w_in, conv_w, conv_w_out, final_norm, loss_target, m_ffn_pre_norm, m_ffn_pre_w_gate, m_ffn_pre_w_up, m_ffn_pre_w_down, m_mix_norm, m_ffn_post_norm, m_ffn_post_w_gate, m_ffn_post_w_up, m_ffn_post_w_down, m_even_w_in, m_q_norm, m_w_uq, m_kv_norm, m_w_ukv, m_sg_norm, m_sg_w, m_sg_b, m_even_w_out, m_conv_w_in, m_conv_w, m_conv_w_out, m_final_norm, v_ffn_pre_norm, v_ffn_pre_w_gate, v_ffn_pre_w_up, v_ffn_pre_w_down, v_mix_norm, v_ffn_post_norm, v_ffn_post_w_gate, v_ffn_post_w_up, v_ffn_post_w_down, v_even_w_in, v_q_norm, v_w_uq, v_kv_norm, v_w_ukv, v_sg_norm, v_sg_w, v_sg_b, v_even_w_out, v_conv_w_in, v_conv_w, v_conv_w_out, v_final_norm):
    given = dict(locals())
    w_loc = {n: given[n] for n in WEIGHTS}
    m_loc = {n: given["m_" + n] for n in WEIGHTS}
    v_loc = {n: given["v_" + n] for n in WEIGHTS}

    S, D = x.shape[1], x.shape[2]
    depth = ffn_pre_norm.shape[0]
    QL, KVL = q_norm.shape[1], kv_norm.shape[1]
    ZW = 2 * SG_WIDTH
    assert x.shape[0] == 1 and ZW % KVL == 0 and (ZW + KVL) % HEAD_PAD == 0 and (ZW + KVL + 2 * HEAD_PAD) % QL == 0
    col_ckv = ZW // KVL
    col_kr = (ZW + KVL) // HEAD_PAD
    col_cq = (ZW + KVL + 2 * HEAD_PAD) // QL

    t_loc = {n: _to_t(n, w_loc[n]) for n in SHARDED}
    wpack, woffs = _pack([t_loc[n].astype(BF16) for n in SHARDED], 0, 16)
    gathered = _all_gather(wpack, "gather_weights")
    full = {}
    for n, off in zip(SHARDED, woffs):
        piece = _unpack(gathered, off, t_loc[n].shape, 1)
        piece = jnp.moveaxis(piece, 0, 1)
        full[n] = piece.reshape(piece.shape[0], N_DEV * piece.shape[2], piece.shape[3])

    n_even = even_w_in.shape[0]
    win_pad, wq_big, wkv_big, wo_attn, wo_sg, wstack, wtstack, bmat = [], [], [], [], [], [], [], []
    tril = jnp.tril(jnp.ones((SG_CHUNK, SG_CHUNK), F32))
    for e in range(n_even):
        wi = full["even_w_in"][e]
        zrow = lambda k: jnp.zeros((k, D), BF16)
        win_pad.append(jnp.concatenate(
            [wi[QL + KVL + ROPE:], wi[QL:QL + KVL], wi[QL + KVL:QL + KVL + ROPE], zrow(HEAD_PAD - ROPE),
             zrow(HEAD_PAD), wi[:QL]], axis=0))
        wq_big.append(_head_pad(full["w_uq"][e], QK_DIM, (0, QK_DIM)))
        wkv = full["w_ukv"][e]
        wkv_big.append(jnp.concatenate([_head_pad(wkv, NOPE + VDIM, (0, NOPE)),
                                        _head_pad(wkv, NOPE + VDIM, (NOPE, NOPE + VDIM))], axis=0))
        wo = full["even_w_out"][e]
        wo_attn.append(_head_pad(wo[:HEADS * VDIM], VDIM, (0, VDIM)))
        wo_sg.append(wo[HEADS * VDIM:])
        wt = sg_w[e] * tril
        wstack.append(wt.reshape(SG_GROUPS * SG_CHUNK, SG_CHUNK).astype(BF16))
        wtstack.append(jnp.swapaxes(wt, 1, 2).reshape(SG_GROUPS * SG_CHUNK, SG_CHUNK).astype(BF16))
        bmat.append(jnp.repeat(sg_b[e].T, SG_GROUP_DIM, axis=1))
    gsum = (jnp.arange(SG_WIDTH)[:, None] // SG_GROUP_DIM == jnp.arange(LANE)[None, :]).astype(F32)
    cw8 = [jnp.pad(jnp.swapaxes(full["conv_w"][o], 0, 1).astype(F32), ((0, 8 - CONV_K), (0, 0)))
           for o in range(conv_w.shape[0])]

    inv_freq = ROPE_THETA ** (-jnp.arange(0, ROPE, 2, dtype=F32) / ROPE)
    ang = positions[0].astype(F32)[:, None] * inv_freq
    cos, sin = jnp.cos(ang), jnp.sin(ang)
    ones, zeros = jnp.ones((S, NOPE), F32), jnp.zeros((S, HEAD_PAD - QK_DIM), F32)
    ct = jnp.concatenate([ones, cos, cos, zeros], axis=1)
    st = jnp.concatenate([0.0 * ones, -sin, sin, zeros], axis=1)

    xs = x[0]
    saved = []
    for l in range(depth):
        sv = dict(x0=xs)
        x1, sv["a1"], sv["b1"] = _ffn_fwd(xs, ffn_pre_norm[l], full["ffn_pre_w_gate"][l], full["ffn_pre_w_up"][l],
                                          full["ffn_pre_w_down"][l], "ffn_fwd")
        h = _rms_fwd(x1, mix_norm[l], "mix_norm_fwd")
        sv.update(x1=x1, h=h)
        if l % 2 == 0:
            e = l // 2
            proj = _mm(h, win_pad[e], "nt", "even_in_proj", out_dtype=F32)
            qn = _rms_fwd(proj, q_norm[e], "q_norm_fwd", col=col_cq, width=QL)
            kvn = _rms_fwd(proj, kv_norm[e], "kv_norm_fwd", col=col_ckv, width=KVL)
            q_big = _mm(qn, wq_big[e], "nt", "q_up_proj", out_dtype=F32)
            kv_big = _mm(kvn, wkv_big[e], "nt", "kv_up_proj", out_dtype=BF16)
            q_r, k_r, v_r = _rope_fwd(q_big, kv_big, proj, col_kr, ct, st, "rope_fwd")
            o_att, lse = _attn_fwd(q_r, k_r, v_r, "attn_fwd")
            sg = _sgu_fwd(proj, sg_norm[e], wstack[e], bmat[e], "sgu_fwd")
            tmp = _mm(o_att, wo_attn[e], "nn", "even_out_attn", out_dtype=F32, res=x1)
            x2 = _mm(sg, wo_sg[e], "nn", "even_out_sg", out_dtype=F32, res=tmp)
            sv.update(proj=proj, qn=qn, kvn=kvn, q=q_r, k=k_r, v=v_r, o=o_att, lse=lse, sg=sg)
        else:
            o = l // 2
            p = _mm(h, full["conv_w_in"][o], "nt", "conv_in_proj", out_dtype=BF16)
            cv = _conv_fwd(p, cw8[o], "conv_fwd")
            x2 = _mm(cv, full["conv_w_out"][o], "nn", "conv_out_proj", out_dtype=F32, res=x1)
            sv.update(p=p, cv=cv)
        sv["x2"] = x2
        xs, sv["a2"], sv["b2"] = _ffn_fwd(x2, ffn_post_norm[l], full["ffn_post_w_gate"][l], full["ffn_post_w_up"][l],
                                          full["ffn_post_w_down"][l], "ffn_fwd")
        saved.append(sv)

    gt = {n: [None] * w_loc[n].shape[0] for n in SHARDED}
    gr = {n: [None] * w_loc[n].shape[0] for n in REPLICATED if n != "final_norm"}
    dx, g_final, loss_part = _loss_head(xs, loss_target[0], final_norm, "loss_head")
    for l in reversed(range(depth)):
        sv = saved[l]
        dx, dz, hh, dy, dgain = _ffn_bwd(dx, sv["x2"], ffn_post_norm[l], sv["a2"], sv["b2"], full["ffn_post_w_gate"][l],
                                         full["ffn_post_w_up"][l], full["ffn_post_w_down"][l], "ffn_bwd")
        gr["ffn_post_norm"][l] = dgain[0]
        gt["ffn_post_w_gate"][l], gt["ffn_post_w_up"][l], gt["ffn_post_w_down"][l] = _ffn_dw(
            sv["a2"], sv["b2"], dz, hh, dy, "ffn_dw")
        h = sv["h"]
        if l % 2 == 0:
            e = l // 2
            d_o = _mm(dx, wo_attn[e], "nt", "even_out_attn_bwd", out_dtype=BF16)
            d_sg = _mm(dx, wo_sg[e], "nt", "even_out_sg_bwd", out_dtype=BF16)
            g_wo_attn = _mm(sv["o"], dx, "tn", "even_out_attn_dw", out_dtype=F32)
            g_wo_sg = _mm(sv["sg"], dx, "tn", "even_out_sg_dw", out_dtype=F32)
            dq, dk, dv = _attn_bwd(sv["q"], sv["k"], sv["v"], sv["o"], d_o, sv["lse"], "attn_bwd")
            dq_big, dkv_big, dkr = _rope_bwd(dq, dk, dv, ct, st, "rope_bwd")
            dz_sg, g_wstack, g_bias, g_sgn = _sgu_bwd(d_sg, sv["proj"], sg_norm[e], wstack[e], wtstack[e], bmat[e],
                                                      gsum, "sgu_bwd")
            dqn = _mm(dq_big, wq_big[e], "nn", "q_up_proj_bwd", out_dtype=F32)
            g_wq_big = _mm(dq_big, sv["qn"], "tn", "q_up_proj_dw", out_dtype=F32)
            dkvn = _mm(dkv_big, wkv_big[e], "nn", "kv_up_proj_bwd", out_dtype=F32)
            g_wkv_big = _mm(dkv_big, sv["kvn"], "tn", "kv_up_proj_dw", out_dtype=F32)
            dcq, g_qn = _rms_bwd(dqn, sv["proj"], q_norm[e], "q_norm_bwd", col=col_cq, out_dtype=BF16)
            dckv, g_kvn = _rms_bwd(dkvn, sv["proj"], kv_norm[e], "kv_norm_bwd", col=col_ckv, out_dtype=BF16)
            dproj = jnp.concatenate([dz_sg, dckv, dkr, jnp.zeros((S, HEAD_PAD), BF16), dcq], axis=1)
            dh = _mm(dproj, win_pad[e], "nn", "even_in_proj_bwd", out_dtype=F32)
            g_win = _mm(dproj, h, "tn", "even_in_proj_dw", out_dtype=F32)
            o_cq, o_ckv, o_kr = col_cq * QL, col_ckv * KVL, col_kr * HEAD_PAD
            gt["even_w_in"][e] = jnp.concatenate(
                [g_win[o_cq:o_cq + QL], g_win[o_ckv:o_ckv + KVL], g_win[o_kr:o_kr + ROPE], g_win[:ZW]], axis=0)
            gt["w_uq"][e] = _head_unpad(g_wq_big, QK_DIM).reshape(HEADS * QK_DIM, QL)
            hw = HEADS * HEAD_PAD
            gt["w_ukv"][e] = jnp.concatenate([_head_unpad(g_wkv_big[:hw], NOPE), _head_unpad(g_wkv_big[hw:], VDIM)],
                                             axis=1).reshape(HEADS * (NOPE + VDIM), KVL)
            gt["even_w_out"][e] = jnp.concatenate([_head_unpad(g_wo_attn, VDIM).reshape(HEADS * VDIM, D), g_wo_sg], axis=0)
            gr["q_norm"][e], gr["kv_norm"][e], gr["sg_norm"][e] = g_qn[0], g_kvn[0], g_sgn[0]
            gr["sg_w"][e] = g_wstack.reshape(SG_GROUPS, SG_CHUNK, SG_CHUNK) * tril
            gr["sg_b"][e] = g_bias[:, :SG_GROUPS].T
        else:
            o = l // 2
            dcv = _mm(dx, full["conv_w_out"][o], "nt", "conv_out_proj_bwd", out_dtype=BF16)
            gt["conv_w_out"][o] = _mm(sv["cv"], dx, "tn", "conv_out_proj_dw", out_dtype=BF16)
            dp, dcw = _conv_bwd(dcv, sv["p"], cw8[o], "conv_bwd")
            dh = _mm(dp, full["conv_w_in"][o], "nn", "conv_in_proj_bwd", out_dtype=F32)
            gt["conv_w_in"][o] = _mm(dp, h, "tn", "conv_in_proj_dw", out_dtype=BF16)
            gt["conv_w"][o] = jnp.swapaxes(dcw[:CONV_K], 0, 1)
        dx, dgain = _rms_bwd(dh, sv["x1"], mix_norm[l], "mix_norm_bwd", res=dx)
        gr["mix_norm"][l] = dgain[0]
        dx, dz, hh, dy, dgain = _ffn_bwd(dx, sv["x0"], ffn_pre_norm[l], sv["a1"], sv["b1"], full["ffn_pre_w_gate"][l],
                                         full["ffn_pre_w_up"][l], full["ffn_pre_w_down"][l], "ffn_bwd")
        gr["ffn_pre_norm"][l] = dgain[0]
        gt["ffn_pre_w_gate"][l], gt["ffn_pre_w_up"][l], gt["ffn_pre_w_down"][l] = _ffn_dw(
            sv["a1"], sv["b1"], dz, hh, dy, "ffn_dw")
    grad_x = dx[None]

    pieces, where, off = [], [], 0
    for n in SHARDED:
        for l, g in enumerate(gt[n]):
            piece = _flat_rows(g.astype(BF16).reshape(N_DEV, -1), 1)
            piece = _pad_rows(piece, 16, 1)
            pieces.append(piece)
            where.append((n, l, off))
            off += piece.shape[1]
    if off % GRAD_ROWS_MULT:
        pieces.append(jnp.zeros((N_DEV, (-off) % GRAD_ROWS_MULT, PACK_W), BF16))
    received = _all_to_all(pieces, "scatter_grads")
    gsum_t = _sum_slots(received, "sum_grad_shards")
    per_layer = {n: [None] * len(gt[n]) for n in SHARDED}
    for n, l, off in where:
        per_layer[n][l] = _unpack(gsum_t, off, t_loc[n].shape[1:], 0)
    grads = {n: _to_t(n, jnp.stack(per_layer[n])) for n in SHARDED}

    small = [jnp.stack(gr[n]) for n in REPLICATED if n != "final_norm"] + [g_final[0], loss_part[0, :1]]
    spack, soffs = _pack(small, 0, SMALL_ROWS_MULT)
    sgath = _all_gather(spack, "gather_small_grads")
    ssum = _sum_slots(sgath, "sum_small_grads")
    names_small = [n for n in REPLICATED if n != "final_norm"] + ["final_norm", "loss"]
    for n, off, piece in zip(names_small, soffs, small):
        val = _unpack(ssum, off, piece.shape, 0)
        if n == "loss":
            loss = val[0]
        else:
            grads[n] = val

    delta, new_m, new_v = {}, {}, {}
    for n in SHARDED:
        two_d = lambda a: a.reshape(-1, a.shape[-1])
        d, nm, nv = _adamw(two_d(w_loc[n]), two_d(grads[n]), two_d(m_loc[n]), two_d(v_loc[n]), "adamw")
        delta[n], new_m[n], new_v[n] = (a.reshape(w_loc[n].shape) for a in (d, nm, nv))
    flat = lambda d: _pack([d[n] for n in REPLICATED], 0, SMALL_ROWS_MULT)
    (wf, aoffs), (gf, _), (mf, _), (vf, _) = flat(w_loc), flat(grads), flat(m_loc), flat(v_loc)
    for res, buf in zip((delta, new_m, new_v), _adamw(wf, gf, mf, vf, "adamw_replicated")):
        for n, off in zip(REPLICATED, aoffs):
            res[n] = _unpack(buf, off, w_loc[n].shape, 0)
    outs = [loss, grad_x] + [grads[n] for n in WEIGHTS]
    for res in (delta, new_m, new_v):
        outs += [res[n] for n in WEIGHTS]
    return tuple(outs)
```

```python
import functools
import math

import numpy as np
import jax
import jax.numpy as jnp
from jax import lax
from jax.experimental import pallas as pl
from jax.experimental.pallas import tpu as pltpu

F32 = jnp.float32
BF16 = jnp.bfloat16

N_DEV = 8
NORM_EPS = 1e-6
HEADS = 8
NOPE = 64
ROPE = 32
VDIM = 64
HEAD_PAD = 128
QK_DIM = NOPE + ROPE
ROPE_THETA = 10000.0
SG_GROUPS = 8
SG_GROUP_DIM = 64
SG_WIDTH = SG_GROUPS * SG_GROUP_DIM
SG_CHUNK = 128
CONV_K = 3
ADAM_LR, ADAM_B1, ADAM_B2, ADAM_EPS, ADAM_WD, ADAM_STEP = 0.001, 0.9, 0.999, 1e-08, 0.01, 10

LANE = 128
PACK_W = 1024
GRAD_ROWS_MULT = 512
SMALL_ROWS_MULT = 64
VMEM_LIMIT = 60 * 1024 * 1024

TILES = dict(ffn_fwd=512, ffn_bwd=256, ffn_dw=2048, mm=512, mm_tn=1024, ew=512, attn=1024, sgu=512, adam=512)

NT = (((1,), (1,)), ((), ()))
NN = (((1,), (0,)), ((), ()))
TN = (((0,), (0,)), ((), ()))


def _dot(a, b, dims):
    return lax.dot_general(a, b, dims, preferred_element_type=F32)


def _cparams(*sem):
    return pltpu.CompilerParams(dimension_semantics=sem if sem else None, vmem_limit_bytes=VMEM_LIMIT)


def _tile(n, want):
    t = min(want, n)
    while n % t:
        t //= 2
    return t if t % 8 == 0 else n


def _lane_tile(n, cap):
    best = None
    for k in range(1, n // LANE + 1):
        t = k * LANE
        if n % t == 0 and t <= cap:
            best = t
    return best or n


def _row(v):
    return v.reshape(1, -1).astype(F32)


def _all_gather(block, name):
    R, W = block.shape

    def body(x_ref, out_ref, send_sems, recv_sems, local_sem):
        x, y, c = lax.axis_index("x"), lax.axis_index("y"), lax.axis_index("c")
        me, sibling = (x, y, c), (x, y, 1 - c)
        chips = [(1 - x, y), (x, 1 - y), (1 - x, 1 - y)]

        def slot(px, py, pc):
            return out_ref.at[4 * px + 2 * py + pc]

        def copy(k, blk, to, src=None):
            return pltpu.make_async_remote_copy(
                src_ref=slot(*blk) if src is None else src, dst_ref=slot(*blk),
                send_sem=send_sems.at[k], recv_sem=recv_sems.at[k],
                device_id=to, device_id_type=pl.DeviceIdType.MESH)

        mine = pltpu.make_async_copy(x_ref, slot(*me), local_sem)
        mine.start()
        first = [copy(0, me, sibling, src=x_ref)]
        first += [copy(1 + j, me, (*chip, c), src=x_ref) for j, chip in enumerate(chips)]
        for cp in first:
            cp.start()
        passed = [copy(4 + j, (*chip, c), sibling) for j, chip in enumerate(chips)]
        for j, chip in enumerate(chips):
            copy(1 + j, (*chip, c), me).wait_recv()
            passed[j].start()
        copy(0, sibling, me).wait_recv()
        for j, chip in enumerate(chips):
            copy(4 + j, (*chip, 1 - c), me).wait_recv()
        for cp in first + passed:
            cp.wait_send()
        mine.wait()

    return pl.pallas_call(
        body, name=name,
        out_shape=jax.ShapeDtypeStruct((N_DEV, R, W), block.dtype),
        in_specs=[pl.BlockSpec(memory_space=pl.ANY)],
        out_specs=pl.BlockSpec(memory_space=pl.ANY),
        scratch_shapes=[pltpu.SemaphoreType.DMA((7,)), pltpu.SemaphoreType.DMA((7,)), pltpu.SemaphoreType.DMA],
    )(block)


class _Exchange:
    def __init__(self, kind, arrays):
        self.kind, self.arrays = kind, list(arrays)
        if kind == "gather":
            (r, w), = [a.shape for a in self.arrays]
            self.rows = [r]
        else:
            self.rows = [a.shape[1] for a in self.arrays]
            w = self.arrays[0].shape[2]
        self.offs = [sum(self.rows[:i]) for i in range(len(self.rows))]
        self.n_in = len(self.arrays)
        self.out_shape = jax.ShapeDtypeStruct((N_DEV, sum(self.rows), w), self.arrays[0].dtype)
        self.in_specs = [pl.BlockSpec(memory_space=pl.ANY)] * self.n_in
        self.out_spec = pl.BlockSpec(memory_space=pl.ANY)
        self.out_specs, self.out_shapes = [self.out_spec], [self.out_shape]
        self.scratch = [pltpu.SemaphoreType.DMA((7,)), pltpu.SemaphoreType.DMA((7,)), pltpu.SemaphoreType.DMA]

    def _peers(self):
        x, y, c = lax.axis_index("x"), lax.axis_index("y"), lax.axis_index("c")
        me = 4 * x + 2 * y + c
        return me, [(k, (x ^ (k >> 2), y ^ ((k >> 1) & 1), c ^ (k & 1))) for k in range(1, N_DEV)]

    @staticmethod
    def _remote(src, dst, k, to, send_sems, recv_sems):
        return pltpu.make_async_remote_copy(
            src_ref=src, dst_ref=dst, send_sem=send_sems.at[k - 1], recv_sem=recv_sems.at[k - 1],
            device_id=to, device_id_type=pl.DeviceIdType.MESH)

    def start(self, s_refs, r_ref, send_sems, recv_sems, local_sem):
        me, peers = self._peers()
        for s_ref, off, r in zip(s_refs, self.offs, self.rows):
            src = s_ref if self.kind == "gather" else s_ref.at[me]
            pltpu.make_async_copy(src, r_ref.at[me, pl.ds(off, r)], local_sem).start()
        for k, to in peers:
            peer = 4 * to[0] + 2 * to[1] + to[2]
            for s_ref, off, r in zip(s_refs, self.offs, self.rows):
                src = s_ref if self.kind == "gather" else s_ref.at[peer]
                self._remote(src, r_ref.at[me, pl.ds(off, r)], k, to, send_sems, recv_sems).start()

    def wait(self, s_refs, r_ref, send_sems, recv_sems, local_sem):
        me, peers = self._peers()
        whole = r_ref.at[me]
        totals = [self._remote(whole, whole, k, to, send_sems, recv_sems) for k, to in peers]
        for cp in totals:
            cp.wait_recv()
        for cp in totals:
            cp.wait_send()
        pltpu.make_async_copy(whole, whole, local_sem).wait()


class _NoRider:
    arrays, in_specs, out_specs, out_shapes, scratch = [], [], [], [], []


_NO_RIDER = _NoRider()


def _ride(rider, refs, n_in, n_out, first, last):
    if rider is None:
        return refs, lambda: None
    k = rider.n_in
    s_refs = refs[n_in:n_in + k]
    r_ref = refs[n_in + k + n_out]
    sems = refs[-3:]
    own = refs[:n_in] + refs[n_in + k:n_in + k + n_out] + refs[n_in + k + n_out + 1:-3]

    @pl.when(first)
    def _():
        rider.start(s_refs, r_ref, *sems)

    def finish():
        @pl.when(last)
        def _():
            rider.wait(s_refs, r_ref, *sems)

    return own, finish


def _exchange(kind, arrays, name):
    ex = _Exchange(kind, arrays)

    def body(*refs):
        s_refs, r_ref, sems = refs[:ex.n_in], refs[ex.n_in], refs[ex.n_in + 1:]
        ex.start(s_refs, r_ref, *sems)
        ex.wait(s_refs, r_ref, *sems)

    return pl.pallas_call(
        body, name=name, out_shape=ex.out_shape, in_specs=ex.in_specs, out_specs=ex.out_spec,
        scratch_shapes=ex.scratch,
    )(*ex.arrays)


def _sum_slots(parts, name):
    _, R, W = parts.shape
    tr = _tile(R, TILES["adam"])

    def body(p_ref, o_ref):
        acc = p_ref[0].astype(F32)
        for s in range(1, N_DEV):
            acc = acc + p_ref[s].astype(F32)
        o_ref[...] = acc

    return pl.pallas_call(
        body, name=name, grid=(R // tr,),
        in_specs=[pl.BlockSpec((N_DEV, tr, W), lambda i: (0, i, 0))],
        out_specs=pl.BlockSpec((tr, W), lambda i: (i, 0)),
        out_shape=jax.ShapeDtypeStruct((R, W), F32),
        compiler_params=_cparams("parallel"),
    )(parts)


def _adamw(w, g, m, v, name):
    R, W = w.shape
    tr = _tile(R, TILES["adam"])
    c1 = 1.0 - ADAM_B1 ** ADAM_STEP
    c2 = 1.0 - ADAM_B2 ** ADAM_STEP

    def body(w_ref, g_ref, m_ref, v_ref, d_ref, nm_ref, nv_ref):
        g = g_ref[...]
        nm = ADAM_B1 * m_ref[...] + (1.0 - ADAM_B1) * g
        nv = ADAM_B2 * v_ref[...] + (1.0 - ADAM_B2) * (g * g)
        d_ref[...] = -ADAM_LR * ((nm / c1) / (jnp.sqrt(nv / c2) + ADAM_EPS) + ADAM_WD * w_ref[...])
        nm_ref[...] = nm
        nv_ref[...] = nv

    spec = pl.BlockSpec((tr, W), lambda i: (i, 0))
    return pl.pallas_call(
        body, name=name, grid=(R // tr,),
        in_specs=[spec] * 4, out_specs=[spec] * 3,
        out_shape=[jax.ShapeDtypeStruct((R, W), F32)] * 3,
        compiler_params=_cparams("parallel"),
    )(w, g, m, v)


def _mm(a, b, mode, name, out_dtype=BF16, res=None, scale=1.0, acol=None, kdim=None):
    if mode == "tn":
        S, M = a.shape
        N = b.shape[1]
        ts = _tile(S, TILES["mm_tn"])
        tmo = _lane_tile(M, 1024)

        def body(a_ref, b_ref, o_ref, acc):
            s = pl.program_id(1)

            @pl.when(s == 0)
            def _():
                acc[...] = jnp.zeros_like(acc)

            acc[...] += _dot(a_ref[...].astype(BF16), b_ref[...].astype(BF16), TN)

            @pl.when(s == pl.num_programs(1) - 1)
            def _():
                o_ref[...] = acc[...].astype(out_dtype)

        return pl.pallas_call(
            functools.partial(body), name=name, grid=(M // tmo, S // ts),
            in_specs=[pl.BlockSpec((ts, tmo), lambda i, s: (s, i)), pl.BlockSpec((ts, N), lambda i, s: (s, 0))],
            out_specs=pl.BlockSpec((tmo, N), lambda i, s: (i, 0)),
            out_shape=jax.ShapeDtypeStruct((M, N), out_dtype),
            scratch_shapes=[pltpu.VMEM((tmo, N), F32)],
            compiler_params=_cparams("parallel", "arbitrary"),
        )(a, b)

    M = a.shape[0]
    K = kdim if kdim is not None else a.shape[1]
    ac = 0 if acol is None else acol
    N = b.shape[1] if mode == "nn" else b.shape[0]
    tm = _tile(M, TILES["mm"])
    dims = NN if mode == "nn" else NT

    def body(*refs):
        if res is None:
            a_ref, b_ref, o_ref = refs
        else:
            a_ref, b_ref, r_ref, o_ref = refs
        acc = _dot(a_ref[...].astype(BF16), b_ref[...].astype(BF16), dims)
        if res is not None:
            acc = r_ref[...] + scale * acc
        o_ref[...] = acc.astype(out_dtype)

    in_specs = [pl.BlockSpec((tm, K), lambda i: (i, ac)), pl.BlockSpec(b.shape, lambda i: (0, 0))]
    args = [a, b]
    if res is not None:
        in_specs.append(pl.BlockSpec((tm, N), lambda i: (i, 0)))
        args.append(res)
    return pl.pallas_call(
        body, name=name, grid=(M // tm,),
        in_specs=in_specs, out_specs=pl.BlockSpec((tm, N), lambda i: (i, 0)),
        out_shape=jax.ShapeDtypeStruct((M, N), out_dtype),
        compiler_params=_cparams("parallel"),
    )(*args)


def _rms_fwd(x, gain, name, col=0, width=None):
    S = x.shape[0]
    W = width if width is not None else x.shape[1]
    tm = _tile(S, TILES["ew"])

    def body(x_ref, g_ref, o_ref):
        xv = x_ref[...].astype(F32)
        r = lax.rsqrt(jnp.mean(xv * xv, axis=-1, keepdims=True) + NORM_EPS)
        o_ref[...] = (xv * r * g_ref[...]).astype(BF16)

    return pl.pallas_call(
        body, name=name, grid=(S // tm,),
        in_specs=[pl.BlockSpec((tm, W), lambda i: (i, col)), pl.BlockSpec((1, W), lambda i: (0, 0))],
        out_specs=pl.BlockSpec((tm, W), lambda i: (i, 0)),
        out_shape=jax.ShapeDtypeStruct((S, W), BF16),
        compiler_params=_cparams("parallel"),
    )(x, _row(gain))


def _rms_bwd(dy, x, gain, name, col=0, res=None, out_dtype=F32):
    S, W = dy.shape
    tm = _tile(S, TILES["ew"])

    def body(*refs):
        if res is None:
            dy_ref, x_ref, g_ref, dx_ref, dg_ref = refs
        else:
            dy_ref, x_ref, g_ref, r_ref, dx_ref, dg_ref = refs

        @pl.when(pl.program_id(0) == 0)
        def _():
            dg_ref[...] = jnp.zeros_like(dg_ref)

        xv = x_ref[...].astype(F32)
        d = dy_ref[...].astype(F32)
        r = lax.rsqrt(jnp.mean(xv * xv, axis=-1, keepdims=True) + NORM_EPS)
        xhat = xv * r
        dg_ref[...] += jnp.sum(d * xhat, axis=0, keepdims=True)
        dxhat = d * g_ref[...]
        dx = r * (dxhat - xhat * jnp.mean(dxhat * xhat, axis=-1, keepdims=True))
        if res is not None:
            dx = dx + r_ref[...]
        dx_ref[...] = dx.astype(out_dtype)

    in_specs = [pl.BlockSpec((tm, W), lambda i: (i, 0)), pl.BlockSpec((tm, W), lambda i: (i, col)),
                pl.BlockSpec((1, W), lambda i: (0, 0))]
    args = [dy, x, _row(gain)]
    if res is not None:
        in_specs.append(pl.BlockSpec((tm, W), lambda i: (i, 0)))
        args.append(res)
    return pl.pallas_call(
        body, name=name, grid=(S // tm,),
        in_specs=in_specs,
        out_specs=[pl.BlockSpec((tm, W), lambda i: (i, 0)), pl.BlockSpec((1, W), lambda i: (0, 0))],
        out_shape=[jax.ShapeDtypeStruct((S, W), out_dtype), jax.ShapeDtypeStruct((1, W), F32)],
        compiler_params=_cparams("arbitrary"),
    )(*args)


def _silu_parts(a):
    s = jax.nn.sigmoid(a)
    return a * s, s * (1.0 + a * (1.0 - s))


def _ffn_fwd(x, gain, wg_t, wu_t, wd, name, rider=None):
    S, D = x.shape
    Fd = wd.shape[0]
    tm = _tile(S, TILES["ffn_fwd"])
    fc = _lane_tile(Fd, 512)

    def body(*refs):
        i = pl.program_id(0)
        own, finish = _ride(rider, refs, 5, 3, i == 0, i == pl.num_programs(0) - 1)
        x_ref, g_ref, wg_ref, wu_ref, wd_ref, o_ref, a_ref, b_ref = own
        xv = x_ref[...]
        r = lax.rsqrt(jnp.mean(xv * xv, axis=-1, keepdims=True) + NORM_EPS)
        h = (xv * r * g_ref[...]).astype(BF16)
        acc = jnp.zeros((tm, D), F32)
        for c in range(Fd // fc):
            sl = slice(c * fc, (c + 1) * fc)
            a = _dot(h, wg_ref[sl, :], NT)
            b = _dot(h, wu_ref[sl, :], NT)
            a_ref[:, sl] = a.astype(BF16)
            b_ref[:, sl] = b.astype(BF16)
            z = (a * jax.nn.sigmoid(a) * b).astype(BF16)
            acc = acc + _dot(z, wd_ref[sl, :], NN)
        o_ref[...] = xv + 0.5 * acc
        finish()

    wspec = pl.BlockSpec((Fd, D), lambda i: (0, 0), pipeline_mode=pl.Buffered(1))
    extra = rider or _NO_RIDER
    return pl.pallas_call(
        body, name=name, grid=(S // tm,),
        in_specs=[pl.BlockSpec((tm, D), lambda i: (i, 0)), pl.BlockSpec((1, D), lambda i: (0, 0)), wspec, wspec,
                  wspec] + extra.in_specs,
        out_specs=[pl.BlockSpec((tm, D), lambda i: (i, 0)), pl.BlockSpec((tm, Fd), lambda i: (i, 0)),
                   pl.BlockSpec((tm, Fd), lambda i: (i, 0))] + extra.out_specs,
        out_shape=[jax.ShapeDtypeStruct((S, D), F32), jax.ShapeDtypeStruct((S, Fd), BF16),
                   jax.ShapeDtypeStruct((S, Fd), BF16)] + extra.out_shapes,
        scratch_shapes=extra.scratch,
        compiler_params=_cparams("arbitrary" if rider else "parallel"),
    )(x, _row(gain), wg_t, wu_t, wd, *extra.arrays)


def _ffn_bwd(g, x, gain, a, b, wg_t, wu_t, wd, name, rider=None):
    S, D = x.shape
    Fd = wd.shape[0]
    tm = _tile(S, TILES["ffn_bwd"])
    fc = Fd

    def body(*refs):
        i = pl.program_id(0)
        own, finish = _ride(rider, refs, 8, 5, i == 0, i == pl.num_programs(0) - 1)
        g_ref, x_ref, gain_ref, a_ref, b_ref, wg_ref, wu_ref, wd_ref, dx_ref, dz_ref, h_ref, dy_ref, dg_ref = own

        @pl.when(i == 0)
        def _():
            dg_ref[...] = jnp.zeros_like(dg_ref)

        gv = g_ref[...]
        xv = x_ref[...]
        r = lax.rsqrt(jnp.mean(xv * xv, axis=-1, keepdims=True) + NORM_EPS)
        xhat = xv * r
        h_ref[...] = (xhat * gain_ref[...]).astype(BF16)
        dy = (0.5 * gv).astype(BF16)
        dy_ref[...] = dy
        dh = jnp.zeros((tm, D), F32)
        for c in range(Fd // fc):
            sl = slice(c * fc, (c + 1) * fc)
            av = a_ref[:, sl].astype(F32)
            bv = b_ref[:, sl].astype(F32)
            dz = _dot(dy, wd_ref[sl, :], NT).astype(BF16)
            dz_ref[:, sl] = dz
            dzf = dz.astype(F32)
            silu, dsilu = _silu_parts(av)
            da = (dzf * bv * dsilu).astype(BF16)
            db = (dzf * silu).astype(BF16)
            dh = dh + _dot(da, wg_ref[sl, :], NN) + _dot(db, wu_ref[sl, :], NN)
        dg_ref[...] += jnp.sum(dh * xhat, axis=0, keepdims=True)
        dxhat = dh * gain_ref[...]
        dx_ref[...] = gv + r * (dxhat - xhat * jnp.mean(dxhat * xhat, axis=-1, keepdims=True))
        finish()

    wspec = pl.BlockSpec((Fd, D), lambda i: (0, 0), pipeline_mode=pl.Buffered(1))
    row = pl.BlockSpec((tm, D), lambda i: (i, 0))
    wide = pl.BlockSpec((tm, Fd), lambda i: (i, 0))
    extra = rider or _NO_RIDER
    return pl.pallas_call(
        body, name=name, grid=(S // tm,),
        in_specs=[row, row, pl.BlockSpec((1, D), lambda i: (0, 0)), wide, wide, wspec, wspec, wspec] + extra.in_specs,
        out_specs=[row, wide, row, row, pl.BlockSpec((1, D), lambda i: (0, 0))] + extra.out_specs,
        out_shape=[jax.ShapeDtypeStruct((S, D), F32), jax.ShapeDtypeStruct((S, Fd), BF16),
                   jax.ShapeDtypeStruct((S, D), BF16), jax.ShapeDtypeStruct((S, D), BF16),
                   jax.ShapeDtypeStruct((1, D), F32)] + extra.out_shapes,
        scratch_shapes=extra.scratch,
        compiler_params=_cparams("arbitrary"),
    )(g, x, _row(gain), a, b, wg_t, wu_t, wd, *extra.arrays)


def _ffn_dw(a, b, dz, h, dy, name):
    S, Fd = a.shape
    D = h.shape[1]
    ts = _tile(S, TILES["ffn_dw"])
    tf = _lane_tile(Fd, 256)

    def body(a_ref, b_ref, dz_ref, h_ref, dy_ref, og_ref, ou_ref, od_ref, accg, accu, accd):
        s = pl.program_id(1)

        @pl.when(s == 0)
        def _():
            accg[...] = jnp.zeros_like(accg)
            accu[...] = jnp.zeros_like(accu)
            accd[...] = jnp.zeros_like(accd)

        av = a_ref[...].astype(F32)
        bv = b_ref[...].astype(F32)
        dzf = dz_ref[...].astype(F32)
        silu, dsilu = _silu_parts(av)
        da = (dzf * bv * dsilu).astype(BF16)
        db = (dzf * silu).astype(BF16)
        z = (silu * bv).astype(BF16)
        hv = h_ref[...]
        accg[...] += _dot(da, hv, TN)
        accu[...] += _dot(db, hv, TN)
        accd[...] += _dot(z, dy_ref[...], TN)

        @pl.when(s == pl.num_programs(1) - 1)
        def _():
            og_ref[...] = accg[...].astype(BF16)
            ou_ref[...] = accu[...].astype(BF16)
            od_ref[...] = accd[...].astype(BF16)

    wide = pl.BlockSpec((ts, tf), lambda f, s: (s, f))
    row = pl.BlockSpec((ts, D), lambda f, s: (s, 0))
    out = pl.BlockSpec((tf, D), lambda f, s: (f, 0))
    return pl.pallas_call(
        body, name=name, grid=(Fd // tf, S // ts),
        in_specs=[wide, wide, wide, row, row], out_specs=[out, out, out],
        out_shape=[jax.ShapeDtypeStruct((Fd, D), BF16)] * 3,
        scratch_shapes=[pltpu.VMEM((tf, D), F32)] * 3,
        compiler_params=_cparams("parallel", "arbitrary"),
    )(a, b, dz, h, dy)


def _loss_head(x, target, gain, name):
    S, D = x.shape
    tm = _tile(S, TILES["ew"])

    def body(x_ref, t_ref, g_ref, dx_ref, dg_ref, loss_ref):
        @pl.when(pl.program_id(0) == 0)
        def _():
            dg_ref[...] = jnp.zeros_like(dg_ref)
            loss_ref[...] = jnp.zeros_like(loss_ref)

        xv = x_ref[...]
        r = lax.rsqrt(jnp.mean(xv * xv, axis=-1, keepdims=True) + NORM_EPS)
        xhat = xv * r
        e = xhat * g_ref[...] - t_ref[...]
        per_tok = jnp.mean(e * e, axis=-1, keepdims=True)
        loss_ref[...] += jnp.broadcast_to(0.5 * jnp.sum(per_tok, axis=0, keepdims=True), (1, LANE))
        dy = e * (1.0 / D)
        dg_ref[...] += jnp.sum(dy * xhat, axis=0, keepdims=True)
        dxhat = dy * g_ref[...]
        dx_ref[...] = r * (dxhat - xhat * jnp.mean(dxhat * xhat, axis=-1, keepdims=True))

    row = pl.BlockSpec((tm, D), lambda i: (i, 0))
    return pl.pallas_call(
        body, name=name, grid=(S // tm,),
        in_specs=[row, row, pl.BlockSpec((1, D), lambda i: (0, 0))],
        out_specs=[row, pl.BlockSpec((1, D), lambda i: (0, 0)), pl.BlockSpec((1, LANE), lambda i: (0, 0))],
        out_shape=[jax.ShapeDtypeStruct((S, D), F32), jax.ShapeDtypeStruct((1, D), F32),
                   jax.ShapeDtypeStruct((1, LANE), F32)],
        compiler_params=_cparams("arbitrary"),
    )(x, target, _row(gain))


def _shift_down(u, halo, k, rows):
    out = pltpu.roll(u, k, 0)
    for j in range(k):
        out = jnp.where(rows == j, halo[8 - k + j:8 - k + j + 1, :], out)
    return out


def _shift_up(u, halo, k, rows, n):
    out = pltpu.roll(u, n - k, 0)
    for j in range(k):
        out = jnp.where(rows == n - k + j, halo[j:j + 1, :], out)
    return out


def _conv_fwd(p, cw, name):
    S, W3 = p.shape
    W = W3 // 3
    tm = _tile(S, TILES["ew"])
    hb = tm // 8

    def body(p_ref, ph_ref, w_ref, v_ref):
        i = pl.program_id(0)
        bg = p_ref[:, 0:W].astype(F32)
        u = p_ref[:, W:2 * W].astype(F32) * p_ref[:, 2 * W:3 * W].astype(F32)
        uh = ph_ref[:, W:2 * W].astype(F32) * ph_ref[:, 2 * W:3 * W].astype(F32)
        uh = jnp.where(i > 0, uh, 0.0)
        rows = lax.broadcasted_iota(jnp.int32, (tm, 1), 0)
        u1 = _shift_down(u, uh, 1, rows)
        u2 = _shift_down(u, uh, 2, rows)
        y = w_ref[0:1, :] * u2 + w_ref[1:2, :] * u1 + w_ref[2:3, :] * u
        v_ref[...] = (bg * y).astype(BF16)

    return pl.pallas_call(
        body, name=name, grid=(S // tm,),
        in_specs=[pl.BlockSpec((tm, W3), lambda i: (i, 0)),
                  pl.BlockSpec((8, W3), lambda i: (jnp.maximum(i * hb - 1, 0), 0)),
                  pl.BlockSpec((8, W), lambda i: (0, 0))],
        out_specs=pl.BlockSpec((tm, W), lambda i: (i, 0)),
        out_shape=jax.ShapeDtypeStruct((S, W), BF16),
        compiler_params=_cparams("parallel"),
    )(p, p, cw)


def _conv_bwd(dv, p, cw, name):
    S, W3 = p.shape
    W = W3 // 3
    tm = _tile(S, TILES["ew"])
    hb = tm // 8
    last = S // 8 - 1

    def body(dv_ref, dvn_ref, p_ref, pp_ref, pn_ref, w_ref, dp_ref, dw_ref):
        i = pl.program_id(0)
        n = pl.num_programs(0)

        @pl.when(i == 0)
        def _():
            dw_ref[...] = jnp.zeros_like(dw_ref)

        bg = p_ref[:, 0:W].astype(F32)
        cg = p_ref[:, W:2 * W].astype(F32)
        zz = p_ref[:, 2 * W:3 * W].astype(F32)
        u = cg * zz
        uh = pp_ref[:, W:2 * W].astype(F32) * pp_ref[:, 2 * W:3 * W].astype(F32)
        uh = jnp.where(i > 0, uh, 0.0)
        rows = lax.broadcasted_iota(jnp.int32, (tm, 1), 0)
        u1 = _shift_down(u, uh, 1, rows)
        u2 = _shift_down(u, uh, 2, rows)
        w0, w1, w2 = w_ref[0:1, :], w_ref[1:2, :], w_ref[2:3, :]
        y = w0 * u2 + w1 * u1 + w2 * u
        dvv = dv_ref[...].astype(F32)
        dy = dvv * bg
        dyh = dvn_ref[...].astype(F32) * pn_ref[:, 0:W].astype(F32)
        dyh = jnp.where(i < n - 1, dyh, 0.0)
        d1 = _shift_up(dy, dyh, 1, rows, tm)
        d2 = _shift_up(dy, dyh, 2, rows, tm)
        du = w2 * dy + w1 * d1 + w0 * d2
        dp_ref[:, 0:W] = (dvv * y).astype(BF16)
        dp_ref[:, W:2 * W] = (du * zz).astype(BF16)
        dp_ref[:, 2 * W:3 * W] = (du * cg).astype(BF16)
        dw_ref[0:1, :] += jnp.sum(dy * u2, axis=0, keepdims=True)
        dw_ref[1:2, :] += jnp.sum(dy * u1, axis=0, keepdims=True)
        dw_ref[2:3, :] += jnp.sum(dy * u, axis=0, keepdims=True)

    return pl.pallas_call(
        body, name=name, grid=(S // tm,),
        in_specs=[pl.BlockSpec((tm, W), lambda i: (i, 0)),
                  pl.BlockSpec((8, W), lambda i: (jnp.minimum((i + 1) * hb, last), 0)),
                  pl.BlockSpec((tm, W3), lambda i: (i, 0)),
                  pl.BlockSpec((8, W3), lambda i: (jnp.maximum(i * hb - 1, 0), 0)),
                  pl.BlockSpec((8, W3), lambda i: (jnp.minimum((i + 1) * hb, last), 0)),
                  pl.BlockSpec((8, W), lambda i: (0, 0))],
        out_specs=[pl.BlockSpec((tm, W3), lambda i: (i, 0)), pl.BlockSpec((8, W), lambda i: (0, 0))],
        out_shape=[jax.ShapeDtypeStruct((S, W3), BF16), jax.ShapeDtypeStruct((8, W), F32)],
        compiler_params=_cparams("arbitrary"),
    )(dv, dv, p, p, p, cw)


def _rope_swap(r, lane):
    mid = NOPE + ROPE // 2
    first = (lane >= NOPE) & (lane < mid)
    second = (lane >= mid) & (lane < QK_DIM)
    return jnp.where(first, pltpu.roll(r, HEAD_PAD - ROPE // 2, 1), jnp.where(second, pltpu.roll(r, ROPE // 2, 1), 0.0))


def _rope_fwd(q_big, kv_big, proj, kr_col, ct, st, name):
    S = q_big.shape[0]
    HW = HEADS * HEAD_PAD
    tm = _tile(S, TILES["ew"])

    def body(q_ref, k_ref, v_ref, kr_ref, ct_ref, st_ref, qo_ref, ko_ref, vo_ref):
        lane = lax.broadcasted_iota(jnp.int32, (1, HEAD_PAD), 1)
        ctv, stv = ct_ref[...], st_ref[...]
        krr = pltpu.roll(kr_ref[...].astype(F32), NOPE, 1)
        kro = krr * ctv + _rope_swap(krr, lane) * stv
        for h in range(HEADS):
            sl = slice(h * HEAD_PAD, (h + 1) * HEAD_PAD)
            qh = q_ref[:, sl].astype(F32)
            qo_ref[:, sl] = (qh * ctv + _rope_swap(qh, lane) * stv).astype(BF16)
            ko_ref[:, sl] = (k_ref[:, sl].astype(F32) + kro).astype(BF16)
            vo_ref[:, sl] = jnp.where(lane == VDIM, 1.0, v_ref[:, sl].astype(F32)).astype(BF16)

    wide = pl.BlockSpec((tm, HW), lambda i: (i, 0))
    narrow = pl.BlockSpec((tm, HEAD_PAD), lambda i: (i, 0))
    return pl.pallas_call(
        body, name=name, grid=(S // tm,),
        in_specs=[wide, wide, pl.BlockSpec((tm, HW), lambda i: (i, 1)),
                  pl.BlockSpec((tm, HEAD_PAD), lambda i: (i, kr_col)), narrow, narrow],
        out_specs=[wide, wide, wide],
        out_shape=[jax.ShapeDtypeStruct((S, HW), BF16)] * 3,
        compiler_params=_cparams("parallel"),
    )(q_big, kv_big, kv_big, proj, ct, st)


def _rope_bwd(dq, dk, dv, ct, st, name):
    S = dq.shape[0]
    HW = HEADS * HEAD_PAD
    tm = _tile(S, TILES["ew"])

    def body(dq_ref, dk_ref, dv_ref, ct_ref, st_ref, oq_ref, okv_ref, okr_ref):
        lane = lax.broadcasted_iota(jnp.int32, (1, HEAD_PAD), 1)
        ctv, stv = ct_ref[...], st_ref[...]
        acc = jnp.zeros((tm, HEAD_PAD), F32)
        for h in range(HEADS):
            sl = slice(h * HEAD_PAD, (h + 1) * HEAD_PAD)
            d = dq_ref[:, sl].astype(F32)
            oq_ref[:, sl] = (d * ctv + _rope_swap(d * stv, lane)).astype(BF16)
            d = dk_ref[:, sl].astype(F32)
            okv_ref[:, sl] = jnp.where(lane < NOPE, d, 0.0).astype(BF16)
            acc = acc + jnp.where(lane >= NOPE, d * ctv + _rope_swap(d * stv, lane), 0.0)
        okv_ref[:, HW:2 * HW] = dv_ref[...].astype(BF16)
        okr_ref[...] = pltpu.roll(acc, HEAD_PAD - NOPE, 1).astype(BF16)

    wide = pl.BlockSpec((tm, HW), lambda i: (i, 0))
    narrow = pl.BlockSpec((tm, HEAD_PAD), lambda i: (i, 0))
    return pl.pallas_call(
        body, name=name, grid=(S // tm,),
        in_specs=[wide, wide, wide, narrow, narrow],
        out_specs=[wide, pl.BlockSpec((tm, 2 * HW), lambda i: (i, 0)), narrow],
        out_shape=[jax.ShapeDtypeStruct((S, HW), BF16), jax.ShapeDtypeStruct((S, 2 * HW), BF16),
                   jax.ShapeDtypeStruct((S, HEAD_PAD), BF16)],
        compiler_params=_cparams("parallel"),
    )(dq, dk, dv, ct, st)


def _pairs(n, by_key):
    if by_key:
        pr = [(i, j) for j in range(n) for i in range(j, n)]
    else:
        pr = [(i, j) for i in range(n) for j in range(i + 1)]
    qi = np.array([p[0] for p in pr], np.int32)
    kj = np.array([p[1] for p in pr], np.int32)
    return jnp.asarray(qi), jnp.asarray(kj)


_LOG2E = 1.4426950408889634
_LN2 = 0.6931471805599453


def _tile_mask(t):
    return lax.broadcasted_iota(jnp.int32, (t, t), 1) <= lax.broadcasted_iota(jnp.int32, (t, t), 0)


def _attn_fwd(q, k, v, name):
    S = q.shape[0]
    HW = HEADS * HEAD_PAD
    t = _tile(S, TILES["attn"])
    n = S // t
    qi, kj = _pairs(n, by_key=False)
    c = (QK_DIM ** -0.5) * _LOG2E

    def body(qi_ref, kj_ref, q_ref, k_ref, v_ref, o_ref, lse_ref, m_s, acc_s):
        p_id = pl.program_id(1)
        i, j = qi_ref[p_id], kj_ref[p_id]

        @pl.when(j == 0)
        def _():
            m_s[...] = jnp.full_like(m_s, -jnp.inf)
            acc_s[...] = jnp.zeros_like(acc_s)

        def step(on_diagonal):
            s = _dot(q_ref[...], k_ref[...], NT)
            if on_diagonal:
                s = jnp.where(_tile_mask(t), s, -jnp.inf)
            m_old = m_s[...]
            m_new = jnp.maximum(m_old, jnp.max(s, axis=-1, keepdims=True))
            p = jnp.exp2((s - m_new) * c).astype(BF16)
            acc_s[...] = jnp.exp2((m_old - m_new) * c) * acc_s[...] + _dot(p, v_ref[...], NN)
            m_s[...] = m_new

        @pl.when(i == j)
        def _():
            step(True)

        @pl.when(i != j)
        def _():
            step(False)

        @pl.when(j == i)
        def _():
            acc = acc_s[...]
            l = acc[:, VDIM:VDIM + 1]
            o_ref[...] = (acc * (1.0 / l)).astype(BF16)
            lse_ref[...] = jnp.broadcast_to(m_s[...] * c + jnp.log2(l), (t, HEAD_PAD))

    qspec = pl.BlockSpec((t, HEAD_PAD), lambda h, p, qi, kj: (qi[p], h))
    kspec = pl.BlockSpec((t, HEAD_PAD), lambda h, p, qi, kj: (kj[p], h))
    grid_spec = pltpu.PrefetchScalarGridSpec(
        num_scalar_prefetch=2, grid=(HEADS, int(qi.shape[0])),
        in_specs=[qspec, kspec, kspec], out_specs=[qspec, qspec],
        scratch_shapes=[pltpu.VMEM((t, 1), F32), pltpu.VMEM((t, HEAD_PAD), F32)])
    return pl.pallas_call(
        body, name=name, grid_spec=grid_spec,
        out_shape=[jax.ShapeDtypeStruct((S, HW), BF16), jax.ShapeDtypeStruct((S, HW), F32)],
        compiler_params=_cparams("parallel", "arbitrary"),
    )(qi, kj, q, k, v)


def _attn_bwd(q, k, v, o, do, lse2, name):
    S = q.shape[0]
    HW = HEADS * HEAD_PAD
    t = _tile(S, TILES["attn"])
    n = S // t
    qi, kj = _pairs(n, by_key=True)
    scale = QK_DIM ** -0.5

    def body(qi_ref, kj_ref, q_ref, k_ref, v_ref, o_ref, do_ref, lse_ref, dq_ref, dk_ref, dv_ref, dk_s, dv_s):
        p_id = pl.program_id(1)
        i, j = qi_ref[p_id], kj_ref[p_id]

        @pl.when(p_id == 0)
        def _():
            dq_ref[...] = jnp.zeros_like(dq_ref)

        @pl.when(i == j)
        def _():
            dk_s[...] = jnp.zeros_like(dk_s)
            dv_s[...] = jnp.zeros_like(dv_s)

        qv, kv, vv = q_ref[...], k_ref[...], v_ref[...]
        dov = do_ref[...]
        s = _dot(qv, kv, NT) * scale
        rows_i = lax.broadcasted_iota(jnp.int32, (t, t), 0) + i * t
        cols_j = lax.broadcasted_iota(jnp.int32, (t, t), 1) + j * t
        p = jnp.where(cols_j <= rows_i, jnp.exp(s - lse_ref[:, 0:1] * _LN2), 0.0)
        delta = jnp.sum(dov.astype(F32) * o_ref[...].astype(F32), axis=-1, keepdims=True)
        dv_s[...] += _dot(p.astype(BF16), dov, TN)
        ds = (p * (_dot(dov, vv, NT) - delta) * scale).astype(BF16)
        dk_s[...] += _dot(ds, qv, TN)
        rows = pl.ds(pl.multiple_of(i * t, t), t)
        dq_ref[rows, :] += _dot(ds, kv, NN)

        @pl.when(i == n - 1)
        def _():
            dk_ref[...] = dk_s[...]
            dv_ref[...] = dv_s[...]

    qspec = pl.BlockSpec((t, HEAD_PAD), lambda h, p, qi, kj: (qi[p], h))
    kspec = pl.BlockSpec((t, HEAD_PAD), lambda h, p, qi, kj: (kj[p], h))
    grid_spec = pltpu.PrefetchScalarGridSpec(
        num_scalar_prefetch=2, grid=(HEADS, int(qi.shape[0])),
        in_specs=[qspec, kspec, kspec, qspec, qspec, qspec],
        out_specs=[pl.BlockSpec((S, HEAD_PAD), lambda h, p, qi, kj: (0, h)), kspec, kspec],
        scratch_shapes=[pltpu.VMEM((t, HEAD_PAD), F32), pltpu.VMEM((t, HEAD_PAD), F32)])
    return pl.pallas_call(
        body, name=name, grid_spec=grid_spec,
        out_shape=[jax.ShapeDtypeStruct((S, HW), F32)] * 3,
        compiler_params=_cparams("parallel", "arbitrary"),
    )(qi, kj, q, k, v, o, do, lse2)


_SQRT_HALF = 0.7071067811865476
_INV_SQRT_2PI = 0.3989422804014327


def _sg_select(r, grp):
    out = jnp.where(grp == 0, r[0:SG_CHUNK, :], 0.0)
    for g in range(1, SG_GROUPS):
        out = out + jnp.where(grp == g, r[g * SG_CHUNK:(g + 1) * SG_CHUNK, :], 0.0)
    return out


def _sgu_fwd(proj, gain, wstack, bmat, name):
    S = proj.shape[0]
    W = SG_WIDTH
    tm = _tile(S, TILES["sgu"])

    def body(z_ref, g_ref, w_ref, b_ref, o_ref):
        z = z_ref[...].astype(F32)
        zg = 0.5 * z * (1.0 + lax.erf(z * _SQRT_HALF))
        u, vv = zg[:, 0:W], zg[:, W:2 * W]
        r = lax.rsqrt(jnp.mean(vv * vv, axis=-1, keepdims=True) + NORM_EPS)
        vn = (vv * r * g_ref[...]).astype(BF16)
        grp = lax.broadcasted_iota(jnp.int32, (1, W), 1) // SG_GROUP_DIM
        for c in range(tm // SG_CHUNK):
            sl = slice(c * SG_CHUNK, (c + 1) * SG_CHUNK)
            mixed = _sg_select(_dot(w_ref[...], vn[sl, :], NN), grp) + b_ref[...]
            o_ref[sl, :] = (u[sl, :] * mixed).astype(BF16)

    return pl.pallas_call(
        body, name=name, grid=(S // tm,),
        in_specs=[pl.BlockSpec((tm, 2 * W), lambda i: (i, 0)), pl.BlockSpec((1, W), lambda i: (0, 0)),
                  pl.BlockSpec(wstack.shape, lambda i: (0, 0)), pl.BlockSpec(bmat.shape, lambda i: (0, 0))],
        out_specs=pl.BlockSpec((tm, W), lambda i: (i, 0)),
        out_shape=jax.ShapeDtypeStruct((S, W), BF16),
        compiler_params=_cparams("parallel"),
    )(proj, _row(gain), wstack, bmat)


def _sgu_bwd(dsg, proj, gain, wstack, wtstack, bmat, gsum, name):
    S = proj.shape[0]
    W = SG_WIDTH
    tm = _tile(S, TILES["sgu"])
    GS = SG_GROUPS * SG_CHUNK

    def body(d_ref, z_ref, g_ref, w_ref, wt_ref, b_ref, e_ref, dz_ref, dw_ref, db_ref, dg_ref, dw_s, db_s):
        i = pl.program_id(0)

        @pl.when(i == 0)
        def _():
            dw_s[...] = jnp.zeros_like(dw_s)
            db_s[...] = jnp.zeros_like(db_s)
            dg_ref[...] = jnp.zeros_like(dg_ref)

        z = z_ref[...].astype(F32)
        cdf = 0.5 * (1.0 + lax.erf(z * _SQRT_HALF))
        zg = z * cdf
        u, vv = zg[:, 0:W], zg[:, W:2 * W]
        r = lax.rsqrt(jnp.mean(vv * vv, axis=-1, keepdims=True) + NORM_EPS)
        vhat = vv * r
        vn = (vhat * g_ref[...]).astype(BF16)
        grp = lax.broadcasted_iota(jnp.int32, (1, W), 1) // SG_GROUP_DIM
        d = d_ref[...].astype(F32)
        du_parts, dvn_parts = [], []
        for c in range(tm // SG_CHUNK):
            sl = slice(c * SG_CHUNK, (c + 1) * SG_CHUNK)
            vc = vn[sl, :]
            mixed = _sg_select(_dot(w_ref[...], vc, NN), grp) + b_ref[...]
            dc = d[sl, :]
            du_parts.append(dc * mixed)
            dmix = dc * u[sl, :]
            db_s[...] += dmix
            dmb = dmix.astype(BF16)
            dvn_parts.append(_sg_select(_dot(wt_ref[...], dmb, NN), grp))
            astack = jnp.concatenate([jnp.where(grp == g, dmb, jnp.zeros_like(dmb)) for g in range(SG_GROUPS)], axis=0)
            dw_s[...] += _dot(astack, vc, NT)
        du = jnp.concatenate(du_parts, axis=0)
        dvn = jnp.concatenate(dvn_parts, axis=0)
        dg_ref[...] += jnp.sum(dvn * vhat, axis=0, keepdims=True)
        dvhat = dvn * g_ref[...]
        dvv = r * (dvhat - vhat * jnp.mean(dvhat * vhat, axis=-1, keepdims=True))
        dgelu = cdf + z * (_INV_SQRT_2PI * jnp.exp(-0.5 * z * z))
        dz_ref[:, 0:W] = (du * dgelu[:, 0:W]).astype(BF16)
        dz_ref[:, W:2 * W] = (dvv * dgelu[:, W:2 * W]).astype(BF16)

        @pl.when(i == pl.num_programs(0) - 1)
        def _():
            dw_ref[...] = dw_s[...]
            db_ref[...] = lax.dot_general(db_s[...], e_ref[...], NN, precision=lax.Precision.HIGHEST,
                                          preferred_element_type=F32)

    full = lambda a: pl.BlockSpec(a.shape, lambda i: (0, 0))
    return pl.pallas_call(
        body, name=name, grid=(S // tm,),
        in_specs=[pl.BlockSpec((tm, W), lambda i: (i, 0)), pl.BlockSpec((tm, 2 * W), lambda i: (i, 0)),
                  pl.BlockSpec((1, W), lambda i: (0, 0)), full(wstack), full(wtstack), full(bmat), full(gsum)],
        out_specs=[pl.BlockSpec((tm, 2 * W), lambda i: (i, 0)), pl.BlockSpec((GS, SG_CHUNK), lambda i: (0, 0)),
                   pl.BlockSpec((SG_CHUNK, LANE), lambda i: (0, 0)), pl.BlockSpec((1, W), lambda i: (0, 0))],
        out_shape=[jax.ShapeDtypeStruct((S, 2 * W), BF16), jax.ShapeDtypeStruct((GS, SG_CHUNK), F32),
                   jax.ShapeDtypeStruct((SG_CHUNK, LANE), F32), jax.ShapeDtypeStruct((1, W), F32)],
        scratch_shapes=[pltpu.VMEM((GS, SG_CHUNK), F32), pltpu.VMEM((SG_CHUNK, W), F32)],
        compiler_params=_cparams("arbitrary"),
    )(dsg, proj, _row(gain), wstack, wtstack, bmat, gsum)


WEIGHTS = ['ffn_pre_norm', 'ffn_pre_w_gate', 'ffn_pre_w_up', 'ffn_pre_w_down', 'mix_norm', 'ffn_post_norm',
           'ffn_post_w_gate', 'ffn_post_w_up', 'ffn_post_w_down', 'even_w_in', 'q_norm', 'w_uq', 'kv_norm', 'w_ukv',
           'sg_norm', 'sg_w', 'sg_b', 'even_w_out', 'conv_w_in', 'conv_w', 'conv_w_out', 'final_norm']
SHARD_AXIS = dict(ffn_pre_w_gate=2, ffn_pre_w_up=2, ffn_pre_w_down=1, ffn_post_w_gate=2, ffn_post_w_up=2,
                  ffn_post_w_down=1, even_w_in=2, w_uq=2, w_ukv=2, even_w_out=1, conv_w_in=2, conv_w=2, conv_w_out=1)
SHARDED = [n for n in WEIGHTS if n in SHARD_AXIS]
REPLICATED = [n for n in WEIGHTS if n not in SHARD_AXIS]


def _to_t(name, w):
    return jnp.swapaxes(w, 1, 2) if SHARD_AXIS[name] == 2 else w


def _rows_of(n):
    return -(-n // PACK_W)


def _pad_rows(a, mult, axis):
    r = a.shape[axis]
    extra = (-r) % mult
    if extra == 0:
        return a
    pad = [(0, 0)] * a.ndim
    pad[axis] = (0, extra)
    return jnp.pad(a, pad)


def _flat_rows(a, lead):
    flat = a.reshape(a.shape[:lead] + (-1,))
    n = flat.shape[-1]
    flat = _pad_rows(flat, PACK_W, lead)
    return flat.reshape(a.shape[:lead] + (_rows_of(n), PACK_W))


def _pack(pieces, lead, mult, piece_mult=1):
    rows, offs, off = [], [], 0
    for p in pieces:
        r = _pad_rows(_flat_rows(p, lead), piece_mult, lead)
        rows.append(r)
        offs.append(off)
        off += r.shape[lead]
    return _pad_rows(jnp.concatenate(rows, axis=lead), mult, lead), offs


def _unpack(buf, off, shape, lead):
    n = math.prod(shape)
    r = _rows_of(n)
    piece = lax.slice_in_dim(buf, off, off + r, axis=lead)
    piece = piece.reshape(buf.shape[:lead] + (r * PACK_W,))
    piece = lax.slice_in_dim(piece, 0, n, axis=lead)
    return piece.reshape(buf.shape[:lead] + tuple(shape))


def _head_pad(w, per_head, keep):
    k = w.shape[-1]
    w = w.reshape(HEADS, per_head, k)[:, keep[0]:keep[1]]
    w = jnp.pad(w, ((0, 0), (0, HEAD_PAD - (keep[1] - keep[0])), (0, 0)))
    return w.reshape(HEADS * HEAD_PAD, k)


def _head_unpad(w, n):
    return w.reshape(HEADS, HEAD_PAD, w.shape[-1])[:, :n]


def kernel(x, positions, ffn_pre_norm, ffn_pre_w_gate, ffn_pre_w_up, ffn_pre_w_down, mix_norm, ffn_post_norm, ffn_post_w_gate, ffn_post_w_up, ffn_post_w_down, even_w_in, q_norm, w_uq, kv_norm, w_ukv, sg_norm, sg_w, sg_b, even_w_out, conv_w_in, conv_w, conv_w_out, final_norm, loss_target, m_ffn_pre_norm, m_ffn_pre_w_gate, m_ffn_pre_w_up, m_ffn_pre_w_down, m_mix_norm, m_ffn_post_norm, m_ffn_post_w_gate, m_ffn_post_w_up, m_ffn_post_w_down, m_even_w_in, m_q_norm, m_w_uq, m_kv_norm, m_w_ukv, m_sg_norm, m_sg_w, m_sg_b, m_even_w_out, m_conv_w_in, m_conv_w, m_conv_w_out, m_final_norm, v_ffn_pre_norm, v_ffn_pre_w_gate, v_ffn_pre_w_up, v_ffn_pre_w_down, v_mix_norm, v_ffn_post_norm, v_ffn_post_w_gate, v_ffn_post_w_up, v_ffn_post_w_down, v_even_w_in, v_q_norm, v_w_uq, v_kv_norm, v_w_ukv, v_sg_norm, v_sg_w, v_sg_b, v_even_w_out, v_conv_w_in, v_conv_w, v_conv_w_out, v_final_norm):
    given = dict(locals())
    w_loc = {n: given[n] for n in WEIGHTS}
    m_loc = {n: given["m_" + n] for n in WEIGHTS}
    v_loc = {n: given["v_" + n] for n in WEIGHTS}

    S, D = x.shape[1], x.shape[2]
    depth = ffn_pre_norm.shape[0]
    QL, KVL = q_norm.shape[1], kv_norm.shape[1]
    ZW = 2 * SG_WIDTH
    assert x.shape[0] == 1 and ZW % KVL == 0 and (ZW + KVL) % HEAD_PAD == 0 and (ZW + KVL + 2 * HEAD_PAD) % QL == 0
    col_ckv = ZW // KVL
    col_kr = (ZW + KVL) // HEAD_PAD
    col_cq = (ZW + KVL + 2 * HEAD_PAD) // QL

    t_loc = {n: _to_t(n, w_loc[n]) for n in SHARDED}
    full = {n: {} for n in SHARDED}

    def ffn_keys(kind, l):
        return [("ffn_%s_w_%s" % (kind, part), l) for part in ("gate", "up", "down")]

    def mixer_keys(l):
        names = ("even_w_in", "w_uq", "w_ukv", "even_w_out") if l % 2 == 0 else ("conv_w_in", "conv_w", "conv_w_out")
        return [(n, l // 2) for n in names]

    def local_pack(keys):
        return _pack([t_loc[n][l].astype(BF16) for n, l in keys], 0, 16, piece_mult=16)

    def take_gathered(gathered, keys, offs):
        for (n, l), off in zip(keys, offs):
            piece = _unpack(gathered, off, t_loc[n].shape[1:], 1)
            full[n][l] = piece.reshape(N_DEV * piece.shape[1], piece.shape[2])

    def gather_rider(keys):
        pack, offs = local_pack(keys)
        return _Exchange("gather", [pack]), offs

    first_keys = ffn_keys("pre", 0) + mixer_keys(0)
    pack0, offs0 = local_pack(first_keys)
    take_gathered(_all_gather(pack0, "gather_weights"), first_keys, offs0)

    tril = jnp.tril(jnp.ones((SG_CHUNK, SG_CHUNK), F32))
    even_ops = {}

    def even_operands(e):
        if e not in even_ops:
            wi = full["even_w_in"][e]
            zrow = lambda k: jnp.zeros((k, D), BF16)
            ops = dict(win_pad=jnp.concatenate(
                [wi[QL + KVL + ROPE:], wi[QL:QL + KVL], wi[QL + KVL:QL + KVL + ROPE], zrow(HEAD_PAD - ROPE),
                 zrow(HEAD_PAD), wi[:QL]], axis=0))
            ops["wq_big"] = _head_pad(full["w_uq"][e], QK_DIM, (0, QK_DIM))
            wkv = full["w_ukv"][e]
            ops["wkv_big"] = jnp.concatenate([_head_pad(wkv, NOPE + VDIM, (0, NOPE)),
                                              _head_pad(wkv, NOPE + VDIM, (NOPE, NOPE + VDIM))], axis=0)
            wo = full["even_w_out"][e]
            ops["wo_attn"] = _head_pad(wo[:HEADS * VDIM], VDIM, (0, VDIM))
            ops["wo_sg"] = wo[HEADS * VDIM:]
            wt = sg_w[e] * tril
            ops["wstack"] = wt.reshape(SG_GROUPS * SG_CHUNK, SG_CHUNK).astype(BF16)
            ops["wtstack"] = jnp.swapaxes(wt, 1, 2).reshape(SG_GROUPS * SG_CHUNK, SG_CHUNK).astype(BF16)
            ops["bmat"] = jnp.repeat(sg_b[e].T, SG_GROUP_DIM, axis=1)
            even_ops[e] = ops
        return even_ops[e]

    gsum = (jnp.arange(SG_WIDTH)[:, None] // SG_GROUP_DIM == jnp.arange(LANE)[None, :]).astype(F32)

    def conv_taps(o):
        return jnp.pad(jnp.swapaxes(full["conv_w"][o], 0, 1).astype(F32), ((0, 8 - CONV_K), (0, 0)))

    inv_freq = ROPE_THETA ** (-jnp.arange(0, ROPE, 2, dtype=F32) / ROPE)
    ang = positions[0].astype(F32)[:, None] * inv_freq
    cos, sin = jnp.cos(ang), jnp.sin(ang)
    ones, zeros = jnp.ones((S, NOPE), F32), jnp.zeros((S, HEAD_PAD - QK_DIM), F32)
    ct = jnp.concatenate([ones, cos, cos, zeros], axis=1)
    st = jnp.concatenate([0.0 * ones, -sin, sin, zeros], axis=1)

    xs = x[0]
    saved = []
    def ffn_forward(xin, kind, l, next_keys):
        gain = (ffn_pre_norm if kind == "pre" else ffn_post_norm)[l]
        wg, wu, wd = (full[n][l] for n, _ in ffn_keys(kind, l))
        if not next_keys:
            return _ffn_fwd(xin, gain, wg, wu, wd, "ffn_fwd")
        rider, offs = gather_rider(next_keys)
        xo, a, b, gathered = _ffn_fwd(xin, gain, wg, wu, wd, "ffn_fwd_gather", rider=rider)
        take_gathered(gathered, next_keys, offs)
        return xo, a, b

    for l in range(depth):
        sv = dict(x0=xs)
        x1, sv["a1"], sv["b1"] = ffn_forward(xs, "pre", l, ffn_keys("post", l))
        h = _rms_fwd(x1, mix_norm[l], "mix_norm_fwd")
        sv.update(x1=x1, h=h)
        if l % 2 == 0:
            e = l // 2
            ops = even_operands(e)
            proj = _mm(h, ops["win_pad"], "nt", "even_in_proj", out_dtype=F32)
            qn = _rms_fwd(proj, q_norm[e], "q_norm_fwd", col=col_cq, width=QL)
            kvn = _rms_fwd(proj, kv_norm[e], "kv_norm_fwd", col=col_ckv, width=KVL)
            q_big = _mm(qn, ops["wq_big"], "nt", "q_up_proj", out_dtype=F32)
            kv_big = _mm(kvn, ops["wkv_big"], "nt", "kv_up_proj", out_dtype=BF16)
            q_r, k_r, v_r = _rope_fwd(q_big, kv_big, proj, col_kr, ct, st, "rope_fwd")
            o_att, lse = _attn_fwd(q_r, k_r, v_r, "attn_fwd")
            sg = _sgu_fwd(proj, sg_norm[e], ops["wstack"], ops["bmat"], "sgu_fwd")
            tmp = _mm(o_att, ops["wo_attn"], "nn", "even_out_attn", out_dtype=F32, res=x1)
            x2 = _mm(sg, ops["wo_sg"], "nn", "even_out_sg", out_dtype=F32, res=tmp)
            sv.update(proj=proj, qn=qn, kvn=kvn, q=q_r, k=k_r, v=v_r, o=o_att, lse=lse, sg=sg)
        else:
            o = l // 2
            p = _mm(h, full["conv_w_in"][o], "nt", "conv_in_proj", out_dtype=BF16)
            cv = _conv_fwd(p, conv_taps(o), "conv_fwd")
            x2 = _mm(cv, full["conv_w_out"][o], "nn", "conv_out_proj", out_dtype=F32, res=x1)
            sv.update(p=p, cv=cv)
        sv["x2"] = x2
        next_keys = ffn_keys("pre", l + 1) + mixer_keys(l + 1) if l + 1 < depth else []
        xs, sv["a2"], sv["b2"] = ffn_forward(x2, "post", l, next_keys)
        saved.append(sv)

    gr = {n: [None] * w_loc[n].shape[0] for n in REPLICATED if n != "final_norm"}
    per_layer = {n: [None] * w_loc[n].shape[0] for n in SHARDED}
    pending = []

    def scatter_rider():
        pieces, where, off = [], [], 0
        for n, l, g in pending:
            piece = _pad_rows(_flat_rows(g.astype(BF16).reshape(N_DEV, -1), 1), 16, 1)
            pieces.append(piece)
            where.append((n, l, off))
            off += piece.shape[1]
        if off % GRAD_ROWS_MULT:
            pieces.append(jnp.zeros((N_DEV, (-off) % GRAD_ROWS_MULT, PACK_W), BF16))
        pending.clear()
        return _Exchange("scatter", pieces), where

    def take_scattered(received, where):
        owned = _sum_slots(received, "sum_grad_shards")
        for n, l, off in where:
            per_layer[n][l] = _unpack(owned, off, t_loc[n].shape[1:], 0)

    def ffn_backward(dxin, xin, kind, l, a, b):
        gain = (ffn_pre_norm if kind == "pre" else ffn_post_norm)[l]
        keys = ffn_keys(kind, l)
        wg, wu, wd = (full[n][l] for n, _ in keys)
        if pending:
            rider, where = scatter_rider()
            dxo, dz, hh, dy, dgain, received = _ffn_bwd(dxin, xin, gain, a, b, wg, wu, wd, "ffn_bwd_scatter", rider=rider)
            take_scattered(received, where)
        else:
            dxo, dz, hh, dy, dgain = _ffn_bwd(dxin, xin, gain, a, b, wg, wu, wd, "ffn_bwd")
        gr["ffn_%s_norm" % kind][l] = dgain[0]
        for (n, _), g in zip(keys, _ffn_dw(a, b, dz, hh, dy, "ffn_dw")):
            pending.append((n, l, g))
        return dxo

    dx, g_final, loss_part = _loss_head(xs, loss_target[0], final_norm, "loss_head")
    for l in reversed(range(depth)):
        sv = saved[l]
        dx = ffn_backward(dx, sv["x2"], "post", l, sv["a2"], sv["b2"])
        h = sv["h"]
        if l % 2 == 0:
            e = l // 2
            ops = even_operands(e)
            d_o = _mm(dx, ops["wo_attn"], "nt", "even_out_attn_bwd", out_dtype=BF16)
            d_sg = _mm(dx, ops["wo_sg"], "nt", "even_out_sg_bwd", out_dtype=BF16)
            g_wo_attn = _mm(sv["o"], dx, "tn", "even_out_attn_dw", out_dtype=F32)
            g_wo_sg = _mm(sv["sg"], dx, "tn", "even_out_sg_dw", out_dtype=F32)
            dq, dk, dv = _attn_bwd(sv["q"], sv["k"], sv["v"], sv["o"], d_o, sv["lse"], "attn_bwd")
            dq_big, dkv_big, dkr = _rope_bwd(dq, dk, dv, ct, st, "rope_bwd")
            dz_sg, g_wstack, g_bias, g_sgn = _sgu_bwd(d_sg, sv["proj"], sg_norm[e], ops["wstack"], ops["wtstack"],
                                                      ops["bmat"], gsum, "sgu_bwd")
            dqn = _mm(dq_big, ops["wq_big"], "nn", "q_up_proj_bwd", out_dtype=F32)
            g_wq_big = _mm(dq_big, sv["qn"], "tn", "q_up_proj_dw", out_dtype=F32)
            dkvn = _mm(dkv_big, ops["wkv_big"], "nn", "kv_up_proj_bwd", out_dtype=F32)
            g_wkv_big = _mm(dkv_big, sv["kvn"], "tn", "kv_up_proj_dw", out_dtype=F32)
            dcq, g_qn = _rms_bwd(dqn, sv["proj"], q_norm[e], "q_norm_bwd", col=col_cq, out_dtype=BF16)
            dckv, g_kvn = _rms_bwd(dkvn, sv["proj"], kv_norm[e], "kv_norm_bwd", col=col_ckv, out_dtype=BF16)
            dproj = jnp.concatenate([dz_sg, dckv, dkr, jnp.zeros((S, HEAD_PAD), BF16), dcq], axis=1)
            dh = _mm(dproj, ops["win_pad"], "nn", "even_in_proj_bwd", out_dtype=F32)
            g_win = _mm(dproj, h, "tn", "even_in_proj_dw", out_dtype=F32)
            o_cq, o_ckv, o_kr = col_cq * QL, col_ckv * KVL, col_kr * HEAD_PAD
            hw = HEADS * HEAD_PAD
            pending.append(("even_w_in", e, jnp.concatenate(
                [g_win[o_cq:o_cq + QL], g_win[o_ckv:o_ckv + KVL], g_win[o_kr:o_kr + ROPE], g_win[:ZW]], axis=0)))
            pending.append(("w_uq", e, _head_unpad(g_wq_big, QK_DIM).reshape(HEADS * QK_DIM, QL)))
            pending.append(("w_ukv", e, jnp.concatenate(
                [_head_unpad(g_wkv_big[:hw], NOPE), _head_unpad(g_wkv_big[hw:], VDIM)],
                axis=1).reshape(HEADS * (NOPE + VDIM), KVL)))
            pending.append(("even_w_out", e, jnp.concatenate(
                [_head_unpad(g_wo_attn, VDIM).reshape(HEADS * VDIM, D), g_wo_sg], axis=0)))
            gr["q_norm"][e], gr["kv_norm"][e], gr["sg_norm"][e] = g_qn[0], g_kvn[0], g_sgn[0]
            gr["sg_w"][e] = g_wstack.reshape(SG_GROUPS, SG_CHUNK, SG_CHUNK) * tril
            gr["sg_b"][e] = g_bias[:, :SG_GROUPS].T
        else:
            o = l // 2
            dcv = _mm(dx, full["conv_w_out"][o], "nt", "conv_out_proj_bwd", out_dtype=BF16)
            pending.append(("conv_w_out", o, _mm(sv["cv"], dx, "tn", "conv_out_proj_dw", out_dtype=BF16)))
            dp, dcw = _conv_bwd(dcv, sv["p"], conv_taps(o), "conv_bwd")
            dh = _mm(dp, full["conv_w_in"][o], "nn", "conv_in_proj_bwd", out_dtype=F32)
            pending.append(("conv_w_in", o, _mm(dp, h, "tn", "conv_in_proj_dw", out_dtype=BF16)))
            pending.append(("conv_w", o, jnp.swapaxes(dcw[:CONV_K], 0, 1)))
        dx, dgain = _rms_bwd(dh, sv["x1"], mix_norm[l], "mix_norm_bwd", res=dx)
        gr["mix_norm"][l] = dgain[0]
        dx = ffn_backward(dx, sv["x0"], "pre", l, sv["a1"], sv["b1"])
    grad_x = dx[None]

    rider, where = scatter_rider()
    take_scattered(_exchange("scatter", rider.arrays, "scatter_grads"), where)
    grads = {n: _to_t(n, jnp.stack(per_layer[n])) for n in SHARDED}

    small = [jnp.stack(gr[n]) for n in REPLICATED if n != "final_norm"] + [g_final[0], loss_part[0, :1]]
    spack, soffs = _pack(small, 0, SMALL_ROWS_MULT)
    sgath = _all_gather(spack, "gather_small_grads")
    ssum = _sum_slots(sgath, "sum_small_grads")
    names_small = [n for n in REPLICATED if n != "final_norm"] + ["final_norm", "loss"]
    for n, off, piece in zip(names_small, soffs, small):
        val = _unpack(ssum, off, piece.shape, 0)
        if n == "loss":
            loss = val[0]
        else:
            grads[n] = val

    delta, new_m, new_v = {}, {}, {}
    for n in SHARDED:
        two_d = lambda a: a.reshape(-1, a.shape[-1])
        d, nm, nv = _adamw(two_d(w_loc[n]), two_d(grads[n]), two_d(m_loc[n]), two_d(v_loc[n]), "adamw")
        delta[n], new_m[n], new_v[n] = (a.reshape(w_loc[n].shape) for a in (d, nm, nv))
    flat = lambda d: _pack([d[n] for n in REPLICATED], 0, SMALL_ROWS_MULT)
    (wf, aoffs), (gf, _), (mf, _), (vf, _) = flat(w_loc), flat(grads), flat(m_loc), flat(v_loc)
    for res, buf in zip((delta, new_m, new_v), _adamw(wf, gf, mf, vf, "adamw_replicated")):
        for n, off in zip(REPLICATED, aoffs):
            res[n] = _unpack(buf, off, w_loc[n].shape, 0)
    outs = [loss, grad_x] + [grads[n] for n in WEIGHTS]
    for res in (delta, new_m, new_v):
        outs += [res[n] for n in WEIGHTS]
    return tuple(outs)
```

```python
import functools
import math

import numpy as np
import jax
import jax.numpy as jnp
from jax import lax
from jax.experimental import pallas as pl
from jax.experimental.pallas import tpu as pltpu

F32 = jnp.float32
BF16 = jnp.bfloat16

N_DEV = 8
NORM_EPS = 1e-6
HEADS = 8
NOPE = 64
ROPE = 32
VDIM = 64
HEAD_PAD = 128
QK_DIM = NOPE + ROPE
ROPE_THETA = 10000.0
SG_GROUPS = 8
SG_GROUP_DIM = 64
SG_WIDTH = SG_GROUPS * SG_GROUP_DIM
SG_CHUNK = 128
CONV_K = 3
ADAM_LR, ADAM_B1, ADAM_B2, ADAM_EPS, ADAM_WD, ADAM_STEP = 0.001, 0.9, 0.999, 1e-08, 0.01, 10

LANE = 128
PACK_W = 1024
GRAD_ROWS_MULT = 512
SMALL_ROWS_MULT = 64
VMEM_LIMIT = 60 * 1024 * 1024

TILES = dict(ffn_fwd=512, ffn_bwd=256, ffn_dw=2048, mm=512, mm_tn=1024, ew=512, attn=1024, attn_wide=2048, sgu=512,
             adam=512)

NT = (((1,), (1,)), ((), ()))
NN = (((1,), (0,)), ((), ()))
TN = (((0,), (0,)), ((), ()))


def _dot(a, b, dims):
    return lax.dot_general(a, b, dims, preferred_element_type=F32)


def _cparams(*sem):
    return pltpu.CompilerParams(dimension_semantics=sem if sem else None, vmem_limit_bytes=VMEM_LIMIT)


def _tile(n, want):
    t = min(want, n)
    while n % t:
        t //= 2
    return t if t % 8 == 0 else n


def _lane_tile(n, cap):
    best = None
    for k in range(1, n // LANE + 1):
        t = k * LANE
        if n % t == 0 and t <= cap:
            best = t
    return best or n


def _row(v):
    return v.reshape(1, -1).astype(F32)


def _all_gather(block, name):
    R, W = block.shape

    def body(x_ref, out_ref, send_sems, recv_sems, local_sem):
        x, y, c = lax.axis_index("x"), lax.axis_index("y"), lax.axis_index("c")
        me, sibling = (x, y, c), (x, y, 1 - c)
        chips = [(1 - x, y), (x, 1 - y), (1 - x, 1 - y)]

        def slot(px, py, pc):
            return out_ref.at[4 * px + 2 * py + pc]

        def copy(k, blk, to, src=None):
            return pltpu.make_async_remote_copy(
                src_ref=slot(*blk) if src is None else src, dst_ref=slot(*blk),
                send_sem=send_sems.at[k], recv_sem=recv_sems.at[k],
                device_id=to, device_id_type=pl.DeviceIdType.MESH)

        mine = pltpu.make_async_copy(x_ref, slot(*me), local_sem)
        mine.start()
        first = [copy(0, me, sibling, src=x_ref)]
        first += [copy(1 + j, me, (*chip, c), src=x_ref) for j, chip in enumerate(chips)]
        for cp in first:
            cp.start()
        passed = [copy(4 + j, (*chip, c), sibling) for j, chip in enumerate(chips)]
        for j, chip in enumerate(chips):
            copy(1 + j, (*chip, c), me).wait_recv()
            passed[j].start()
        copy(0, sibling, me).wait_recv()
        for j, chip in enumerate(chips):
            copy(4 + j, (*chip, 1 - c), me).wait_recv()
        for cp in first + passed:
            cp.wait_send()
        mine.wait()

    return pl.pallas_call(
        body, name=name,
        out_shape=jax.ShapeDtypeStruct((N_DEV, R, W), block.dtype),
        in_specs=[pl.BlockSpec(memory_space=pl.ANY)],
        out_specs=pl.BlockSpec(memory_space=pl.ANY),
        scratch_shapes=[pltpu.SemaphoreType.DMA((7,)), pltpu.SemaphoreType.DMA((7,)), pltpu.SemaphoreType.DMA],
    )(block)


class _Exchange:
    def __init__(self, kind, arrays):
        self.kind, self.arrays = kind, list(arrays)
        if kind == "gather":
            (r, w), = [a.shape for a in self.arrays]
            self.rows = [r]
        else:
            self.rows = [a.shape[1] for a in self.arrays]
            w = self.arrays[0].shape[2]
        self.offs = [sum(self.rows[:i]) for i in range(len(self.rows))]
        self.n_in = len(self.arrays)
        self.out_shape = jax.ShapeDtypeStruct((N_DEV, sum(self.rows), w), self.arrays[0].dtype)
        self.in_specs = [pl.BlockSpec(memory_space=pl.ANY)] * self.n_in
        self.out_spec = pl.BlockSpec(memory_space=pl.ANY)
        self.out_specs, self.out_shapes = [self.out_spec], [self.out_shape]
        self.scratch = [pltpu.SemaphoreType.DMA((7,)), pltpu.SemaphoreType.DMA((7,)), pltpu.SemaphoreType.DMA]

    def _peers(self):
        x, y, c = lax.axis_index("x"), lax.axis_index("y"), lax.axis_index("c")
        me = 4 * x + 2 * y + c
        return me, [(k, (x ^ (k >> 2), y ^ ((k >> 1) & 1), c ^ (k & 1))) for k in range(1, N_DEV)]

    @staticmethod
    def _remote(src, dst, k, to, send_sems, recv_sems):
        return pltpu.make_async_remote_copy(
            src_ref=src, dst_ref=dst, send_sem=send_sems.at[k - 1], recv_sem=recv_sems.at[k - 1],
            device_id=to, device_id_type=pl.DeviceIdType.MESH)

    def start(self, s_refs, r_ref, send_sems, recv_sems, local_sem):
        me, peers = self._peers()
        for s_ref, off, r in zip(s_refs, self.offs, self.rows):
            src = s_ref if self.kind == "gather" else s_ref.at[me]
            pltpu.make_async_copy(src, r_ref.at[me, pl.ds(off, r)], local_sem).start()
        for k, to in peers:
            peer = 4 * to[0] + 2 * to[1] + to[2]
            for s_ref, off, r in zip(s_refs, self.offs, self.rows):
                src = s_ref if self.kind == "gather" else s_ref.at[peer]
                self._remote(src, r_ref.at[me, pl.ds(off, r)], k, to, send_sems, recv_sems).start()

    def wait(self, s_refs, r_ref, send_sems, recv_sems, local_sem):
        me, peers = self._peers()
        whole = r_ref.at[me]
        totals = [self._remote(whole, whole, k, to, send_sems, recv_sems) for k, to in peers]
        for cp in totals:
            cp.wait_recv()
        for cp in totals:
            cp.wait_send()
        pltpu.make_async_copy(whole, whole, local_sem).wait()


class _NoRider:
    arrays, in_specs, out_specs, out_shapes, scratch = [], [], [], [], []


_NO_RIDER = _NoRider()


def _ride(rider, refs, n_in, n_out, first, last):
    if rider is None:
        return refs, lambda: None
    k = rider.n_in
    s_refs = refs[n_in:n_in + k]
    r_ref = refs[n_in + k + n_out]
    sems = refs[-3:]
    own = refs[:n_in] + refs[n_in + k:n_in + k + n_out] + refs[n_in + k + n_out + 1:-3]

    @pl.when(first)
    def _():
        rider.start(s_refs, r_ref, *sems)

    def finish():
        @pl.when(last)
        def _():
            rider.wait(s_refs, r_ref, *sems)

    return own, finish


def _exchange(kind, arrays, name):
    ex = _Exchange(kind, arrays)

    def body(*refs):
        s_refs, r_ref, sems = refs[:ex.n_in], refs[ex.n_in], refs[ex.n_in + 1:]
        ex.start(s_refs, r_ref, *sems)
        ex.wait(s_refs, r_ref, *sems)

    return pl.pallas_call(
        body, name=name, out_shape=ex.out_shape, in_specs=ex.in_specs, out_specs=ex.out_spec,
        scratch_shapes=ex.scratch,
    )(*ex.arrays)


def _sum_slots(parts, name):
    _, R, W = parts.shape
    tr = _tile(R, TILES["adam"])

    def body(p_ref, o_ref):
        acc = p_ref[0].astype(F32)
        for s in range(1, N_DEV):
            acc = acc + p_ref[s].astype(F32)
        o_ref[...] = acc

    return pl.pallas_call(
        body, name=name, grid=(R // tr,),
        in_specs=[pl.BlockSpec((N_DEV, tr, W), lambda i: (0, i, 0))],
        out_specs=pl.BlockSpec((tr, W), lambda i: (i, 0)),
        out_shape=jax.ShapeDtypeStruct((R, W), F32),
        compiler_params=_cparams("parallel"),
    )(parts)


def _adamw(w, g, m, v, name):
    R, W = w.shape
    tr = _tile(R, TILES["adam"])
    c1 = 1.0 - ADAM_B1 ** ADAM_STEP
    c2 = 1.0 - ADAM_B2 ** ADAM_STEP

    def body(w_ref, g_ref, m_ref, v_ref, d_ref, nm_ref, nv_ref):
        g = g_ref[...]
        nm = ADAM_B1 * m_ref[...] + (1.0 - ADAM_B1) * g
        nv = ADAM_B2 * v_ref[...] + (1.0 - ADAM_B2) * (g * g)
        d_ref[...] = -ADAM_LR * ((nm / c1) / (jnp.sqrt(nv / c2) + ADAM_EPS) + ADAM_WD * w_ref[...])
        nm_ref[...] = nm
        nv_ref[...] = nv

    spec = pl.BlockSpec((tr, W), lambda i: (i, 0))
    return pl.pallas_call(
        body, name=name, grid=(R // tr,),
        in_specs=[spec] * 4, out_specs=[spec] * 3,
        out_shape=[jax.ShapeDtypeStruct((R, W), F32)] * 3,
        compiler_params=_cparams("parallel"),
    )(w, g, m, v)


def _mm(a, b, mode, name, out_dtype=BF16, res=None, scale=1.0, acol=None, kdim=None):
    if mode == "tn":
        S, M = a.shape
        N = b.shape[1]
        ts = _tile(S, TILES["mm_tn"])
        tmo = _lane_tile(M, 1024)

        def body(a_ref, b_ref, o_ref, acc):
            s = pl.program_id(1)

            @pl.when(s == 0)
            def _():
                acc[...] = jnp.zeros_like(acc)

            acc[...] += _dot(a_ref[...].astype(BF16), b_ref[...].astype(BF16), TN)

            @pl.when(s == pl.num_programs(1) - 1)
            def _():
                o_ref[...] = acc[...].astype(out_dtype)

        return pl.pallas_call(
            functools.partial(body), name=name, grid=(M // tmo, S // ts),
            in_specs=[pl.BlockSpec((ts, tmo), lambda i, s: (s, i)), pl.BlockSpec((ts, N), lambda i, s: (s, 0))],
            out_specs=pl.BlockSpec((tmo, N), lambda i, s: (i, 0)),
            out_shape=jax.ShapeDtypeStruct((M, N), out_dtype),
            scratch_shapes=[pltpu.VMEM((tmo, N), F32)],
            compiler_params=_cparams("parallel", "arbitrary"),
        )(a, b)

    M = a.shape[0]
    K = kdim if kdim is not None else a.shape[1]
    ac = 0 if acol is None else acol
    N = b.shape[1] if mode == "nn" else b.shape[0]
    tm = _tile(M, TILES["mm"])
    dims = NN if mode == "nn" else NT

    def body(*refs):
        if res is None:
            a_ref, b_ref, o_ref = refs
        else:
            a_ref, b_ref, r_ref, o_ref = refs
        acc = _dot(a_ref[...].astype(BF16), b_ref[...].astype(BF16), dims)
        if res is not None:
            acc = r_ref[...] + scale * acc
        o_ref[...] = acc.astype(out_dtype)

    in_specs = [pl.BlockSpec((tm, K), lambda i: (i, ac)), pl.BlockSpec(b.shape, lambda i: (0, 0))]
    args = [a, b]
    if res is not None:
        in_specs.append(pl.BlockSpec((tm, N), lambda i: (i, 0)))
        args.append(res)
    return pl.pallas_call(
        body, name=name, grid=(M // tm,),
        in_specs=in_specs, out_specs=pl.BlockSpec((tm, N), lambda i: (i, 0)),
        out_shape=jax.ShapeDtypeStruct((M, N), out_dtype),
        compiler_params=_cparams("parallel"),
    )(*args)


def _rms_fwd(x, gain, name, col=0, width=None):
    S = x.shape[0]
    W = width if width is not None else x.shape[1]
    tm = _tile(S, TILES["ew"])

    def body(x_ref, g_ref, o_ref):
        xv = x_ref[...].astype(F32)
        r = lax.rsqrt(jnp.mean(xv * xv, axis=-1, keepdims=True) + NORM_EPS)
        o_ref[...] = (xv * r * g_ref[...]).astype(BF16)

    return pl.pallas_call(
        body, name=name, grid=(S // tm,),
        in_specs=[pl.BlockSpec((tm, W), lambda i: (i, col)), pl.BlockSpec((1, W), lambda i: (0, 0))],
        out_specs=pl.BlockSpec((tm, W), lambda i: (i, 0)),
        out_shape=jax.ShapeDtypeStruct((S, W), BF16),
        compiler_params=_cparams("parallel"),
    )(x, _row(gain))


def _rms_bwd(dy, x, gain, name, col=0, res=None, out_dtype=F32):
    S, W = dy.shape
    tm = _tile(S, TILES["ew"])

    def body(*refs):
        if res is None:
            dy_ref, x_ref, g_ref, dx_ref, dg_ref = refs
        else:
            dy_ref, x_ref, g_ref, r_ref, dx_ref, dg_ref = refs

        @pl.when(pl.program_id(0) == 0)
        def _():
            dg_ref[...] = jnp.zeros_like(dg_ref)

        xv = x_ref[...].astype(F32)
        d = dy_ref[...].astype(F32)
        r = lax.rsqrt(jnp.mean(xv * xv, axis=-1, keepdims=True) + NORM_EPS)
        xhat = xv * r
        dg_ref[...] += jnp.sum(d * xhat, axis=0, keepdims=True)
        dxhat = d * g_ref[...]
        dx = r * (dxhat - xhat * jnp.mean(dxhat * xhat, axis=-1, keepdims=True))
        if res is not None:
            dx = dx + r_ref[...]
        dx_ref[...] = dx.astype(out_dtype)

    in_specs = [pl.BlockSpec((tm, W), lambda i: (i, 0)), pl.BlockSpec((tm, W), lambda i: (i, col)),
                pl.BlockSpec((1, W), lambda i: (0, 0))]
    args = [dy, x, _row(gain)]
    if res is not None:
        in_specs.append(pl.BlockSpec((tm, W), lambda i: (i, 0)))
        args.append(res)
    return pl.pallas_call(
        body, name=name, grid=(S // tm,),
        in_specs=in_specs,
        out_specs=[pl.BlockSpec((tm, W), lambda i: (i, 0)), pl.BlockSpec((1, W), lambda i: (0, 0))],
        out_shape=[jax.ShapeDtypeStruct((S, W), out_dtype), jax.ShapeDtypeStruct((1, W), F32)],
        compiler_params=_cparams("arbitrary"),
    )(*args)


def _silu_parts(a):
    s = jax.nn.sigmoid(a)
    return a * s, s * (1.0 + a * (1.0 - s))


def _ffn_fwd(x, gain, wg_t, wu_t, wd, name, rider=None):
    S, D = x.shape
    Fd = wd.shape[0]
    tm = _tile(S, TILES["ffn_fwd"])
    fc = _lane_tile(Fd, 512)

    def body(*refs):
        i = pl.program_id(0)
        own, finish = _ride(rider, refs, 5, 3, i == 0, i == pl.num_programs(0) - 1)
        x_ref, g_ref, wg_ref, wu_ref, wd_ref, o_ref, a_ref, b_ref = own
        xv = x_ref[...]
        r = lax.rsqrt(jnp.mean(xv * xv, axis=-1, keepdims=True) + NORM_EPS)
        h = (xv * r * g_ref[...]).astype(BF16)
        acc = jnp.zeros((tm, D), F32)
        for c in range(Fd // fc):
            sl = slice(c * fc, (c + 1) * fc)
            a = _dot(h, wg_ref[sl, :], NT)
            b = _dot(h, wu_ref[sl, :], NT)
            a_ref[:, sl] = a.astype(BF16)
            b_ref[:, sl] = b.astype(BF16)
            z = (a * jax.nn.sigmoid(a) * b).astype(BF16)
            acc = acc + _dot(z, wd_ref[sl, :], NN)
        o_ref[...] = xv + 0.5 * acc
        finish()

    wspec = pl.BlockSpec((Fd, D), lambda i: (0, 0), pipeline_mode=pl.Buffered(1))
    extra = rider or _NO_RIDER
    return pl.pallas_call(
        body, name=name, grid=(S // tm,),
        in_specs=[pl.BlockSpec((tm, D), lambda i: (i, 0)), pl.BlockSpec((1, D), lambda i: (0, 0)), wspec, wspec,
                  wspec] + extra.in_specs,
        out_specs=[pl.BlockSpec((tm, D), lambda i: (i, 0)), pl.BlockSpec((tm, Fd), lambda i: (i, 0)),
                   pl.BlockSpec((tm, Fd), lambda i: (i, 0))] + extra.out_specs,
        out_shape=[jax.ShapeDtypeStruct((S, D), F32), jax.ShapeDtypeStruct((S, Fd), BF16),
                   jax.ShapeDtypeStruct((S, Fd), BF16)] + extra.out_shapes,
        scratch_shapes=extra.scratch,
        compiler_params=_cparams("arbitrary" if rider else "parallel"),
    )(x, _row(gain), wg_t, wu_t, wd, *extra.arrays)


def _ffn_bwd(g, x, gain, a, b, wg_t, wu_t, wd, name, rider=None):
    S, D = x.shape
    Fd = wd.shape[0]
    tm = _tile(S, TILES["ffn_bwd"])
    fc = Fd

    def body(*refs):
        i = pl.program_id(0)
        own, finish = _ride(rider, refs, 8, 5, i == 0, i == pl.num_programs(0) - 1)
        g_ref, x_ref, gain_ref, a_ref, b_ref, wg_ref, wu_ref, wd_ref, dx_ref, dz_ref, h_ref, dy_ref, dg_ref = own

        @pl.when(i == 0)
        def _():
            dg_ref[...] = jnp.zeros_like(dg_ref)

        gv = g_ref[...]
        xv = x_ref[...]
        r = lax.rsqrt(jnp.mean(xv * xv, axis=-1, keepdims=True) + NORM_EPS)
        xhat = xv * r
        h_ref[...] = (xhat * gain_ref[...]).astype(BF16)
        dy = (0.5 * gv).astype(BF16)
        dy_ref[...] = dy
        dh = jnp.zeros((tm, D), F32)
        for c in range(Fd // fc):
            sl = slice(c * fc, (c + 1) * fc)
            av = a_ref[:, sl].astype(F32)
            bv = b_ref[:, sl].astype(F32)
            dz = _dot(dy, wd_ref[sl, :], NT).astype(BF16)
            dz_ref[:, sl] = dz
            dzf = dz.astype(F32)
            silu, dsilu = _silu_parts(av)
            da = (dzf * bv * dsilu).astype(BF16)
            db = (dzf * silu).astype(BF16)
            dh = dh + _dot(da, wg_ref[sl, :], NN) + _dot(db, wu_ref[sl, :], NN)
        dg_ref[...] += jnp.sum(dh * xhat, axis=0, keepdims=True)
        dxhat = dh * gain_ref[...]
        dx_ref[...] = gv + r * (dxhat - xhat * jnp.mean(dxhat * xhat, axis=-1, keepdims=True))
        finish()

    wspec = pl.BlockSpec((Fd, D), lambda i: (0, 0), pipeline_mode=pl.Buffered(1))
    row = pl.BlockSpec((tm, D), lambda i: (i, 0))
    wide = pl.BlockSpec((tm, Fd), lambda i: (i, 0))
    extra = rider or _NO_RIDER
    return pl.pallas_call(
        body, name=name, grid=(S // tm,),
        in_specs=[row, row, pl.BlockSpec((1, D), lambda i: (0, 0)), wide, wide, wspec, wspec, wspec] + extra.in_specs,
        out_specs=[row, wide, row, row, pl.BlockSpec((1, D), lambda i: (0, 0))] + extra.out_specs,
        out_shape=[jax.ShapeDtypeStruct((S, D), F32), jax.ShapeDtypeStruct((S, Fd), BF16),
                   jax.ShapeDtypeStruct((S, D), BF16), jax.ShapeDtypeStruct((S, D), BF16),
                   jax.ShapeDtypeStruct((1, D), F32)] + extra.out_shapes,
        scratch_shapes=extra.scratch,
        compiler_params=_cparams("arbitrary"),
    )(g, x, _row(gain), a, b, wg_t, wu_t, wd, *extra.arrays)


def _ffn_dw(a, b, dz, h, dy, name):
    S, Fd = a.shape
    D = h.shape[1]
    ts = _tile(S, TILES["ffn_dw"])
    tf = _lane_tile(Fd, 256)

    def body(a_ref, b_ref, dz_ref, h_ref, dy_ref, og_ref, ou_ref, od_ref, accg, accu, accd):
        s = pl.program_id(1)

        @pl.when(s == 0)
        def _():
            accg[...] = jnp.zeros_like(accg)
            accu[...] = jnp.zeros_like(accu)
            accd[...] = jnp.zeros_like(accd)

        av = a_ref[...].astype(F32)
        bv = b_ref[...].astype(F32)
        dzf = dz_ref[...].astype(F32)
        silu, dsilu = _silu_parts(av)
        da = (dzf * bv * dsilu).astype(BF16)
        db = (dzf * silu).astype(BF16)
        z = (silu * bv).astype(BF16)
        hv = h_ref[...]
        accg[...] += _dot(da, hv, TN)
        accu[...] += _dot(db, hv, TN)
        accd[...] += _dot(z, dy_ref[...], TN)

        @pl.when(s == pl.num_programs(1) - 1)
        def _():
            og_ref[...] = accg[...].astype(BF16)
            ou_ref[...] = accu[...].astype(BF16)
            od_ref[...] = accd[...].astype(BF16)

    wide = pl.BlockSpec((ts, tf), lambda f, s: (s, f))
    row = pl.BlockSpec((ts, D), lambda f, s: (s, 0))
    out = pl.BlockSpec((tf, D), lambda f, s: (f, 0))
    return pl.pallas_call(
        body, name=name, grid=(Fd // tf, S // ts),
        in_specs=[wide, wide, wide, row, row], out_specs=[out, out, out],
        out_shape=[jax.ShapeDtypeStruct((Fd, D), BF16)] * 3,
        scratch_shapes=[pltpu.VMEM((tf, D), F32)] * 3,
        compiler_params=_cparams("parallel", "arbitrary"),
    )(a, b, dz, h, dy)


def _loss_head(x, target, gain, name):
    S, D = x.shape
    tm = _tile(S, TILES["ew"])

    def body(x_ref, t_ref, g_ref, dx_ref, dg_ref, loss_ref):
        @pl.when(pl.program_id(0) == 0)
        def _():
            dg_ref[...] = jnp.zeros_like(dg_ref)
            loss_ref[...] = jnp.zeros_like(loss_ref)

        xv = x_ref[...]
        r = lax.rsqrt(jnp.mean(xv * xv, axis=-1, keepdims=True) + NORM_EPS)
        xhat = xv * r
        e = xhat * g_ref[...] - t_ref[...]
        per_tok = jnp.mean(e * e, axis=-1, keepdims=True)
        loss_ref[...] += jnp.broadcast_to(0.5 * jnp.sum(per_tok, axis=0, keepdims=True), (1, LANE))
        dy = e * (1.0 / D)
        dg_ref[...] += jnp.sum(dy * xhat, axis=0, keepdims=True)
        dxhat = dy * g_ref[...]
        dx_ref[...] = r * (dxhat - xhat * jnp.mean(dxhat * xhat, axis=-1, keepdims=True))

    row = pl.BlockSpec((tm, D), lambda i: (i, 0))
    return pl.pallas_call(
        body, name=name, grid=(S // tm,),
        in_specs=[row, row, pl.BlockSpec((1, D), lambda i: (0, 0))],
        out_specs=[row, pl.BlockSpec((1, D), lambda i: (0, 0)), pl.BlockSpec((1, LANE), lambda i: (0, 0))],
        out_shape=[jax.ShapeDtypeStruct((S, D), F32), jax.ShapeDtypeStruct((1, D), F32),
                   jax.ShapeDtypeStruct((1, LANE), F32)],
        compiler_params=_cparams("arbitrary"),
    )(x, target, _row(gain))


def _shift_down(u, halo, k, rows):
    out = pltpu.roll(u, k, 0)
    for j in range(k):
        out = jnp.where(rows == j, halo[8 - k + j:8 - k + j + 1, :], out)
    return out


def _shift_up(u, halo, k, rows, n):
    out = pltpu.roll(u, n - k, 0)
    for j in range(k):
        out = jnp.where(rows == n - k + j, halo[j:j + 1, :], out)
    return out


def _conv_fwd(p, cw, name):
    S, W3 = p.shape
    W = W3 // 3
    tm = _tile(S, TILES["ew"])
    hb = tm // 8

    def body(p_ref, ph_ref, w_ref, v_ref):
        i = pl.program_id(0)
        bg = p_ref[:, 0:W].astype(F32)
        u = p_ref[:, W:2 * W].astype(F32) * p_ref[:, 2 * W:3 * W].astype(F32)
        uh = ph_ref[:, W:2 * W].astype(F32) * ph_ref[:, 2 * W:3 * W].astype(F32)
        uh = jnp.where(i > 0, uh, 0.0)
        rows = lax.broadcasted_iota(jnp.int32, (tm, 1), 0)
        u1 = _shift_down(u, uh, 1, rows)
        u2 = _shift_down(u, uh, 2, rows)
        y = w_ref[0:1, :] * u2 + w_ref[1:2, :] * u1 + w_ref[2:3, :] * u
        v_ref[...] = (bg * y).astype(BF16)

    return pl.pallas_call(
        body, name=name, grid=(S // tm,),
        in_specs=[pl.BlockSpec((tm, W3), lambda i: (i, 0)),
                  pl.BlockSpec((8, W3), lambda i: (jnp.maximum(i * hb - 1, 0), 0)),
                  pl.BlockSpec((8, W), lambda i: (0, 0))],
        out_specs=pl.BlockSpec((tm, W), lambda i: (i, 0)),
        out_shape=jax.ShapeDtypeStruct((S, W), BF16),
        compiler_params=_cparams("parallel"),
    )(p, p, cw)


def _conv_bwd(dv, p, cw, name):
    S, W3 = p.shape
    W = W3 // 3
    tm = _tile(S, TILES["ew"])
    hb = tm // 8
    last = S // 8 - 1

    def body(dv_ref, dvn_ref, p_ref, pp_ref, pn_ref, w_ref, dp_ref, dw_ref):
        i = pl.program_id(0)
        n = pl.num_programs(0)

        @pl.when(i == 0)
        def _():
            dw_ref[...] = jnp.zeros_like(dw_ref)

        bg = p_ref[:, 0:W].astype(F32)
        cg = p_ref[:, W:2 * W].astype(F32)
        zz = p_ref[:, 2 * W:3 * W].astype(F32)
        u = cg * zz
        uh = pp_ref[:, W:2 * W].astype(F32) * pp_ref[:, 2 * W:3 * W].astype(F32)
        uh = jnp.where(i > 0, uh, 0.0)
        rows = lax.broadcasted_iota(jnp.int32, (tm, 1), 0)
        u1 = _shift_down(u, uh, 1, rows)
        u2 = _shift_down(u, uh, 2, rows)
        w0, w1, w2 = w_ref[0:1, :], w_ref[1:2, :], w_ref[2:3, :]
        y = w0 * u2 + w1 * u1 + w2 * u
        dvv = dv_ref[...].astype(F32)
        dy = dvv * bg
        dyh = dvn_ref[...].astype(F32) * pn_ref[:, 0:W].astype(F32)
        dyh = jnp.where(i < n - 1, dyh, 0.0)
        d1 = _shift_up(dy, dyh, 1, rows, tm)
        d2 = _shift_up(dy, dyh, 2, rows, tm)
        du = w2 * dy + w1 * d1 + w0 * d2
        dp_ref[:, 0:W] = (dvv * y).astype(BF16)
        dp_ref[:, W:2 * W] = (du * zz).astype(BF16)
        dp_ref[:, 2 * W:3 * W] = (du * cg).astype(BF16)
        dw_ref[0:1, :] += jnp.sum(dy * u2, axis=0, keepdims=True)
        dw_ref[1:2, :] += jnp.sum(dy * u1, axis=0, keepdims=True)
        dw_ref[2:3, :] += jnp.sum(dy * u, axis=0, keepdims=True)

    return pl.pallas_call(
        body, name=name, grid=(S // tm,),
        in_specs=[pl.BlockSpec((tm, W), lambda i: (i, 0)),
                  pl.BlockSpec((8, W), lambda i: (jnp.minimum((i + 1) * hb, last), 0)),
                  pl.BlockSpec((tm, W3), lambda i: (i, 0)),
                  pl.BlockSpec((8, W3), lambda i: (jnp.maximum(i * hb - 1, 0), 0)),
                  pl.BlockSpec((8, W3), lambda i: (jnp.minimum((i + 1) * hb, last), 0)),
                  pl.BlockSpec((8, W), lambda i: (0, 0))],
        out_specs=[pl.BlockSpec((tm, W3), lambda i: (i, 0)), pl.BlockSpec((8, W), lambda i: (0, 0))],
        out_shape=[jax.ShapeDtypeStruct((S, W3), BF16), jax.ShapeDtypeStruct((8, W), F32)],
        compiler_params=_cparams("arbitrary"),
    )(dv, dv, p, p, p, cw)


def _rope_swap(r, lane):
    mid = NOPE + ROPE // 2
    first = (lane >= NOPE) & (lane < mid)
    second = (lane >= mid) & (lane < QK_DIM)
    return jnp.where(first, pltpu.roll(r, HEAD_PAD - ROPE // 2, 1), jnp.where(second, pltpu.roll(r, ROPE // 2, 1), 0.0))


def _rope_fwd(q_big, kv_big, proj, kr_col, ct, st, name):
    S = q_big.shape[0]
    HW = HEADS * HEAD_PAD
    tm = _tile(S, TILES["ew"])

    def body(q_ref, k_ref, v_ref, kr_ref, ct_ref, st_ref, qo_ref, ko_ref, vo_ref):
        lane = lax.broadcasted_iota(jnp.int32, (1, HEAD_PAD), 1)
        ctv, stv = ct_ref[...], st_ref[...]
        krr = pltpu.roll(kr_ref[...].astype(F32), NOPE, 1)
        kro = krr * ctv + _rope_swap(krr, lane) * stv
        for h in range(HEADS):
            sl = slice(h * HEAD_PAD, (h + 1) * HEAD_PAD)
            qh = q_ref[:, sl].astype(F32)
            qo_ref[:, sl] = (qh * ctv + _rope_swap(qh, lane) * stv).astype(BF16)
            ko_ref[:, sl] = (k_ref[:, sl].astype(F32) + kro).astype(BF16)
            vo_ref[:, sl] = jnp.where(lane == VDIM, 1.0, v_ref[:, sl].astype(F32)).astype(BF16)

    wide = pl.BlockSpec((tm, HW), lambda i: (i, 0))
    narrow = pl.BlockSpec((tm, HEAD_PAD), lambda i: (i, 0))
    return pl.pallas_call(
        body, name=name, grid=(S // tm,),
        in_specs=[wide, wide, pl.BlockSpec((tm, HW), lambda i: (i, 1)),
                  pl.BlockSpec((tm, HEAD_PAD), lambda i: (i, kr_col)), narrow, narrow],
        out_specs=[wide, wide, wide],
        out_shape=[jax.ShapeDtypeStruct((S, HW), BF16)] * 3,
        compiler_params=_cparams("parallel"),
    )(q_big, kv_big, kv_big, proj, ct, st)


def _rope_bwd(dq, dk, dv, ct, st, name):
    S = dq.shape[0]
    HW = HEADS * HEAD_PAD
    tm = _tile(S, TILES["ew"])

    def body(dq_ref, dk_ref, dv_ref, ct_ref, st_ref, oq_ref, okv_ref, okr_ref):
        lane = lax.broadcasted_iota(jnp.int32, (1, HEAD_PAD), 1)
        ctv, stv = ct_ref[...], st_ref[...]
        acc = jnp.zeros((tm, HEAD_PAD), F32)
        for h in range(HEADS):
            sl = slice(h * HEAD_PAD, (h + 1) * HEAD_PAD)
            d = dq_ref[:, sl].astype(F32)
            oq_ref[:, sl] = (d * ctv + _rope_swap(d * stv, lane)).astype(BF16)
            d = dk_ref[:, sl].astype(F32)
            okv_ref[:, sl] = jnp.where(lane < NOPE, d, 0.0).astype(BF16)
            acc = acc + jnp.where(lane >= NOPE, d * ctv + _rope_swap(d * stv, lane), 0.0)
        okv_ref[:, HW:2 * HW] = dv_ref[...].astype(BF16)
        okr_ref[...] = pltpu.roll(acc, HEAD_PAD - NOPE, 1).astype(BF16)

    wide = pl.BlockSpec((tm, HW), lambda i: (i, 0))
    narrow = pl.BlockSpec((tm, HEAD_PAD), lambda i: (i, 0))
    return pl.pallas_call(
        body, name=name, grid=(S // tm,),
        in_specs=[wide, wide, wide, narrow, narrow],
        out_specs=[wide, pl.BlockSpec((tm, 2 * HW), lambda i: (i, 0)), narrow],
        out_shape=[jax.ShapeDtypeStruct((S, HW), BF16), jax.ShapeDtypeStruct((S, 2 * HW), BF16),
                   jax.ShapeDtypeStruct((S, HEAD_PAD), BF16)],
        compiler_params=_cparams("parallel"),
    )(dq, dk, dv, ct, st)


def _pairs(n, by_key):
    if by_key:
        pr = [(i, j) for j in range(n) for i in range(j, n)]
    else:
        pr = [(i, j) for i in range(n) for j in range(i + 1)]
    qi = np.array([p[0] for p in pr], np.int32)
    kj = np.array([p[1] for p in pr], np.int32)
    return jnp.asarray(qi), jnp.asarray(kj)


_LOG2E = 1.4426950408889634
_LN2 = 0.6931471805599453


def _tile_mask(t):
    return lax.broadcasted_iota(jnp.int32, (t, t), 1) <= lax.broadcasted_iota(jnp.int32, (t, t), 0)


def _attn_fwd(q, k, v, name, tile, heads_per_step=1):
    S = q.shape[0]
    HW = HEADS * HEAD_PAD
    t = _tile(S, tile)
    n = S // t
    qi, kj = _pairs(n, by_key=False)
    c = (QK_DIM ** -0.5) * _LOG2E
    W = heads_per_step * HEAD_PAD

    def body(qi_ref, kj_ref, q_ref, k_ref, v_ref, o_ref, lse_ref, m_s, acc_s):
        p_id = pl.program_id(1)
        i, j = qi_ref[p_id], kj_ref[p_id]
        lanes = [slice(u * HEAD_PAD, (u + 1) * HEAD_PAD) for u in range(heads_per_step)]

        @pl.when(j == 0)
        def _():
            m_s[...] = jnp.full_like(m_s, -jnp.inf)
            acc_s[...] = jnp.zeros_like(acc_s)

        def step(on_diagonal):
            for u, sl in enumerate(lanes):
                s = _dot(q_ref[:, sl], k_ref[:, sl], NT)
                if on_diagonal:
                    s = jnp.where(_tile_mask(t), s, -jnp.inf)
                m_old = m_s[u]
                m_new = jnp.maximum(m_old, jnp.max(s, axis=-1, keepdims=True))
                p = jnp.exp2((s - m_new) * c).astype(BF16)
                acc_s[:, sl] = jnp.exp2((m_old - m_new) * c) * acc_s[:, sl] + _dot(p, v_ref[:, sl], NN)
                m_s[u] = m_new

        @pl.when(i == j)
        def _():
            step(True)

        @pl.when(i != j)
        def _():
            step(False)

        @pl.when(j == i)
        def _():
            for u, sl in enumerate(lanes):
                acc = acc_s[:, sl]
                l = acc[:, VDIM:VDIM + 1]
                o_ref[:, sl] = (acc * (1.0 / l)).astype(BF16)
                lse_ref[:, sl] = jnp.broadcast_to(m_s[u] * c + jnp.log2(l), (t, HEAD_PAD))

    qspec = pl.BlockSpec((t, W), lambda h, p, qi, kj: (qi[p], h))
    kspec = pl.BlockSpec((t, W), lambda h, p, qi, kj: (kj[p], h))
    grid_spec = pltpu.PrefetchScalarGridSpec(
        num_scalar_prefetch=2, grid=(HEADS // heads_per_step, int(qi.shape[0])),
        in_specs=[qspec, kspec, kspec], out_specs=[qspec, qspec],
        scratch_shapes=[pltpu.VMEM((heads_per_step, t, 1), F32), pltpu.VMEM((t, W), F32)])
    return pl.pallas_call(
        body, name=name, grid_spec=grid_spec,
        out_shape=[jax.ShapeDtypeStruct((S, HW), BF16), jax.ShapeDtypeStruct((S, HW), F32)],
        compiler_params=_cparams("parallel", "arbitrary"),
    )(qi, kj, q, k, v)


def _attn_delta(o, do, name):
    S, HW = o.shape
    tm = _tile(S, TILES["ew"])

    def body(o_ref, do_ref, d_ref):
        for h in range(HEADS):
            sl = slice(h * HEAD_PAD, (h + 1) * HEAD_PAD)
            d = jnp.sum(o_ref[:, sl].astype(F32) * do_ref[:, sl].astype(F32), axis=-1, keepdims=True)
            d_ref[:, sl] = jnp.broadcast_to(d, (tm, HEAD_PAD))

    wide = pl.BlockSpec((tm, HW), lambda i: (i, 0))
    return pl.pallas_call(
        body, name=name, grid=(S // tm,), in_specs=[wide, wide], out_specs=wide,
        out_shape=jax.ShapeDtypeStruct((S, HW), F32), compiler_params=_cparams("parallel"),
    )(o, do)


def _attn_bwd(q, k, v, o, do, lse2, name, o_is_delta=False):
    S = q.shape[0]
    HW = HEADS * HEAD_PAD
    t = _tile(S, TILES["attn"])
    n = S // t
    qi, kj = _pairs(n, by_key=True)
    scale = QK_DIM ** -0.5

    def body(qi_ref, kj_ref, q_ref, k_ref, v_ref, o_ref, do_ref, lse_ref, dq_ref, dk_ref, dv_ref, dk_s, dv_s):
        p_id = pl.program_id(1)
        i, j = qi_ref[p_id], kj_ref[p_id]

        @pl.when(p_id == 0)
        def _():
            dq_ref[...] = jnp.zeros_like(dq_ref)

        @pl.when(i == j)
        def _():
            dk_s[...] = jnp.zeros_like(dk_s)
            dv_s[...] = jnp.zeros_like(dv_s)

        qv, kv, vv = q_ref[...], k_ref[...], v_ref[...]
        dov = do_ref[...]
        s = _dot(qv, kv, NT) * scale
        rows_i = lax.broadcasted_iota(jnp.int32, (t, t), 0) + i * t
        cols_j = lax.broadcasted_iota(jnp.int32, (t, t), 1) + j * t
        p = jnp.where(cols_j <= rows_i, jnp.exp(s - lse_ref[:, 0:1] * _LN2), 0.0)
        if o_is_delta:
            delta = o_ref[:, 0:1]
        else:
            delta = jnp.sum(dov.astype(F32) * o_ref[...].astype(F32), axis=-1, keepdims=True)
        dv_s[...] += _dot(p.astype(BF16), dov, TN)
        ds = (p * (_dot(dov, vv, NT) - delta) * scale).astype(BF16)
        dk_s[...] += _dot(ds, qv, TN)
        rows = pl.ds(pl.multiple_of(i * t, t), t)
        dq_ref[rows, :] += _dot(ds, kv, NN)

        @pl.when(i == n - 1)
        def _():
            dk_ref[...] = dk_s[...]
            dv_ref[...] = dv_s[...]

    qspec = pl.BlockSpec((t, HEAD_PAD), lambda h, p, qi, kj: (qi[p], h))
    kspec = pl.BlockSpec((t, HEAD_PAD), lambda h, p, qi, kj: (kj[p], h))
    grid_spec = pltpu.PrefetchScalarGridSpec(
        num_scalar_prefetch=2, grid=(HEADS, int(qi.shape[0])),
        in_specs=[qspec, kspec, kspec, qspec, qspec, qspec],
        out_specs=[pl.BlockSpec((S, HEAD_PAD), lambda h, p, qi, kj: (0, h)), kspec, kspec],
        scratch_shapes=[pltpu.VMEM((t, HEAD_PAD), F32), pltpu.VMEM((t, HEAD_PAD), F32)])
    return pl.pallas_call(
        body, name=name, grid_spec=grid_spec,
        out_shape=[jax.ShapeDtypeStruct((S, HW), F32)] * 3,
        compiler_params=_cparams("parallel", "arbitrary"),
    )(qi, kj, q, k, v, o, do, lse2)


_SQRT_HALF = 0.7071067811865476
_INV_SQRT_2PI = 0.3989422804014327


def _sg_select(r, grp):
    out = jnp.where(grp == 0, r[0:SG_CHUNK, :], 0.0)
    for g in range(1, SG_GROUPS):
        out = out + jnp.where(grp == g, r[g * SG_CHUNK:(g + 1) * SG_CHUNK, :], 0.0)
    return out


def _sgu_fwd(proj, gain, wstack, bmat, name):
    S = proj.shape[0]
    W = SG_WIDTH
    tm = _tile(S, TILES["sgu"])

    def body(z_ref, g_ref, w_ref, b_ref, o_ref):
        z = z_ref[...].astype(F32)
        zg = 0.5 * z * (1.0 + lax.erf(z * _SQRT_HALF))
        u, vv = zg[:, 0:W], zg[:, W:2 * W]
        r = lax.rsqrt(jnp.mean(vv * vv, axis=-1, keepdims=True) + NORM_EPS)
        vn = (vv * r * g_ref[...]).astype(BF16)
        grp = lax.broadcasted_iota(jnp.int32, (1, W), 1) // SG_GROUP_DIM
        for c in range(tm // SG_CHUNK):
            sl = slice(c * SG_CHUNK, (c + 1) * SG_CHUNK)
            mixed = _sg_select(_dot(w_ref[...], vn[sl, :], NN), grp) + b_ref[...]
            o_ref[sl, :] = (u[sl, :] * mixed).astype(BF16)

    return pl.pallas_call(
        body, name=name, grid=(S // tm,),
        in_specs=[pl.BlockSpec((tm, 2 * W), lambda i: (i, 0)), pl.BlockSpec((1, W), lambda i: (0, 0)),
                  pl.BlockSpec(wstack.shape, lambda i: (0, 0)), pl.BlockSpec(bmat.shape, lambda i: (0, 0))],
        out_specs=pl.BlockSpec((tm, W), lambda i: (i, 0)),
        out_shape=jax.ShapeDtypeStruct((S, W), BF16),
        compiler_params=_cparams("parallel"),
    )(proj, _row(gain), wstack, bmat)


def _sgu_bwd(dsg, proj, gain, wstack, wtstack, bmat, gsum, name):
    S = proj.shape[0]
    W = SG_WIDTH
    tm = _tile(S, TILES["sgu"])
    GS = SG_GROUPS * SG_CHUNK

    def body(d_ref, z_ref, g_ref, w_ref, wt_ref, b_ref, e_ref, dz_ref, dw_ref, db_ref, dg_ref, dw_s, db_s):
        i = pl.program_id(0)

        @pl.when(i == 0)
        def _():
            dw_s[...] = jnp.zeros_like(dw_s)
            db_s[...] = jnp.zeros_like(db_s)
            dg_ref[...] = jnp.zeros_like(dg_ref)

        z = z_ref[...].astype(F32)
        cdf = 0.5 * (1.0 + lax.erf(z * _SQRT_HALF))
        zg = z * cdf
        u, vv = zg[:, 0:W], zg[:, W:2 * W]
        r = lax.rsqrt(jnp.mean(vv * vv, axis=-1, keepdims=True) + NORM_EPS)
        vhat = vv * r
        vn = (vhat * g_ref[...]).astype(BF16)
        grp = lax.broadcasted_iota(jnp.int32, (1, W), 1) // SG_GROUP_DIM
        d = d_ref[...].astype(F32)
        du_parts, dvn_parts = [], []
        for c in range(tm // SG_CHUNK):
            sl = slice(c * SG_CHUNK, (c + 1) * SG_CHUNK)
            vc = vn[sl, :]
            mixed = _sg_select(_dot(w_ref[...], vc, NN), grp) + b_ref[...]
            dc = d[sl, :]
            du_parts.append(dc * mixed)
            dmix = dc * u[sl, :]
            db_s[...] += dmix
            dmb = dmix.astype(BF16)
            dvn_parts.append(_sg_select(_dot(wt_ref[...], dmb, NN), grp))
            astack = jnp.concatenate([jnp.where(grp == g, dmb, jnp.zeros_like(dmb)) for g in range(SG_GROUPS)], axis=0)
            dw_s[...] += _dot(astack, vc, NT)
        du = jnp.concatenate(du_parts, axis=0)
        dvn = jnp.concatenate(dvn_parts, axis=0)
        dg_ref[...] += jnp.sum(dvn * vhat, axis=0, keepdims=True)
        dvhat = dvn * g_ref[...]
        dvv = r * (dvhat - vhat * jnp.mean(dvhat * vhat, axis=-1, keepdims=True))
        dgelu = cdf + z * (_INV_SQRT_2PI * jnp.exp(-0.5 * z * z))
        dz_ref[:, 0:W] = (du * dgelu[:, 0:W]).astype(BF16)
        dz_ref[:, W:2 * W] = (dvv * dgelu[:, W:2 * W]).astype(BF16)

        @pl.when(i == pl.num_programs(0) - 1)
        def _():
            dw_ref[...] = dw_s[...]
            db_ref[...] = lax.dot_general(db_s[...], e_ref[...], NN, precision=lax.Precision.HIGHEST,
                                          preferred_element_type=F32)

    full = lambda a: pl.BlockSpec(a.shape, lambda i: (0, 0))
    return pl.pallas_call(
        body, name=name, grid=(S // tm,),
        in_specs=[pl.BlockSpec((tm, W), lambda i: (i, 0)), pl.BlockSpec((tm, 2 * W), lambda i: (i, 0)),
                  pl.BlockSpec((1, W), lambda i: (0, 0)), full(wstack), full(wtstack), full(bmat), full(gsum)],
        out_specs=[pl.BlockSpec((tm, 2 * W), lambda i: (i, 0)), pl.BlockSpec((GS, SG_CHUNK), lambda i: (0, 0)),
                   pl.BlockSpec((SG_CHUNK, LANE), lambda i: (0, 0)), pl.BlockSpec((1, W), lambda i: (0, 0))],
        out_shape=[jax.ShapeDtypeStruct((S, 2 * W), BF16), jax.ShapeDtypeStruct((GS, SG_CHUNK), F32),
                   jax.ShapeDtypeStruct((SG_CHUNK, LANE), F32), jax.ShapeDtypeStruct((1, W), F32)],
        scratch_shapes=[pltpu.VMEM((GS, SG_CHUNK), F32), pltpu.VMEM((SG_CHUNK, W), F32)],
        compiler_params=_cparams("arbitrary"),
    )(dsg, proj, _row(gain), wstack, wtstack, bmat, gsum)


WEIGHTS = ['ffn_pre_norm', 'ffn_pre_w_gate', 'ffn_pre_w_up', 'ffn_pre_w_down', 'mix_norm', 'ffn_post_norm',
           'ffn_post_w_gate', 'ffn_post_w_up', 'ffn_post_w_down', 'even_w_in', 'q_norm', 'w_uq', 'kv_norm', 'w_ukv',
           'sg_norm', 'sg_w', 'sg_b', 'even_w_out', 'conv_w_in', 'conv_w', 'conv_w_out', 'final_norm']
SHARD_AXIS = dict(ffn_pre_w_gate=2, ffn_pre_w_up=2, ffn_pre_w_down=1, ffn_post_w_gate=2, ffn_post_w_up=2,
                  ffn_post_w_down=1, even_w_in=2, w_uq=2, w_ukv=2, even_w_out=1, conv_w_in=2, conv_w=2, conv_w_out=1)
SHARDED = [n for n in WEIGHTS if n in SHARD_AXIS]
REPLICATED = [n for n in WEIGHTS if n not in SHARD_AXIS]


def _to_t(name, w):
    return jnp.swapaxes(w, 1, 2) if SHARD_AXIS[name] == 2 else w


def _rows_of(n):
    return -(-n // PACK_W)


def _pad_rows(a, mult, axis):
    r = a.shape[axis]
    extra = (-r) % mult
    if extra == 0:
        return a
    pad = [(0, 0)] * a.ndim
    pad[axis] = (0, extra)
    return jnp.pad(a, pad)


def _flat_rows(a, lead):
    flat = a.reshape(a.shape[:lead] + (-1,))
    n = flat.shape[-1]
    flat = _pad_rows(flat, PACK_W, lead)
    return flat.reshape(a.shape[:lead] + (_rows_of(n), PACK_W))


def _pack(pieces, lead, mult, piece_mult=1):
    rows, offs, off = [], [], 0
    for p in pieces:
        r = _pad_rows(_flat_rows(p, lead), piece_mult, lead)
        rows.append(r)
        offs.append(off)
        off += r.shape[lead]
    return _pad_rows(jnp.concatenate(rows, axis=lead), mult, lead), offs


def _unpack(buf, off, shape, lead):
    n = math.prod(shape)
    r = _rows_of(n)
    piece = lax.slice_in_dim(buf, off, off + r, axis=lead)
    piece = piece.reshape(buf.shape[:lead] + (r * PACK_W,))
    piece = lax.slice_in_dim(piece, 0, n, axis=lead)
    return piece.reshape(buf.shape[:lead] + tuple(shape))


def _head_pad(w, per_head, keep):
    k = w.shape[-1]
    w = w.reshape(HEADS, per_head, k)[:, keep[0]:keep[1]]
    w = jnp.pad(w, ((0, 0), (0, HEAD_PAD - (keep[1] - keep[0])), (0, 0)))
    return w.reshape(HEADS * HEAD_PAD, k)


def _head_unpad(w, n):
    return w.reshape(HEADS, HEAD_PAD, w.shape[-1])[:, :n]


def kernel(x, positions, ffn_pre_norm, ffn_pre_w_gate, ffn_pre_w_up, ffn_pre_w_down, mix_norm, ffn_post_norm, ffn_post_w_gate, ffn_post_w_up, ffn_post_w_down, even_w_in, q_norm, w_uq, kv_norm, w_ukv, sg_norm, sg_w, sg_b, even_w_out, conv_w_in, conv_w, conv_w_out, final_norm, loss_target, m_ffn_pre_norm, m_ffn_pre_w_gate, m_ffn_pre_w_up, m_ffn_pre_w_down, m_mix_norm, m_ffn_post_norm, m_ffn_post_w_gate, m_ffn_post_w_up, m_ffn_post_w_down, m_even_w_in, m_q_norm, m_w_uq, m_kv_norm, m_w_ukv, m_sg_norm, m_sg_w, m_sg_b, m_even_w_out, m_conv_w_in, m_conv_w, m_conv_w_out, m_final_norm, v_ffn_pre_norm, v_ffn_pre_w_gate, v_ffn_pre_w_up, v_ffn_pre_w_down, v_mix_norm, v_ffn_post_norm, v_ffn_post_w_gate, v_ffn_post_w_up, v_ffn_post_w_down, v_even_w_in, v_q_norm, v_w_uq, v_kv_norm, v_w_ukv, v_sg_norm, v_sg_w, v_sg_b, v_even_w_out, v_conv_w_in, v_conv_w, v_conv_w_out, v_final_norm):
    given = dict(locals())
    w_loc = {n: given[n] for n in WEIGHTS}
    m_loc = {n: given["m_" + n] for n in WEIGHTS}
    v_loc = {n: given["v_" + n] for n in WEIGHTS}

    S, D = x.shape[1], x.shape[2]
    depth = ffn_pre_norm.shape[0]
    QL, KVL = q_norm.shape[1], kv_norm.shape[1]
    ZW = 2 * SG_WIDTH
    assert x.shape[0] == 1 and ZW % KVL == 0 and (ZW + KVL) % HEAD_PAD == 0 and (ZW + KVL + 2 * HEAD_PAD) % QL == 0
    col_ckv = ZW // KVL
    col_kr = (ZW + KVL) // HEAD_PAD
    col_cq = (ZW + KVL + 2 * HEAD_PAD) // QL

    t_loc = {n: _to_t(n, w_loc[n]) for n in SHARDED}
    full = {n: {} for n in SHARDED}

    def ffn_keys(kind, l):
        return [("ffn_%s_w_%s" % (kind, part), l) for part in ("gate", "up", "down")]

    def mixer_keys(l):
        names = ("even_w_in", "w_uq", "w_ukv", "even_w_out") if l % 2 == 0 else ("conv_w_in", "conv_w", "conv_w_out")
        return [(n, l // 2) for n in names]

    def local_pack(keys):
        return _pack([t_loc[n][l].astype(BF16) for n, l in keys], 0, 16, piece_mult=16)

    def take_gathered(gathered, keys, offs):
        for (n, l), off in zip(keys, offs):
            piece = _unpack(gathered, off, t_loc[n].shape[1:], 1)
            full[n][l] = piece.reshape(N_DEV * piece.shape[1], piece.shape[2])

    def gather_rider(keys):
        pack, offs = local_pack(keys)
        return _Exchange("gather", [pack]), offs

    first_keys = ffn_keys("pre", 0) + mixer_keys(0)
    pack0, offs0 = local_pack(first_keys)
    take_gathered(_all_gather(pack0, "gather_weights"), first_keys, offs0)

    tril = jnp.tril(jnp.ones((SG_CHUNK, SG_CHUNK), F32))
    even_ops = {}

    def even_operands(e):
        if e not in even_ops:
            wi = full["even_w_in"][e]
            zrow = lambda k: jnp.zeros((k, D), BF16)
            ops = dict(win_pad=jnp.concatenate(
                [wi[QL + KVL + ROPE:], wi[QL:QL + KVL], wi[QL + KVL:QL + KVL + ROPE], zrow(HEAD_PAD - ROPE),
                 zrow(HEAD_PAD), wi[:QL]], axis=0))
            ops["wq_big"] = _head_pad(full["w_uq"][e], QK_DIM, (0, QK_DIM))
            wkv = full["w_ukv"][e]
            ops["wkv_big"] = jnp.concatenate([_head_pad(wkv, NOPE + VDIM, (0, NOPE)),
                                              _head_pad(wkv, NOPE + VDIM, (NOPE, NOPE + VDIM))], axis=0)
            wo = full["even_w_out"][e]
            ops["wo_attn"] = _head_pad(wo[:HEADS * VDIM], VDIM, (0, VDIM))
            ops["wo_sg"] = wo[HEADS * VDIM:]
            wt = sg_w[e] * tril
            ops["wstack"] = wt.reshape(SG_GROUPS * SG_CHUNK, SG_CHUNK).astype(BF16)
            ops["wtstack"] = jnp.swapaxes(wt, 1, 2).reshape(SG_GROUPS * SG_CHUNK, SG_CHUNK).astype(BF16)
            ops["bmat"] = jnp.repeat(sg_b[e].T, SG_GROUP_DIM, axis=1)
            even_ops[e] = ops
        return even_ops[e]

    gsum = (jnp.arange(SG_WIDTH)[:, None] // SG_GROUP_DIM == jnp.arange(LANE)[None, :]).astype(F32)

    def conv_taps(o):
        return jnp.pad(jnp.swapaxes(full["conv_w"][o], 0, 1).astype(F32), ((0, 8 - CONV_K), (0, 0)))

    inv_freq = ROPE_THETA ** (-jnp.arange(0, ROPE, 2, dtype=F32) / ROPE)
    ang = positions[0].astype(F32)[:, None] * inv_freq
    cos, sin = jnp.cos(ang), jnp.sin(ang)
    ones, zeros = jnp.ones((S, NOPE), F32), jnp.zeros((S, HEAD_PAD - QK_DIM), F32)
    ct = jnp.concatenate([ones, cos, cos, zeros], axis=1)
    st = jnp.concatenate([0.0 * ones, -sin, sin, zeros], axis=1)

    xs = x[0]
    saved = []
    def ffn_forward(xin, kind, l, next_keys):
        gain = (ffn_pre_norm if kind == "pre" else ffn_post_norm)[l]
        wg, wu, wd = (full[n][l] for n, _ in ffn_keys(kind, l))
        if not next_keys:
            return _ffn_fwd(xin, gain, wg, wu, wd, "ffn_fwd")
        rider, offs = gather_rider(next_keys)
        xo, a, b, gathered = _ffn_fwd(xin, gain, wg, wu, wd, "ffn_fwd_gather", rider=rider)
        take_gathered(gathered, next_keys, offs)
        return xo, a, b

    for l in range(depth):
        sv = dict(x0=xs)
        x1, sv["a1"], sv["b1"] = ffn_forward(xs, "pre", l, ffn_keys("post", l))
        h = _rms_fwd(x1, mix_norm[l], "mix_norm_fwd")
        sv.update(x1=x1, h=h)
        if l % 2 == 0:
            e = l // 2
            ops = even_operands(e)
            proj = _mm(h, ops["win_pad"], "nt", "even_in_proj", out_dtype=F32)
            qn = _rms_fwd(proj, q_norm[e], "q_norm_fwd", col=col_cq, width=QL)
            kvn = _rms_fwd(proj, kv_norm[e], "kv_norm_fwd", col=col_ckv, width=KVL)
            q_big = _mm(qn, ops["wq_big"], "nt", "q_up_proj", out_dtype=F32)
            kv_big = _mm(kvn, ops["wkv_big"], "nt", "kv_up_proj", out_dtype=BF16)
            q_r, k_r, v_r = _rope_fwd(q_big, kv_big, proj, col_kr, ct, st, "rope_fwd")
            if e % 2 == 0:
                o_att, lse = _attn_fwd(q_r, k_r, v_r, "attn_fwd_wide", TILES["attn_wide"])
            else:
                o_att, lse = _attn_fwd(q_r, k_r, v_r, "attn_fwd_pair", TILES["attn"], heads_per_step=2)
            sg = _sgu_fwd(proj, sg_norm[e], ops["wstack"], ops["bmat"], "sgu_fwd")
            tmp = _mm(o_att, ops["wo_attn"], "nn", "even_out_attn", out_dtype=F32, res=x1)
            x2 = _mm(sg, ops["wo_sg"], "nn", "even_out_sg", out_dtype=F32, res=tmp)
            sv.update(proj=proj, qn=qn, kvn=kvn, q=q_r, k=k_r, v=v_r, o=o_att, lse=lse, sg=sg)
        else:
            o = l // 2
            p = _mm(h, full["conv_w_in"][o], "nt", "conv_in_proj", out_dtype=BF16)
            cv = _conv_fwd(p, conv_taps(o), "conv_fwd")
            x2 = _mm(cv, full["conv_w_out"][o], "nn", "conv_out_proj", out_dtype=F32, res=x1)
            sv.update(p=p, cv=cv)
        sv["x2"] = x2
        next_keys = ffn_keys("pre", l + 1) + mixer_keys(l + 1) if l + 1 < depth else []
        xs, sv["a2"], sv["b2"] = ffn_forward(x2, "post", l, next_keys)
        saved.append(sv)

    gr = {n: [None] * w_loc[n].shape[0] for n in REPLICATED if n != "final_norm"}
    per_layer = {n: [None] * w_loc[n].shape[0] for n in SHARDED}
    pending = []

    def scatter_rider():
        pieces, where, off = [], [], 0
        for n, l, g in pending:
            piece = _pad_rows(_flat_rows(g.astype(BF16).reshape(N_DEV, -1), 1), 16, 1)
            pieces.append(piece)
            where.append((n, l, off))
            off += piece.shape[1]
        if off % GRAD_ROWS_MULT:
            pieces.append(jnp.zeros((N_DEV, (-off) % GRAD_ROWS_MULT, PACK_W), BF16))
        pending.clear()
        return _Exchange("scatter", pieces), where

    def take_scattered(received, where):
        owned = _sum_slots(received, "sum_grad_shards")
        for n, l, off in where:
            per_layer[n][l] = _unpack(owned, off, t_loc[n].shape[1:], 0)

    def ffn_backward(dxin, xin, kind, l, a, b):
        gain = (ffn_pre_norm if kind == "pre" else ffn_post_norm)[l]
        keys = ffn_keys(kind, l)
        wg, wu, wd = (full[n][l] for n, _ in keys)
        if pending:
            rider, where = scatter_rider()
            dxo, dz, hh, dy, dgain, received = _ffn_bwd(dxin, xin, gain, a, b, wg, wu, wd, "ffn_bwd_scatter", rider=rider)
            take_scattered(received, where)
        else:
            dxo, dz, hh, dy, dgain = _ffn_bwd(dxin, xin, gain, a, b, wg, wu, wd, "ffn_bwd")
        gr["ffn_%s_norm" % kind][l] = dgain[0]
        for (n, _), g in zip(keys, _ffn_dw(a, b, dz, hh, dy, "ffn_dw")):
            pending.append((n, l, g))
        return dxo

    dx, g_final, loss_part = _loss_head(xs, loss_target[0], final_norm, "loss_head")
    for l in reversed(range(depth)):
        sv = saved[l]
        dx = ffn_backward(dx, sv["x2"], "post", l, sv["a2"], sv["b2"])
        h = sv["h"]
        if l % 2 == 0:
            e = l // 2
            ops = even_operands(e)
            d_o = _mm(dx, ops["wo_attn"], "nt", "even_out_attn_bwd", out_dtype=BF16)
            d_sg = _mm(dx, ops["wo_sg"], "nt", "even_out_sg_bwd", out_dtype=BF16)
            g_wo_attn = _mm(sv["o"], dx, "tn", "even_out_attn_dw", out_dtype=F32)
            g_wo_sg = _mm(sv["sg"], dx, "tn", "even_out_sg_dw", out_dtype=F32)
            if e % 2 == 0:
                delta = _attn_delta(sv["o"], d_o, "attn_delta")
                dq, dk, dv = _attn_bwd(sv["q"], sv["k"], sv["v"], delta, d_o, sv["lse"], "attn_bwd_delta",
                                       o_is_delta=True)
            else:
                dq, dk, dv = _attn_bwd(sv["q"], sv["k"], sv["v"], sv["o"], d_o, sv["lse"], "attn_bwd")
            dq_big, dkv_big, dkr = _rope_bwd(dq, dk, dv, ct, st, "rope_bwd")
            dz_sg, g_wstack, g_bias, g_sgn = _sgu_bwd(d_sg, sv["proj"], sg_norm[e], ops["wstack"], ops["wtstack"],
                                                      ops["bmat"], gsum, "sgu_bwd")
            dqn = _mm(dq_big, ops["wq_big"], "nn", "q_up_proj_bwd", out_dtype=F32)
            g_wq_big = _mm(dq_big, sv["qn"], "tn", "q_up_proj_dw", out_dtype=F32)
            dkvn = _mm(dkv_big, ops["wkv_big"], "nn", "kv_up_proj_bwd", out_dtype=F32)
            g_wkv_big = _mm(dkv_big, sv["kvn"], "tn", "kv_up_proj_dw", out_dtype=F32)
            dcq, g_qn = _rms_bwd(dqn, sv["proj"], q_norm[e], "q_norm_bwd", col=col_cq, out_dtype=BF16)
            dckv, g_kvn = _rms_bwd(dkvn, sv["proj"], kv_norm[e], "kv_norm_bwd", col=col_ckv, out_dtype=BF16)
            dproj = jnp.concatenate([dz_sg, dckv, dkr, jnp.zeros((S, HEAD_PAD), BF16), dcq], axis=1)
            dh = _mm(dproj, ops["win_pad"], "nn", "even_in_proj_bwd", out_dtype=F32)
            g_win = _mm(dproj, h, "tn", "even_in_proj_dw", out_dtype=F32)
            o_cq, o_ckv, o_kr = col_cq * QL, col_ckv * KVL, col_kr * HEAD_PAD
            hw = HEADS * HEAD_PAD
            pending.append(("even_w_in", e, jnp.concatenate(
                [g_win[o_cq:o_cq + QL], g_win[o_ckv:o_ckv + KVL], g_win[o_kr:o_kr + ROPE], g_win[:ZW]], axis=0)))
            pending.append(("w_uq", e, _head_unpad(g_wq_big, QK_DIM).reshape(HEADS * QK_DIM, QL)))
            pending.append(("w_ukv", e, jnp.concatenate(
                [_head_unpad(g_wkv_big[:hw], NOPE), _head_unpad(g_wkv_big[hw:], VDIM)],
                axis=1).reshape(HEADS * (NOPE + VDIM), KVL)))
            pending.append(("even_w_out", e, jnp.concatenate(
                [_head_unpad(g_wo_attn, VDIM).reshape(HEADS * VDIM, D), g_wo_sg], axis=0)))
            gr["q_norm"][e], gr["kv_norm"][e], gr["sg_norm"][e] = g_qn[0], g_kvn[0], g_sgn[0]
            gr["sg_w"][e] = g_wstack.reshape(SG_GROUPS, SG_CHUNK, SG_CHUNK) * tril
            gr["sg_b"][e] = g_bias[:, :SG_GROUPS].T
        else:
            o = l // 2
            dcv = _mm(dx, full["conv_w_out"][o], "nt", "conv_out_proj_bwd", out_dtype=BF16)
            pending.append(("conv_w_out", o, _mm(sv["cv"], dx, "tn", "conv_out_proj_dw", out_dtype=BF16)))
            dp, dcw = _conv_bwd(dcv, sv["p"], conv_taps(o), "conv_bwd")
            dh = _mm(dp, full["conv_w_in"][o], "nn", "conv_in_proj_bwd", out_dtype=F32)
            pending.append(("conv_w_in", o, _mm(dp, h, "tn", "conv_in_proj_dw", out_dtype=BF16)))
            pending.append(("conv_w", o, jnp.swapaxes(dcw[:CONV_K], 0, 1)))
        dx, dgain = _rms_bwd(dh, sv["x1"], mix_norm[l], "mix_norm_bwd", res=dx)
        gr["mix_norm"][l] = dgain[0]
        dx = ffn_backward(dx, sv["x0"], "pre", l, sv["a1"], sv["b1"])
    grad_x = dx[None]

    rider, where = scatter_rider()
    take_scattered(_exchange("scatter", rider.arrays, "scatter_grads"), where)
    grads = {n: _to_t(n, jnp.stack(per_layer[n])) for n in SHARDED}

    small = [jnp.stack(gr[n]) for n in REPLICATED if n != "final_norm"] + [g_final[0], loss_part[0, :1]]
    spack, soffs = _pack(small, 0, SMALL_ROWS_MULT)
    sgath = _all_gather(spack, "gather_small_grads")
    ssum = _sum_slots(sgath, "sum_small_grads")
    names_small = [n for n in REPLICATED if n != "final_norm"] + ["final_norm", "loss"]
    for n, off, piece in zip(names_small, soffs, small):
        val = _unpack(ssum, off, piece.shape, 0)
        if n == "loss":
            loss = val[0]
        else:
            grads[n] = val

    delta, new_m, new_v = {}, {}, {}
    for n in SHARDED:
        two_d = lambda a: a.reshape(-1, a.shape[-1])
        d, nm, nv = _adamw(two_d(w_loc[n]), two_d(grads[n]), two_d(m_loc[n]), two_d(v_loc[n]), "adamw")
        delta[n], new_m[n], new_v[n] = (a.reshape(w_loc[n].shape) for a in (d, nm, nv))
    flat = lambda d: _pack([d[n] for n in REPLICATED], 0, SMALL_ROWS_MULT)
    (wf, aoffs), (gf, _), (mf, _), (vf, _) = flat(w_loc), flat(grads), flat(m_loc), flat(v_loc)
    for res, buf in zip((delta, new_m, new_v), _adamw(wf, gf, mf, vf, "adamw_replicated")):
        for n, off in zip(REPLICATED, aoffs):
            res[n] = _unpack(buf, off, w_loc[n].shape, 0)
    outs = [loss, grad_x] + [grads[n] for n in WEIGHTS]
    for res in (delta, new_m, new_v):
        outs += [res[n] for n in WEIGHTS]
    return tuple(outs)
```

```python
import functools
import math

import numpy as np
import jax
import jax.numpy as jnp
from jax import lax
from jax.experimental import pallas as pl
from jax.experimental.pallas import tpu as pltpu

F32 = jnp.float32
BF16 = jnp.bfloat16

N_DEV = 8
NORM_EPS = 1e-6
HEADS = 8
NOPE = 64
ROPE = 32
VDIM = 64
HEAD_PAD = 128
QK_DIM = NOPE + ROPE
ROPE_THETA = 10000.0
SG_GROUPS = 8
SG_GROUP_DIM = 64
SG_WIDTH = SG_GROUPS * SG_GROUP_DIM
SG_CHUNK = 128
CONV_K = 3
ADAM_LR, ADAM_B1, ADAM_B2, ADAM_EPS, ADAM_WD, ADAM_STEP = 0.001, 0.9, 0.999, 1e-08, 0.01, 10

LANE = 128
PACK_W = 1024
GRAD_ROWS_MULT = 512
SMALL_ROWS_MULT = 64
VMEM_LIMIT = 60 * 1024 * 1024

TILES = dict(ffn_fwd=512, ffn_bwd=256, ffn_dw=2048, mm=512, mm_tn=1024, ew=512, attn=1024, attn_wide=2048, sgu=512,
             adam=512)

NT = (((1,), (1,)), ((), ()))
NN = (((1,), (0,)), ((), ()))
TN = (((0,), (0,)), ((), ()))


def _dot(a, b, dims):
    return lax.dot_general(a, b, dims, preferred_element_type=F32)


def _cparams(*sem):
    return pltpu.CompilerParams(dimension_semantics=sem if sem else None, vmem_limit_bytes=VMEM_LIMIT)


def _tile(n, want):
    t = min(want, n)
    while n % t:
        t //= 2
    return t if t % 8 == 0 else n


def _lane_tile(n, cap):
    best = None
    for k in range(1, n // LANE + 1):
        t = k * LANE
        if n % t == 0 and t <= cap:
            best = t
    return best or n


def _row(v):
    return v.reshape(1, -1).astype(F32)


def _all_gather(block, name):
    R, W = block.shape

    def body(x_ref, out_ref, send_sems, recv_sems, local_sem):
        x, y, c = lax.axis_index("x"), lax.axis_index("y"), lax.axis_index("c")
        me, sibling = (x, y, c), (x, y, 1 - c)
        chips = [(1 - x, y), (x, 1 - y), (1 - x, 1 - y)]

        def slot(px, py, pc):
            return out_ref.at[4 * px + 2 * py + pc]

        def copy(k, blk, to, src=None):
            return pltpu.make_async_remote_copy(
                src_ref=slot(*blk) if src is None else src, dst_ref=slot(*blk),
                send_sem=send_sems.at[k], recv_sem=recv_sems.at[k],
                device_id=to, device_id_type=pl.DeviceIdType.MESH)

        mine = pltpu.make_async_copy(x_ref, slot(*me), local_sem)
        mine.start()
        first = [copy(0, me, sibling, src=x_ref)]
        first += [copy(1 + j, me, (*chip, c), src=x_ref) for j, chip in enumerate(chips)]
        for cp in first:
            cp.start()
        passed = [copy(4 + j, (*chip, c), sibling) for j, chip in enumerate(chips)]
        for j, chip in enumerate(chips):
            copy(1 + j, (*chip, c), me).wait_recv()
            passed[j].start()
        copy(0, sibling, me).wait_recv()
        for j, chip in enumerate(chips):
            copy(4 + j, (*chip, 1 - c), me).wait_recv()
        for cp in first + passed:
            cp.wait_send()
        mine.wait()

    return pl.pallas_call(
        body, name=name,
        out_shape=jax.ShapeDtypeStruct((N_DEV, R, W), block.dtype),
        in_specs=[pl.BlockSpec(memory_space=pl.ANY)],
        out_specs=pl.BlockSpec(memory_space=pl.ANY),
        scratch_shapes=[pltpu.SemaphoreType.DMA((7,)), pltpu.SemaphoreType.DMA((7,)), pltpu.SemaphoreType.DMA],
    )(block)


class _Exchange:
    def __init__(self, kind, arrays):
        self.kind, self.arrays = kind, list(arrays)
        if kind == "gather":
            (r, w), = [a.shape for a in self.arrays]
            self.rows = [r]
        else:
            self.rows = [a.shape[1] for a in self.arrays]
            w = self.arrays[0].shape[2]
        self.offs = [sum(self.rows[:i]) for i in range(len(self.rows))]
        self.n_in = len(self.arrays)
        self.out_shape = jax.ShapeDtypeStruct((N_DEV, sum(self.rows), w), self.arrays[0].dtype)
        self.in_specs = [pl.BlockSpec(memory_space=pl.ANY)] * self.n_in
        self.out_spec = pl.BlockSpec(memory_space=pl.ANY)
        self.out_specs, self.out_shapes = [self.out_spec], [self.out_shape]
        self.scratch = [pltpu.SemaphoreType.DMA((7,)), pltpu.SemaphoreType.DMA((7,)), pltpu.SemaphoreType.DMA]

    def _peers(self):
        x, y, c = lax.axis_index("x"), lax.axis_index("y"), lax.axis_index("c")
        me = 4 * x + 2 * y + c
        return me, [(k, (x ^ (k >> 2), y ^ ((k >> 1) & 1), c ^ (k & 1))) for k in range(1, N_DEV)]

    @staticmethod
    def _remote(src, dst, k, to, send_sems, recv_sems):
        return pltpu.make_async_remote_copy(
            src_ref=src, dst_ref=dst, send_sem=send_sems.at[k - 1], recv_sem=recv_sems.at[k - 1],
            device_id=to, device_id_type=pl.DeviceIdType.MESH)

    def start(self, s_refs, r_ref, send_sems, recv_sems, local_sem):
        me, peers = self._peers()
        for s_ref, off, r in zip(s_refs, self.offs, self.rows):
            src = s_ref if self.kind == "gather" else s_ref.at[me]
            pltpu.make_async_copy(src, r_ref.at[me, pl.ds(off, r)], local_sem).start()
        for k, to in peers:
            peer = 4 * to[0] + 2 * to[1] + to[2]
            for s_ref, off, r in zip(s_refs, self.offs, self.rows):
                src = s_ref if self.kind == "gather" else s_ref.at[peer]
                self._remote(src, r_ref.at[me, pl.ds(off, r)], k, to, send_sems, recv_sems).start()

    def wait(self, s_refs, r_ref, send_sems, recv_sems, local_sem):
        me, peers = self._peers()
        whole = r_ref.at[me]
        totals = [self._remote(whole, whole, k, to, send_sems, recv_sems) for k, to in peers]
        for cp in totals:
            cp.wait_recv()
        for cp in totals:
            cp.wait_send()
        pltpu.make_async_copy(whole, whole, local_sem).wait()


class _NoRider:
    arrays, in_specs, out_specs, out_shapes, scratch = [], [], [], [], []


_NO_RIDER = _NoRider()


def _ride(rider, refs, n_in, n_out, first, last):
    if rider is None:
        return refs, lambda: None
    k = rider.n_in
    s_refs = refs[n_in:n_in + k]
    r_ref = refs[n_in + k + n_out]
    sems = refs[-3:]
    own = refs[:n_in] + refs[n_in + k:n_in + k + n_out] + refs[n_in + k + n_out + 1:-3]

    @pl.when(first)
    def _():
        rider.start(s_refs, r_ref, *sems)

    def finish():
        @pl.when(last)
        def _():
            rider.wait(s_refs, r_ref, *sems)

    return own, finish


def _exchange(kind, arrays, name):
    ex = _Exchange(kind, arrays)

    def body(*refs):
        s_refs, r_ref, sems = refs[:ex.n_in], refs[ex.n_in], refs[ex.n_in + 1:]
        ex.start(s_refs, r_ref, *sems)
        ex.wait(s_refs, r_ref, *sems)

    return pl.pallas_call(
        body, name=name, out_shape=ex.out_shape, in_specs=ex.in_specs, out_specs=ex.out_spec,
        scratch_shapes=ex.scratch,
    )(*ex.arrays)


def _sum_slots(parts, name):
    _, R, W = parts.shape
    tr = _tile(R, TILES["adam"])

    def body(p_ref, o_ref):
        acc = p_ref[0].astype(F32)
        for s in range(1, N_DEV):
            acc = acc + p_ref[s].astype(F32)
        o_ref[...] = acc

    return pl.pallas_call(
        body, name=name, grid=(R // tr,),
        in_specs=[pl.BlockSpec((N_DEV, tr, W), lambda i: (0, i, 0))],
        out_specs=pl.BlockSpec((tr, W), lambda i: (i, 0)),
        out_shape=jax.ShapeDtypeStruct((R, W), F32),
        compiler_params=_cparams("parallel"),
    )(parts)


def _adamw(w, g, m, v, name):
    R, W = w.shape
    tr = _tile(R, TILES["adam"])
    c1 = 1.0 - ADAM_B1 ** ADAM_STEP
    c2 = 1.0 - ADAM_B2 ** ADAM_STEP

    def body(w_ref, g_ref, m_ref, v_ref, d_ref, nm_ref, nv_ref):
        g = g_ref[...]
        nm = ADAM_B1 * m_ref[...] + (1.0 - ADAM_B1) * g
        nv = ADAM_B2 * v_ref[...] + (1.0 - ADAM_B2) * (g * g)
        d_ref[...] = -ADAM_LR * ((nm / c1) / (jnp.sqrt(nv / c2) + ADAM_EPS) + ADAM_WD * w_ref[...])
        nm_ref[...] = nm
        nv_ref[...] = nv

    spec = pl.BlockSpec((tr, W), lambda i: (i, 0))
    return pl.pallas_call(
        body, name=name, grid=(R // tr,),
        in_specs=[spec] * 4, out_specs=[spec] * 3,
        out_shape=[jax.ShapeDtypeStruct((R, W), F32)] * 3,
        compiler_params=_cparams("parallel"),
    )(w, g, m, v)


def _mm(a, b, mode, name, out_dtype=BF16, res=None, scale=1.0, acol=None, kdim=None):
    if mode == "tn":
        S, M = a.shape
        N = b.shape[1]
        ts = _tile(S, TILES["mm_tn"])
        tmo = _lane_tile(M, 1024)

        def body(a_ref, b_ref, o_ref, acc):
            s = pl.program_id(1)

            @pl.when(s == 0)
            def _():
                acc[...] = jnp.zeros_like(acc)

            acc[...] += _dot(a_ref[...].astype(BF16), b_ref[...].astype(BF16), TN)

            @pl.when(s == pl.num_programs(1) - 1)
            def _():
                o_ref[...] = acc[...].astype(out_dtype)

        return pl.pallas_call(
            functools.partial(body), name=name, grid=(M // tmo, S // ts),
            in_specs=[pl.BlockSpec((ts, tmo), lambda i, s: (s, i)), pl.BlockSpec((ts, N), lambda i, s: (s, 0))],
            out_specs=pl.BlockSpec((tmo, N), lambda i, s: (i, 0)),
            out_shape=jax.ShapeDtypeStruct((M, N), out_dtype),
            scratch_shapes=[pltpu.VMEM((tmo, N), F32)],
            compiler_params=_cparams("parallel", "arbitrary"),
        )(a, b)

    M = a.shape[0]
    K = kdim if kdim is not None else a.shape[1]
    ac = 0 if acol is None else acol
    N = b.shape[1] if mode == "nn" else b.shape[0]
    tm = _tile(M, TILES["mm"])
    dims = NN if mode == "nn" else NT

    def body(*refs):
        if res is None:
            a_ref, b_ref, o_ref = refs
        else:
            a_ref, b_ref, r_ref, o_ref = refs
        acc = _dot(a_ref[...].astype(BF16), b_ref[...].astype(BF16), dims)
        if res is not None:
            acc = r_ref[...] + scale * acc
        o_ref[...] = acc.astype(out_dtype)

    in_specs = [pl.BlockSpec((tm, K), lambda i: (i, ac)), pl.BlockSpec(b.shape, lambda i: (0, 0))]
    args = [a, b]
    if res is not None:
        in_specs.append(pl.BlockSpec((tm, N), lambda i: (i, 0)))
        args.append(res)
    return pl.pallas_call(
        body, name=name, grid=(M // tm,),
        in_specs=in_specs, out_specs=pl.BlockSpec((tm, N), lambda i: (i, 0)),
        out_shape=jax.ShapeDtypeStruct((M, N), out_dtype),
        compiler_params=_cparams("parallel"),
    )(*args)


def _rms_fwd(x, gain, name, col=0, width=None):
    S = x.shape[0]
    W = width if width is not None else x.shape[1]
    tm = _tile(S, TILES["ew"])

    def body(x_ref, g_ref, o_ref):
        xv = x_ref[...].astype(F32)
        r = lax.rsqrt(jnp.mean(xv * xv, axis=-1, keepdims=True) + NORM_EPS)
        o_ref[...] = (xv * r * g_ref[...]).astype(BF16)

    return pl.pallas_call(
        body, name=name, grid=(S // tm,),
        in_specs=[pl.BlockSpec((tm, W), lambda i: (i, col)), pl.BlockSpec((1, W), lambda i: (0, 0))],
        out_specs=pl.BlockSpec((tm, W), lambda i: (i, 0)),
        out_shape=jax.ShapeDtypeStruct((S, W), BF16),
        compiler_params=_cparams("parallel"),
    )(x, _row(gain))


def _rms_bwd(dy, x, gain, name, col=0, res=None, out_dtype=F32):
    S, W = dy.shape
    tm = _tile(S, TILES["ew"])

    def body(*refs):
        if res is None:
            dy_ref, x_ref, g_ref, dx_ref, dg_ref = refs
        else:
            dy_ref, x_ref, g_ref, r_ref, dx_ref, dg_ref = refs

        @pl.when(pl.program_id(0) == 0)
        def _():
            dg_ref[...] = jnp.zeros_like(dg_ref)

        xv = x_ref[...].astype(F32)
        d = dy_ref[...].astype(F32)
        r = lax.rsqrt(jnp.mean(xv * xv, axis=-1, keepdims=True) + NORM_EPS)
        xhat = xv * r
        dg_ref[...] += jnp.sum(d * xhat, axis=0, keepdims=True)
        dxhat = d * g_ref[...]
        dx = r * (dxhat - xhat * jnp.mean(dxhat * xhat, axis=-1, keepdims=True))
        if res is not None:
            dx = dx + r_ref[...]
        dx_ref[...] = dx.astype(out_dtype)

    in_specs = [pl.BlockSpec((tm, W), lambda i: (i, 0)), pl.BlockSpec((tm, W), lambda i: (i, col)),
                pl.BlockSpec((1, W), lambda i: (0, 0))]
    args = [dy, x, _row(gain)]
    if res is not None:
        in_specs.append(pl.BlockSpec((tm, W), lambda i: (i, 0)))
        args.append(res)
    return pl.pallas_call(
        body, name=name, grid=(S // tm,),
        in_specs=in_specs,
        out_specs=[pl.BlockSpec((tm, W), lambda i: (i, 0)), pl.BlockSpec((1, W), lambda i: (0, 0))],
        out_shape=[jax.ShapeDtypeStruct((S, W), out_dtype), jax.ShapeDtypeStruct((1, W), F32)],
        compiler_params=_cparams("arbitrary"),
    )(*args)


def _silu_parts(a):
    s = jax.nn.sigmoid(a)
    return a * s, s * (1.0 + a * (1.0 - s))


def _ffn_fwd(x, gain, wg_t, wu_t, wd, name, rider=None):
    S, D = x.shape
    Fd = wd.shape[0]
    tm = _tile(S, TILES["ffn_fwd"])
    fc = _lane_tile(Fd, 512)

    def body(*refs):
        i = pl.program_id(0)
        own, finish = _ride(rider, refs, 5, 3, i == 0, i == pl.num_programs(0) - 1)
        x_ref, g_ref, wg_ref, wu_ref, wd_ref, o_ref, a_ref, b_ref = own
        xv = x_ref[...]
        r = lax.rsqrt(jnp.mean(xv * xv, axis=-1, keepdims=True) + NORM_EPS)
        h = (xv * r * g_ref[...]).astype(BF16)
        acc = jnp.zeros((tm, D), F32)
        for c in range(Fd // fc):
            sl = slice(c * fc, (c + 1) * fc)
            a = _dot(h, wg_ref[sl, :], NT)
            b = _dot(h, wu_ref[sl, :], NT)
            a_ref[:, sl] = a.astype(BF16)
            b_ref[:, sl] = b.astype(BF16)
            z = (a * jax.nn.sigmoid(a) * b).astype(BF16)
            acc = acc + _dot(z, wd_ref[sl, :], NN)
        o_ref[...] = xv + 0.5 * acc
        finish()

    wspec = pl.BlockSpec((Fd, D), lambda i: (0, 0), pipeline_mode=pl.Buffered(1))
    extra = rider or _NO_RIDER
    return pl.pallas_call(
        body, name=name, grid=(S // tm,),
        in_specs=[pl.BlockSpec((tm, D), lambda i: (i, 0)), pl.BlockSpec((1, D), lambda i: (0, 0)), wspec, wspec,
                  wspec] + extra.in_specs,
        out_specs=[pl.BlockSpec((tm, D), lambda i: (i, 0)), pl.BlockSpec((tm, Fd), lambda i: (i, 0)),
                   pl.BlockSpec((tm, Fd), lambda i: (i, 0))] + extra.out_specs,
        out_shape=[jax.ShapeDtypeStruct((S, D), F32), jax.ShapeDtypeStruct((S, Fd), BF16),
                   jax.ShapeDtypeStruct((S, Fd), BF16)] + extra.out_shapes,
        scratch_shapes=extra.scratch,
        compiler_params=_cparams("arbitrary" if rider else "parallel"),
    )(x, _row(gain), wg_t, wu_t, wd, *extra.arrays)


def _ffn_bwd(g, x, gain, a, b, wg_t, wu_t, wd, name, rider=None):
    S, D = x.shape
    Fd = wd.shape[0]
    tm = _tile(S, TILES["ffn_bwd"])
    fc = Fd

    def body(*refs):
        i = pl.program_id(0)
        own, finish = _ride(rider, refs, 8, 5, i == 0, i == pl.num_programs(0) - 1)
        g_ref, x_ref, gain_ref, a_ref, b_ref, wg_ref, wu_ref, wd_ref, dx_ref, dz_ref, h_ref, dy_ref, dg_ref = own

        @pl.when(i == 0)
        def _():
            dg_ref[...] = jnp.zeros_like(dg_ref)

        gv = g_ref[...]
        xv = x_ref[...]
        r = lax.rsqrt(jnp.mean(xv * xv, axis=-1, keepdims=True) + NORM_EPS)
        xhat = xv * r
        h_ref[...] = (xhat * gain_ref[...]).astype(BF16)
        dy = (0.5 * gv).astype(BF16)
        dy_ref[...] = dy
        dh = jnp.zeros((tm, D), F32)
        for c in range(Fd // fc):
            sl = slice(c * fc, (c + 1) * fc)
            av = a_ref[:, sl].astype(F32)
            bv = b_ref[:, sl].astype(F32)
            dz = _dot(dy, wd_ref[sl, :], NT).astype(BF16)
            dz_ref[:, sl] = dz
            dzf = dz.astype(F32)
            silu, dsilu = _silu_parts(av)
            da = (dzf * bv * dsilu).astype(BF16)
            db = (dzf * silu).astype(BF16)
            dh = dh + _dot(da, wg_ref[sl, :], NN) + _dot(db, wu_ref[sl, :], NN)
        dg_ref[...] += jnp.sum(dh * xhat, axis=0, keepdims=True)
        dxhat = dh * gain_ref[...]
        dx_ref[...] = gv + r * (dxhat - xhat * jnp.mean(dxhat * xhat, axis=-1, keepdims=True))
        finish()

    wspec = pl.BlockSpec((Fd, D), lambda i: (0, 0), pipeline_mode=pl.Buffered(1))
    row = pl.BlockSpec((tm, D), lambda i: (i, 0))
    wide = pl.BlockSpec((tm, Fd), lambda i: (i, 0))
    extra = rider or _NO_RIDER
    return pl.pallas_call(
        body, name=name, grid=(S // tm,),
        in_specs=[row, row, pl.BlockSpec((1, D), lambda i: (0, 0)), wide, wide, wspec, wspec, wspec] + extra.in_specs,
        out_specs=[row, wide, row, row, pl.BlockSpec((1, D), lambda i: (0, 0))] + extra.out_specs,
        out_shape=[jax.ShapeDtypeStruct((S, D), F32), jax.ShapeDtypeStruct((S, Fd), BF16),
                   jax.ShapeDtypeStruct((S, D), BF16), jax.ShapeDtypeStruct((S, D), BF16),
                   jax.ShapeDtypeStruct((1, D), F32)] + extra.out_shapes,
        scratch_shapes=extra.scratch,
        compiler_params=_cparams("arbitrary"),
    )(g, x, _row(gain), a, b, wg_t, wu_t, wd, *extra.arrays)


def _ffn_dw(a, b, dz, h, dy, name):
    S, Fd = a.shape
    D = h.shape[1]
    ts = _tile(S, TILES["ffn_dw"])
    tf = _lane_tile(Fd, 256)

    def body(a_ref, b_ref, dz_ref, h_ref, dy_ref, og_ref, ou_ref, od_ref, accg, accu, accd):
        s = pl.program_id(1)

        @pl.when(s == 0)
        def _():
            accg[...] = jnp.zeros_like(accg)
            accu[...] = jnp.zeros_like(accu)
            accd[...] = jnp.zeros_like(accd)

        av = a_ref[...].astype(F32)
        bv = b_ref[...].astype(F32)
        dzf = dz_ref[...].astype(F32)
        silu, dsilu = _silu_parts(av)
        da = (dzf * bv * dsilu).astype(BF16)
        db = (dzf * silu).astype(BF16)
        z = (silu * bv).astype(BF16)
        hv = h_ref[...]
        accg[...] += _dot(da, hv, TN)
        accu[...] += _dot(db, hv, TN)
        accd[...] += _dot(z, dy_ref[...], TN)

        @pl.when(s == pl.num_programs(1) - 1)
        def _():
            og_ref[...] = accg[...].astype(BF16)
            ou_ref[...] = accu[...].astype(BF16)
            od_ref[...] = accd[...].astype(BF16)

    wide = pl.BlockSpec((ts, tf), lambda f, s: (s, f))
    row = pl.BlockSpec((ts, D), lambda f, s: (s, 0))
    out = pl.BlockSpec((tf, D), lambda f, s: (f, 0))
    return pl.pallas_call(
        body, name=name, grid=(Fd // tf, S // ts),
        in_specs=[wide, wide, wide, row, row], out_specs=[out, out, out],
        out_shape=[jax.ShapeDtypeStruct((Fd, D), BF16)] * 3,
        scratch_shapes=[pltpu.VMEM((tf, D), F32)] * 3,
        compiler_params=_cparams("parallel", "arbitrary"),
    )(a, b, dz, h, dy)


def _loss_head(x, target, gain, name):
    S, D = x.shape
    tm = _tile(S, TILES["ew"])

    def body(x_ref, t_ref, g_ref, dx_ref, dg_ref, loss_ref):
        @pl.when(pl.program_id(0) == 0)
        def _():
            dg_ref[...] = jnp.zeros_like(dg_ref)
            loss_ref[...] = jnp.zeros_like(loss_ref)

        xv = x_ref[...]
        r = lax.rsqrt(jnp.mean(xv * xv, axis=-1, keepdims=True) + NORM_EPS)
        xhat = xv * r
        e = xhat * g_ref[...] - t_ref[...]
        per_tok = jnp.mean(e * e, axis=-1, keepdims=True)
        loss_ref[...] += jnp.broadcast_to(0.5 * jnp.sum(per_tok, axis=0, keepdims=True), (1, LANE))
        dy = e * (1.0 / D)
        dg_ref[...] += jnp.sum(dy * xhat, axis=0, keepdims=True)
        dxhat = dy * g_ref[...]
        dx_ref[...] = r * (dxhat - xhat * jnp.mean(dxhat * xhat, axis=-1, keepdims=True))

    row = pl.BlockSpec((tm, D), lambda i: (i, 0))
    return pl.pallas_call(
        body, name=name, grid=(S // tm,),
        in_specs=[row, row, pl.BlockSpec((1, D), lambda i: (0, 0))],
        out_specs=[row, pl.BlockSpec((1, D), lambda i: (0, 0)), pl.BlockSpec((1, LANE), lambda i: (0, 0))],
        out_shape=[jax.ShapeDtypeStruct((S, D), F32), jax.ShapeDtypeStruct((1, D), F32),
                   jax.ShapeDtypeStruct((1, LANE), F32)],
        compiler_params=_cparams("arbitrary"),
    )(x, target, _row(gain))


def _shift_down(u, halo, k, rows):
    out = pltpu.roll(u, k, 0)
    for j in range(k):
        out = jnp.where(rows == j, halo[8 - k + j:8 - k + j + 1, :], out)
    return out


def _shift_up(u, halo, k, rows, n):
    out = pltpu.roll(u, n - k, 0)
    for j in range(k):
        out = jnp.where(rows == n - k + j, halo[j:j + 1, :], out)
    return out


def _conv_fwd(p, cw, name):
    S, W3 = p.shape
    W = W3 // 3
    tm = _tile(S, TILES["ew"])
    hb = tm // 8

    def body(p_ref, ph_ref, w_ref, v_ref):
        i = pl.program_id(0)
        bg = p_ref[:, 0:W].astype(F32)
        u = p_ref[:, W:2 * W].astype(F32) * p_ref[:, 2 * W:3 * W].astype(F32)
        uh = ph_ref[:, W:2 * W].astype(F32) * ph_ref[:, 2 * W:3 * W].astype(F32)
        uh = jnp.where(i > 0, uh, 0.0)
        rows = lax.broadcasted_iota(jnp.int32, (tm, 1), 0)
        u1 = _shift_down(u, uh, 1, rows)
        u2 = _shift_down(u, uh, 2, rows)
        y = w_ref[0:1, :] * u2 + w_ref[1:2, :] * u1 + w_ref[2:3, :] * u
        v_ref[...] = (bg * y).astype(BF16)

    return pl.pallas_call(
        body, name=name, grid=(S // tm,),
        in_specs=[pl.BlockSpec((tm, W3), lambda i: (i, 0)),
                  pl.BlockSpec((8, W3), lambda i: (jnp.maximum(i * hb - 1, 0), 0)),
                  pl.BlockSpec((8, W), lambda i: (0, 0))],
        out_specs=pl.BlockSpec((tm, W), lambda i: (i, 0)),
        out_shape=jax.ShapeDtypeStruct((S, W), BF16),
        compiler_params=_cparams("parallel"),
    )(p, p, cw)


def _conv_bwd(dv, p, cw, name):
    S, W3 = p.shape
    W = W3 // 3
    tm = _tile(S, TILES["ew"])
    hb = tm // 8
    last = S // 8 - 1

    def body(dv_ref, dvn_ref, p_ref, pp_ref, pn_ref, w_ref, dp_ref, dw_ref):
        i = pl.program_id(0)
        n = pl.num_programs(0)

        @pl.when(i == 0)
        def _():
            dw_ref[...] = jnp.zeros_like(dw_ref)

        bg = p_ref[:, 0:W].astype(F32)
        cg = p_ref[:, W:2 * W].astype(F32)
        zz = p_ref[:, 2 * W:3 * W].astype(F32)
        u = cg * zz
        uh = pp_ref[:, W:2 * W].astype(F32) * pp_ref[:, 2 * W:3 * W].astype(F32)
        uh = jnp.where(i > 0, uh, 0.0)
        rows = lax.broadcasted_iota(jnp.int32, (tm, 1), 0)
        u1 = _shift_down(u, uh, 1, rows)
        u2 = _shift_down(u, uh, 2, rows)
        w0, w1, w2 = w_ref[0:1, :], w_ref[1:2, :], w_ref[2:3, :]
        y = w0 * u2 + w1 * u1 + w2 * u
        dvv = dv_ref[...].astype(F32)
        dy = dvv * bg
        dyh = dvn_ref[...].astype(F32) * pn_ref[:, 0:W].astype(F32)
        dyh = jnp.where(i < n - 1, dyh, 0.0)
        d1 = _shift_up(dy, dyh, 1, rows, tm)
        d2 = _shift_up(dy, dyh, 2, rows, tm)
        du = w2 * dy + w1 * d1 + w0 * d2
        dp_ref[:, 0:W] = (dvv * y).astype(BF16)
        dp_ref[:, W:2 * W] = (du * zz).astype(BF16)
        dp_ref[:, 2 * W:3 * W] = (du * cg).astype(BF16)
        dw_ref[0:1, :] += jnp.sum(dy * u2, axis=0, keepdims=True)
        dw_ref[1:2, :] += jnp.sum(dy * u1, axis=0, keepdims=True)
        dw_ref[2:3, :] += jnp.sum(dy * u, axis=0, keepdims=True)

    return pl.pallas_call(
        body, name=name, grid=(S // tm,),
        in_specs=[pl.BlockSpec((tm, W), lambda i: (i, 0)),
                  pl.BlockSpec((8, W), lambda i: (jnp.minimum((i + 1) * hb, last), 0)),
                  pl.BlockSpec((tm, W3), lambda i: (i, 0)),
                  pl.BlockSpec((8, W3), lambda i: (jnp.maximum(i * hb - 1, 0), 0)),
                  pl.BlockSpec((8, W3), lambda i: (jnp.minimum((i + 1) * hb, last), 0)),
                  pl.BlockSpec((8, W), lambda i: (0, 0))],
        out_specs=[pl.BlockSpec((tm, W3), lambda i: (i, 0)), pl.BlockSpec((8, W), lambda i: (0, 0))],
        out_shape=[jax.ShapeDtypeStruct((S, W3), BF16), jax.ShapeDtypeStruct((8, W), F32)],
        compiler_params=_cparams("arbitrary"),
    )(dv, dv, p, p, p, cw)


def _rope_swap(r, lane):
    mid = NOPE + ROPE // 2
    first = (lane >= NOPE) & (lane < mid)
    second = (lane >= mid) & (lane < QK_DIM)
    return jnp.where(first, pltpu.roll(r, HEAD_PAD - ROPE // 2, 1), jnp.where(second, pltpu.roll(r, ROPE // 2, 1), 0.0))


def _rope_fwd(q_big, kv_big, proj, kr_col, ct, st, name):
    S = q_big.shape[0]
    HW = HEADS * HEAD_PAD
    tm = _tile(S, TILES["ew"])

    def body(q_ref, k_ref, v_ref, kr_ref, ct_ref, st_ref, qo_ref, ko_ref, vo_ref):
        lane = lax.broadcasted_iota(jnp.int32, (1, HEAD_PAD), 1)
        ctv, stv = ct_ref[...], st_ref[...]
        krr = pltpu.roll(kr_ref[...].astype(F32), NOPE, 1)
        kro = krr * ctv + _rope_swap(krr, lane) * stv
        for h in range(HEADS):
            sl = slice(h * HEAD_PAD, (h + 1) * HEAD_PAD)
            qh = q_ref[:, sl].astype(F32)
            qo_ref[:, sl] = (qh * ctv + _rope_swap(qh, lane) * stv).astype(BF16)
            ko_ref[:, sl] = (k_ref[:, sl].astype(F32) + kro).astype(BF16)
            vo_ref[:, sl] = jnp.where(lane == VDIM, 1.0, v_ref[:, sl].astype(F32)).astype(BF16)

    wide = pl.BlockSpec((tm, HW), lambda i: (i, 0))
    narrow = pl.BlockSpec((tm, HEAD_PAD), lambda i: (i, 0))
    return pl.pallas_call(
        body, name=name, grid=(S // tm,),
        in_specs=[wide, wide, pl.BlockSpec((tm, HW), lambda i: (i, 1)),
                  pl.BlockSpec((tm, HEAD_PAD), lambda i: (i, kr_col)), narrow, narrow],
        out_specs=[wide, wide, wide],
        out_shape=[jax.ShapeDtypeStruct((S, HW), BF16)] * 3,
        compiler_params=_cparams("parallel"),
    )(q_big, kv_big, kv_big, proj, ct, st)


def _rope_bwd(dq, dk, dv, ct, st, name):
    S = dq.shape[0]
    HW = HEADS * HEAD_PAD
    tm = _tile(S, TILES["ew"])

    def body(dq_ref, dk_ref, dv_ref, ct_ref, st_ref, oq_ref, okv_ref, okr_ref):
        lane = lax.broadcasted_iota(jnp.int32, (1, HEAD_PAD), 1)
        ctv, stv = ct_ref[...], st_ref[...]
        acc = jnp.zeros((tm, HEAD_PAD), F32)
        for h in range(HEADS):
            sl = slice(h * HEAD_PAD, (h + 1) * HEAD_PAD)
            d = dq_ref[:, sl].astype(F32)
            oq_ref[:, sl] = (d * ctv + _rope_swap(d * stv, lane)).astype(BF16)
            d = dk_ref[:, sl].astype(F32)
            okv_ref[:, sl] = jnp.where(lane < NOPE, d, 0.0).astype(BF16)
            acc = acc + jnp.where(lane >= NOPE, d * ctv + _rope_swap(d * stv, lane), 0.0)
        okv_ref[:, HW:2 * HW] = dv_ref[...].astype(BF16)
        okr_ref[...] = pltpu.roll(acc, HEAD_PAD - NOPE, 1).astype(BF16)

    wide = pl.BlockSpec((tm, HW), lambda i: (i, 0))
    narrow = pl.BlockSpec((tm, HEAD_PAD), lambda i: (i, 0))
    return pl.pallas_call(
        body, name=name, grid=(S // tm,),
        in_specs=[wide, wide, wide, narrow, narrow],
        out_specs=[wide, pl.BlockSpec((tm, 2 * HW), lambda i: (i, 0)), narrow],
        out_shape=[jax.ShapeDtypeStruct((S, HW), BF16), jax.ShapeDtypeStruct((S, 2 * HW), BF16),
                   jax.ShapeDtypeStruct((S, HEAD_PAD), BF16)],
        compiler_params=_cparams("parallel"),
    )(dq, dk, dv, ct, st)


def _pairs(n, by_key):
    if by_key:
        pr = [(i, j) for j in range(n) for i in range(j, n)]
    else:
        pr = [(i, j) for i in range(n) for j in range(i + 1)]
    qi = np.array([p[0] for p in pr], np.int32)
    kj = np.array([p[1] for p in pr], np.int32)
    return jnp.asarray(qi), jnp.asarray(kj)


_LOG2E = 1.4426950408889634
_LN2 = 0.6931471805599453


def _tile_mask(t):
    return lax.broadcasted_iota(jnp.int32, (t, t), 1) <= lax.broadcasted_iota(jnp.int32, (t, t), 0)


def _attn_fwd(q, k, v, name, tile, heads_per_step=1):
    S = q.shape[0]
    HW = HEADS * HEAD_PAD
    t = _tile(S, tile)
    n = S // t
    qi, kj = _pairs(n, by_key=False)
    c = (QK_DIM ** -0.5) * _LOG2E
    W = heads_per_step * HEAD_PAD

    def body(qi_ref, kj_ref, q_ref, k_ref, v_ref, o_ref, lse_ref, m_s, acc_s):
        p_id = pl.program_id(1)
        i, j = qi_ref[p_id], kj_ref[p_id]
        lanes = [slice(u * HEAD_PAD, (u + 1) * HEAD_PAD) for u in range(heads_per_step)]

        @pl.when(j == 0)
        def _():
            m_s[...] = jnp.full_like(m_s, -jnp.inf)
            acc_s[...] = jnp.zeros_like(acc_s)

        def step(on_diagonal):
            for u, sl in enumerate(lanes):
                s = _dot(q_ref[:, sl], k_ref[:, sl], NT)
                if on_diagonal:
                    s = jnp.where(_tile_mask(t), s, -jnp.inf)
                m_old = m_s[u]
                m_new = jnp.maximum(m_old, jnp.max(s, axis=-1, keepdims=True))
                p = jnp.exp2((s - m_new) * c).astype(BF16)
                acc_s[:, sl] = jnp.exp2((m_old - m_new) * c) * acc_s[:, sl] + _dot(p, v_ref[:, sl], NN)
                m_s[u] = m_new

        @pl.when(i == j)
        def _():
            step(True)

        @pl.when(i != j)
        def _():
            step(False)

        @pl.when(j == i)
        def _():
            for u, sl in enumerate(lanes):
                acc = acc_s[:, sl]
                l = acc[:, VDIM:VDIM + 1]
                o_ref[:, sl] = (acc * (1.0 / l)).astype(BF16)
                lse_ref[:, sl] = jnp.broadcast_to(m_s[u] * c + jnp.log2(l), (t, HEAD_PAD))

    qspec = pl.BlockSpec((t, W), lambda h, p, qi, kj: (qi[p], h))
    kspec = pl.BlockSpec((t, W), lambda h, p, qi, kj: (kj[p], h))
    grid_spec = pltpu.PrefetchScalarGridSpec(
        num_scalar_prefetch=2, grid=(HEADS // heads_per_step, int(qi.shape[0])),
        in_specs=[qspec, kspec, kspec], out_specs=[qspec, qspec],
        scratch_shapes=[pltpu.VMEM((heads_per_step, t, 1), F32), pltpu.VMEM((t, W), F32)])
    return pl.pallas_call(
        body, name=name, grid_spec=grid_spec,
        out_shape=[jax.ShapeDtypeStruct((S, HW), BF16), jax.ShapeDtypeStruct((S, HW), F32)],
        compiler_params=_cparams("parallel", "arbitrary"),
    )(qi, kj, q, k, v)


def _attn_bwd(q, k, v, o, do, lse2, name, split_diagonal=False):
    S = q.shape[0]
    HW = HEADS * HEAD_PAD
    t = _tile(S, TILES["attn"])
    n = S // t
    qi, kj = _pairs(n, by_key=True)
    scale = QK_DIM ** -0.5

    def body(qi_ref, kj_ref, q_ref, k_ref, v_ref, o_ref, do_ref, lse_ref, dq_ref, dk_ref, dv_ref, dk_s, dv_s):
        p_id = pl.program_id(1)
        i, j = qi_ref[p_id], kj_ref[p_id]

        @pl.when(p_id == 0)
        def _():
            dq_ref[...] = jnp.zeros_like(dq_ref)

        @pl.when(i == j)
        def _():
            dk_s[...] = jnp.zeros_like(dk_s)
            dv_s[...] = jnp.zeros_like(dv_s)

        def step(mask):
            qv, kv, vv = q_ref[...], k_ref[...], v_ref[...]
            dov = do_ref[...]
            p = jnp.exp(_dot(qv, kv, NT) * scale - lse_ref[:, 0:1] * _LN2)
            if mask is not None:
                p = jnp.where(mask(), p, 0.0)
            delta = jnp.sum(dov.astype(F32) * o_ref[...].astype(F32), axis=-1, keepdims=True)
            dv_s[...] += _dot(p.astype(BF16), dov, TN)
            ds = (p * (_dot(dov, vv, NT) - delta) * scale).astype(BF16)
            dk_s[...] += _dot(ds, qv, TN)
            rows = pl.ds(pl.multiple_of(i * t, t), t)
            dq_ref[rows, :] += _dot(ds, kv, NN)

        def by_index():
            rows_i = lax.broadcasted_iota(jnp.int32, (t, t), 0) + i * t
            cols_j = lax.broadcasted_iota(jnp.int32, (t, t), 1) + j * t
            return cols_j <= rows_i

        if split_diagonal:
            @pl.when(i == j)
            def _():
                step(lambda: _tile_mask(t))

            @pl.when(i != j)
            def _():
                step(None)
        else:
            step(by_index)

        @pl.when(i == n - 1)
        def _():
            dk_ref[...] = dk_s[...]
            dv_ref[...] = dv_s[...]

    qspec = pl.BlockSpec((t, HEAD_PAD), lambda h, p, qi, kj: (qi[p], h))
    kspec = pl.BlockSpec((t, HEAD_PAD), lambda h, p, qi, kj: (kj[p], h))
    grid_spec = pltpu.PrefetchScalarGridSpec(
        num_scalar_prefetch=2, grid=(HEADS, int(qi.shape[0])),
        in_specs=[qspec, kspec, kspec, qspec, qspec, qspec],
        out_specs=[pl.BlockSpec((S, HEAD_PAD), lambda h, p, qi, kj: (0, h)), kspec, kspec],
        scratch_shapes=[pltpu.VMEM((t, HEAD_PAD), F32), pltpu.VMEM((t, HEAD_PAD), F32)])
    return pl.pallas_call(
        body, name=name, grid_spec=grid_spec,
        out_shape=[jax.ShapeDtypeStruct((S, HW), F32)] * 3,
        compiler_params=_cparams("parallel", "arbitrary"),
    )(qi, kj, q, k, v, o, do, lse2)


_SQRT_HALF = 0.7071067811865476
_INV_SQRT_2PI = 0.3989422804014327


def _sg_select(r, grp):
    out = jnp.where(grp == 0, r[0:SG_CHUNK, :], 0.0)
    for g in range(1, SG_GROUPS):
        out = out + jnp.where(grp == g, r[g * SG_CHUNK:(g + 1) * SG_CHUNK, :], 0.0)
    return out


def _sgu_fwd(proj, gain, wstack, bmat, name):
    S = proj.shape[0]
    W = SG_WIDTH
    tm = _tile(S, TILES["sgu"])

    def body(z_ref, g_ref, w_ref, b_ref, o_ref):
        z = z_ref[...].astype(F32)
        zg = 0.5 * z * (1.0 + lax.erf(z * _SQRT_HALF))
        u, vv = zg[:, 0:W], zg[:, W:2 * W]
        r = lax.rsqrt(jnp.mean(vv * vv, axis=-1, keepdims=True) + NORM_EPS)
        vn = (vv * r * g_ref[...]).astype(BF16)
        grp = lax.broadcasted_iota(jnp.int32, (1, W), 1) // SG_GROUP_DIM
        for c in range(tm // SG_CHUNK):
            sl = slice(c * SG_CHUNK, (c + 1) * SG_CHUNK)
            mixed = _sg_select(_dot(w_ref[...], vn[sl, :], NN), grp) + b_ref[...]
            o_ref[sl, :] = (u[sl, :] * mixed).astype(BF16)

    return pl.pallas_call(
        body, name=name, grid=(S // tm,),
        in_specs=[pl.BlockSpec((tm, 2 * W), lambda i: (i, 0)), pl.BlockSpec((1, W), lambda i: (0, 0)),
                  pl.BlockSpec(wstack.shape, lambda i: (0, 0)), pl.BlockSpec(bmat.shape, lambda i: (0, 0))],
        out_specs=pl.BlockSpec((tm, W), lambda i: (i, 0)),
        out_shape=jax.ShapeDtypeStruct((S, W), BF16),
        compiler_params=_cparams("parallel"),
    )(proj, _row(gain), wstack, bmat)


def _sgu_bwd(dsg, proj, gain, wstack, wtstack, bmat, gsum, name):
    S = proj.shape[0]
    W = SG_WIDTH
    tm = _tile(S, TILES["sgu"])
    GS = SG_GROUPS * SG_CHUNK

    def body(d_ref, z_ref, g_ref, w_ref, wt_ref, b_ref, e_ref, dz_ref, dw_ref, db_ref, dg_ref, dw_s, db_s):
        i = pl.program_id(0)

        @pl.when(i == 0)
        def _():
            dw_s[...] = jnp.zeros_like(dw_s)
            db_s[...] = jnp.zeros_like(db_s)
            dg_ref[...] = jnp.zeros_like(dg_ref)

        z = z_ref[...].astype(F32)
        cdf = 0.5 * (1.0 + lax.erf(z * _SQRT_HALF))
        zg = z * cdf
        u, vv = zg[:, 0:W], zg[:, W:2 * W]
        r = lax.rsqrt(jnp.mean(vv * vv, axis=-1, keepdims=True) + NORM_EPS)
        vhat = vv * r
        vn = (vhat * g_ref[...]).astype(BF16)
        grp = lax.broadcasted_iota(jnp.int32, (1, W), 1) // SG_GROUP_DIM
        d = d_ref[...].astype(F32)
        du_parts, dvn_parts = [], []
        for c in range(tm // SG_CHUNK):
            sl = slice(c * SG_CHUNK, (c + 1) * SG_CHUNK)
            vc = vn[sl, :]
            mixed = _sg_select(_dot(w_ref[...], vc, NN), grp) + b_ref[...]
            dc = d[sl, :]
            du_parts.append(dc * mixed)
            dmix = dc * u[sl, :]
            db_s[...] += dmix
            dmb = dmix.astype(BF16)
            dvn_parts.append(_sg_select(_dot(wt_ref[...], dmb, NN), grp))
            astack = jnp.concatenate([jnp.where(grp == g, dmb, jnp.zeros_like(dmb)) for g in range(SG_GROUPS)], axis=0)
            dw_s[...] += _dot(astack, vc, NT)
        du = jnp.concatenate(du_parts, axis=0)
        dvn = jnp.concatenate(dvn_parts, axis=0)
        dg_ref[...] += jnp.sum(dvn * vhat, axis=0, keepdims=True)
        dvhat = dvn * g_ref[...]
        dvv = r * (dvhat - vhat * jnp.mean(dvhat * vhat, axis=-1, keepdims=True))
        dgelu = cdf + z * (_INV_SQRT_2PI * jnp.exp(-0.5 * z * z))
        dz_ref[:, 0:W] = (du * dgelu[:, 0:W]).astype(BF16)
        dz_ref[:, W:2 * W] = (dvv * dgelu[:, W:2 * W]).astype(BF16)

        @pl.when(i == pl.num_programs(0) - 1)
        def _():
            dw_ref[...] = dw_s[...]
            db_ref[...] = lax.dot_general(db_s[...], e_ref[...], NN, precision=lax.Precision.HIGHEST,
                                          preferred_element_type=F32)

    full = lambda a: pl.BlockSpec(a.shape, lambda i: (0, 0))
    return pl.pallas_call(
        body, name=name, grid=(S // tm,),
        in_specs=[pl.BlockSpec((tm, W), lambda i: (i, 0)), pl.BlockSpec((tm, 2 * W), lambda i: (i, 0)),
                  pl.BlockSpec((1, W), lambda i: (0, 0)), full(wstack), full(wtstack), full(bmat), full(gsum)],
        out_specs=[pl.BlockSpec((tm, 2 * W), lambda i: (i, 0)), pl.BlockSpec((GS, SG_CHUNK), lambda i: (0, 0)),
                   pl.BlockSpec((SG_CHUNK, LANE), lambda i: (0, 0)), pl.BlockSpec((1, W), lambda i: (0, 0))],
        out_shape=[jax.ShapeDtypeStruct((S, 2 * W), BF16), jax.ShapeDtypeStruct((GS, SG_CHUNK), F32),
                   jax.ShapeDtypeStruct((SG_CHUNK, LANE), F32), jax.ShapeDtypeStruct((1, W), F32)],
        scratch_shapes=[pltpu.VMEM((GS, SG_CHUNK), F32), pltpu.VMEM((SG_CHUNK, W), F32)],
        compiler_params=_cparams("arbitrary"),
    )(dsg, proj, _row(gain), wstack, wtstack, bmat, gsum)


WEIGHTS = ['ffn_pre_norm', 'ffn_pre_w_gate', 'ffn_pre_w_up', 'ffn_pre_w_down', 'mix_norm', 'ffn_post_norm',
           'ffn_post_w_gate', 'ffn_post_w_up', 'ffn_post_w_down', 'even_w_in', 'q_norm', 'w_uq', 'kv_norm', 'w_ukv',
           'sg_norm', 'sg_w', 'sg_b', 'even_w_out', 'conv_w_in', 'conv_w', 'conv_w_out', 'final_norm']
SHARD_AXIS = dict(ffn_pre_w_gate=2, ffn_pre_w_up=2, ffn_pre_w_down=1, ffn_post_w_gate=2, ffn_post_w_up=2,
                  ffn_post_w_down=1, even_w_in=2, w_uq=2, w_ukv=2, even_w_out=1, conv_w_in=2, conv_w=2, conv_w_out=1)
SHARDED = [n for n in WEIGHTS if n in SHARD_AXIS]
REPLICATED = [n for n in WEIGHTS if n not in SHARD_AXIS]


def _to_t(name, w):
    return jnp.swapaxes(w, 1, 2) if SHARD_AXIS[name] == 2 else w


def _rows_of(n):
    return -(-n // PACK_W)


def _pad_rows(a, mult, axis):
    r = a.shape[axis]
    extra = (-r) % mult
    if extra == 0:
        return a
    pad = [(0, 0)] * a.ndim
    pad[axis] = (0, extra)
    return jnp.pad(a, pad)


def _flat_rows(a, lead):
    flat = a.reshape(a.shape[:lead] + (-1,))
    n = flat.shape[-1]
    flat = _pad_rows(flat, PACK_W, lead)
    return flat.reshape(a.shape[:lead] + (_rows_of(n), PACK_W))


def _pack(pieces, lead, mult, piece_mult=1):
    rows, offs, off = [], [], 0
    for p in pieces:
        r = _pad_rows(_flat_rows(p, lead), piece_mult, lead)
        rows.append(r)
        offs.append(off)
        off += r.shape[lead]
    return _pad_rows(jnp.concatenate(rows, axis=lead), mult, lead), offs


def _unpack(buf, off, shape, lead):
    n = math.prod(shape)
    r = _rows_of(n)
    piece = lax.slice_in_dim(buf, off, off + r, axis=lead)
    piece = piece.reshape(buf.shape[:lead] + (r * PACK_W,))
    piece = lax.slice_in_dim(piece, 0, n, axis=lead)
    return piece.reshape(buf.shape[:lead] + tuple(shape))


def _head_pad(w, per_head, keep):
    k = w.shape[-1]
    w = w.reshape(HEADS, per_head, k)[:, keep[0]:keep[1]]
    w = jnp.pad(w, ((0, 0), (0, HEAD_PAD - (keep[1] - keep[0])), (0, 0)))
    return w.reshape(HEADS * HEAD_PAD, k)


def _head_unpad(w, n):
    return w.reshape(HEADS, HEAD_PAD, w.shape[-1])[:, :n]


def kernel(x, positions, ffn_pre_norm, ffn_pre_w_gate, ffn_pre_w_up, ffn_pre_w_down, mix_norm, ffn_post_norm, ffn_post_w_gate, ffn_post_w_up, ffn_post_w_down, even_w_in, q_norm, w_uq, kv_norm, w_ukv, sg_norm, sg_w, sg_b, even_w_out, conv_w_in, conv_w, conv_w_out, final_norm, loss_target, m_ffn_pre_norm, m_ffn_pre_w_gate, m_ffn_pre_w_up, m_ffn_pre_w_down, m_mix_norm, m_ffn_post_norm, m_ffn_post_w_gate, m_ffn_post_w_up, m_ffn_post_w_down, m_even_w_in, m_q_norm, m_w_uq, m_kv_norm, m_w_ukv, m_sg_norm, m_sg_w, m_sg_b, m_even_w_out, m_conv_w_in, m_conv_w, m_conv_w_out, m_final_norm, v_ffn_pre_norm, v_ffn_pre_w_gate, v_ffn_pre_w_up, v_ffn_pre_w_down, v_mix_norm, v_ffn_post_norm, v_ffn_post_w_gate, v_ffn_post_w_up, v_ffn_post_w_down, v_even_w_in, v_q_norm, v_w_uq, v_kv_norm, v_w_ukv, v_sg_norm, v_sg_w, v_sg_b, v_even_w_out, v_conv_w_in, v_conv_w, v_conv_w_out, v_final_norm):
    given = dict(locals())
    w_loc = {n: given[n] for n in WEIGHTS}
    m_loc = {n: given["m_" + n] for n in WEIGHTS}
    v_loc = {n: given["v_" + n] for n in WEIGHTS}

    S, D = x.shape[1], x.shape[2]
    depth = ffn_pre_norm.shape[0]
    QL, KVL = q_norm.shape[1], kv_norm.shape[1]
    ZW = 2 * SG_WIDTH
    assert x.shape[0] == 1 and ZW % KVL == 0 and (ZW + KVL) % HEAD_PAD == 0 and (ZW + KVL + 2 * HEAD_PAD) % QL == 0
    col_ckv = ZW // KVL
    col_kr = (ZW + KVL) // HEAD_PAD
    col_cq = (ZW + KVL + 2 * HEAD_PAD) // QL

    t_loc = {n: _to_t(n, w_loc[n]) for n in SHARDED}
    full = {n: {} for n in SHARDED}

    def ffn_keys(kind, l):
        return [("ffn_%s_w_%s" % (kind, part), l) for part in ("gate", "up", "down")]

    def mixer_keys(l):
        names = ("even_w_in", "w_uq", "w_ukv", "even_w_out") if l % 2 == 0 else ("conv_w_in", "conv_w", "conv_w_out")
        return [(n, l // 2) for n in names]

    def local_pack(keys):
        return _pack([t_loc[n][l].astype(BF16) for n, l in keys], 0, 16, piece_mult=16)

    def take_gathered(gathered, keys, offs):
        for (n, l), off in zip(keys, offs):
            piece = _unpack(gathered, off, t_loc[n].shape[1:], 1)
            full[n][l] = piece.reshape(N_DEV * piece.shape[1], piece.shape[2])

    def gather_rider(keys):
        pack, offs = local_pack(keys)
        return _Exchange("gather", [pack]), offs

    first_keys = ffn_keys("pre", 0)
    pack0, offs0 = local_pack(first_keys)
    take_gathered(_all_gather(pack0, "gather_weights"), first_keys, offs0)

    tril = jnp.tril(jnp.ones((SG_CHUNK, SG_CHUNK), F32))
    even_ops = {}

    def even_operands(e):
        if e not in even_ops:
            wi = full["even_w_in"][e]
            zrow = lambda k: jnp.zeros((k, D), BF16)
            ops = dict(win_pad=jnp.concatenate(
                [wi[QL + KVL + ROPE:], wi[QL:QL + KVL], wi[QL + KVL:QL + KVL + ROPE], zrow(HEAD_PAD - ROPE),
                 zrow(HEAD_PAD), wi[:QL]], axis=0))
            ops["wq_big"] = _head_pad(full["w_uq"][e], QK_DIM, (0, QK_DIM))
            wkv = full["w_ukv"][e]
            ops["wkv_big"] = jnp.concatenate([_head_pad(wkv, NOPE + VDIM, (0, NOPE)),
                                              _head_pad(wkv, NOPE + VDIM, (NOPE, NOPE + VDIM))], axis=0)
            wo = full["even_w_out"][e]
            ops["wo_attn"] = _head_pad(wo[:HEADS * VDIM], VDIM, (0, VDIM))
            ops["wo_sg"] = wo[HEADS * VDIM:]
            wt = sg_w[e] * tril
            ops["wstack"] = wt.reshape(SG_GROUPS * SG_CHUNK, SG_CHUNK).astype(BF16)
            ops["wtstack"] = jnp.swapaxes(wt, 1, 2).reshape(SG_GROUPS * SG_CHUNK, SG_CHUNK).astype(BF16)
            ops["bmat"] = jnp.repeat(sg_b[e].T, SG_GROUP_DIM, axis=1)
            even_ops[e] = ops
        return even_ops[e]

    gsum = (jnp.arange(SG_WIDTH)[:, None] // SG_GROUP_DIM == jnp.arange(LANE)[None, :]).astype(F32)

    def conv_taps(o):
        return jnp.pad(jnp.swapaxes(full["conv_w"][o], 0, 1).astype(F32), ((0, 8 - CONV_K), (0, 0)))

    inv_freq = ROPE_THETA ** (-jnp.arange(0, ROPE, 2, dtype=F32) / ROPE)
    ang = positions[0].astype(F32)[:, None] * inv_freq
    cos, sin = jnp.cos(ang), jnp.sin(ang)
    ones, zeros = jnp.ones((S, NOPE), F32), jnp.zeros((S, HEAD_PAD - QK_DIM), F32)
    ct = jnp.concatenate([ones, cos, cos, zeros], axis=1)
    st = jnp.concatenate([0.0 * ones, -sin, sin, zeros], axis=1)

    xs = x[0]
    saved = []
    def ffn_forward(xin, kind, l, next_keys):
        gain = (ffn_pre_norm if kind == "pre" else ffn_post_norm)[l]
        wg, wu, wd = (full[n][l] for n, _ in ffn_keys(kind, l))
        if not next_keys:
            return _ffn_fwd(xin, gain, wg, wu, wd, "ffn_fwd")
        rider, offs = gather_rider(next_keys)
        xo, a, b, gathered = _ffn_fwd(xin, gain, wg, wu, wd, "ffn_fwd_gather", rider=rider)
        take_gathered(gathered, next_keys, offs)
        return xo, a, b

    for l in range(depth):
        sv = dict(x0=xs)
        x1, sv["a1"], sv["b1"] = ffn_forward(xs, "pre", l, (mixer_keys(0) if l == 0 else []) + ffn_keys("post", l))
        h = _rms_fwd(x1, mix_norm[l], "mix_norm_fwd")
        sv.update(x1=x1, h=h)
        if l % 2 == 0:
            e = l // 2
            ops = even_operands(e)
            proj = _mm(h, ops["win_pad"], "nt", "even_in_proj", out_dtype=F32)
            qn = _rms_fwd(proj, q_norm[e], "q_norm_fwd", col=col_cq, width=QL)
            kvn = _rms_fwd(proj, kv_norm[e], "kv_norm_fwd", col=col_ckv, width=KVL)
            q_big = _mm(qn, ops["wq_big"], "nt", "q_up_proj", out_dtype=F32)
            kv_big = _mm(kvn, ops["wkv_big"], "nt", "kv_up_proj", out_dtype=BF16)
            q_r, k_r, v_r = _rope_fwd(q_big, kv_big, proj, col_kr, ct, st, "rope_fwd")
            o_att, lse = _attn_fwd(q_r, k_r, v_r, "attn_fwd", TILES["attn_wide"])
            sg = _sgu_fwd(proj, sg_norm[e], ops["wstack"], ops["bmat"], "sgu_fwd")
            tmp = _mm(o_att, ops["wo_attn"], "nn", "even_out_attn", out_dtype=F32, res=x1)
            x2 = _mm(sg, ops["wo_sg"], "nn", "even_out_sg", out_dtype=F32, res=tmp)
            sv.update(proj=proj, qn=qn, kvn=kvn, q=q_r, k=k_r, v=v_r, o=o_att, lse=lse, sg=sg)
        else:
            o = l // 2
            p = _mm(h, full["conv_w_in"][o], "nt", "conv_in_proj", out_dtype=BF16)
            cv = _conv_fwd(p, conv_taps(o), "conv_fwd")
            x2 = _mm(cv, full["conv_w_out"][o], "nn", "conv_out_proj", out_dtype=F32, res=x1)
            sv.update(p=p, cv=cv)
        sv["x2"] = x2
        next_keys = ffn_keys("pre", l + 1) + mixer_keys(l + 1) if l + 1 < depth else []
        xs, sv["a2"], sv["b2"] = ffn_forward(x2, "post", l, next_keys)
        saved.append(sv)

    gr = {n: [None] * w_loc[n].shape[0] for n in REPLICATED if n != "final_norm"}
    per_layer = {n: [None] * w_loc[n].shape[0] for n in SHARDED}
    pending = []

    def scatter_rider():
        pieces, where, off = [], [], 0
        for n, l, g in pending:
            piece = _pad_rows(_flat_rows(g.astype(BF16).reshape(N_DEV, -1), 1), 16, 1)
            pieces.append(piece)
            where.append((n, l, off))
            off += piece.shape[1]
        if off % GRAD_ROWS_MULT:
            pieces.append(jnp.zeros((N_DEV, (-off) % GRAD_ROWS_MULT, PACK_W), BF16))
        pending.clear()
        return _Exchange("scatter", pieces), where

    def take_scattered(received, where):
        owned = _sum_slots(received, "sum_grad_shards")
        for n, l, off in where:
            per_layer[n][l] = _unpack(owned, off, t_loc[n].shape[1:], 0)

    def ffn_backward(dxin, xin, kind, l, a, b):
        gain = (ffn_pre_norm if kind == "pre" else ffn_post_norm)[l]
        keys = ffn_keys(kind, l)
        wg, wu, wd = (full[n][l] for n, _ in keys)
        if pending:
            rider, where = scatter_rider()
            dxo, dz, hh, dy, dgain, received = _ffn_bwd(dxin, xin, gain, a, b, wg, wu, wd, "ffn_bwd_scatter", rider=rider)
            take_scattered(received, where)
        else:
            dxo, dz, hh, dy, dgain = _ffn_bwd(dxin, xin, gain, a, b, wg, wu, wd, "ffn_bwd")
        gr["ffn_%s_norm" % kind][l] = dgain[0]
        for (n, _), g in zip(keys, _ffn_dw(a, b, dz, hh, dy, "ffn_dw")):
            pending.append((n, l, g))
        return dxo

    dx, g_final, loss_part = _loss_head(xs, loss_target[0], final_norm, "loss_head")
    for l in reversed(range(depth)):
        sv = saved[l]
        dx = ffn_backward(dx, sv["x2"], "post", l, sv["a2"], sv["b2"])
        h = sv["h"]
        if l % 2 == 0:
            e = l // 2
            ops = even_operands(e)
            d_o = _mm(dx, ops["wo_attn"], "nt", "even_out_attn_bwd", out_dtype=BF16)
            d_sg = _mm(dx, ops["wo_sg"], "nt", "even_out_sg_bwd", out_dtype=BF16)
            g_wo_attn = _mm(sv["o"], dx, "tn", "even_out_attn_dw", out_dtype=F32)
            g_wo_sg = _mm(sv["sg"], dx, "tn", "even_out_sg_dw", out_dtype=F32)
            if e % 2 == 0:
                dq, dk, dv = _attn_bwd(sv["q"], sv["k"], sv["v"], sv["o"], d_o, sv["lse"], "attn_bwd_split",
                                       split_diagonal=True)
            else:
                dq, dk, dv = _attn_bwd(sv["q"], sv["k"], sv["v"], sv["o"], d_o, sv["lse"], "attn_bwd")
            dq_big, dkv_big, dkr = _rope_bwd(dq, dk, dv, ct, st, "rope_bwd")
            dz_sg, g_wstack, g_bias, g_sgn = _sgu_bwd(d_sg, sv["proj"], sg_norm[e], ops["wstack"], ops["wtstack"],
                                                      ops["bmat"], gsum, "sgu_bwd")
            dqn = _mm(dq_big, ops["wq_big"], "nn", "q_up_proj_bwd", out_dtype=F32)
            g_wq_big = _mm(dq_big, sv["qn"], "tn", "q_up_proj_dw", out_dtype=F32)
            dkvn = _mm(dkv_big, ops["wkv_big"], "nn", "kv_up_proj_bwd", out_dtype=F32)
            g_wkv_big = _mm(dkv_big, sv["kvn"], "tn", "kv_up_proj_dw", out_dtype=F32)
            dcq, g_qn = _rms_bwd(dqn, sv["proj"], q_norm[e], "q_norm_bwd", col=col_cq, out_dtype=BF16)
            dckv, g_kvn = _rms_bwd(dkvn, sv["proj"], kv_norm[e], "kv_norm_bwd", col=col_ckv, out_dtype=BF16)
            dproj = jnp.concatenate([dz_sg, dckv, dkr, jnp.zeros((S, HEAD_PAD), BF16), dcq], axis=1)
            dh = _mm(dproj, ops["win_pad"], "nn", "even_in_proj_bwd", out_dtype=F32)
            g_win = _mm(dproj, h, "tn", "even_in_proj_dw", out_dtype=F32)
            o_cq, o_ckv, o_kr = col_cq * QL, col_ckv * KVL, col_kr * HEAD_PAD
            hw = HEADS * HEAD_PAD
            pending.append(("even_w_in", e, jnp.concatenate(
                [g_win[o_cq:o_cq + QL], g_win[o_ckv:o_ckv + KVL], g_win[o_kr:o_kr + ROPE], g_win[:ZW]], axis=0)))
            pending.append(("w_uq", e, _head_unpad(g_wq_big, QK_DIM).reshape(HEADS * QK_DIM, QL)))
            pending.append(("w_ukv", e, jnp.concatenate(
                [_head_unpad(g_wkv_big[:hw], NOPE), _head_unpad(g_wkv_big[hw:], VDIM)],
                axis=1).reshape(HEADS * (NOPE + VDIM), KVL)))
            pending.append(("even_w_out", e, jnp.concatenate(
                [_head_unpad(g_wo_attn, VDIM).reshape(HEADS * VDIM, D), g_wo_sg], axis=0)))
            gr["q_norm"][e], gr["kv_norm"][e], gr["sg_norm"][e] = g_qn[0], g_kvn[0], g_sgn[0]
            gr["sg_w"][e] = g_wstack.reshape(SG_GROUPS, SG_CHUNK, SG_CHUNK) * tril
            gr["sg_b"][e] = g_bias[:, :SG_GROUPS].T
        else:
            o = l // 2
            dcv = _mm(dx, full["conv_w_out"][o], "nt", "conv_out_proj_bwd", out_dtype=BF16)
            pending.append(("conv_w_out", o, _mm(sv["cv"], dx, "tn", "conv_out_proj_dw", out_dtype=BF16)))
            dp, dcw = _conv_bwd(dcv, sv["p"], conv_taps(o), "conv_bwd")
            dh = _mm(dp, full["conv_w_in"][o], "nn", "conv_in_proj_bwd", out_dtype=F32)
            pending.append(("conv_w_in", o, _mm(dp, h, "tn", "conv_in_proj_dw", out_dtype=BF16)))
            pending.append(("conv_w", o, jnp.swapaxes(dcw[:CONV_K], 0, 1)))
        dx, dgain = _rms_bwd(dh, sv["x1"], mix_norm[l], "mix_norm_bwd", res=dx)
        gr["mix_norm"][l] = dgain[0]
        dx = ffn_backward(dx, sv["x0"], "pre", l, sv["a1"], sv["b1"])
    grad_x = dx[None]

    rider, where = scatter_rider()
    take_scattered(_exchange("scatter", rider.arrays, "scatter_grads"), where)
    grads = {n: _to_t(n, jnp.stack(per_layer[n])) for n in SHARDED}

    small = [jnp.stack(gr[n]) for n in REPLICATED if n != "final_norm"] + [g_final[0], loss_part[0, :1]]
    spack, soffs = _pack(small, 0, SMALL_ROWS_MULT)
    sgath = _all_gather(spack, "gather_small_grads")
    ssum = _sum_slots(sgath, "sum_small_grads")
    names_small = [n for n in REPLICATED if n != "final_norm"] + ["final_norm", "loss"]
    for n, off, piece in zip(names_small, soffs, small):
        val = _unpack(ssum, off, piece.shape, 0)
        if n == "loss":
            loss = val[0]
        else:
            grads[n] = val

    delta, new_m, new_v = {}, {}, {}
    for n in SHARDED:
        two_d = lambda a: a.reshape(-1, a.shape[-1])
        d, nm, nv = _adamw(two_d(w_loc[n]), two_d(grads[n]), two_d(m_loc[n]), two_d(v_loc[n]), "adamw")
        delta[n], new_m[n], new_v[n] = (a.reshape(w_loc[n].shape) for a in (d, nm, nv))
    flat = lambda d: _pack([d[n] for n in REPLICATED], 0, SMALL_ROWS_MULT)
    (wf, aoffs), (gf, _), (mf, _), (vf, _) = flat(w_loc), flat(grads), flat(m_loc), flat(v_loc)
    for res, buf in zip((delta, new_m, new_v), _adamw(wf, gf, mf, vf, "adamw_replicated")):
        for n, off in zip(REPLICATED, aoffs):
            res[n] = _unpack(buf, off, w_loc[n].shape, 0)
    outs = [loss, grad_x] + [grads[n] for n in WEIGHTS]
    for res in (delta, new_m, new_v):
        outs += [res[n] for n in WEIGHTS]
    return tuple(outs)
```

```python
import functools
import math

import numpy as np
import jax
import jax.numpy as jnp
from jax import lax
from jax.experimental import pallas as pl
from jax.experimental.pallas import tpu as pltpu

F32 = jnp.float32
BF16 = jnp.bfloat16

N_DEV = 8
NORM_EPS = 1e-6
HEADS = 8
NOPE = 64
ROPE = 32
VDIM = 64
HEAD_PAD = 128
QK_DIM = NOPE + ROPE
ROPE_THETA = 10000.0
SG_GROUPS = 8
SG_GROUP_DIM = 64
SG_WIDTH = SG_GROUPS * SG_GROUP_DIM
SG_CHUNK = 128
CONV_K = 3
ADAM_LR, ADAM_B1, ADAM_B2, ADAM_EPS, ADAM_WD, ADAM_STEP = 0.001, 0.9, 0.999, 1e-08, 0.01, 10

LANE = 128
PACK_W = 1024
GRAD_ROWS_MULT = 512
SMALL_ROWS_MULT = 64
VMEM_LIMIT = 60 * 1024 * 1024

TILES = dict(ffn_fwd=512, ffn_bwd=256, ffn_dw=2048, mm=512, mm_tn=1024, ew=512, attn=1024, attn_wide=2048, sgu=512,
             adam=512, mixer_big=1024)

NT = (((1,), (1,)), ((), ()))
NN = (((1,), (0,)), ((), ()))
TN = (((0,), (0,)), ((), ()))


def _dot(a, b, dims):
    return lax.dot_general(a, b, dims, preferred_element_type=F32)


def _cparams(*sem):
    return pltpu.CompilerParams(dimension_semantics=sem if sem else None, vmem_limit_bytes=VMEM_LIMIT)


def _tile(n, want):
    t = min(want, n)
    while n % t:
        t //= 2
    return t if t % 8 == 0 else n


def _lane_tile(n, cap):
    best = None
    for k in range(1, n // LANE + 1):
        t = k * LANE
        if n % t == 0 and t <= cap:
            best = t
    return best or n


def _row(v):
    return v.reshape(1, -1).astype(F32)


def _all_gather(block, name):
    R, W = block.shape

    def body(x_ref, out_ref, send_sems, recv_sems, local_sem):
        x, y, c = lax.axis_index("x"), lax.axis_index("y"), lax.axis_index("c")
        me, sibling = (x, y, c), (x, y, 1 - c)
        chips = [(1 - x, y), (x, 1 - y), (1 - x, 1 - y)]

        def slot(px, py, pc):
            return out_ref.at[4 * px + 2 * py + pc]

        def copy(k, blk, to, src=None):
            return pltpu.make_async_remote_copy(
                src_ref=slot(*blk) if src is None else src, dst_ref=slot(*blk),
                send_sem=send_sems.at[k], recv_sem=recv_sems.at[k],
                device_id=to, device_id_type=pl.DeviceIdType.MESH)

        mine = pltpu.make_async_copy(x_ref, slot(*me), local_sem)
        mine.start()
        first = [copy(0, me, sibling, src=x_ref)]
        first += [copy(1 + j, me, (*chip, c), src=x_ref) for j, chip in enumerate(chips)]
        for cp in first:
            cp.start()
        passed = [copy(4 + j, (*chip, c), sibling) for j, chip in enumerate(chips)]
        for j, chip in enumerate(chips):
            copy(1 + j, (*chip, c), me).wait_recv()
            passed[j].start()
        copy(0, sibling, me).wait_recv()
        for j, chip in enumerate(chips):
            copy(4 + j, (*chip, 1 - c), me).wait_recv()
        for cp in first + passed:
            cp.wait_send()
        mine.wait()

    return pl.pallas_call(
        body, name=name,
        out_shape=jax.ShapeDtypeStruct((N_DEV, R, W), block.dtype),
        in_specs=[pl.BlockSpec(memory_space=pl.ANY)],
        out_specs=pl.BlockSpec(memory_space=pl.ANY),
        scratch_shapes=[pltpu.SemaphoreType.DMA((7,)), pltpu.SemaphoreType.DMA((7,)), pltpu.SemaphoreType.DMA],
    )(block)


class _Exchange:
    def __init__(self, kind, arrays):
        self.kind, self.arrays = kind, list(arrays)
        if kind == "gather":
            (r, w), = [a.shape for a in self.arrays]
            self.rows = [r]
        else:
            self.rows = [a.shape[1] for a in self.arrays]
            w = self.arrays[0].shape[2]
        self.offs = [sum(self.rows[:i]) for i in range(len(self.rows))]
        self.n_in = len(self.arrays)
        self.out_shape = jax.ShapeDtypeStruct((N_DEV, sum(self.rows), w), self.arrays[0].dtype)
        self.in_specs = [pl.BlockSpec(memory_space=pl.ANY)] * self.n_in
        self.out_spec = pl.BlockSpec(memory_space=pl.ANY)
        self.out_specs, self.out_shapes = [self.out_spec], [self.out_shape]
        self.scratch = [pltpu.SemaphoreType.DMA((7,)), pltpu.SemaphoreType.DMA((7,)), pltpu.SemaphoreType.DMA]

    def _peers(self):
        x, y, c = lax.axis_index("x"), lax.axis_index("y"), lax.axis_index("c")
        me = 4 * x + 2 * y + c
        return me, [(k, (x ^ (k >> 2), y ^ ((k >> 1) & 1), c ^ (k & 1))) for k in range(1, N_DEV)]

    @staticmethod
    def _remote(src, dst, k, to, send_sems, recv_sems):
        return pltpu.make_async_remote_copy(
            src_ref=src, dst_ref=dst, send_sem=send_sems.at[k - 1], recv_sem=recv_sems.at[k - 1],
            device_id=to, device_id_type=pl.DeviceIdType.MESH)

    def start(self, s_refs, r_ref, send_sems, recv_sems, local_sem):
        me, peers = self._peers()
        for s_ref, off, r in zip(s_refs, self.offs, self.rows):
            src = s_ref if self.kind == "gather" else s_ref.at[me]
            pltpu.make_async_copy(src, r_ref.at[me, pl.ds(off, r)], local_sem).start()
        for k, to in peers:
            peer = 4 * to[0] + 2 * to[1] + to[2]
            for s_ref, off, r in zip(s_refs, self.offs, self.rows):
                src = s_ref if self.kind == "gather" else s_ref.at[peer]
                self._remote(src, r_ref.at[me, pl.ds(off, r)], k, to, send_sems, recv_sems).start()

    def wait(self, s_refs, r_ref, send_sems, recv_sems, local_sem):
        me, peers = self._peers()
        whole = r_ref.at[me]
        totals = [self._remote(whole, whole, k, to, send_sems, recv_sems) for k, to in peers]
        for cp in totals:
            cp.wait_recv()
        for cp in totals:
            cp.wait_send()
        pltpu.make_async_copy(whole, whole, local_sem).wait()


class _NoRider:
    arrays, in_specs, out_specs, out_shapes, scratch = [], [], [], [], []


_NO_RIDER = _NoRider()


def _ride(rider, refs, n_in, n_out, first, last):
    if rider is None:
        return refs, lambda: None
    k = rider.n_in
    s_refs = refs[n_in:n_in + k]
    r_ref = refs[n_in + k + n_out]
    sems = refs[-3:]
    own = refs[:n_in] + refs[n_in + k:n_in + k + n_out] + refs[n_in + k + n_out + 1:-3]

    @pl.when(first)
    def _():
        rider.start(s_refs, r_ref, *sems)

    def finish():
        @pl.when(last)
        def _():
            rider.wait(s_refs, r_ref, *sems)

    return own, finish


def _exchange(kind, arrays, name):
    ex = _Exchange(kind, arrays)

    def body(*refs):
        s_refs, r_ref, sems = refs[:ex.n_in], refs[ex.n_in], refs[ex.n_in + 1:]
        ex.start(s_refs, r_ref, *sems)
        ex.wait(s_refs, r_ref, *sems)

    return pl.pallas_call(
        body, name=name, out_shape=ex.out_shape, in_specs=ex.in_specs, out_specs=ex.out_spec,
        scratch_shapes=ex.scratch,
    )(*ex.arrays)


def _sum_slots(parts, name):
    _, R, W = parts.shape
    tr = _tile(R, TILES["adam"])

    def body(p_ref, o_ref):
        acc = p_ref[0].astype(F32)
        for s in range(1, N_DEV):
            acc = acc + p_ref[s].astype(F32)
        o_ref[...] = acc

    return pl.pallas_call(
        body, name=name, grid=(R // tr,),
        in_specs=[pl.BlockSpec((N_DEV, tr, W), lambda i: (0, i, 0))],
        out_specs=pl.BlockSpec((tr, W), lambda i: (i, 0)),
        out_shape=jax.ShapeDtypeStruct((R, W), F32),
        compiler_params=_cparams("parallel"),
    )(parts)


def _adamw(w, g, m, v, name):
    R, W = w.shape
    tr = _tile(R, TILES["adam"])
    c1 = 1.0 - ADAM_B1 ** ADAM_STEP
    c2 = 1.0 - ADAM_B2 ** ADAM_STEP

    def body(w_ref, g_ref, m_ref, v_ref, d_ref, nm_ref, nv_ref):
        g = g_ref[...]
        nm = ADAM_B1 * m_ref[...] + (1.0 - ADAM_B1) * g
        nv = ADAM_B2 * v_ref[...] + (1.0 - ADAM_B2) * (g * g)
        d_ref[...] = -ADAM_LR * ((nm / c1) / (jnp.sqrt(nv / c2) + ADAM_EPS) + ADAM_WD * w_ref[...])
        nm_ref[...] = nm
        nv_ref[...] = nv

    spec = pl.BlockSpec((tr, W), lambda i: (i, 0))
    return pl.pallas_call(
        body, name=name, grid=(R // tr,),
        in_specs=[spec] * 4, out_specs=[spec] * 3,
        out_shape=[jax.ShapeDtypeStruct((R, W), F32)] * 3,
        compiler_params=_cparams("parallel"),
    )(w, g, m, v)


def _mm(a, b, mode, name, out_dtype=BF16, res=None, scale=1.0, acol=None, kdim=None, tile=None):
    if mode == "tn":
        S, M = a.shape
        N = b.shape[1]
        ts = _tile(S, TILES["mm_tn"])
        tmo = _lane_tile(M, 1024)

        def body(a_ref, b_ref, o_ref, acc):
            s = pl.program_id(1)

            @pl.when(s == 0)
            def _():
                acc[...] = jnp.zeros_like(acc)

            acc[...] += _dot(a_ref[...].astype(BF16), b_ref[...].astype(BF16), TN)

            @pl.when(s == pl.num_programs(1) - 1)
            def _():
                o_ref[...] = acc[...].astype(out_dtype)

        return pl.pallas_call(
            functools.partial(body), name=name, grid=(M // tmo, S // ts),
            in_specs=[pl.BlockSpec((ts, tmo), lambda i, s: (s, i)), pl.BlockSpec((ts, N), lambda i, s: (s, 0))],
            out_specs=pl.BlockSpec((tmo, N), lambda i, s: (i, 0)),
            out_shape=jax.ShapeDtypeStruct((M, N), out_dtype),
            scratch_shapes=[pltpu.VMEM((tmo, N), F32)],
            compiler_params=_cparams("parallel", "arbitrary"),
        )(a, b)

    M = a.shape[0]
    K = kdim if kdim is not None else a.shape[1]
    ac = 0 if acol is None else acol
    N = b.shape[1] if mode == "nn" else b.shape[0]
    tm = _tile(M, tile or TILES["mm"])
    dims = NN if mode == "nn" else NT

    def body(*refs):
        if res is None:
            a_ref, b_ref, o_ref = refs
        else:
            a_ref, b_ref, r_ref, o_ref = refs
        acc = _dot(a_ref[...].astype(BF16), b_ref[...].astype(BF16), dims)
        if res is not None:
            acc = r_ref[...] + scale * acc
        o_ref[...] = acc.astype(out_dtype)

    in_specs = [pl.BlockSpec((tm, K), lambda i: (i, ac)), pl.BlockSpec(b.shape, lambda i: (0, 0))]
    args = [a, b]
    if res is not None:
        in_specs.append(pl.BlockSpec((tm, N), lambda i: (i, 0)))
        args.append(res)
    return pl.pallas_call(
        body, name=name, grid=(M // tm,),
        in_specs=in_specs, out_specs=pl.BlockSpec((tm, N), lambda i: (i, 0)),
        out_shape=jax.ShapeDtypeStruct((M, N), out_dtype),
        compiler_params=_cparams("parallel"),
    )(*args)


def _rms_fwd(x, gain, name, col=0, width=None, tile=None):
    S = x.shape[0]
    W = width if width is not None else x.shape[1]
    tm = _tile(S, tile or TILES["ew"])

    def body(x_ref, g_ref, o_ref):
        xv = x_ref[...].astype(F32)
        r = lax.rsqrt(jnp.mean(xv * xv, axis=-1, keepdims=True) + NORM_EPS)
        o_ref[...] = (xv * r * g_ref[...]).astype(BF16)

    return pl.pallas_call(
        body, name=name, grid=(S // tm,),
        in_specs=[pl.BlockSpec((tm, W), lambda i: (i, col)), pl.BlockSpec((1, W), lambda i: (0, 0))],
        out_specs=pl.BlockSpec((tm, W), lambda i: (i, 0)),
        out_shape=jax.ShapeDtypeStruct((S, W), BF16),
        compiler_params=_cparams("parallel"),
    )(x, _row(gain))


def _rms_bwd(dy, x, gain, name, col=0, res=None, out_dtype=F32, tile=None):
    S, W = dy.shape
    tm = _tile(S, tile or TILES["ew"])

    def body(*refs):
        if res is None:
            dy_ref, x_ref, g_ref, dx_ref, dg_ref = refs
        else:
            dy_ref, x_ref, g_ref, r_ref, dx_ref, dg_ref = refs

        @pl.when(pl.program_id(0) == 0)
        def _():
            dg_ref[...] = jnp.zeros_like(dg_ref)

        xv = x_ref[...].astype(F32)
        d = dy_ref[...].astype(F32)
        r = lax.rsqrt(jnp.mean(xv * xv, axis=-1, keepdims=True) + NORM_EPS)
        xhat = xv * r
        dg_ref[...] += jnp.sum(d * xhat, axis=0, keepdims=True)
        dxhat = d * g_ref[...]
        dx = r * (dxhat - xhat * jnp.mean(dxhat * xhat, axis=-1, keepdims=True))
        if res is not None:
            dx = dx + r_ref[...]
        dx_ref[...] = dx.astype(out_dtype)

    in_specs = [pl.BlockSpec((tm, W), lambda i: (i, 0)), pl.BlockSpec((tm, W), lambda i: (i, col)),
                pl.BlockSpec((1, W), lambda i: (0, 0))]
    args = [dy, x, _row(gain)]
    if res is not None:
        in_specs.append(pl.BlockSpec((tm, W), lambda i: (i, 0)))
        args.append(res)
    return pl.pallas_call(
        body, name=name, grid=(S // tm,),
        in_specs=in_specs,
        out_specs=[pl.BlockSpec((tm, W), lambda i: (i, 0)), pl.BlockSpec((1, W), lambda i: (0, 0))],
        out_shape=[jax.ShapeDtypeStruct((S, W), out_dtype), jax.ShapeDtypeStruct((1, W), F32)],
        compiler_params=_cparams("arbitrary"),
    )(*args)


def _silu_parts(a):
    s = jax.nn.sigmoid(a)
    return a * s, s * (1.0 + a * (1.0 - s))


def _ffn_fwd(x, gain, wg_t, wu_t, wd, name, rider=None):
    S, D = x.shape
    Fd = wd.shape[0]
    tm = _tile(S, TILES["ffn_fwd"])
    fc = _lane_tile(Fd, 512)

    def body(*refs):
        i = pl.program_id(0)
        own, finish = _ride(rider, refs, 5, 3, i == 0, i == pl.num_programs(0) - 1)
        x_ref, g_ref, wg_ref, wu_ref, wd_ref, o_ref, a_ref, b_ref = own
        xv = x_ref[...]
        r = lax.rsqrt(jnp.mean(xv * xv, axis=-1, keepdims=True) + NORM_EPS)
        h = (xv * r * g_ref[...]).astype(BF16)
        acc = jnp.zeros((tm, D), F32)
        for c in range(Fd // fc):
            sl = slice(c * fc, (c + 1) * fc)
            a = _dot(h, wg_ref[sl, :], NT)
            b = _dot(h, wu_ref[sl, :], NT)
            a_ref[:, sl] = a.astype(BF16)
            b_ref[:, sl] = b.astype(BF16)
            z = (a * jax.nn.sigmoid(a) * b).astype(BF16)
            acc = acc + _dot(z, wd_ref[sl, :], NN)
        o_ref[...] = xv + 0.5 * acc
        finish()

    wspec = pl.BlockSpec((Fd, D), lambda i: (0, 0), pipeline_mode=pl.Buffered(1))
    extra = rider or _NO_RIDER
    return pl.pallas_call(
        body, name=name, grid=(S // tm,),
        in_specs=[pl.BlockSpec((tm, D), lambda i: (i, 0)), pl.BlockSpec((1, D), lambda i: (0, 0)), wspec, wspec,
                  wspec] + extra.in_specs,
        out_specs=[pl.BlockSpec((tm, D), lambda i: (i, 0)), pl.BlockSpec((tm, Fd), lambda i: (i, 0)),
                   pl.BlockSpec((tm, Fd), lambda i: (i, 0))] + extra.out_specs,
        out_shape=[jax.ShapeDtypeStruct((S, D), F32), jax.ShapeDtypeStruct((S, Fd), BF16),
                   jax.ShapeDtypeStruct((S, Fd), BF16)] + extra.out_shapes,
        scratch_shapes=extra.scratch,
        compiler_params=_cparams("arbitrary" if rider else "parallel"),
    )(x, _row(gain), wg_t, wu_t, wd, *extra.arrays)


def _ffn_bwd(g, x, gain, a, b, wg_t, wu_t, wd, name, rider=None):
    S, D = x.shape
    Fd = wd.shape[0]
    tm = _tile(S, TILES["ffn_bwd"])
    fc = Fd

    def body(*refs):
        i = pl.program_id(0)
        own, finish = _ride(rider, refs, 8, 5, i == 0, i == pl.num_programs(0) - 1)
        g_ref, x_ref, gain_ref, a_ref, b_ref, wg_ref, wu_ref, wd_ref, dx_ref, dz_ref, h_ref, dy_ref, dg_ref = own

        @pl.when(i == 0)
        def _():
            dg_ref[...] = jnp.zeros_like(dg_ref)

        gv = g_ref[...]
        xv = x_ref[...]
        r = lax.rsqrt(jnp.mean(xv * xv, axis=-1, keepdims=True) + NORM_EPS)
        xhat = xv * r
        h_ref[...] = (xhat * gain_ref[...]).astype(BF16)
        dy = (0.5 * gv).astype(BF16)
        dy_ref[...] = dy
        dh = jnp.zeros((tm, D), F32)
        for c in range(Fd // fc):
            sl = slice(c * fc, (c + 1) * fc)
            av = a_ref[:, sl].astype(F32)
            bv = b_ref[:, sl].astype(F32)
            dz = _dot(dy, wd_ref[sl, :], NT).astype(BF16)
            dz_ref[:, sl] = dz
            dzf = dz.astype(F32)
            silu, dsilu = _silu_parts(av)
            da = (dzf * bv * dsilu).astype(BF16)
            db = (dzf * silu).astype(BF16)
            dh = dh + _dot(da, wg_ref[sl, :], NN) + _dot(db, wu_ref[sl, :], NN)
        dg_ref[...] += jnp.sum(dh * xhat, axis=0, keepdims=True)
        dxhat = dh * gain_ref[...]
        dx_ref[...] = gv + r * (dxhat - xhat * jnp.mean(dxhat * xhat, axis=-1, keepdims=True))
        finish()

    wspec = pl.BlockSpec((Fd, D), lambda i: (0, 0), pipeline_mode=pl.Buffered(1))
    row = pl.BlockSpec((tm, D), lambda i: (i, 0))
    wide = pl.BlockSpec((tm, Fd), lambda i: (i, 0))
    extra = rider or _NO_RIDER
    return pl.pallas_call(
        body, name=name, grid=(S // tm,),
        in_specs=[row, row, pl.BlockSpec((1, D), lambda i: (0, 0)), wide, wide, wspec, wspec, wspec] + extra.in_specs,
        out_specs=[row, wide, row, row, pl.BlockSpec((1, D), lambda i: (0, 0))] + extra.out_specs,
        out_shape=[jax.ShapeDtypeStruct((S, D), F32), jax.ShapeDtypeStruct((S, Fd), BF16),
                   jax.ShapeDtypeStruct((S, D), BF16), jax.ShapeDtypeStruct((S, D), BF16),
                   jax.ShapeDtypeStruct((1, D), F32)] + extra.out_shapes,
        scratch_shapes=extra.scratch,
        compiler_params=_cparams("arbitrary"),
    )(g, x, _row(gain), a, b, wg_t, wu_t, wd, *extra.arrays)


def _ffn_dw(a, b, dz, h, dy, name):
    S, Fd = a.shape
    D = h.shape[1]
    ts = _tile(S, TILES["ffn_dw"])
    tf = _lane_tile(Fd, 256)

    def body(a_ref, b_ref, dz_ref, h_ref, dy_ref, og_ref, ou_ref, od_ref, accg, accu, accd):
        s = pl.program_id(1)

        @pl.when(s == 0)
        def _():
            accg[...] = jnp.zeros_like(accg)
            accu[...] = jnp.zeros_like(accu)
            accd[...] = jnp.zeros_like(accd)

        av = a_ref[...].astype(F32)
        bv = b_ref[...].astype(F32)
        dzf = dz_ref[...].astype(F32)
        silu, dsilu = _silu_parts(av)
        da = (dzf * bv * dsilu).astype(BF16)
        db = (dzf * silu).astype(BF16)
        z = (silu * bv).astype(BF16)
        hv = h_ref[...]
        accg[...] += _dot(da, hv, TN)
        accu[...] += _dot(db, hv, TN)
        accd[...] += _dot(z, dy_ref[...], TN)

        @pl.when(s == pl.num_programs(1) - 1)
        def _():
            og_ref[...] = accg[...].astype(BF16)
            ou_ref[...] = accu[...].astype(BF16)
            od_ref[...] = accd[...].astype(BF16)

    wide = pl.BlockSpec((ts, tf), lambda f, s: (s, f))
    row = pl.BlockSpec((ts, D), lambda f, s: (s, 0))
    out = pl.BlockSpec((tf, D), lambda f, s: (f, 0))
    return pl.pallas_call(
        body, name=name, grid=(Fd // tf, S // ts),
        in_specs=[wide, wide, wide, row, row], out_specs=[out, out, out],
        out_shape=[jax.ShapeDtypeStruct((Fd, D), BF16)] * 3,
        scratch_shapes=[pltpu.VMEM((tf, D), F32)] * 3,
        compiler_params=_cparams("parallel", "arbitrary"),
    )(a, b, dz, h, dy)


def _loss_head(x, target, gain, name):
    S, D = x.shape
    tm = _tile(S, TILES["ew"])

    def body(x_ref, t_ref, g_ref, dx_ref, dg_ref, loss_ref):
        @pl.when(pl.program_id(0) == 0)
        def _():
            dg_ref[...] = jnp.zeros_like(dg_ref)
            loss_ref[...] = jnp.zeros_like(loss_ref)

        xv = x_ref[...]
        r = lax.rsqrt(jnp.mean(xv * xv, axis=-1, keepdims=True) + NORM_EPS)
        xhat = xv * r
        e = xhat * g_ref[...] - t_ref[...]
        per_tok = jnp.mean(e * e, axis=-1, keepdims=True)
        loss_ref[...] += jnp.broadcast_to(0.5 * jnp.sum(per_tok, axis=0, keepdims=True), (1, LANE))
        dy = e * (1.0 / D)
        dg_ref[...] += jnp.sum(dy * xhat, axis=0, keepdims=True)
        dxhat = dy * g_ref[...]
        dx_ref[...] = r * (dxhat - xhat * jnp.mean(dxhat * xhat, axis=-1, keepdims=True))

    row = pl.BlockSpec((tm, D), lambda i: (i, 0))
    return pl.pallas_call(
        body, name=name, grid=(S // tm,),
        in_specs=[row, row, pl.BlockSpec((1, D), lambda i: (0, 0))],
        out_specs=[row, pl.BlockSpec((1, D), lambda i: (0, 0)), pl.BlockSpec((1, LANE), lambda i: (0, 0))],
        out_shape=[jax.ShapeDtypeStruct((S, D), F32), jax.ShapeDtypeStruct((1, D), F32),
                   jax.ShapeDtypeStruct((1, LANE), F32)],
        compiler_params=_cparams("arbitrary"),
    )(x, target, _row(gain))


def _shift_down(u, halo, k, rows):
    out = pltpu.roll(u, k, 0)
    for j in range(k):
        out = jnp.where(rows == j, halo[8 - k + j:8 - k + j + 1, :], out)
    return out


def _shift_up(u, halo, k, rows, n):
    out = pltpu.roll(u, n - k, 0)
    for j in range(k):
        out = jnp.where(rows == n - k + j, halo[j:j + 1, :], out)
    return out


def _conv_fwd(p, cw, name):
    S, W3 = p.shape
    W = W3 // 3
    tm = _tile(S, TILES["ew"])
    hb = tm // 8

    def body(p_ref, ph_ref, w_ref, v_ref):
        i = pl.program_id(0)
        bg = p_ref[:, 0:W].astype(F32)
        u = p_ref[:, W:2 * W].astype(F32) * p_ref[:, 2 * W:3 * W].astype(F32)
        uh = ph_ref[:, W:2 * W].astype(F32) * ph_ref[:, 2 * W:3 * W].astype(F32)
        uh = jnp.where(i > 0, uh, 0.0)
        rows = lax.broadcasted_iota(jnp.int32, (tm, 1), 0)
        u1 = _shift_down(u, uh, 1, rows)
        u2 = _shift_down(u, uh, 2, rows)
        y = w_ref[0:1, :] * u2 + w_ref[1:2, :] * u1 + w_ref[2:3, :] * u
        v_ref[...] = (bg * y).astype(BF16)

    return pl.pallas_call(
        body, name=name, grid=(S // tm,),
        in_specs=[pl.BlockSpec((tm, W3), lambda i: (i, 0)),
                  pl.BlockSpec((8, W3), lambda i: (jnp.maximum(i * hb - 1, 0), 0)),
                  pl.BlockSpec((8, W), lambda i: (0, 0))],
        out_specs=pl.BlockSpec((tm, W), lambda i: (i, 0)),
        out_shape=jax.ShapeDtypeStruct((S, W), BF16),
        compiler_params=_cparams("parallel"),
    )(p, p, cw)


def _conv_bwd(dv, p, cw, name):
    S, W3 = p.shape
    W = W3 // 3
    tm = _tile(S, TILES["ew"])
    hb = tm // 8
    last = S // 8 - 1

    def body(dv_ref, dvn_ref, p_ref, pp_ref, pn_ref, w_ref, dp_ref, dw_ref):
        i = pl.program_id(0)
        n = pl.num_programs(0)

        @pl.when(i == 0)
        def _():
            dw_ref[...] = jnp.zeros_like(dw_ref)

        bg = p_ref[:, 0:W].astype(F32)
        cg = p_ref[:, W:2 * W].astype(F32)
        zz = p_ref[:, 2 * W:3 * W].astype(F32)
        u = cg * zz
        uh = pp_ref[:, W:2 * W].astype(F32) * pp_ref[:, 2 * W:3 * W].astype(F32)
        uh = jnp.where(i > 0, uh, 0.0)
        rows = lax.broadcasted_iota(jnp.int32, (tm, 1), 0)
        u1 = _shift_down(u, uh, 1, rows)
        u2 = _shift_down(u, uh, 2, rows)
        w0, w1, w2 = w_ref[0:1, :], w_ref[1:2, :], w_ref[2:3, :]
        y = w0 * u2 + w1 * u1 + w2 * u
        dvv = dv_ref[...].astype(F32)
        dy = dvv * bg
        dyh = dvn_ref[...].astype(F32) * pn_ref[:, 0:W].astype(F32)
        dyh = jnp.where(i < n - 1, dyh, 0.0)
        d1 = _shift_up(dy, dyh, 1, rows, tm)
        d2 = _shift_up(dy, dyh, 2, rows, tm)
        du = w2 * dy + w1 * d1 + w0 * d2
        dp_ref[:, 0:W] = (dvv * y).astype(BF16)
        dp_ref[:, W:2 * W] = (du * zz).astype(BF16)
        dp_ref[:, 2 * W:3 * W] = (du * cg).astype(BF16)
        dw_ref[0:1, :] += jnp.sum(dy * u2, axis=0, keepdims=True)
        dw_ref[1:2, :] += jnp.sum(dy * u1, axis=0, keepdims=True)
        dw_ref[2:3, :] += jnp.sum(dy * u, axis=0, keepdims=True)

    return pl.pallas_call(
        body, name=name, grid=(S // tm,),
        in_specs=[pl.BlockSpec((tm, W), lambda i: (i, 0)),
                  pl.BlockSpec((8, W), lambda i: (jnp.minimum((i + 1) * hb, last), 0)),
                  pl.BlockSpec((tm, W3), lambda i: (i, 0)),
                  pl.BlockSpec((8, W3), lambda i: (jnp.maximum(i * hb - 1, 0), 0)),
                  pl.BlockSpec((8, W3), lambda i: (jnp.minimum((i + 1) * hb, last), 0)),
                  pl.BlockSpec((8, W), lambda i: (0, 0))],
        out_specs=[pl.BlockSpec((tm, W3), lambda i: (i, 0)), pl.BlockSpec((8, W), lambda i: (0, 0))],
        out_shape=[jax.ShapeDtypeStruct((S, W3), BF16), jax.ShapeDtypeStruct((8, W), F32)],
        compiler_params=_cparams("arbitrary"),
    )(dv, dv, p, p, p, cw)


def _rope_swap(r, lane):
    mid = NOPE + ROPE // 2
    first = (lane >= NOPE) & (lane < mid)
    second = (lane >= mid) & (lane < QK_DIM)
    return jnp.where(first, pltpu.roll(r, HEAD_PAD - ROPE // 2, 1), jnp.where(second, pltpu.roll(r, ROPE // 2, 1), 0.0))


def _rope_fwd(q_big, kv_big, proj, kr_col, ct, st, name):
    S = q_big.shape[0]
    HW = HEADS * HEAD_PAD
    tm = _tile(S, TILES["ew"])

    def body(q_ref, k_ref, v_ref, kr_ref, ct_ref, st_ref, qo_ref, ko_ref, vo_ref):
        lane = lax.broadcasted_iota(jnp.int32, (1, HEAD_PAD), 1)
        ctv, stv = ct_ref[...], st_ref[...]
        krr = pltpu.roll(kr_ref[...].astype(F32), NOPE, 1)
        kro = krr * ctv + _rope_swap(krr, lane) * stv
        for h in range(HEADS):
            sl = slice(h * HEAD_PAD, (h + 1) * HEAD_PAD)
            qh = q_ref[:, sl].astype(F32)
            qo_ref[:, sl] = (qh * ctv + _rope_swap(qh, lane) * stv).astype(BF16)
            ko_ref[:, sl] = (k_ref[:, sl].astype(F32) + kro).astype(BF16)
            vo_ref[:, sl] = jnp.where(lane == VDIM, 1.0, v_ref[:, sl].astype(F32)).astype(BF16)

    wide = pl.BlockSpec((tm, HW), lambda i: (i, 0))
    narrow = pl.BlockSpec((tm, HEAD_PAD), lambda i: (i, 0))
    return pl.pallas_call(
        body, name=name, grid=(S // tm,),
        in_specs=[wide, wide, pl.BlockSpec((tm, HW), lambda i: (i, 1)),
                  pl.BlockSpec((tm, HEAD_PAD), lambda i: (i, kr_col)), narrow, narrow],
        out_specs=[wide, wide, wide],
        out_shape=[jax.ShapeDtypeStruct((S, HW), BF16)] * 3,
        compiler_params=_cparams("parallel"),
    )(q_big, kv_big, kv_big, proj, ct, st)


def _rope_bwd(dq, dk, dv, ct, st, name):
    S = dq.shape[0]
    HW = HEADS * HEAD_PAD
    tm = _tile(S, TILES["ew"])

    def body(dq_ref, dk_ref, dv_ref, ct_ref, st_ref, oq_ref, okv_ref, okr_ref):
        lane = lax.broadcasted_iota(jnp.int32, (1, HEAD_PAD), 1)
        ctv, stv = ct_ref[...], st_ref[...]
        acc = jnp.zeros((tm, HEAD_PAD), F32)
        for h in range(HEADS):
            sl = slice(h * HEAD_PAD, (h + 1) * HEAD_PAD)
            d = dq_ref[:, sl].astype(F32)
            oq_ref[:, sl] = (d * ctv + _rope_swap(d * stv, lane)).astype(BF16)
            d = dk_ref[:, sl].astype(F32)
            okv_ref[:, sl] = jnp.where(lane < NOPE, d, 0.0).astype(BF16)
            acc = acc + jnp.where(lane >= NOPE, d * ctv + _rope_swap(d * stv, lane), 0.0)
        okv_ref[:, HW:2 * HW] = dv_ref[...].astype(BF16)
        okr_ref[...] = pltpu.roll(acc, HEAD_PAD - NOPE, 1).astype(BF16)

    wide = pl.BlockSpec((tm, HW), lambda i: (i, 0))
    narrow = pl.BlockSpec((tm, HEAD_PAD), lambda i: (i, 0))
    return pl.pallas_call(
        body, name=name, grid=(S // tm,),
        in_specs=[wide, wide, wide, narrow, narrow],
        out_specs=[wide, pl.BlockSpec((tm, 2 * HW), lambda i: (i, 0)), narrow],
        out_shape=[jax.ShapeDtypeStruct((S, HW), BF16), jax.ShapeDtypeStruct((S, 2 * HW), BF16),
                   jax.ShapeDtypeStruct((S, HEAD_PAD), BF16)],
        compiler_params=_cparams("parallel"),
    )(dq, dk, dv, ct, st)


def _pairs(n, by_key):
    if by_key:
        pr = [(i, j) for j in range(n) for i in range(j, n)]
    else:
        pr = [(i, j) for i in range(n) for j in range(i + 1)]
    qi = np.array([p[0] for p in pr], np.int32)
    kj = np.array([p[1] for p in pr], np.int32)
    return jnp.asarray(qi), jnp.asarray(kj)


_LOG2E = 1.4426950408889634
_LN2 = 0.6931471805599453


def _tile_mask(t):
    return lax.broadcasted_iota(jnp.int32, (t, t), 1) <= lax.broadcasted_iota(jnp.int32, (t, t), 0)


def _attn_fwd(q, k, v, name, tile, heads_per_step=1):
    S = q.shape[0]
    HW = HEADS * HEAD_PAD
    t = _tile(S, tile)
    n = S // t
    qi, kj = _pairs(n, by_key=False)
    c = (QK_DIM ** -0.5) * _LOG2E
    W = heads_per_step * HEAD_PAD

    def body(qi_ref, kj_ref, q_ref, k_ref, v_ref, o_ref, lse_ref, m_s, acc_s):
        p_id = pl.program_id(1)
        i, j = qi_ref[p_id], kj_ref[p_id]
        lanes = [slice(u * HEAD_PAD, (u + 1) * HEAD_PAD) for u in range(heads_per_step)]

        @pl.when(j == 0)
        def _():
            m_s[...] = jnp.full_like(m_s, -jnp.inf)
            acc_s[...] = jnp.zeros_like(acc_s)

        def step(on_diagonal):
            for u, sl in enumerate(lanes):
                s = _dot(q_ref[:, sl], k_ref[:, sl], NT)
                if on_diagonal:
                    s = jnp.where(_tile_mask(t), s, -jnp.inf)
                m_old = m_s[u]
                m_new = jnp.maximum(m_old, jnp.max(s, axis=-1, keepdims=True))
                p = jnp.exp2((s - m_new) * c).astype(BF16)
                acc_s[:, sl] = jnp.exp2((m_old - m_new) * c) * acc_s[:, sl] + _dot(p, v_ref[:, sl], NN)
                m_s[u] = m_new

        @pl.when(i == j)
        def _():
            step(True)

        @pl.when(i != j)
        def _():
            step(False)

        @pl.when(j == i)
        def _():
            for u, sl in enumerate(lanes):
                acc = acc_s[:, sl]
                l = acc[:, VDIM:VDIM + 1]
                o_ref[:, sl] = (acc * (1.0 / l)).astype(BF16)
                lse_ref[:, sl] = jnp.broadcast_to(m_s[u] * c + jnp.log2(l), (t, HEAD_PAD))

    qspec = pl.BlockSpec((t, W), lambda h, p, qi, kj: (qi[p], h))
    kspec = pl.BlockSpec((t, W), lambda h, p, qi, kj: (kj[p], h))
    grid_spec = pltpu.PrefetchScalarGridSpec(
        num_scalar_prefetch=2, grid=(HEADS // heads_per_step, int(qi.shape[0])),
        in_specs=[qspec, kspec, kspec], out_specs=[qspec, qspec],
        scratch_shapes=[pltpu.VMEM((heads_per_step, t, 1), F32), pltpu.VMEM((t, W), F32)])
    return pl.pallas_call(
        body, name=name, grid_spec=grid_spec,
        out_shape=[jax.ShapeDtypeStruct((S, HW), BF16), jax.ShapeDtypeStruct((S, HW), F32)],
        compiler_params=_cparams("parallel", "arbitrary"),
    )(qi, kj, q, k, v)


def _attn_bwd(q, k, v, o, do, lse2, name, split_diagonal=False):
    S = q.shape[0]
    HW = HEADS * HEAD_PAD
    t = _tile(S, TILES["attn"])
    n = S // t
    qi, kj = _pairs(n, by_key=True)
    scale = QK_DIM ** -0.5

    def body(qi_ref, kj_ref, q_ref, k_ref, v_ref, o_ref, do_ref, lse_ref, dq_ref, dk_ref, dv_ref, dk_s, dv_s):
        p_id = pl.program_id(1)
        i, j = qi_ref[p_id], kj_ref[p_id]

        @pl.when(p_id == 0)
        def _():
            dq_ref[...] = jnp.zeros_like(dq_ref)

        @pl.when(i == j)
        def _():
            dk_s[...] = jnp.zeros_like(dk_s)
            dv_s[...] = jnp.zeros_like(dv_s)

        def step(mask):
            qv, kv, vv = q_ref[...], k_ref[...], v_ref[...]
            dov = do_ref[...]
            p = jnp.exp(_dot(qv, kv, NT) * scale - lse_ref[:, 0:1] * _LN2)
            if mask is not None:
                p = jnp.where(mask(), p, 0.0)
            delta = jnp.sum(dov.astype(F32) * o_ref[...].astype(F32), axis=-1, keepdims=True)
            dv_s[...] += _dot(p.astype(BF16), dov, TN)
            ds = (p * (_dot(dov, vv, NT) - delta) * scale).astype(BF16)
            dk_s[...] += _dot(ds, qv, TN)
            rows = pl.ds(pl.multiple_of(i * t, t), t)
            dq_ref[rows, :] += _dot(ds, kv, NN)

        def by_index():
            rows_i = lax.broadcasted_iota(jnp.int32, (t, t), 0) + i * t
            cols_j = lax.broadcasted_iota(jnp.int32, (t, t), 1) + j * t
            return cols_j <= rows_i

        if split_diagonal:
            @pl.when(i == j)
            def _():
                step(lambda: _tile_mask(t))

            @pl.when(i != j)
            def _():
                step(None)
        else:
            step(by_index)

        @pl.when(i == n - 1)
        def _():
            dk_ref[...] = dk_s[...]
            dv_ref[...] = dv_s[...]

    qspec = pl.BlockSpec((t, HEAD_PAD), lambda h, p, qi, kj: (qi[p], h))
    kspec = pl.BlockSpec((t, HEAD_PAD), lambda h, p, qi, kj: (kj[p], h))
    grid_spec = pltpu.PrefetchScalarGridSpec(
        num_scalar_prefetch=2, grid=(HEADS, int(qi.shape[0])),
        in_specs=[qspec, kspec, kspec, qspec, qspec, qspec],
        out_specs=[pl.BlockSpec((S, HEAD_PAD), lambda h, p, qi, kj: (0, h)), kspec, kspec],
        scratch_shapes=[pltpu.VMEM((t, HEAD_PAD), F32), pltpu.VMEM((t, HEAD_PAD), F32)])
    return pl.pallas_call(
        body, name=name, grid_spec=grid_spec,
        out_shape=[jax.ShapeDtypeStruct((S, HW), F32)] * 3,
        compiler_params=_cparams("parallel", "arbitrary"),
    )(qi, kj, q, k, v, o, do, lse2)


_SQRT_HALF = 0.7071067811865476
_INV_SQRT_2PI = 0.3989422804014327


def _sg_select(r, grp):
    out = jnp.where(grp == 0, r[0:SG_CHUNK, :], 0.0)
    for g in range(1, SG_GROUPS):
        out = out + jnp.where(grp == g, r[g * SG_CHUNK:(g + 1) * SG_CHUNK, :], 0.0)
    return out


def _sgu_fwd(proj, gain, wstack, bmat, name):
    S = proj.shape[0]
    W = SG_WIDTH
    tm = _tile(S, TILES["sgu"])

    def body(z_ref, g_ref, w_ref, b_ref, o_ref):
        z = z_ref[...].astype(F32)
        zg = 0.5 * z * (1.0 + lax.erf(z * _SQRT_HALF))
        u, vv = zg[:, 0:W], zg[:, W:2 * W]
        r = lax.rsqrt(jnp.mean(vv * vv, axis=-1, keepdims=True) + NORM_EPS)
        vn = (vv * r * g_ref[...]).astype(BF16)
        grp = lax.broadcasted_iota(jnp.int32, (1, W), 1) // SG_GROUP_DIM
        for c in range(tm // SG_CHUNK):
            sl = slice(c * SG_CHUNK, (c + 1) * SG_CHUNK)
            mixed = _sg_select(_dot(w_ref[...], vn[sl, :], NN), grp) + b_ref[...]
            o_ref[sl, :] = (u[sl, :] * mixed).astype(BF16)

    return pl.pallas_call(
        body, name=name, grid=(S // tm,),
        in_specs=[pl.BlockSpec((tm, 2 * W), lambda i: (i, 0)), pl.BlockSpec((1, W), lambda i: (0, 0)),
                  pl.BlockSpec(wstack.shape, lambda i: (0, 0)), pl.BlockSpec(bmat.shape, lambda i: (0, 0))],
        out_specs=pl.BlockSpec((tm, W), lambda i: (i, 0)),
        out_shape=jax.ShapeDtypeStruct((S, W), BF16),
        compiler_params=_cparams("parallel"),
    )(proj, _row(gain), wstack, bmat)


def _sgu_bwd(dsg, proj, gain, wstack, wtstack, bmat, gsum, name):
    S = proj.shape[0]
    W = SG_WIDTH
    tm = _tile(S, TILES["sgu"])
    GS = SG_GROUPS * SG_CHUNK

    def body(d_ref, z_ref, g_ref, w_ref, wt_ref, b_ref, e_ref, dz_ref, dw_ref, db_ref, dg_ref, dw_s, db_s):
        i = pl.program_id(0)

        @pl.when(i == 0)
        def _():
            dw_s[...] = jnp.zeros_like(dw_s)
            db_s[...] = jnp.zeros_like(db_s)
            dg_ref[...] = jnp.zeros_like(dg_ref)

        z = z_ref[...].astype(F32)
        cdf = 0.5 * (1.0 + lax.erf(z * _SQRT_HALF))
        zg = z * cdf
        u, vv = zg[:, 0:W], zg[:, W:2 * W]
        r = lax.rsqrt(jnp.mean(vv * vv, axis=-1, keepdims=True) + NORM_EPS)
        vhat = vv * r
        vn = (vhat * g_ref[...]).astype(BF16)
        grp = lax.broadcasted_iota(jnp.int32, (1, W), 1) // SG_GROUP_DIM
        d = d_ref[...].astype(F32)
        du_parts, dvn_parts = [], []
        for c in range(tm // SG_CHUNK):
            sl = slice(c * SG_CHUNK, (c + 1) * SG_CHUNK)
            vc = vn[sl, :]
            mixed = _sg_select(_dot(w_ref[...], vc, NN), grp) + b_ref[...]
            dc = d[sl, :]
            du_parts.append(dc * mixed)
            dmix = dc * u[sl, :]
            db_s[...] += dmix
            dmb = dmix.astype(BF16)
            dvn_parts.append(_sg_select(_dot(wt_ref[...], dmb, NN), grp))
            astack = jnp.concatenate([jnp.where(grp == g, dmb, jnp.zeros_like(dmb)) for g in range(SG_GROUPS)], axis=0)
            dw_s[...] += _dot(astack, vc, NT)
        du = jnp.concatenate(du_parts, axis=0)
        dvn = jnp.concatenate(dvn_parts, axis=0)
        dg_ref[...] += jnp.sum(dvn * vhat, axis=0, keepdims=True)
        dvhat = dvn * g_ref[...]
        dvv = r * (dvhat - vhat * jnp.mean(dvhat * vhat, axis=-1, keepdims=True))
        dgelu = cdf + z * (_INV_SQRT_2PI * jnp.exp(-0.5 * z * z))
        dz_ref[:, 0:W] = (du * dgelu[:, 0:W]).astype(BF16)
        dz_ref[:, W:2 * W] = (dvv * dgelu[:, W:2 * W]).astype(BF16)

        @pl.when(i == pl.num_programs(0) - 1)
        def _():
            dw_ref[...] = dw_s[...]
            db_ref[...] = lax.dot_general(db_s[...], e_ref[...], NN, precision=lax.Precision.HIGHEST,
                                          preferred_element_type=F32)

    full = lambda a: pl.BlockSpec(a.shape, lambda i: (0, 0))
    return pl.pallas_call(
        body, name=name, grid=(S // tm,),
        in_specs=[pl.BlockSpec((tm, W), lambda i: (i, 0)), pl.BlockSpec((tm, 2 * W), lambda i: (i, 0)),
                  pl.BlockSpec((1, W), lambda i: (0, 0)), full(wstack), full(wtstack), full(bmat), full(gsum)],
        out_specs=[pl.BlockSpec((tm, 2 * W), lambda i: (i, 0)), pl.BlockSpec((GS, SG_CHUNK), lambda i: (0, 0)),
                   pl.BlockSpec((SG_CHUNK, LANE), lambda i: (0, 0)), pl.BlockSpec((1, W), lambda i: (0, 0))],
        out_shape=[jax.ShapeDtypeStruct((S, 2 * W), BF16), jax.ShapeDtypeStruct((GS, SG_CHUNK), F32),
                   jax.ShapeDtypeStruct((SG_CHUNK, LANE), F32), jax.ShapeDtypeStruct((1, W), F32)],
        scratch_shapes=[pltpu.VMEM((GS, SG_CHUNK), F32), pltpu.VMEM((SG_CHUNK, W), F32)],
        compiler_params=_cparams("arbitrary"),
    )(dsg, proj, _row(gain), wstack, wtstack, bmat, gsum)


WEIGHTS = ['ffn_pre_norm', 'ffn_pre_w_gate', 'ffn_pre_w_up', 'ffn_pre_w_down', 'mix_norm', 'ffn_post_norm',
           'ffn_post_w_gate', 'ffn_post_w_up', 'ffn_post_w_down', 'even_w_in', 'q_norm', 'w_uq', 'kv_norm', 'w_ukv',
           'sg_norm', 'sg_w', 'sg_b', 'even_w_out', 'conv_w_in', 'conv_w', 'conv_w_out', 'final_norm']
SHARD_AXIS = dict(ffn_pre_w_gate=2, ffn_pre_w_up=2, ffn_pre_w_down=1, ffn_post_w_gate=2, ffn_post_w_up=2,
                  ffn_post_w_down=1, even_w_in=2, w_uq=2, w_ukv=2, even_w_out=1, conv_w_in=2, conv_w=2, conv_w_out=1)
SHARDED = [n for n in WEIGHTS if n in SHARD_AXIS]
REPLICATED = [n for n in WEIGHTS if n not in SHARD_AXIS]


def _to_t(name, w):
    return jnp.swapaxes(w, 1, 2) if SHARD_AXIS[name] == 2 else w


def _rows_of(n):
    return -(-n // PACK_W)


def _pad_rows(a, mult, axis):
    r = a.shape[axis]
    extra = (-r) % mult
    if extra == 0:
        return a
    pad = [(0, 0)] * a.ndim
    pad[axis] = (0, extra)
    return jnp.pad(a, pad)


def _flat_rows(a, lead):
    flat = a.reshape(a.shape[:lead] + (-1,))
    n = flat.shape[-1]
    flat = _pad_rows(flat, PACK_W, lead)
    return flat.reshape(a.shape[:lead] + (_rows_of(n), PACK_W))


def _pack(pieces, lead, mult, piece_mult=1):
    rows, offs, off = [], [], 0
    for p in pieces:
        r = _pad_rows(_flat_rows(p, lead), piece_mult, lead)
        rows.append(r)
        offs.append(off)
        off += r.shape[lead]
    return _pad_rows(jnp.concatenate(rows, axis=lead), mult, lead), offs


def _unpack(buf, off, shape, lead):
    n = math.prod(shape)
    r = _rows_of(n)
    piece = lax.slice_in_dim(buf, off, off + r, axis=lead)
    piece = piece.reshape(buf.shape[:lead] + (r * PACK_W,))
    piece = lax.slice_in_dim(piece, 0, n, axis=lead)
    return piece.reshape(buf.shape[:lead] + tuple(shape))


def _head_pad(w, per_head, keep):
    k = w.shape[-1]
    w = w.reshape(HEADS, per_head, k)[:, keep[0]:keep[1]]
    w = jnp.pad(w, ((0, 0), (0, HEAD_PAD - (keep[1] - keep[0])), (0, 0)))
    return w.reshape(HEADS * HEAD_PAD, k)


def _head_unpad(w, n):
    return w.reshape(HEADS, HEAD_PAD, w.shape[-1])[:, :n]


def kernel(x, positions, ffn_pre_norm, ffn_pre_w_gate, ffn_pre_w_up, ffn_pre_w_down, mix_norm, ffn_post_norm, ffn_post_w_gate, ffn_post_w_up, ffn_post_w_down, even_w_in, q_norm, w_uq, kv_norm, w_ukv, sg_norm, sg_w, sg_b, even_w_out, conv_w_in, conv_w, conv_w_out, final_norm, loss_target, m_ffn_pre_norm, m_ffn_pre_w_gate, m_ffn_pre_w_up, m_ffn_pre_w_down, m_mix_norm, m_ffn_post_norm, m_ffn_post_w_gate, m_ffn_post_w_up, m_ffn_post_w_down, m_even_w_in, m_q_norm, m_w_uq, m_kv_norm, m_w_ukv, m_sg_norm, m_sg_w, m_sg_b, m_even_w_out, m_conv_w_in, m_conv_w, m_conv_w_out, m_final_norm, v_ffn_pre_norm, v_ffn_pre_w_gate, v_ffn_pre_w_up, v_ffn_pre_w_down, v_mix_norm, v_ffn_post_norm, v_ffn_post_w_gate, v_ffn_post_w_up, v_ffn_post_w_down, v_even_w_in, v_q_norm, v_w_uq, v_kv_norm, v_w_ukv, v_sg_norm, v_sg_w, v_sg_b, v_even_w_out, v_conv_w_in, v_conv_w, v_conv_w_out, v_final_norm):
    given = dict(locals())
    w_loc = {n: given[n] for n in WEIGHTS}
    m_loc = {n: given["m_" + n] for n in WEIGHTS}
    v_loc = {n: given["v_" + n] for n in WEIGHTS}

    S, D = x.shape[1], x.shape[2]
    depth = ffn_pre_norm.shape[0]
    QL, KVL = q_norm.shape[1], kv_norm.shape[1]
    ZW = 2 * SG_WIDTH
    assert x.shape[0] == 1 and ZW % KVL == 0 and (ZW + KVL) % HEAD_PAD == 0 and (ZW + KVL + 2 * HEAD_PAD) % QL == 0
    col_ckv = ZW // KVL
    col_kr = (ZW + KVL) // HEAD_PAD
    col_cq = (ZW + KVL + 2 * HEAD_PAD) // QL

    t_loc = {n: _to_t(n, w_loc[n]) for n in SHARDED}
    full = {n: {} for n in SHARDED}

    def ffn_keys(kind, l):
        return [("ffn_%s_w_%s" % (kind, part), l) for part in ("gate", "up", "down")]

    def mixer_keys(l):
        names = ("even_w_in", "w_uq", "w_ukv", "even_w_out") if l % 2 == 0 else ("conv_w_in", "conv_w", "conv_w_out")
        return [(n, l // 2) for n in names]

    def local_pack(keys):
        return _pack([t_loc[n][l].astype(BF16) for n, l in keys], 0, 16, piece_mult=16)

    def take_gathered(gathered, keys, offs):
        for (n, l), off in zip(keys, offs):
            piece = _unpack(gathered, off, t_loc[n].shape[1:], 1)
            full[n][l] = piece.reshape(N_DEV * piece.shape[1], piece.shape[2])

    def gather_rider(keys):
        pack, offs = local_pack(keys)
        return _Exchange("gather", [pack]), offs

    first_keys = ffn_keys("pre", 0)
    pack0, offs0 = local_pack(first_keys)
    take_gathered(_all_gather(pack0, "gather_weights"), first_keys, offs0)

    tril = jnp.tril(jnp.ones((SG_CHUNK, SG_CHUNK), F32))
    even_ops = {}

    def even_operands(e):
        if e not in even_ops:
            wi = full["even_w_in"][e]
            zrow = lambda k: jnp.zeros((k, D), BF16)
            ops = dict(win_pad=jnp.concatenate(
                [wi[QL + KVL + ROPE:], wi[QL:QL + KVL], wi[QL + KVL:QL + KVL + ROPE], zrow(HEAD_PAD - ROPE),
                 zrow(HEAD_PAD), wi[:QL]], axis=0))
            ops["wq_big"] = _head_pad(full["w_uq"][e], QK_DIM, (0, QK_DIM))
            wkv = full["w_ukv"][e]
            ops["wkv_big"] = jnp.concatenate([_head_pad(wkv, NOPE + VDIM, (0, NOPE)),
                                              _head_pad(wkv, NOPE + VDIM, (NOPE, NOPE + VDIM))], axis=0)
            wo = full["even_w_out"][e]
            ops["wo_attn"] = _head_pad(wo[:HEADS * VDIM], VDIM, (0, VDIM))
            ops["wo_sg"] = wo[HEADS * VDIM:]
            wt = sg_w[e] * tril
            ops["wstack"] = wt.reshape(SG_GROUPS * SG_CHUNK, SG_CHUNK).astype(BF16)
            ops["wtstack"] = jnp.swapaxes(wt, 1, 2).reshape(SG_GROUPS * SG_CHUNK, SG_CHUNK).astype(BF16)
            ops["bmat"] = jnp.repeat(sg_b[e].T, SG_GROUP_DIM, axis=1)
            even_ops[e] = ops
        return even_ops[e]

    gsum = (jnp.arange(SG_WIDTH)[:, None] // SG_GROUP_DIM == jnp.arange(LANE)[None, :]).astype(F32)

    def conv_taps(o):
        return jnp.pad(jnp.swapaxes(full["conv_w"][o], 0, 1).astype(F32), ((0, 8 - CONV_K), (0, 0)))

    inv_freq = ROPE_THETA ** (-jnp.arange(0, ROPE, 2, dtype=F32) / ROPE)
    ang = positions[0].astype(F32)[:, None] * inv_freq
    cos, sin = jnp.cos(ang), jnp.sin(ang)
    ones, zeros = jnp.ones((S, NOPE), F32), jnp.zeros((S, HEAD_PAD - QK_DIM), F32)
    ct = jnp.concatenate([ones, cos, cos, zeros], axis=1)
    st = jnp.concatenate([0.0 * ones, -sin, sin, zeros], axis=1)

    xs = x[0]
    saved = []
    def ffn_forward(xin, kind, l, next_keys):
        gain = (ffn_pre_norm if kind == "pre" else ffn_post_norm)[l]
        wg, wu, wd = (full[n][l] for n, _ in ffn_keys(kind, l))
        if not next_keys:
            return _ffn_fwd(xin, gain, wg, wu, wd, "ffn_fwd")
        rider, offs = gather_rider(next_keys)
        xo, a, b, gathered = _ffn_fwd(xin, gain, wg, wu, wd, "ffn_fwd_gather", rider=rider)
        take_gathered(gathered, next_keys, offs)
        return xo, a, b

    def layer_helpers(l):
        tile = TILES["mixer_big"] if l >= depth // 2 else None
        return (functools.partial(_mm, tile=tile), functools.partial(_rms_fwd, tile=tile),
                functools.partial(_rms_bwd, tile=tile))

    for l in range(depth):
        mm, rms_fwd, rms_bwd = layer_helpers(l)
        sv = dict(x0=xs)
        x1, sv["a1"], sv["b1"] = ffn_forward(xs, "pre", l, (mixer_keys(0) if l == 0 else []) + ffn_keys("post", l))
        h = rms_fwd(x1, mix_norm[l], "mix_norm_fwd")
        sv.update(x1=x1, h=h)
        if l % 2 == 0:
            e = l // 2
            ops = even_operands(e)
            proj = mm(h, ops["win_pad"], "nt", "even_in_proj", out_dtype=F32)
            qn = rms_fwd(proj, q_norm[e], "q_norm_fwd", col=col_cq, width=QL)
            kvn = rms_fwd(proj, kv_norm[e], "kv_norm_fwd", col=col_ckv, width=KVL)
            q_big = mm(qn, ops["wq_big"], "nt", "q_up_proj", out_dtype=F32)
            kv_big = mm(kvn, ops["wkv_big"], "nt", "kv_up_proj", out_dtype=BF16)
            q_r, k_r, v_r = _rope_fwd(q_big, kv_big, proj, col_kr, ct, st, "rope_fwd")
            o_att, lse = _attn_fwd(q_r, k_r, v_r, "attn_fwd", TILES["attn_wide"])
            sg = _sgu_fwd(proj, sg_norm[e], ops["wstack"], ops["bmat"], "sgu_fwd")
            tmp = mm(o_att, ops["wo_attn"], "nn", "even_out_attn", out_dtype=F32, res=x1)
            x2 = mm(sg, ops["wo_sg"], "nn", "even_out_sg", out_dtype=F32, res=tmp)
            sv.update(proj=proj, qn=qn, kvn=kvn, q=q_r, k=k_r, v=v_r, o=o_att, lse=lse, sg=sg)
        else:
            o = l // 2
            p = mm(h, full["conv_w_in"][o], "nt", "conv_in_proj", out_dtype=BF16)
            cv = _conv_fwd(p, conv_taps(o), "conv_fwd")
            x2 = mm(cv, full["conv_w_out"][o], "nn", "conv_out_proj", out_dtype=F32, res=x1)
            sv.update(p=p, cv=cv)
        sv["x2"] = x2
        next_keys = ffn_keys("pre", l + 1) + mixer_keys(l + 1) if l + 1 < depth else []
        xs, sv["a2"], sv["b2"] = ffn_forward(x2, "post", l, next_keys)
        saved.append(sv)

    gr = {n: [None] * w_loc[n].shape[0] for n in REPLICATED if n != "final_norm"}
    per_layer = {n: [None] * w_loc[n].shape[0] for n in SHARDED}
    pending = []

    def scatter_rider():
        pieces, where, off = [], [], 0
        for n, l, g in pending:
            piece = _pad_rows(_flat_rows(g.astype(BF16).reshape(N_DEV, -1), 1), 16, 1)
            pieces.append(piece)
            where.append((n, l, off))
            off += piece.shape[1]
        if off % GRAD_ROWS_MULT:
            pieces.append(jnp.zeros((N_DEV, (-off) % GRAD_ROWS_MULT, PACK_W), BF16))
        pending.clear()
        return _Exchange("scatter", pieces), where

    def take_scattered(received, where):
        owned = _sum_slots(received, "sum_grad_shards")
        for n, l, off in where:
            per_layer[n][l] = _unpack(owned, off, t_loc[n].shape[1:], 0)

    def ffn_backward(dxin, xin, kind, l, a, b):
        gain = (ffn_pre_norm if kind == "pre" else ffn_post_norm)[l]
        keys = ffn_keys(kind, l)
        wg, wu, wd = (full[n][l] for n, _ in keys)
        if pending:
            rider, where = scatter_rider()
            dxo, dz, hh, dy, dgain, received = _ffn_bwd(dxin, xin, gain, a, b, wg, wu, wd, "ffn_bwd_scatter", rider=rider)
            take_scattered(received, where)
        else:
            dxo, dz, hh, dy, dgain = _ffn_bwd(dxin, xin, gain, a, b, wg, wu, wd, "ffn_bwd")
        gr["ffn_%s_norm" % kind][l] = dgain[0]
        for (n, _), g in zip(keys, _ffn_dw(a, b, dz, hh, dy, "ffn_dw")):
            pending.append((n, l, g))
        return dxo

    dx, g_final, loss_part = _loss_head(xs, loss_target[0], final_norm, "loss_head")
    for l in reversed(range(depth)):
        mm, rms_fwd, rms_bwd = layer_helpers(l)
        sv = saved[l]
        dx = ffn_backward(dx, sv["x2"], "post", l, sv["a2"], sv["b2"])
        h = sv["h"]
        if l % 2 == 0:
            e = l // 2
            ops = even_operands(e)
            d_o = mm(dx, ops["wo_attn"], "nt", "even_out_attn_bwd", out_dtype=BF16)
            d_sg = mm(dx, ops["wo_sg"], "nt", "even_out_sg_bwd", out_dtype=BF16)
            g_wo_attn = mm(sv["o"], dx, "tn", "even_out_attn_dw", out_dtype=F32)
            g_wo_sg = mm(sv["sg"], dx, "tn", "even_out_sg_dw", out_dtype=F32)
            dq, dk, dv = _attn_bwd(sv["q"], sv["k"], sv["v"], sv["o"], d_o, sv["lse"], "attn_bwd", split_diagonal=True)
            dq_big, dkv_big, dkr = _rope_bwd(dq, dk, dv, ct, st, "rope_bwd")
            dz_sg, g_wstack, g_bias, g_sgn = _sgu_bwd(d_sg, sv["proj"], sg_norm[e], ops["wstack"], ops["wtstack"],
                                                      ops["bmat"], gsum, "sgu_bwd")
            dqn = mm(dq_big, ops["wq_big"], "nn", "q_up_proj_bwd", out_dtype=F32)
            g_wq_big = mm(dq_big, sv["qn"], "tn", "q_up_proj_dw", out_dtype=F32)
            dkvn = mm(dkv_big, ops["wkv_big"], "nn", "kv_up_proj_bwd", out_dtype=F32)
            g_wkv_big = mm(dkv_big, sv["kvn"], "tn", "kv_up_proj_dw", out_dtype=F32)
            dcq, g_qn = rms_bwd(dqn, sv["proj"], q_norm[e], "q_norm_bwd", col=col_cq, out_dtype=BF16)
            dckv, g_kvn = rms_bwd(dkvn, sv["proj"], kv_norm[e], "kv_norm_bwd", col=col_ckv, out_dtype=BF16)
            dproj = jnp.concatenate([dz_sg, dckv, dkr, jnp.zeros((S, HEAD_PAD), BF16), dcq], axis=1)
            dh = mm(dproj, ops["win_pad"], "nn", "even_in_proj_bwd", out_dtype=F32)
            g_win = mm(dproj, h, "tn", "even_in_proj_dw", out_dtype=F32)
            o_cq, o_ckv, o_kr = col_cq * QL, col_ckv * KVL, col_kr * HEAD_PAD
            hw = HEADS * HEAD_PAD
            pending.append(("even_w_in", e, jnp.concatenate(
                [g_win[o_cq:o_cq + QL], g_win[o_ckv:o_ckv + KVL], g_win[o_kr:o_kr + ROPE], g_win[:ZW]], axis=0)))
            pending.append(("w_uq", e, _head_unpad(g_wq_big, QK_DIM).reshape(HEADS * QK_DIM, QL)))
            pending.append(("w_ukv", e, jnp.concatenate(
                [_head_unpad(g_wkv_big[:hw], NOPE), _head_unpad(g_wkv_big[hw:], VDIM)],
                axis=1).reshape(HEADS * (NOPE + VDIM), KVL)))
            pending.append(("even_w_out", e, jnp.concatenate(
                [_head_unpad(g_wo_attn, VDIM).reshape(HEADS * VDIM, D), g_wo_sg], axis=0)))
            gr["q_norm"][e], gr["kv_norm"][e], gr["sg_norm"][e] = g_qn[0], g_kvn[0], g_sgn[0]
            gr["sg_w"][e] = g_wstack.reshape(SG_GROUPS, SG_CHUNK, SG_CHUNK) * tril
            gr["sg_b"][e] = g_bias[:, :SG_GROUPS].T
        else:
            o = l // 2
            dcv = mm(dx, full["conv_w_out"][o], "nt", "conv_out_proj_bwd", out_dtype=BF16)
            pending.append(("conv_w_out", o, mm(sv["cv"], dx, "tn", "conv_out_proj_dw", out_dtype=BF16)))
            dp, dcw = _conv_bwd(dcv, sv["p"], conv_taps(o), "conv_bwd")
            dh = mm(dp, full["conv_w_in"][o], "nn", "conv_in_proj_bwd", out_dtype=F32)
            pending.append(("conv_w_in", o, mm(dp, h, "tn", "conv_in_proj_dw", out_dtype=BF16)))
            pending.append(("conv_w", o, jnp.swapaxes(dcw[:CONV_K], 0, 1)))
        dx, dgain = rms_bwd(dh, sv["x1"], mix_norm[l], "mix_norm_bwd", res=dx)
        gr["mix_norm"][l] = dgain[0]
        dx = ffn_backward(dx, sv["x0"], "pre", l, sv["a1"], sv["b1"])
    grad_x = dx[None]

    rider, where = scatter_rider()
    take_scattered(_exchange("scatter", rider.arrays, "scatter_grads"), where)
    grads = {n: _to_t(n, jnp.stack(per_layer[n])) for n in SHARDED}

    small = [jnp.stack(gr[n]) for n in REPLICATED if n != "final_norm"] + [g_final[0], loss_part[0, :1]]
    spack, soffs = _pack(small, 0, SMALL_ROWS_MULT)
    sgath = _all_gather(spack, "gather_small_grads")
    ssum = _sum_slots(sgath, "sum_small_grads")
    names_small = [n for n in REPLICATED if n != "final_norm"] + ["final_norm", "loss"]
    for n, off, piece in zip(names_small, soffs, small):
        val = _unpack(ssum, off, piece.shape, 0)
        if n == "loss":
            loss = val[0]
        else:
            grads[n] = val

    delta, new_m, new_v = {}, {}, {}
    for n in SHARDED:
        two_d = lambda a: a.reshape(-1, a.shape[-1])
        d, nm, nv = _adamw(two_d(w_loc[n]), two_d(grads[n]), two_d(m_loc[n]), two_d(v_loc[n]), "adamw")
        delta[n], new_m[n], new_v[n] = (a.reshape(w_loc[n].shape) for a in (d, nm, nv))
    flat = lambda d: _pack([d[n] for n in REPLICATED], 0, SMALL_ROWS_MULT)
    (wf, aoffs), (gf, _), (mf, _), (vf, _) = flat(w_loc), flat(grads), flat(m_loc), flat(v_loc)
    for res, buf in zip((delta, new_m, new_v), _adamw(wf, gf, mf, vf, "adamw_replicated")):
        for n, off in zip(REPLICATED, aoffs):
            res[n] = _unpack(buf, off, w_loc[n].shape, 0)
    outs = [loss, grad_x] + [grads[n] for n in WEIGHTS]
    for res in (delta, new_m, new_v):
        outs += [res[n] for n in WEIGHTS]
    return tuple(outs)
```

```python
import functools
import math

import numpy as np
import jax
import jax.numpy as jnp
from jax import lax
from jax.experimental import pallas as pl
from jax.experimental.pallas import tpu as pltpu

F32 = jnp.float32
BF16 = jnp.bfloat16

N_DEV = 8
NORM_EPS = 1e-6
HEADS = 8
NOPE = 64
ROPE = 32
VDIM = 64
HEAD_PAD = 128
QK_DIM = NOPE + ROPE
ROPE_THETA = 10000.0
SG_GROUPS = 8
SG_GROUP_DIM = 64
SG_WIDTH = SG_GROUPS * SG_GROUP_DIM
SG_CHUNK = 128
CONV_K = 3
ADAM_LR, ADAM_B1, ADAM_B2, ADAM_EPS, ADAM_WD, ADAM_STEP = 0.001, 0.9, 0.999, 1e-08, 0.01, 10

LANE = 128
PACK_W = 1024
GRAD_ROWS_MULT = 512
SMALL_ROWS_MULT = 64
VMEM_LIMIT = 60 * 1024 * 1024

TILES = dict(ffn_fwd=512, ffn_bwd=256, ffn_dw=2048, mm=1024, rms=1024, mm_tn=1024, ew=512, attn=1024, attn_wide=2048,
             sgu=512, adam=512, mixer_big=1024, mm_tn_big=2048)

NT = (((1,), (1,)), ((), ()))
NN = (((1,), (0,)), ((), ()))
TN = (((0,), (0,)), ((), ()))


def _dot(a, b, dims):
    return lax.dot_general(a, b, dims, preferred_element_type=F32)


def _cparams(*sem):
    return pltpu.CompilerParams(dimension_semantics=sem if sem else None, vmem_limit_bytes=VMEM_LIMIT)


def _tile(n, want):
    t = min(want, n)
    while n % t:
        t //= 2
    return t if t % 8 == 0 else n


def _lane_tile(n, cap):
    best = None
    for k in range(1, n // LANE + 1):
        t = k * LANE
        if n % t == 0 and t <= cap:
            best = t
    return best or n


def _row(v):
    return v.reshape(1, -1).astype(F32)


def _all_gather(block, name):
    R, W = block.shape

    def body(x_ref, out_ref, send_sems, recv_sems, local_sem):
        x, y, c = lax.axis_index("x"), lax.axis_index("y"), lax.axis_index("c")
        me, sibling = (x, y, c), (x, y, 1 - c)
        chips = [(1 - x, y), (x, 1 - y), (1 - x, 1 - y)]

        def slot(px, py, pc):
            return out_ref.at[4 * px + 2 * py + pc]

        def copy(k, blk, to, src=None):
            return pltpu.make_async_remote_copy(
                src_ref=slot(*blk) if src is None else src, dst_ref=slot(*blk),
                send_sem=send_sems.at[k], recv_sem=recv_sems.at[k],
                device_id=to, device_id_type=pl.DeviceIdType.MESH)

        mine = pltpu.make_async_copy(x_ref, slot(*me), local_sem)
        mine.start()
        first = [copy(0, me, sibling, src=x_ref)]
        first += [copy(1 + j, me, (*chip, c), src=x_ref) for j, chip in enumerate(chips)]
        for cp in first:
            cp.start()
        passed = [copy(4 + j, (*chip, c), sibling) for j, chip in enumerate(chips)]
        for j, chip in enumerate(chips):
            copy(1 + j, (*chip, c), me).wait_recv()
            passed[j].start()
        copy(0, sibling, me).wait_recv()
        for j, chip in enumerate(chips):
            copy(4 + j, (*chip, 1 - c), me).wait_recv()
        for cp in first + passed:
            cp.wait_send()
        mine.wait()

    return pl.pallas_call(
        body, name=name,
        out_shape=jax.ShapeDtypeStruct((N_DEV, R, W), block.dtype),
        in_specs=[pl.BlockSpec(memory_space=pl.ANY)],
        out_specs=pl.BlockSpec(memory_space=pl.ANY),
        scratch_shapes=[pltpu.SemaphoreType.DMA((7,)), pltpu.SemaphoreType.DMA((7,)), pltpu.SemaphoreType.DMA],
    )(block)


class _Exchange:
    def __init__(self, kind, arrays):
        self.kind, self.arrays = kind, list(arrays)
        if kind == "gather":
            (r, w), = [a.shape for a in self.arrays]
            self.rows = [r]
        else:
            self.rows = [a.shape[1] for a in self.arrays]
            w = self.arrays[0].shape[2]
        self.offs = [sum(self.rows[:i]) for i in range(len(self.rows))]
        self.n_in = len(self.arrays)
        self.out_shape = jax.ShapeDtypeStruct((N_DEV, sum(self.rows), w), self.arrays[0].dtype)
        self.in_specs = [pl.BlockSpec(memory_space=pl.ANY)] * self.n_in
        self.out_spec = pl.BlockSpec(memory_space=pl.ANY)
        self.out_specs, self.out_shapes = [self.out_spec], [self.out_shape]
        self.scratch = [pltpu.SemaphoreType.DMA((7,)), pltpu.SemaphoreType.DMA((7,)), pltpu.SemaphoreType.DMA]

    def _peers(self):
        x, y, c = lax.axis_index("x"), lax.axis_index("y"), lax.axis_index("c")
        me = 4 * x + 2 * y + c
        return me, [(k, (x ^ (k >> 2), y ^ ((k >> 1) & 1), c ^ (k & 1))) for k in range(1, N_DEV)]

    @staticmethod
    def _remote(src, dst, k, to, send_sems, recv_sems):
        return pltpu.make_async_remote_copy(
            src_ref=src, dst_ref=dst, send_sem=send_sems.at[k - 1], recv_sem=recv_sems.at[k - 1],
            device_id=to, device_id_type=pl.DeviceIdType.MESH)

    def start(self, s_refs, r_ref, send_sems, recv_sems, local_sem):
        me, peers = self._peers()
        for s_ref, off, r in zip(s_refs, self.offs, self.rows):
            src = s_ref if self.kind == "gather" else s_ref.at[me]
            pltpu.make_async_copy(src, r_ref.at[me, pl.ds(off, r)], local_sem).start()
        for k, to in peers:
            peer = 4 * to[0] + 2 * to[1] + to[2]
            for s_ref, off, r in zip(s_refs, self.offs, self.rows):
                src = s_ref if self.kind == "gather" else s_ref.at[peer]
                self._remote(src, r_ref.at[me, pl.ds(off, r)], k, to, send_sems, recv_sems).start()

    def wait(self, s_refs, r_ref, send_sems, recv_sems, local_sem):
        me, peers = self._peers()
        whole = r_ref.at[me]
        totals = [self._remote(whole, whole, k, to, send_sems, recv_sems) for k, to in peers]
        for cp in totals:
            cp.wait_recv()
        for cp in totals:
            cp.wait_send()
        pltpu.make_async_copy(whole, whole, local_sem).wait()


class _NoRider:
    arrays, in_specs, out_specs, out_shapes, scratch = [], [], [], [], []


_NO_RIDER = _NoRider()


def _ride(rider, refs, n_in, n_out, first, last):
    if rider is None:
        return refs, lambda: None
    k = rider.n_in
    s_refs = refs[n_in:n_in + k]
    r_ref = refs[n_in + k + n_out]
    sems = refs[-3:]
    own = refs[:n_in] + refs[n_in + k:n_in + k + n_out] + refs[n_in + k + n_out + 1:-3]

    @pl.when(first)
    def _():
        rider.start(s_refs, r_ref, *sems)

    def finish():
        @pl.when(last)
        def _():
            rider.wait(s_refs, r_ref, *sems)

    return own, finish


def _exchange(kind, arrays, name):
    ex = _Exchange(kind, arrays)

    def body(*refs):
        s_refs, r_ref, sems = refs[:ex.n_in], refs[ex.n_in], refs[ex.n_in + 1:]
        ex.start(s_refs, r_ref, *sems)
        ex.wait(s_refs, r_ref, *sems)

    return pl.pallas_call(
        body, name=name, out_shape=ex.out_shape, in_specs=ex.in_specs, out_specs=ex.out_spec,
        scratch_shapes=ex.scratch,
    )(*ex.arrays)


def _sum_slots(parts, name):
    _, R, W = parts.shape
    tr = _tile(R, TILES["adam"])

    def body(p_ref, o_ref):
        acc = p_ref[0].astype(F32)
        for s in range(1, N_DEV):
            acc = acc + p_ref[s].astype(F32)
        o_ref[...] = acc

    return pl.pallas_call(
        body, name=name, grid=(R // tr,),
        in_specs=[pl.BlockSpec((N_DEV, tr, W), lambda i: (0, i, 0))],
        out_specs=pl.BlockSpec((tr, W), lambda i: (i, 0)),
        out_shape=jax.ShapeDtypeStruct((R, W), F32),
        compiler_params=_cparams("parallel"),
    )(parts)


def _adamw(w, g, m, v, name):
    R, W = w.shape
    tr = _tile(R, TILES["adam"])
    c1 = 1.0 - ADAM_B1 ** ADAM_STEP
    c2 = 1.0 - ADAM_B2 ** ADAM_STEP

    def body(w_ref, g_ref, m_ref, v_ref, d_ref, nm_ref, nv_ref):
        g = g_ref[...]
        nm = ADAM_B1 * m_ref[...] + (1.0 - ADAM_B1) * g
        nv = ADAM_B2 * v_ref[...] + (1.0 - ADAM_B2) * (g * g)
        d_ref[...] = -ADAM_LR * ((nm / c1) / (jnp.sqrt(nv / c2) + ADAM_EPS) + ADAM_WD * w_ref[...])
        nm_ref[...] = nm
        nv_ref[...] = nv

    spec = pl.BlockSpec((tr, W), lambda i: (i, 0))
    return pl.pallas_call(
        body, name=name, grid=(R // tr,),
        in_specs=[spec] * 4, out_specs=[spec] * 3,
        out_shape=[jax.ShapeDtypeStruct((R, W), F32)] * 3,
        compiler_params=_cparams("parallel"),
    )(w, g, m, v)


def _mm(a, b, mode, name, out_dtype=BF16, res=None, scale=1.0, acol=None, kdim=None, tile=None, tn_tile=None):
    if mode == "tn":
        S, M = a.shape
        N = b.shape[1]
        ts = _tile(S, tn_tile or TILES["mm_tn"])
        tmo = _lane_tile(M, 1024)

        def body(a_ref, b_ref, o_ref, acc):
            s = pl.program_id(1)

            @pl.when(s == 0)
            def _():
                acc[...] = jnp.zeros_like(acc)

            acc[...] += _dot(a_ref[...].astype(BF16), b_ref[...].astype(BF16), TN)

            @pl.when(s == pl.num_programs(1) - 1)
            def _():
                o_ref[...] = acc[...].astype(out_dtype)

        return pl.pallas_call(
            functools.partial(body), name=name, grid=(M // tmo, S // ts),
            in_specs=[pl.BlockSpec((ts, tmo), lambda i, s: (s, i)), pl.BlockSpec((ts, N), lambda i, s: (s, 0))],
            out_specs=pl.BlockSpec((tmo, N), lambda i, s: (i, 0)),
            out_shape=jax.ShapeDtypeStruct((M, N), out_dtype),
            scratch_shapes=[pltpu.VMEM((tmo, N), F32)],
            compiler_params=_cparams("parallel", "arbitrary"),
        )(a, b)

    M = a.shape[0]
    K = kdim if kdim is not None else a.shape[1]
    ac = 0 if acol is None else acol
    N = b.shape[1] if mode == "nn" else b.shape[0]
    tm = _tile(M, tile or TILES["mm"])
    dims = NN if mode == "nn" else NT

    def body(*refs):
        if res is None:
            a_ref, b_ref, o_ref = refs
        else:
            a_ref, b_ref, r_ref, o_ref = refs
        acc = _dot(a_ref[...].astype(BF16), b_ref[...].astype(BF16), dims)
        if res is not None:
            acc = r_ref[...] + scale * acc
        o_ref[...] = acc.astype(out_dtype)

    in_specs = [pl.BlockSpec((tm, K), lambda i: (i, ac)), pl.BlockSpec(b.shape, lambda i: (0, 0))]
    args = [a, b]
    if res is not None:
        in_specs.append(pl.BlockSpec((tm, N), lambda i: (i, 0)))
        args.append(res)
    return pl.pallas_call(
        body, name=name, grid=(M // tm,),
        in_specs=in_specs, out_specs=pl.BlockSpec((tm, N), lambda i: (i, 0)),
        out_shape=jax.ShapeDtypeStruct((M, N), out_dtype),
        compiler_params=_cparams("parallel"),
    )(*args)


def _rms_fwd(x, gain, name, col=0, width=None, tile=None):
    S = x.shape[0]
    W = width if width is not None else x.shape[1]
    tm = _tile(S, tile or TILES["rms"])

    def body(x_ref, g_ref, o_ref):
        xv = x_ref[...].astype(F32)
        r = lax.rsqrt(jnp.mean(xv * xv, axis=-1, keepdims=True) + NORM_EPS)
        o_ref[...] = (xv * r * g_ref[...]).astype(BF16)

    return pl.pallas_call(
        body, name=name, grid=(S // tm,),
        in_specs=[pl.BlockSpec((tm, W), lambda i: (i, col)), pl.BlockSpec((1, W), lambda i: (0, 0))],
        out_specs=pl.BlockSpec((tm, W), lambda i: (i, 0)),
        out_shape=jax.ShapeDtypeStruct((S, W), BF16),
        compiler_params=_cparams("parallel"),
    )(x, _row(gain))


def _rms_bwd(dy, x, gain, name, col=0, res=None, out_dtype=F32, tile=None):
    S, W = dy.shape
    tm = _tile(S, tile or TILES["rms"])

    def body(*refs):
        if res is None:
            dy_ref, x_ref, g_ref, dx_ref, dg_ref = refs
        else:
            dy_ref, x_ref, g_ref, r_ref, dx_ref, dg_ref = refs

        @pl.when(pl.program_id(0) == 0)
        def _():
            dg_ref[...] = jnp.zeros_like(dg_ref)

        xv = x_ref[...].astype(F32)
        d = dy_ref[...].astype(F32)
        r = lax.rsqrt(jnp.mean(xv * xv, axis=-1, keepdims=True) + NORM_EPS)
        xhat = xv * r
        dg_ref[...] += jnp.sum(d * xhat, axis=0, keepdims=True)
        dxhat = d * g_ref[...]
        dx = r * (dxhat - xhat * jnp.mean(dxhat * xhat, axis=-1, keepdims=True))
        if res is not None:
            dx = dx + r_ref[...]
        dx_ref[...] = dx.astype(out_dtype)

    in_specs = [pl.BlockSpec((tm, W), lambda i: (i, 0)), pl.BlockSpec((tm, W), lambda i: (i, col)),
                pl.BlockSpec((1, W), lambda i: (0, 0))]
    args = [dy, x, _row(gain)]
    if res is not None:
        in_specs.append(pl.BlockSpec((tm, W), lambda i: (i, 0)))
        args.append(res)
    return pl.pallas_call(
        body, name=name, grid=(S // tm,),
        in_specs=in_specs,
        out_specs=[pl.BlockSpec((tm, W), lambda i: (i, 0)), pl.BlockSpec((1, W), lambda i: (0, 0))],
        out_shape=[jax.ShapeDtypeStruct((S, W), out_dtype), jax.ShapeDtypeStruct((1, W), F32)],
        compiler_params=_cparams("arbitrary"),
    )(*args)


def _silu_parts(a):
    s = jax.nn.sigmoid(a)
    return a * s, s * (1.0 + a * (1.0 - s))


def _ffn_fwd(x, gain, wg_t, wu_t, wd, name, rider=None):
    S, D = x.shape
    Fd = wd.shape[0]
    tm = _tile(S, TILES["ffn_fwd"])
    fc = _lane_tile(Fd, 512)

    def body(*refs):
        i = pl.program_id(0)
        own, finish = _ride(rider, refs, 5, 3, i == 0, i == pl.num_programs(0) - 1)
        x_ref, g_ref, wg_ref, wu_ref, wd_ref, o_ref, a_ref, b_ref = own
        xv = x_ref[...]
        r = lax.rsqrt(jnp.mean(xv * xv, axis=-1, keepdims=True) + NORM_EPS)
        h = (xv * r * g_ref[...]).astype(BF16)
        acc = jnp.zeros((tm, D), F32)
        for c in range(Fd // fc):
            sl = slice(c * fc, (c + 1) * fc)
            a = _dot(h, wg_ref[sl, :], NT)
            b = _dot(h, wu_ref[sl, :], NT)
            a_ref[:, sl] = a.astype(BF16)
            b_ref[:, sl] = b.astype(BF16)
            z = (a * jax.nn.sigmoid(a) * b).astype(BF16)
            acc = acc + _dot(z, wd_ref[sl, :], NN)
        o_ref[...] = xv + 0.5 * acc
        finish()

    wspec = pl.BlockSpec((Fd, D), lambda i: (0, 0), pipeline_mode=pl.Buffered(1))
    extra = rider or _NO_RIDER
    return pl.pallas_call(
        body, name=name, grid=(S // tm,),
        in_specs=[pl.BlockSpec((tm, D), lambda i: (i, 0)), pl.BlockSpec((1, D), lambda i: (0, 0)), wspec, wspec,
                  wspec] + extra.in_specs,
        out_specs=[pl.BlockSpec((tm, D), lambda i: (i, 0)), pl.BlockSpec((tm, Fd), lambda i: (i, 0)),
                   pl.BlockSpec((tm, Fd), lambda i: (i, 0))] + extra.out_specs,
        out_shape=[jax.ShapeDtypeStruct((S, D), F32), jax.ShapeDtypeStruct((S, Fd), BF16),
                   jax.ShapeDtypeStruct((S, Fd), BF16)] + extra.out_shapes,
        scratch_shapes=extra.scratch,
        compiler_params=_cparams("arbitrary" if rider else "parallel"),
    )(x, _row(gain), wg_t, wu_t, wd, *extra.arrays)


def _ffn_bwd(g, x, gain, a, b, wg_t, wu_t, wd, name, rider=None):
    S, D = x.shape
    Fd = wd.shape[0]
    tm = _tile(S, TILES["ffn_bwd"])
    fc = Fd

    def body(*refs):
        i = pl.program_id(0)
        own, finish = _ride(rider, refs, 8, 5, i == 0, i == pl.num_programs(0) - 1)
        g_ref, x_ref, gain_ref, a_ref, b_ref, wg_ref, wu_ref, wd_ref, dx_ref, dz_ref, h_ref, dy_ref, dg_ref = own

        @pl.when(i == 0)
        def _():
            dg_ref[...] = jnp.zeros_like(dg_ref)

        gv = g_ref[...]
        xv = x_ref[...]
        r = lax.rsqrt(jnp.mean(xv * xv, axis=-1, keepdims=True) + NORM_EPS)
        xhat = xv * r
        h_ref[...] = (xhat * gain_ref[...]).astype(BF16)
        dy = (0.5 * gv).astype(BF16)
        dy_ref[...] = dy
        dh = jnp.zeros((tm, D), F32)
        for c in range(Fd // fc):
            sl = slice(c * fc, (c + 1) * fc)
            av = a_ref[:, sl].astype(F32)
            bv = b_ref[:, sl].astype(F32)
            dz = _dot(dy, wd_ref[sl, :], NT).astype(BF16)
            dz_ref[:, sl] = dz
            dzf = dz.astype(F32)
            silu, dsilu = _silu_parts(av)
            da = (dzf * bv * dsilu).astype(BF16)
            db = (dzf * silu).astype(BF16)
            dh = dh + _dot(da, wg_ref[sl, :], NN) + _dot(db, wu_ref[sl, :], NN)
        dg_ref[...] += jnp.sum(dh * xhat, axis=0, keepdims=True)
        dxhat = dh * gain_ref[...]
        dx_ref[...] = gv + r * (dxhat - xhat * jnp.mean(dxhat * xhat, axis=-1, keepdims=True))
        finish()

    wspec = pl.BlockSpec((Fd, D), lambda i: (0, 0), pipeline_mode=pl.Buffered(1))
    row = pl.BlockSpec((tm, D), lambda i: (i, 0))
    wide = pl.BlockSpec((tm, Fd), lambda i: (i, 0))
    extra = rider or _NO_RIDER
    return pl.pallas_call(
        body, name=name, grid=(S // tm,),
        in_specs=[row, row, pl.BlockSpec((1, D), lambda i: (0, 0)), wide, wide, wspec, wspec, wspec] + extra.in_specs,
        out_specs=[row, wide, row, row, pl.BlockSpec((1, D), lambda i: (0, 0))] + extra.out_specs,
        out_shape=[jax.ShapeDtypeStruct((S, D), F32), jax.ShapeDtypeStruct((S, Fd), BF16),
                   jax.ShapeDtypeStruct((S, D), BF16), jax.ShapeDtypeStruct((S, D), BF16),
                   jax.ShapeDtypeStruct((1, D), F32)] + extra.out_shapes,
        scratch_shapes=extra.scratch,
        compiler_params=_cparams("arbitrary"),
    )(g, x, _row(gain), a, b, wg_t, wu_t, wd, *extra.arrays)


def _ffn_dw(a, b, dz, h, dy, name):
    S, Fd = a.shape
    D = h.shape[1]
    ts = _tile(S, TILES["ffn_dw"])
    tf = _lane_tile(Fd, 256)

    def body(a_ref, b_ref, dz_ref, h_ref, dy_ref, og_ref, ou_ref, od_ref, accg, accu, accd):
        s = pl.program_id(1)

        @pl.when(s == 0)
        def _():
            accg[...] = jnp.zeros_like(accg)
            accu[...] = jnp.zeros_like(accu)
            accd[...] = jnp.zeros_like(accd)

        av = a_ref[...].astype(F32)
        bv = b_ref[...].astype(F32)
        dzf = dz_ref[...].astype(F32)
        silu, dsilu = _silu_parts(av)
        da = (dzf * bv * dsilu).astype(BF16)
        db = (dzf * silu).astype(BF16)
        z = (silu * bv).astype(BF16)
        hv = h_ref[...]
        accg[...] += _dot(da, hv, TN)
        accu[...] += _dot(db, hv, TN)
        accd[...] += _dot(z, dy_ref[...], TN)

        @pl.when(s == pl.num_programs(1) - 1)
        def _():
            og_ref[...] = accg[...].astype(BF16)
            ou_ref[...] = accu[...].astype(BF16)
            od_ref[...] = accd[...].astype(BF16)

    wide = pl.BlockSpec((ts, tf), lambda f, s: (s, f))
    row = pl.BlockSpec((ts, D), lambda f, s: (s, 0))
    out = pl.BlockSpec((tf, D), lambda f, s: (f, 0))
    return pl.pallas_call(
        body, name=name, grid=(Fd // tf, S // ts),
        in_specs=[wide, wide, wide, row, row], out_specs=[out, out, out],
        out_shape=[jax.ShapeDtypeStruct((Fd, D), BF16)] * 3,
        scratch_shapes=[pltpu.VMEM((tf, D), F32)] * 3,
        compiler_params=_cparams("parallel", "arbitrary"),
    )(a, b, dz, h, dy)


def _loss_head(x, target, gain, name, tile=None):
    S, D = x.shape
    tm = _tile(S, tile or TILES["ew"])

    def body(x_ref, t_ref, g_ref, dx_ref, dg_ref, loss_ref):
        @pl.when(pl.program_id(0) == 0)
        def _():
            dg_ref[...] = jnp.zeros_like(dg_ref)
            loss_ref[...] = jnp.zeros_like(loss_ref)

        xv = x_ref[...]
        r = lax.rsqrt(jnp.mean(xv * xv, axis=-1, keepdims=True) + NORM_EPS)
        xhat = xv * r
        e = xhat * g_ref[...] - t_ref[...]
        per_tok = jnp.mean(e * e, axis=-1, keepdims=True)
        loss_ref[...] += jnp.broadcast_to(0.5 * jnp.sum(per_tok, axis=0, keepdims=True), (1, LANE))
        dy = e * (1.0 / D)
        dg_ref[...] += jnp.sum(dy * xhat, axis=0, keepdims=True)
        dxhat = dy * g_ref[...]
        dx_ref[...] = r * (dxhat - xhat * jnp.mean(dxhat * xhat, axis=-1, keepdims=True))

    row = pl.BlockSpec((tm, D), lambda i: (i, 0))
    return pl.pallas_call(
        body, name=name, grid=(S // tm,),
        in_specs=[row, row, pl.BlockSpec((1, D), lambda i: (0, 0))],
        out_specs=[row, pl.BlockSpec((1, D), lambda i: (0, 0)), pl.BlockSpec((1, LANE), lambda i: (0, 0))],
        out_shape=[jax.ShapeDtypeStruct((S, D), F32), jax.ShapeDtypeStruct((1, D), F32),
                   jax.ShapeDtypeStruct((1, LANE), F32)],
        compiler_params=_cparams("arbitrary"),
    )(x, target, _row(gain))


def _shift_down(u, halo, k, rows):
    out = pltpu.roll(u, k, 0)
    for j in range(k):
        out = jnp.where(rows == j, halo[8 - k + j:8 - k + j + 1, :], out)
    return out


def _shift_up(u, halo, k, rows, n):
    out = pltpu.roll(u, n - k, 0)
    for j in range(k):
        out = jnp.where(rows == n - k + j, halo[j:j + 1, :], out)
    return out


def _conv_fwd(p, cw, name, tile=None):
    S, W3 = p.shape
    W = W3 // 3
    tm = _tile(S, tile or TILES["ew"])
    hb = tm // 8

    def body(p_ref, ph_ref, w_ref, v_ref):
        i = pl.program_id(0)
        bg = p_ref[:, 0:W].astype(F32)
        u = p_ref[:, W:2 * W].astype(F32) * p_ref[:, 2 * W:3 * W].astype(F32)
        uh = ph_ref[:, W:2 * W].astype(F32) * ph_ref[:, 2 * W:3 * W].astype(F32)
        uh = jnp.where(i > 0, uh, 0.0)
        rows = lax.broadcasted_iota(jnp.int32, (tm, 1), 0)
        u1 = _shift_down(u, uh, 1, rows)
        u2 = _shift_down(u, uh, 2, rows)
        y = w_ref[0:1, :] * u2 + w_ref[1:2, :] * u1 + w_ref[2:3, :] * u
        v_ref[...] = (bg * y).astype(BF16)

    return pl.pallas_call(
        body, name=name, grid=(S // tm,),
        in_specs=[pl.BlockSpec((tm, W3), lambda i: (i, 0)),
                  pl.BlockSpec((8, W3), lambda i: (jnp.maximum(i * hb - 1, 0), 0)),
                  pl.BlockSpec((8, W), lambda i: (0, 0))],
        out_specs=pl.BlockSpec((tm, W), lambda i: (i, 0)),
        out_shape=jax.ShapeDtypeStruct((S, W), BF16),
        compiler_params=_cparams("parallel"),
    )(p, p, cw)


def _conv_bwd(dv, p, cw, name, tile=None):
    S, W3 = p.shape
    W = W3 // 3
    tm = _tile(S, tile or TILES["ew"])
    hb = tm // 8
    last = S // 8 - 1

    def body(dv_ref, dvn_ref, p_ref, pp_ref, pn_ref, w_ref, dp_ref, dw_ref):
        i = pl.program_id(0)
        n = pl.num_programs(0)

        @pl.when(i == 0)
        def _():
            dw_ref[...] = jnp.zeros_like(dw_ref)

        bg = p_ref[:, 0:W].astype(F32)
        cg = p_ref[:, W:2 * W].astype(F32)
        zz = p_ref[:, 2 * W:3 * W].astype(F32)
        u = cg * zz
        uh = pp_ref[:, W:2 * W].astype(F32) * pp_ref[:, 2 * W:3 * W].astype(F32)
        uh = jnp.where(i > 0, uh, 0.0)
        rows = lax.broadcasted_iota(jnp.int32, (tm, 1), 0)
        u1 = _shift_down(u, uh, 1, rows)
        u2 = _shift_down(u, uh, 2, rows)
        w0, w1, w2 = w_ref[0:1, :], w_ref[1:2, :], w_ref[2:3, :]
        y = w0 * u2 + w1 * u1 + w2 * u
        dvv = dv_ref[...].astype(F32)
        dy = dvv * bg
        dyh = dvn_ref[...].astype(F32) * pn_ref[:, 0:W].astype(F32)
        dyh = jnp.where(i < n - 1, dyh, 0.0)
        d1 = _shift_up(dy, dyh, 1, rows, tm)
        d2 = _shift_up(dy, dyh, 2, rows, tm)
        du = w2 * dy + w1 * d1 + w0 * d2
        dp_ref[:, 0:W] = (dvv * y).astype(BF16)
        dp_ref[:, W:2 * W] = (du * zz).astype(BF16)
        dp_ref[:, 2 * W:3 * W] = (du * cg).astype(BF16)
        dw_ref[0:1, :] += jnp.sum(dy * u2, axis=0, keepdims=True)
        dw_ref[1:2, :] += jnp.sum(dy * u1, axis=0, keepdims=True)
        dw_ref[2:3, :] += jnp.sum(dy * u, axis=0, keepdims=True)

    return pl.pallas_call(
        body, name=name, grid=(S // tm,),
        in_specs=[pl.BlockSpec((tm, W), lambda i: (i, 0)),
                  pl.BlockSpec((8, W), lambda i: (jnp.minimum((i + 1) * hb, last), 0)),
                  pl.BlockSpec((tm, W3), lambda i: (i, 0)),
                  pl.BlockSpec((8, W3), lambda i: (jnp.maximum(i * hb - 1, 0), 0)),
                  pl.BlockSpec((8, W3), lambda i: (jnp.minimum((i + 1) * hb, last), 0)),
                  pl.BlockSpec((8, W), lambda i: (0, 0))],
        out_specs=[pl.BlockSpec((tm, W3), lambda i: (i, 0)), pl.BlockSpec((8, W), lambda i: (0, 0))],
        out_shape=[jax.ShapeDtypeStruct((S, W3), BF16), jax.ShapeDtypeStruct((8, W), F32)],
        compiler_params=_cparams("arbitrary"),
    )(dv, dv, p, p, p, cw)


def _rope_swap(r, lane):
    mid = NOPE + ROPE // 2
    first = (lane >= NOPE) & (lane < mid)
    second = (lane >= mid) & (lane < QK_DIM)
    return jnp.where(first, pltpu.roll(r, HEAD_PAD - ROPE // 2, 1), jnp.where(second, pltpu.roll(r, ROPE // 2, 1), 0.0))


def _rope_fwd(q_big, kv_big, proj, kr_col, ct, st, name, tile=None):
    S = q_big.shape[0]
    HW = HEADS * HEAD_PAD
    tm = _tile(S, tile or TILES["ew"])

    def body(q_ref, k_ref, v_ref, kr_ref, ct_ref, st_ref, qo_ref, ko_ref, vo_ref):
        lane = lax.broadcasted_iota(jnp.int32, (1, HEAD_PAD), 1)
        ctv, stv = ct_ref[...], st_ref[...]
        krr = pltpu.roll(kr_ref[...].astype(F32), NOPE, 1)
        kro = krr * ctv + _rope_swap(krr, lane) * stv
        for h in range(HEADS):
            sl = slice(h * HEAD_PAD, (h + 1) * HEAD_PAD)
            qh = q_ref[:, sl].astype(F32)
            qo_ref[:, sl] = (qh * ctv + _rope_swap(qh, lane) * stv).astype(BF16)
            ko_ref[:, sl] = (k_ref[:, sl].astype(F32) + kro).astype(BF16)
            vo_ref[:, sl] = jnp.where(lane == VDIM, 1.0, v_ref[:, sl].astype(F32)).astype(BF16)

    wide = pl.BlockSpec((tm, HW), lambda i: (i, 0))
    narrow = pl.BlockSpec((tm, HEAD_PAD), lambda i: (i, 0))
    return pl.pallas_call(
        body, name=name, grid=(S // tm,),
        in_specs=[wide, wide, pl.BlockSpec((tm, HW), lambda i: (i, 1)),
                  pl.BlockSpec((tm, HEAD_PAD), lambda i: (i, kr_col)), narrow, narrow],
        out_specs=[wide, wide, wide],
        out_shape=[jax.ShapeDtypeStruct((S, HW), BF16)] * 3,
        compiler_params=_cparams("parallel"),
    )(q_big, kv_big, kv_big, proj, ct, st)


def _rope_bwd(dq, dk, dv, ct, st, name, tile=None):
    S = dq.shape[0]
    HW = HEADS * HEAD_PAD
    tm = _tile(S, tile or TILES["ew"])

    def body(dq_ref, dk_ref, dv_ref, ct_ref, st_ref, oq_ref, okv_ref, okr_ref):
        lane = lax.broadcasted_iota(jnp.int32, (1, HEAD_PAD), 1)
        ctv, stv = ct_ref[...], st_ref[...]
        acc = jnp.zeros((tm, HEAD_PAD), F32)
        for h in range(HEADS):
            sl = slice(h * HEAD_PAD, (h + 1) * HEAD_PAD)
            d = dq_ref[:, sl].astype(F32)
            oq_ref[:, sl] = (d * ctv + _rope_swap(d * stv, lane)).astype(BF16)
            d = dk_ref[:, sl].astype(F32)
            okv_ref[:, sl] = jnp.where(lane < NOPE, d, 0.0).astype(BF16)
            acc = acc + jnp.where(lane >= NOPE, d * ctv + _rope_swap(d * stv, lane), 0.0)
        okv_ref[:, HW:2 * HW] = dv_ref[...].astype(BF16)
        okr_ref[...] = pltpu.roll(acc, HEAD_PAD - NOPE, 1).astype(BF16)

    wide = pl.BlockSpec((tm, HW), lambda i: (i, 0))
    narrow = pl.BlockSpec((tm, HEAD_PAD), lambda i: (i, 0))
    return pl.pallas_call(
        body, name=name, grid=(S // tm,),
        in_specs=[wide, wide, wide, narrow, narrow],
        out_specs=[wide, pl.BlockSpec((tm, 2 * HW), lambda i: (i, 0)), narrow],
        out_shape=[jax.ShapeDtypeStruct((S, HW), BF16), jax.ShapeDtypeStruct((S, 2 * HW), BF16),
                   jax.ShapeDtypeStruct((S, HEAD_PAD), BF16)],
        compiler_params=_cparams("parallel"),
    )(dq, dk, dv, ct, st)


def _pairs(n, by_key):
    if by_key:
        pr = [(i, j) for j in range(n) for i in range(j, n)]
    else:
        pr = [(i, j) for i in range(n) for j in range(i + 1)]
    qi = np.array([p[0] for p in pr], np.int32)
    kj = np.array([p[1] for p in pr], np.int32)
    return jnp.asarray(qi), jnp.asarray(kj)


_LOG2E = 1.4426950408889634
_LN2 = 0.6931471805599453


def _tile_mask(t):
    return lax.broadcasted_iota(jnp.int32, (t, t), 1) <= lax.broadcasted_iota(jnp.int32, (t, t), 0)


def _attn_fwd(q, k, v, name, tile, heads_per_step=1):
    S = q.shape[0]
    HW = HEADS * HEAD_PAD
    t = _tile(S, tile)
    n = S // t
    qi, kj = _pairs(n, by_key=False)
    c = (QK_DIM ** -0.5) * _LOG2E
    W = heads_per_step * HEAD_PAD

    def body(qi_ref, kj_ref, q_ref, k_ref, v_ref, o_ref, lse_ref, m_s, acc_s):
        p_id = pl.program_id(1)
        i, j = qi_ref[p_id], kj_ref[p_id]
        lanes = [slice(u * HEAD_PAD, (u + 1) * HEAD_PAD) for u in range(heads_per_step)]

        @pl.when(j == 0)
        def _():
            m_s[...] = jnp.full_like(m_s, -jnp.inf)
            acc_s[...] = jnp.zeros_like(acc_s)

        def step(on_diagonal):
            for u, sl in enumerate(lanes):
                s = _dot(q_ref[:, sl], k_ref[:, sl], NT)
                if on_diagonal:
                    s = jnp.where(_tile_mask(t), s, -jnp.inf)
                m_old = m_s[u]
                m_new = jnp.maximum(m_old, jnp.max(s, axis=-1, keepdims=True))
                p = jnp.exp2((s - m_new) * c).astype(BF16)
                acc_s[:, sl] = jnp.exp2((m_old - m_new) * c) * acc_s[:, sl] + _dot(p, v_ref[:, sl], NN)
                m_s[u] = m_new

        @pl.when(i == j)
        def _():
            step(True)

        @pl.when(i != j)
        def _():
            step(False)

        @pl.when(j == i)
        def _():
            for u, sl in enumerate(lanes):
                acc = acc_s[:, sl]
                l = acc[:, VDIM:VDIM + 1]
                o_ref[:, sl] = (acc * (1.0 / l)).astype(BF16)
                lse_ref[:, sl] = jnp.broadcast_to(m_s[u] * c + jnp.log2(l), (t, HEAD_PAD))

    qspec = pl.BlockSpec((t, W), lambda h, p, qi, kj: (qi[p], h))
    kspec = pl.BlockSpec((t, W), lambda h, p, qi, kj: (kj[p], h))
    grid_spec = pltpu.PrefetchScalarGridSpec(
        num_scalar_prefetch=2, grid=(HEADS // heads_per_step, int(qi.shape[0])),
        in_specs=[qspec, kspec, kspec], out_specs=[qspec, qspec],
        scratch_shapes=[pltpu.VMEM((heads_per_step, t, 1), F32), pltpu.VMEM((t, W), F32)])
    return pl.pallas_call(
        body, name=name, grid_spec=grid_spec,
        out_shape=[jax.ShapeDtypeStruct((S, HW), BF16), jax.ShapeDtypeStruct((S, HW), F32)],
        compiler_params=_cparams("parallel", "arbitrary"),
    )(qi, kj, q, k, v)


def _attn_bwd(q, k, v, o, do, lse2, name, split_diagonal=False):
    S = q.shape[0]
    HW = HEADS * HEAD_PAD
    t = _tile(S, TILES["attn"])
    n = S // t
    qi, kj = _pairs(n, by_key=True)
    scale = QK_DIM ** -0.5

    def body(qi_ref, kj_ref, q_ref, k_ref, v_ref, o_ref, do_ref, lse_ref, dq_ref, dk_ref, dv_ref, dk_s, dv_s):
        p_id = pl.program_id(1)
        i, j = qi_ref[p_id], kj_ref[p_id]

        @pl.when(p_id == 0)
        def _():
            dq_ref[...] = jnp.zeros_like(dq_ref)

        @pl.when(i == j)
        def _():
            dk_s[...] = jnp.zeros_like(dk_s)
            dv_s[...] = jnp.zeros_like(dv_s)

        def step(mask):
            qv, kv, vv = q_ref[...], k_ref[...], v_ref[...]
            dov = do_ref[...]
            p = jnp.exp(_dot(qv, kv, NT) * scale - lse_ref[:, 0:1] * _LN2)
            if mask is not None:
                p = jnp.where(mask(), p, 0.0)
            delta = jnp.sum(dov.astype(F32) * o_ref[...].astype(F32), axis=-1, keepdims=True)
            dv_s[...] += _dot(p.astype(BF16), dov, TN)
            ds = (p * (_dot(dov, vv, NT) - delta) * scale).astype(BF16)
            dk_s[...] += _dot(ds, qv, TN)
            rows = pl.ds(pl.multiple_of(i * t, t), t)
            dq_ref[rows, :] += _dot(ds, kv, NN)

        def by_index():
            rows_i = lax.broadcasted_iota(jnp.int32, (t, t), 0) + i * t
            cols_j = lax.broadcasted_iota(jnp.int32, (t, t), 1) + j * t
            return cols_j <= rows_i

        if split_diagonal:
            @pl.when(i == j)
            def _():
                step(lambda: _tile_mask(t))

            @pl.when(i != j)
            def _():
                step(None)
        else:
            step(by_index)

        @pl.when(i == n - 1)
        def _():
            dk_ref[...] = dk_s[...]
            dv_ref[...] = dv_s[...]

    qspec = pl.BlockSpec((t, HEAD_PAD), lambda h, p, qi, kj: (qi[p], h))
    kspec = pl.BlockSpec((t, HEAD_PAD), lambda h, p, qi, kj: (kj[p], h))
    grid_spec = pltpu.PrefetchScalarGridSpec(
        num_scalar_prefetch=2, grid=(HEADS, int(qi.shape[0])),
        in_specs=[qspec, kspec, kspec, qspec, qspec, qspec],
        out_specs=[pl.BlockSpec((S, HEAD_PAD), lambda h, p, qi, kj: (0, h)), kspec, kspec],
        scratch_shapes=[pltpu.VMEM((t, HEAD_PAD), F32), pltpu.VMEM((t, HEAD_PAD), F32)])
    return pl.pallas_call(
        body, name=name, grid_spec=grid_spec,
        out_shape=[jax.ShapeDtypeStruct((S, HW), F32)] * 3,
        compiler_params=_cparams("parallel", "arbitrary"),
    )(qi, kj, q, k, v, o, do, lse2)


_SQRT_HALF = 0.7071067811865476
_INV_SQRT_2PI = 0.3989422804014327


def _sg_select(r, grp):
    out = jnp.where(grp == 0, r[0:SG_CHUNK, :], 0.0)
    for g in range(1, SG_GROUPS):
        out = out + jnp.where(grp == g, r[g * SG_CHUNK:(g + 1) * SG_CHUNK, :], 0.0)
    return out


def _sgu_fwd(proj, gain, wstack, bmat, name, tile=None):
    S = proj.shape[0]
    W = SG_WIDTH
    tm = _tile(S, tile or TILES["sgu"])

    def body(z_ref, g_ref, w_ref, b_ref, o_ref):
        z = z_ref[...].astype(F32)
        zg = 0.5 * z * (1.0 + lax.erf(z * _SQRT_HALF))
        u, vv = zg[:, 0:W], zg[:, W:2 * W]
        r = lax.rsqrt(jnp.mean(vv * vv, axis=-1, keepdims=True) + NORM_EPS)
        vn = (vv * r * g_ref[...]).astype(BF16)
        grp = lax.broadcasted_iota(jnp.int32, (1, W), 1) // SG_GROUP_DIM
        for c in range(tm // SG_CHUNK):
            sl = slice(c * SG_CHUNK, (c + 1) * SG_CHUNK)
            mixed = _sg_select(_dot(w_ref[...], vn[sl, :], NN), grp) + b_ref[...]
            o_ref[sl, :] = (u[sl, :] * mixed).astype(BF16)

    return pl.pallas_call(
        body, name=name, grid=(S // tm,),
        in_specs=[pl.BlockSpec((tm, 2 * W), lambda i: (i, 0)), pl.BlockSpec((1, W), lambda i: (0, 0)),
                  pl.BlockSpec(wstack.shape, lambda i: (0, 0)), pl.BlockSpec(bmat.shape, lambda i: (0, 0))],
        out_specs=pl.BlockSpec((tm, W), lambda i: (i, 0)),
        out_shape=jax.ShapeDtypeStruct((S, W), BF16),
        compiler_params=_cparams("parallel"),
    )(proj, _row(gain), wstack, bmat)


def _sgu_bwd(dsg, proj, gain, wstack, wtstack, bmat, gsum, name, tile=None):
    S = proj.shape[0]
    W = SG_WIDTH
    tm = _tile(S, tile or TILES["sgu"])
    GS = SG_GROUPS * SG_CHUNK

    def body(d_ref, z_ref, g_ref, w_ref, wt_ref, b_ref, e_ref, dz_ref, dw_ref, db_ref, dg_ref, dw_s, db_s):
        i = pl.program_id(0)

        @pl.when(i == 0)
        def _():
            dw_s[...] = jnp.zeros_like(dw_s)
            db_s[...] = jnp.zeros_like(db_s)
            dg_ref[...] = jnp.zeros_like(dg_ref)

        z = z_ref[...].astype(F32)
        cdf = 0.5 * (1.0 + lax.erf(z * _SQRT_HALF))
        zg = z * cdf
        u, vv = zg[:, 0:W], zg[:, W:2 * W]
        r = lax.rsqrt(jnp.mean(vv * vv, axis=-1, keepdims=True) + NORM_EPS)
        vhat = vv * r
        vn = (vhat * g_ref[...]).astype(BF16)
        grp = lax.broadcasted_iota(jnp.int32, (1, W), 1) // SG_GROUP_DIM
        d = d_ref[...].astype(F32)
        du_parts, dvn_parts = [], []
        for c in range(tm // SG_CHUNK):
            sl = slice(c * SG_CHUNK, (c + 1) * SG_CHUNK)
            vc = vn[sl, :]
            mixed = _sg_select(_dot(w_ref[...], vc, NN), grp) + b_ref[...]
            dc = d[sl, :]
            du_parts.append(dc * mixed)
            dmix = dc * u[sl, :]
            db_s[...] += dmix
            dmb = dmix.astype(BF16)
            dvn_parts.append(_sg_select(_dot(wt_ref[...], dmb, NN), grp))
            astack = jnp.concatenate([jnp.where(grp == g, dmb, jnp.zeros_like(dmb)) for g in range(SG_GROUPS)], axis=0)
            dw_s[...] += _dot(astack, vc, NT)
        du = jnp.concatenate(du_parts, axis=0)
        dvn = jnp.concatenate(dvn_parts, axis=0)
        dg_ref[...] += jnp.sum(dvn * vhat, axis=0, keepdims=True)
        dvhat = dvn * g_ref[...]
        dvv = r * (dvhat - vhat * jnp.mean(dvhat * vhat, axis=-1, keepdims=True))
        dgelu = cdf + z * (_INV_SQRT_2PI * jnp.exp(-0.5 * z * z))
        dz_ref[:, 0:W] = (du * dgelu[:, 0:W]).astype(BF16)
        dz_ref[:, W:2 * W] = (dvv * dgelu[:, W:2 * W]).astype(BF16)

        @pl.when(i == pl.num_programs(0) - 1)
        def _():
            dw_ref[...] = dw_s[...]
            db_ref[...] = lax.dot_general(db_s[...], e_ref[...], NN, precision=lax.Precision.HIGHEST,
                                          preferred_element_type=F32)

    full = lambda a: pl.BlockSpec(a.shape, lambda i: (0, 0))
    return pl.pallas_call(
        body, name=name, grid=(S // tm,),
        in_specs=[pl.BlockSpec((tm, W), lambda i: (i, 0)), pl.BlockSpec((tm, 2 * W), lambda i: (i, 0)),
                  pl.BlockSpec((1, W), lambda i: (0, 0)), full(wstack), full(wtstack), full(bmat), full(gsum)],
        out_specs=[pl.BlockSpec((tm, 2 * W), lambda i: (i, 0)), pl.BlockSpec((GS, SG_CHUNK), lambda i: (0, 0)),
                   pl.BlockSpec((SG_CHUNK, LANE), lambda i: (0, 0)), pl.BlockSpec((1, W), lambda i: (0, 0))],
        out_shape=[jax.ShapeDtypeStruct((S, 2 * W), BF16), jax.ShapeDtypeStruct((GS, SG_CHUNK), F32),
                   jax.ShapeDtypeStruct((SG_CHUNK, LANE), F32), jax.ShapeDtypeStruct((1, W), F32)],
        scratch_shapes=[pltpu.VMEM((GS, SG_CHUNK), F32), pltpu.VMEM((SG_CHUNK, W), F32)],
        compiler_params=_cparams("arbitrary"),
    )(dsg, proj, _row(gain), wstack, wtstack, bmat, gsum)


WEIGHTS = ['ffn_pre_norm', 'ffn_pre_w_gate', 'ffn_pre_w_up', 'ffn_pre_w_down', 'mix_norm', 'ffn_post_norm',
           'ffn_post_w_gate', 'ffn_post_w_up', 'ffn_post_w_down', 'even_w_in', 'q_norm', 'w_uq', 'kv_norm', 'w_ukv',
           'sg_norm', 'sg_w', 'sg_b', 'even_w_out', 'conv_w_in', 'conv_w', 'conv_w_out', 'final_norm']
SHARD_AXIS = dict(ffn_pre_w_gate=2, ffn_pre_w_up=2, ffn_pre_w_down=1, ffn_post_w_gate=2, ffn_post_w_up=2,
                  ffn_post_w_down=1, even_w_in=2, w_uq=2, w_ukv=2, even_w_out=1, conv_w_in=2, conv_w=2, conv_w_out=1)
SHARDED = [n for n in WEIGHTS if n in SHARD_AXIS]
REPLICATED = [n for n in WEIGHTS if n not in SHARD_AXIS]


def _to_t(name, w):
    return jnp.swapaxes(w, 1, 2) if SHARD_AXIS[name] == 2 else w


def _rows_of(n):
    return -(-n // PACK_W)


def _pad_rows(a, mult, axis):
    r = a.shape[axis]
    extra = (-r) % mult
    if extra == 0:
        return a
    pad = [(0, 0)] * a.ndim
    pad[axis] = (0, extra)
    return jnp.pad(a, pad)


def _flat_rows(a, lead):
    flat = a.reshape(a.shape[:lead] + (-1,))
    n = flat.shape[-1]
    flat = _pad_rows(flat, PACK_W, lead)
    return flat.reshape(a.shape[:lead] + (_rows_of(n), PACK_W))


def _pack(pieces, lead, mult, piece_mult=1):
    rows, offs, off = [], [], 0
    for p in pieces:
        r = _pad_rows(_flat_rows(p, lead), piece_mult, lead)
        rows.append(r)
        offs.append(off)
        off += r.shape[lead]
    return _pad_rows(jnp.concatenate(rows, axis=lead), mult, lead), offs


def _unpack(buf, off, shape, lead):
    n = math.prod(shape)
    r = _rows_of(n)
    piece = lax.slice_in_dim(buf, off, off + r, axis=lead)
    piece = piece.reshape(buf.shape[:lead] + (r * PACK_W,))
    piece = lax.slice_in_dim(piece, 0, n, axis=lead)
    return piece.reshape(buf.shape[:lead] + tuple(shape))


def _head_pad(w, per_head, keep):
    k = w.shape[-1]
    w = w.reshape(HEADS, per_head, k)[:, keep[0]:keep[1]]
    w = jnp.pad(w, ((0, 0), (0, HEAD_PAD - (keep[1] - keep[0])), (0, 0)))
    return w.reshape(HEADS * HEAD_PAD, k)


def _head_unpad(w, n):
    return w.reshape(HEADS, HEAD_PAD, w.shape[-1])[:, :n]


def kernel(x, positions, ffn_pre_norm, ffn_pre_w_gate, ffn_pre_w_up, ffn_pre_w_down, mix_norm, ffn_post_norm, ffn_post_w_gate, ffn_post_w_up, ffn_post_w_down, even_w_in, q_norm, w_uq, kv_norm, w_ukv, sg_norm, sg_w, sg_b, even_w_out, conv_w_in, conv_w, conv_w_out, final_norm, loss_target, m_ffn_pre_norm, m_ffn_pre_w_gate, m_ffn_pre_w_up, m_ffn_pre_w_down, m_mix_norm, m_ffn_post_norm, m_ffn_post_w_gate, m_ffn_post_w_up, m_ffn_post_w_down, m_even_w_in, m_q_norm, m_w_uq, m_kv_norm, m_w_ukv, m_sg_norm, m_sg_w, m_sg_b, m_even_w_out, m_conv_w_in, m_conv_w, m_conv_w_out, m_final_norm, v_ffn_pre_norm, v_ffn_pre_w_gate, v_ffn_pre_w_up, v_ffn_pre_w_down, v_mix_norm, v_ffn_post_norm, v_ffn_post_w_gate, v_ffn_post_w_up, v_ffn_post_w_down, v_even_w_in, v_q_norm, v_w_uq, v_kv_norm, v_w_ukv, v_sg_norm, v_sg_w, v_sg_b, v_even_w_out, v_conv_w_in, v_conv_w, v_conv_w_out, v_final_norm):
    given = dict(locals())
    w_loc = {n: given[n] for n in WEIGHTS}
    m_loc = {n: given["m_" + n] for n in WEIGHTS}
    v_loc = {n: given["v_" + n] for n in WEIGHTS}

    S, D = x.shape[1], x.shape[2]
    depth = ffn_pre_norm.shape[0]
    QL, KVL = q_norm.shape[1], kv_norm.shape[1]
    ZW = 2 * SG_WIDTH
    assert x.shape[0] == 1 and ZW % KVL == 0 and (ZW + KVL) % HEAD_PAD == 0 and (ZW + KVL + 2 * HEAD_PAD) % QL == 0
    col_ckv = ZW // KVL
    col_kr = (ZW + KVL) // HEAD_PAD
    col_cq = (ZW + KVL + 2 * HEAD_PAD) // QL

    t_loc = {n: _to_t(n, w_loc[n]) for n in SHARDED}
    full = {n: {} for n in SHARDED}

    def ffn_keys(kind, l):
        return [("ffn_%s_w_%s" % (kind, part), l) for part in ("gate", "up", "down")]

    def mixer_keys(l):
        names = ("even_w_in", "w_uq", "w_ukv", "even_w_out") if l % 2 == 0 else ("conv_w_in", "conv_w", "conv_w_out")
        return [(n, l // 2) for n in names]

    def local_pack(keys):
        return _pack([t_loc[n][l].astype(BF16) for n, l in keys], 0, 16, piece_mult=16)

    def take_gathered(gathered, keys, offs):
        for (n, l), off in zip(keys, offs):
            piece = _unpack(gathered, off, t_loc[n].shape[1:], 1)
            full[n][l] = piece.reshape(N_DEV * piece.shape[1], piece.shape[2])

    def gather_rider(keys):
        pack, offs = local_pack(keys)
        return _Exchange("gather", [pack]), offs

    first_keys = ffn_keys("pre", 0)
    pack0, offs0 = local_pack(first_keys)
    take_gathered(_all_gather(pack0, "gather_weights"), first_keys, offs0)

    tril = jnp.tril(jnp.ones((SG_CHUNK, SG_CHUNK), F32))
    even_ops = {}

    def even_operands(e):
        if e not in even_ops:
            wi = full["even_w_in"][e]
            zrow = lambda k: jnp.zeros((k, D), BF16)
            ops = dict(win_pad=jnp.concatenate(
                [wi[QL + KVL + ROPE:], wi[QL:QL + KVL], wi[QL + KVL:QL + KVL + ROPE], zrow(HEAD_PAD - ROPE),
                 zrow(HEAD_PAD), wi[:QL]], axis=0))
            ops["wq_big"] = _head_pad(full["w_uq"][e], QK_DIM, (0, QK_DIM))
            wkv = full["w_ukv"][e]
            ops["wkv_big"] = jnp.concatenate([_head_pad(wkv, NOPE + VDIM, (0, NOPE)),
                                              _head_pad(wkv, NOPE + VDIM, (NOPE, NOPE + VDIM))], axis=0)
            wo = full["even_w_out"][e]
            ops["wo_attn"] = _head_pad(wo[:HEADS * VDIM], VDIM, (0, VDIM))
            ops["wo_sg"] = wo[HEADS * VDIM:]
            wt = sg_w[e] * tril
            ops["wstack"] = wt.reshape(SG_GROUPS * SG_CHUNK, SG_CHUNK).astype(BF16)
            ops["wtstack"] = jnp.swapaxes(wt, 1, 2).reshape(SG_GROUPS * SG_CHUNK, SG_CHUNK).astype(BF16)
            ops["bmat"] = jnp.repeat(sg_b[e].T, SG_GROUP_DIM, axis=1)
            even_ops[e] = ops
        return even_ops[e]

    gsum = (jnp.arange(SG_WIDTH)[:, None] // SG_GROUP_DIM == jnp.arange(LANE)[None, :]).astype(F32)

    def conv_taps(o):
        return jnp.pad(jnp.swapaxes(full["conv_w"][o], 0, 1).astype(F32), ((0, 8 - CONV_K), (0, 0)))

    inv_freq = ROPE_THETA ** (-jnp.arange(0, ROPE, 2, dtype=F32) / ROPE)
    ang = positions[0].astype(F32)[:, None] * inv_freq
    cos, sin = jnp.cos(ang), jnp.sin(ang)
    ones, zeros = jnp.ones((S, NOPE), F32), jnp.zeros((S, HEAD_PAD - QK_DIM), F32)
    ct = jnp.concatenate([ones, cos, cos, zeros], axis=1)
    st = jnp.concatenate([0.0 * ones, -sin, sin, zeros], axis=1)

    xs = x[0]
    saved = []
    def ffn_forward(xin, kind, l, next_keys):
        gain = (ffn_pre_norm if kind == "pre" else ffn_post_norm)[l]
        wg, wu, wd = (full[n][l] for n, _ in ffn_keys(kind, l))
        if not next_keys:
            return _ffn_fwd(xin, gain, wg, wu, wd, "ffn_fwd")
        rider, offs = gather_rider(next_keys)
        xo, a, b, gathered = _ffn_fwd(xin, gain, wg, wu, wd, "ffn_fwd_gather", rider=rider)
        take_gathered(gathered, next_keys, offs)
        return xo, a, b

    rms_fwd, rms_bwd = _rms_fwd, _rms_bwd

    def layer_helpers(l):
        big = l >= depth // 2
        ew = dict(tile=TILES["mixer_big"] if big else None)
        return [functools.partial(_mm, tn_tile=TILES["mm_tn_big"] if big else None)] + [
            functools.partial(f, **ew) for f in (_rope_fwd, _rope_bwd, _sgu_fwd, _sgu_bwd, _conv_fwd, _conv_bwd)]

    for l in range(depth):
        mm, rope_fwd, rope_bwd, sgu_fwd, sgu_bwd, conv_fwd, conv_bwd = layer_helpers(l)
        sv = dict(x0=xs)
        x1, sv["a1"], sv["b1"] = ffn_forward(xs, "pre", l, (mixer_keys(0) if l == 0 else []) + ffn_keys("post", l))
        h = rms_fwd(x1, mix_norm[l], "mix_norm_fwd")
        sv.update(x1=x1, h=h)
        if l % 2 == 0:
            e = l // 2
            ops = even_operands(e)
            proj = mm(h, ops["win_pad"], "nt", "even_in_proj", out_dtype=F32)
            qn = rms_fwd(proj, q_norm[e], "q_norm_fwd", col=col_cq, width=QL)
            kvn = rms_fwd(proj, kv_norm[e], "kv_norm_fwd", col=col_ckv, width=KVL)
            q_big = mm(qn, ops["wq_big"], "nt", "q_up_proj", out_dtype=F32)
            kv_big = mm(kvn, ops["wkv_big"], "nt", "kv_up_proj", out_dtype=BF16)
            q_r, k_r, v_r = rope_fwd(q_big, kv_big, proj, col_kr, ct, st, "rope_fwd")
            o_att, lse = _attn_fwd(q_r, k_r, v_r, "attn_fwd", TILES["attn_wide"])
            sg = sgu_fwd(proj, sg_norm[e], ops["wstack"], ops["bmat"], "sgu_fwd")
            tmp = mm(o_att, ops["wo_attn"], "nn", "even_out_attn", out_dtype=F32, res=x1)
            x2 = mm(sg, ops["wo_sg"], "nn", "even_out_sg", out_dtype=F32, res=tmp)
            sv.update(proj=proj, qn=qn, kvn=kvn, q=q_r, k=k_r, v=v_r, o=o_att, lse=lse, sg=sg)
        else:
            o = l // 2
            p = mm(h, full["conv_w_in"][o], "nt", "conv_in_proj", out_dtype=BF16)
            cv = conv_fwd(p, conv_taps(o), "conv_fwd")
            x2 = mm(cv, full["conv_w_out"][o], "nn", "conv_out_proj", out_dtype=F32, res=x1)
            sv.update(p=p, cv=cv)
        sv["x2"] = x2
        next_keys = ffn_keys("pre", l + 1) + mixer_keys(l + 1) if l + 1 < depth else []
        xs, sv["a2"], sv["b2"] = ffn_forward(x2, "post", l, next_keys)
        saved.append(sv)

    gr = {n: [None] * w_loc[n].shape[0] for n in REPLICATED if n != "final_norm"}
    per_layer = {n: [None] * w_loc[n].shape[0] for n in SHARDED}
    pending = []

    def scatter_rider():
        pieces, where, off = [], [], 0
        for n, l, g in pending:
            piece = _pad_rows(_flat_rows(g.astype(BF16).reshape(N_DEV, -1), 1), 16, 1)
            pieces.append(piece)
            where.append((n, l, off))
            off += piece.shape[1]
        if off % GRAD_ROWS_MULT:
            pieces.append(jnp.zeros((N_DEV, (-off) % GRAD_ROWS_MULT, PACK_W), BF16))
        pending.clear()
        return _Exchange("scatter", pieces), where

    def take_scattered(received, where):
        owned = _sum_slots(received, "sum_grad_shards")
        for n, l, off in where:
            per_layer[n][l] = _unpack(owned, off, t_loc[n].shape[1:], 0)

    def ffn_backward(dxin, xin, kind, l, a, b):
        gain = (ffn_pre_norm if kind == "pre" else ffn_post_norm)[l]
        keys = ffn_keys(kind, l)
        wg, wu, wd = (full[n][l] for n, _ in keys)
        if pending:
            rider, where = scatter_rider()
            dxo, dz, hh, dy, dgain, received = _ffn_bwd(dxin, xin, gain, a, b, wg, wu, wd, "ffn_bwd_scatter", rider=rider)
            take_scattered(received, where)
        else:
            dxo, dz, hh, dy, dgain = _ffn_bwd(dxin, xin, gain, a, b, wg, wu, wd, "ffn_bwd")
        gr["ffn_%s_norm" % kind][l] = dgain[0]
        for (n, _), g in zip(keys, _ffn_dw(a, b, dz, hh, dy, "ffn_dw")):
            pending.append((n, l, g))
        return dxo

    dx, g_final, loss_part = _loss_head(xs, loss_target[0], final_norm, "loss_head")
    for l in reversed(range(depth)):
        mm, rope_fwd, rope_bwd, sgu_fwd, sgu_bwd, conv_fwd, conv_bwd = layer_helpers(l)
        sv = saved[l]
        dx = ffn_backward(dx, sv["x2"], "post", l, sv["a2"], sv["b2"])
        h = sv["h"]
        if l % 2 == 0:
            e = l // 2
            ops = even_operands(e)
            d_o = mm(dx, ops["wo_attn"], "nt", "even_out_attn_bwd", out_dtype=BF16)
            d_sg = mm(dx, ops["wo_sg"], "nt", "even_out_sg_bwd", out_dtype=BF16)
            g_wo_attn = mm(sv["o"], dx, "tn", "even_out_attn_dw", out_dtype=F32)
            g_wo_sg = mm(sv["sg"], dx, "tn", "even_out_sg_dw", out_dtype=F32)
            dq, dk, dv = _attn_bwd(sv["q"], sv["k"], sv["v"], sv["o"], d_o, sv["lse"], "attn_bwd", split_diagonal=True)
            dq_big, dkv_big, dkr = rope_bwd(dq, dk, dv, ct, st, "rope_bwd")
            dz_sg, g_wstack, g_bias, g_sgn = sgu_bwd(d_sg, sv["proj"], sg_norm[e], ops["wstack"], ops["wtstack"],
                                                      ops["bmat"], gsum, "sgu_bwd")
            dqn = mm(dq_big, ops["wq_big"], "nn", "q_up_proj_bwd", out_dtype=F32)
            g_wq_big = mm(dq_big, sv["qn"], "tn", "q_up_proj_dw", out_dtype=F32)
            dkvn = mm(dkv_big, ops["wkv_big"], "nn", "kv_up_proj_bwd", out_dtype=F32)
            g_wkv_big = mm(dkv_big, sv["kvn"], "tn", "kv_up_proj_dw", out_dtype=F32)
            dcq, g_qn = rms_bwd(dqn, sv["proj"], q_norm[e], "q_norm_bwd", col=col_cq, out_dtype=BF16)
            dckv, g_kvn = rms_bwd(dkvn, sv["proj"], kv_norm[e], "kv_norm_bwd", col=col_ckv, out_dtype=BF16)
            dproj = jnp.concatenate([dz_sg, dckv, dkr, jnp.zeros((S, HEAD_PAD), BF16), dcq], axis=1)
            dh = mm(dproj, ops["win_pad"], "nn", "even_in_proj_bwd", out_dtype=F32)
            g_win = mm(dproj, h, "tn", "even_in_proj_dw", out_dtype=F32)
            o_cq, o_ckv, o_kr = col_cq * QL, col_ckv * KVL, col_kr * HEAD_PAD
            hw = HEADS * HEAD_PAD
            pending.append(("even_w_in", e, jnp.concatenate(
                [g_win[o_cq:o_cq + QL], g_win[o_ckv:o_ckv + KVL], g_win[o_kr:o_kr + ROPE], g_win[:ZW]], axis=0)))
            pending.append(("w_uq", e, _head_unpad(g_wq_big, QK_DIM).reshape(HEADS * QK_DIM, QL)))
            pending.append(("w_ukv", e, jnp.concatenate(
                [_head_unpad(g_wkv_big[:hw], NOPE), _head_unpad(g_wkv_big[hw:], VDIM)],
                axis=1).reshape(HEADS * (NOPE + VDIM), KVL)))
            pending.append(("even_w_out", e, jnp.concatenate(
                [_head_unpad(g_wo_attn, VDIM).reshape(HEADS * VDIM, D), g_wo_sg], axis=0)))
            gr["q_norm"][e], gr["kv_norm"][e], gr["sg_norm"][e] = g_qn[0], g_kvn[0], g_sgn[0]
            gr["sg_w"][e] = g_wstack.reshape(SG_GROUPS, SG_CHUNK, SG_CHUNK) * tril
            gr["sg_b"][e] = g_bias[:, :SG_GROUPS].T
        else:
            o = l // 2
            dcv = mm(dx, full["conv_w_out"][o], "nt", "conv_out_proj_bwd", out_dtype=BF16)
            pending.append(("conv_w_out", o, mm(sv["cv"], dx, "tn", "conv_out_proj_dw", out_dtype=BF16)))
            dp, dcw = conv_bwd(dcv, sv["p"], conv_taps(o), "conv_bwd")
            dh = mm(dp, full["conv_w_in"][o], "nn", "conv_in_proj_bwd", out_dtype=F32)
            pending.append(("conv_w_in", o, mm(dp, h, "tn", "conv_in_proj_dw", out_dtype=BF16)))
            pending.append(("conv_w", o, jnp.swapaxes(dcw[:CONV_K], 0, 1)))
        dx, dgain = rms_bwd(dh, sv["x1"], mix_norm[l], "mix_norm_bwd", res=dx)
        gr["mix_norm"][l] = dgain[0]
        dx = ffn_backward(dx, sv["x0"], "pre", l, sv["a1"], sv["b1"])
    grad_x = dx[None]

    rider, where = scatter_rider()
    take_scattered(_exchange("scatter", rider.arrays, "scatter_grads"), where)
    grads = {n: _to_t(n, jnp.stack(per_layer[n])) for n in SHARDED}

    small = [jnp.stack(gr[n]) for n in REPLICATED if n != "final_norm"] + [g_final[0], loss_part[0, :1]]
    spack, soffs = _pack(small, 0, SMALL_ROWS_MULT)
    sgath = _all_gather(spack, "gather_small_grads")
    ssum = _sum_slots(sgath, "sum_small_grads")
    names_small = [n for n in REPLICATED if n != "final_norm"] + ["final_norm", "loss"]
    for n, off, piece in zip(names_small, soffs, small):
        val = _unpack(ssum, off, piece.shape, 0)
        if n == "loss":
            loss = val[0]
        else:
            grads[n] = val

    delta, new_m, new_v = {}, {}, {}
    for n in SHARDED:
        two_d = lambda a: a.reshape(-1, a.shape[-1])
        d, nm, nv = _adamw(two_d(w_loc[n]), two_d(grads[n]), two_d(m_loc[n]), two_d(v_loc[n]), "adamw")
        delta[n], new_m[n], new_v[n] = (a.reshape(w_loc[n].shape) for a in (d, nm, nv))
    flat = lambda d: _pack([d[n] for n in REPLICATED], 0, SMALL_ROWS_MULT)
    (wf, aoffs), (gf, _), (mf, _), (vf, _) = flat(w_loc), flat(grads), flat(m_loc), flat(v_loc)
    for res, buf in zip((delta, new_m, new_v), _adamw(wf, gf, mf, vf, "adamw_replicated")):
        for n, off in zip(REPLICATED, aoffs):
            res[n] = _unpack(buf, off, w_loc[n].shape, 0)
    outs = [loss, grad_x] + [grads[n] for n in WEIGHTS]
    for res in (delta, new_m, new_v):
        outs += [res[n] for n in WEIGHTS]
    return tuple(outs)
```

```python
import math

import numpy as np
import jax
import jax.numpy as jnp
from jax import lax
from jax.experimental import pallas as pl
from jax.experimental.pallas import tpu as pltpu

F32 = jnp.float32
BF16 = jnp.bfloat16

N_DEV = 8
NORM_EPS = 1e-6
HEADS = 8
NOPE = 64
ROPE = 32
VDIM = 64
HEAD_PAD = 128
QK_DIM = NOPE + ROPE
ROPE_THETA = 10000.0
SG_GROUPS = 8
SG_GROUP_DIM = 64
SG_WIDTH = SG_GROUPS * SG_GROUP_DIM
SG_CHUNK = 128
CONV_K = 3
ADAM_LR, ADAM_B1, ADAM_B2, ADAM_EPS, ADAM_WD, ADAM_STEP = 0.001, 0.9, 0.999, 1e-08, 0.01, 10

LANE = 128
PACK_W = 1024
GRAD_ROWS_MULT = 512
SMALL_ROWS_MULT = 64
VMEM_LIMIT = 60 * 1024 * 1024

TILES = dict(ffn_fwd=512, ffn_bwd=256, ffn_dw=2048, mm=1024, rms=1024, mm_tn=2048, ew=1024, attn_fwd=2048, attn_bwd=1024,
             sgu=1024, adam=512)

NT = (((1,), (1,)), ((), ()))
NN = (((1,), (0,)), ((), ()))
TN = (((0,), (0,)), ((), ()))


def _dot(a, b, dims):
    return lax.dot_general(a, b, dims, preferred_element_type=F32)


def _cparams(*sem):
    return pltpu.CompilerParams(dimension_semantics=sem if sem else None, vmem_limit_bytes=VMEM_LIMIT)


def _tile(n, want):
    t = min(want, n)
    while n % t:
        t //= 2
    return t if t % 8 == 0 else n


def _lane_tile(n, cap):
    best = None
    for k in range(1, n // LANE + 1):
        t = k * LANE
        if n % t == 0 and t <= cap:
            best = t
    return best or n


def _row(v):
    return v.reshape(1, -1).astype(F32)


def _all_gather(block, name):
    R, W = block.shape

    def body(x_ref, out_ref, send_sems, recv_sems, local_sem):
        x, y, c = lax.axis_index("x"), lax.axis_index("y"), lax.axis_index("c")
        me, sibling = (x, y, c), (x, y, 1 - c)
        chips = [(1 - x, y), (x, 1 - y), (1 - x, 1 - y)]

        def slot(px, py, pc):
            return out_ref.at[4 * px + 2 * py + pc]

        def copy(k, blk, to, src=None):
            return pltpu.make_async_remote_copy(
                src_ref=slot(*blk) if src is None else src, dst_ref=slot(*blk),
                send_sem=send_sems.at[k], recv_sem=recv_sems.at[k],
                device_id=to, device_id_type=pl.DeviceIdType.MESH)

        mine = pltpu.make_async_copy(x_ref, slot(*me), local_sem)
        mine.start()
        first = [copy(0, me, sibling, src=x_ref)]
        first += [copy(1 + j, me, (*chip, c), src=x_ref) for j, chip in enumerate(chips)]
        for cp in first:
            cp.start()
        passed = [copy(4 + j, (*chip, c), sibling) for j, chip in enumerate(chips)]
        for j, chip in enumerate(chips):
            copy(1 + j, (*chip, c), me).wait_recv()
            passed[j].start()
        copy(0, sibling, me).wait_recv()
        for j, chip in enumerate(chips):
            copy(4 + j, (*chip, 1 - c), me).wait_recv()
        for cp in first + passed:
            cp.wait_send()
        mine.wait()

    return pl.pallas_call(
        body, name=name,
        out_shape=jax.ShapeDtypeStruct((N_DEV, R, W), block.dtype),
        in_specs=[pl.BlockSpec(memory_space=pl.ANY)],
        out_specs=pl.BlockSpec(memory_space=pl.ANY),
        scratch_shapes=[pltpu.SemaphoreType.DMA((7,)), pltpu.SemaphoreType.DMA((7,)), pltpu.SemaphoreType.DMA],
    )(block)


class _Exchange:
    def __init__(self, kind, arrays):
        self.kind, self.arrays = kind, list(arrays)
        if kind == "gather":
            (r, w), = [a.shape for a in self.arrays]
            self.rows = [r]
        else:
            self.rows = [a.shape[1] for a in self.arrays]
            w = self.arrays[0].shape[2]
        self.offs = [sum(self.rows[:i]) for i in range(len(self.rows))]
        self.n_in = len(self.arrays)
        self.out_shape = jax.ShapeDtypeStruct((N_DEV, sum(self.rows), w), self.arrays[0].dtype)
        self.in_specs = [pl.BlockSpec(memory_space=pl.ANY)] * self.n_in
        self.out_spec = pl.BlockSpec(memory_space=pl.ANY)
        self.out_specs, self.out_shapes = [self.out_spec], [self.out_shape]
        self.scratch = [pltpu.SemaphoreType.DMA((7,)), pltpu.SemaphoreType.DMA((7,)), pltpu.SemaphoreType.DMA]

    def _peers(self):
        x, y, c = lax.axis_index("x"), lax.axis_index("y"), lax.axis_index("c")
        me = 4 * x + 2 * y + c
        return me, [(k, (x ^ (k >> 2), y ^ ((k >> 1) & 1), c ^ (k & 1))) for k in range(1, N_DEV)]

    @staticmethod
    def _remote(src, dst, k, to, send_sems, recv_sems):
        return pltpu.make_async_remote_copy(
            src_ref=src, dst_ref=dst, send_sem=send_sems.at[k - 1], recv_sem=recv_sems.at[k - 1],
            device_id=to, device_id_type=pl.DeviceIdType.MESH)

    def start(self, s_refs, r_ref, send_sems, recv_sems, local_sem):
        me, peers = self._peers()
        for s_ref, off, r in zip(s_refs, self.offs, self.rows):
            src = s_ref if self.kind == "gather" else s_ref.at[me]
            pltpu.make_async_copy(src, r_ref.at[me, pl.ds(off, r)], local_sem).start()
        for k, to in peers:
            peer = 4 * to[0] + 2 * to[1] + to[2]
            for s_ref, off, r in zip(s_refs, self.offs, self.rows):
                src = s_ref if self.kind == "gather" else s_ref.at[peer]
                self._remote(src, r_ref.at[me, pl.ds(off, r)], k, to, send_sems, recv_sems).start()

    def wait(self, s_refs, r_ref, send_sems, recv_sems, local_sem):
        me, peers = self._peers()
        whole = r_ref.at[me]
        totals = [self._remote(whole, whole, k, to, send_sems, recv_sems) for k, to in peers]
        for cp in totals:
            cp.wait_recv()
        for cp in totals:
            cp.wait_send()
        pltpu.make_async_copy(whole, whole, local_sem).wait()


class _NoRider:
    arrays, in_specs, out_specs, out_shapes, scratch = [], [], [], [], []


_NO_RIDER = _NoRider()


def _ride(rider, refs, n_in, n_out, first, last):
    if rider is None:
        return refs, lambda: None
    k = rider.n_in
    s_refs = refs[n_in:n_in + k]
    r_ref = refs[n_in + k + n_out]
    sems = refs[-3:]
    own = refs[:n_in] + refs[n_in + k:n_in + k + n_out] + refs[n_in + k + n_out + 1:-3]

    @pl.when(first)
    def _():
        rider.start(s_refs, r_ref, *sems)

    def finish():
        @pl.when(last)
        def _():
            rider.wait(s_refs, r_ref, *sems)

    return own, finish


def _exchange(kind, arrays, name):
    ex = _Exchange(kind, arrays)

    def body(*refs):
        s_refs, r_ref, sems = refs[:ex.n_in], refs[ex.n_in], refs[ex.n_in + 1:]
        ex.start(s_refs, r_ref, *sems)
        ex.wait(s_refs, r_ref, *sems)

    return pl.pallas_call(
        body, name=name, out_shape=ex.out_shape, in_specs=ex.in_specs, out_specs=ex.out_spec,
        scratch_shapes=ex.scratch,
    )(*ex.arrays)


def _sum_slots(parts, name):
    _, R, W = parts.shape
    tr = _tile(R, TILES["adam"])

    def body(p_ref, o_ref):
        acc = p_ref[0].astype(F32)
        for s in range(1, N_DEV):
            acc = acc + p_ref[s].astype(F32)
        o_ref[...] = acc

    return pl.pallas_call(
        body, name=name, grid=(R // tr,),
        in_specs=[pl.BlockSpec((N_DEV, tr, W), lambda i: (0, i, 0))],
        out_specs=pl.BlockSpec((tr, W), lambda i: (i, 0)),
        out_shape=jax.ShapeDtypeStruct((R, W), F32),
        compiler_params=_cparams("parallel"),
    )(parts)


def _adamw(w, g, m, v, name):
    R, W = w.shape
    tr = _tile(R, TILES["adam"])
    c1 = 1.0 - ADAM_B1 ** ADAM_STEP
    c2 = 1.0 - ADAM_B2 ** ADAM_STEP

    def body(w_ref, g_ref, m_ref, v_ref, d_ref, nm_ref, nv_ref):
        g = g_ref[...]
        nm = ADAM_B1 * m_ref[...] + (1.0 - ADAM_B1) * g
        nv = ADAM_B2 * v_ref[...] + (1.0 - ADAM_B2) * (g * g)
        d_ref[...] = -ADAM_LR * ((nm / c1) / (jnp.sqrt(nv / c2) + ADAM_EPS) + ADAM_WD * w_ref[...])
        nm_ref[...] = nm
        nv_ref[...] = nv

    spec = pl.BlockSpec((tr, W), lambda i: (i, 0))
    return pl.pallas_call(
        body, name=name, grid=(R // tr,),
        in_specs=[spec] * 4, out_specs=[spec] * 3,
        out_shape=[jax.ShapeDtypeStruct((R, W), F32)] * 3,
        compiler_params=_cparams("parallel"),
    )(w, g, m, v)


def _mm(a, b, mode, name, out_dtype=BF16, res=None, scale=1.0, acol=None, kdim=None):
    if mode == "tn":
        S, M = a.shape
        N = b.shape[1]
        ts = _tile(S, TILES["mm_tn"])
        tmo = _lane_tile(M, 1024)

        def body(a_ref, b_ref, o_ref, acc):
            s = pl.program_id(1)

            @pl.when(s == 0)
            def _():
                acc[...] = jnp.zeros_like(acc)

            acc[...] += _dot(a_ref[...].astype(BF16), b_ref[...].astype(BF16), TN)

            @pl.when(s == pl.num_programs(1) - 1)
            def _():
                o_ref[...] = acc[...].astype(out_dtype)

        return pl.pallas_call(
            body, name=name, grid=(M // tmo, S // ts),
            in_specs=[pl.BlockSpec((ts, tmo), lambda i, s: (s, i)), pl.BlockSpec((ts, N), lambda i, s: (s, 0))],
            out_specs=pl.BlockSpec((tmo, N), lambda i, s: (i, 0)),
            out_shape=jax.ShapeDtypeStruct((M, N), out_dtype),
            scratch_shapes=[pltpu.VMEM((tmo, N), F32)],
            compiler_params=_cparams("parallel", "arbitrary"),
        )(a, b)

    M = a.shape[0]
    K = kdim if kdim is not None else a.shape[1]
    ac = 0 if acol is None else acol
    N = b.shape[1] if mode == "nn" else b.shape[0]
    tm = _tile(M, TILES["mm"])
    dims = NN if mode == "nn" else NT

    def body(*refs):
        if res is None:
            a_ref, b_ref, o_ref = refs
        else:
            a_ref, b_ref, r_ref, o_ref = refs
        acc = _dot(a_ref[...].astype(BF16), b_ref[...].astype(BF16), dims)
        if res is not None:
            acc = r_ref[...] + scale * acc
        o_ref[...] = acc.astype(out_dtype)

    in_specs = [pl.BlockSpec((tm, K), lambda i: (i, ac)), pl.BlockSpec(b.shape, lambda i: (0, 0))]
    args = [a, b]
    if res is not None:
        in_specs.append(pl.BlockSpec((tm, N), lambda i: (i, 0)))
        args.append(res)
    return pl.pallas_call(
        body, name=name, grid=(M // tm,),
        in_specs=in_specs, out_specs=pl.BlockSpec((tm, N), lambda i: (i, 0)),
        out_shape=jax.ShapeDtypeStruct((M, N), out_dtype),
        compiler_params=_cparams("parallel"),
    )(*args)


def _rms_fwd(x, gain, name, col=0, width=None):
    S = x.shape[0]
    W = width if width is not None else x.shape[1]
    tm = _tile(S, TILES["rms"])

    def body(x_ref, g_ref, o_ref):
        xv = x_ref[...].astype(F32)
        r = lax.rsqrt(jnp.mean(xv * xv, axis=-1, keepdims=True) + NORM_EPS)
        o_ref[...] = (xv * r * g_ref[...]).astype(BF16)

    return pl.pallas_call(
        body, name=name, grid=(S // tm,),
        in_specs=[pl.BlockSpec((tm, W), lambda i: (i, col)), pl.BlockSpec((1, W), lambda i: (0, 0))],
        out_specs=pl.BlockSpec((tm, W), lambda i: (i, 0)),
        out_shape=jax.ShapeDtypeStruct((S, W), BF16),
        compiler_params=_cparams("parallel"),
    )(x, _row(gain))


def _rms_bwd(dy, x, gain, name, col=0, res=None, out_dtype=F32):
    S, W = dy.shape
    tm = _tile(S, TILES["rms"])

    def body(*refs):
        if res is None:
            dy_ref, x_ref, g_ref, dx_ref, dg_ref = refs
        else:
            dy_ref, x_ref, g_ref, r_ref, dx_ref, dg_ref = refs

        @pl.when(pl.program_id(0) == 0)
        def _():
            dg_ref[...] = jnp.zeros_like(dg_ref)

        xv = x_ref[...].astype(F32)
        d = dy_ref[...].astype(F32)
        r = lax.rsqrt(jnp.mean(xv * xv, axis=-1, keepdims=True) + NORM_EPS)
        xhat = xv * r
        dg_ref[...] += jnp.sum(d * xhat, axis=0, keepdims=True)
        dxhat = d * g_ref[...]
        dx = r * (dxhat - xhat * jnp.mean(dxhat * xhat, axis=-1, keepdims=True))
        if res is not None:
            dx = dx + r_ref[...]
        dx_ref[...] = dx.astype(out_dtype)

    in_specs = [pl.BlockSpec((tm, W), lambda i: (i, 0)), pl.BlockSpec((tm, W), lambda i: (i, col)),
                pl.BlockSpec((1, W), lambda i: (0, 0))]
    args = [dy, x, _row(gain)]
    if res is not None:
        in_specs.append(pl.BlockSpec((tm, W), lambda i: (i, 0)))
        args.append(res)
    return pl.pallas_call(
        body, name=name, grid=(S // tm,),
        in_specs=in_specs,
        out_specs=[pl.BlockSpec((tm, W), lambda i: (i, 0)), pl.BlockSpec((1, W), lambda i: (0, 0))],
        out_shape=[jax.ShapeDtypeStruct((S, W), out_dtype), jax.ShapeDtypeStruct((1, W), F32)],
        compiler_params=_cparams("arbitrary"),
    )(*args)


def _silu_parts(a):
    s = jax.nn.sigmoid(a)
    return a * s, s * (1.0 + a * (1.0 - s))


def _ffn_fwd(x, gain, wg_t, wu_t, wd, name, rider=None):
    S, D = x.shape
    Fd = wd.shape[0]
    tm = _tile(S, TILES["ffn_fwd"])
    fc = _lane_tile(Fd, 512)

    def body(*refs):
        i = pl.program_id(0)
        own, finish = _ride(rider, refs, 5, 3, i == 0, i == pl.num_programs(0) - 1)
        x_ref, g_ref, wg_ref, wu_ref, wd_ref, o_ref, a_ref, b_ref = own
        xv = x_ref[...]
        r = lax.rsqrt(jnp.mean(xv * xv, axis=-1, keepdims=True) + NORM_EPS)
        h = (xv * r * g_ref[...]).astype(BF16)
        acc = jnp.zeros((tm, D), F32)
        for c in range(Fd // fc):
            sl = slice(c * fc, (c + 1) * fc)
            a = _dot(h, wg_ref[sl, :], NT)
            b = _dot(h, wu_ref[sl, :], NT)
            a_ref[:, sl] = a.astype(BF16)
            b_ref[:, sl] = b.astype(BF16)
            z = (a * jax.nn.sigmoid(a) * b).astype(BF16)
            acc = acc + _dot(z, wd_ref[sl, :], NN)
        o_ref[...] = xv + 0.5 * acc
        finish()

    wspec = pl.BlockSpec((Fd, D), lambda i: (0, 0), pipeline_mode=pl.Buffered(1))
    extra = rider or _NO_RIDER
    return pl.pallas_call(
        body, name=name, grid=(S // tm,),
        in_specs=[pl.BlockSpec((tm, D), lambda i: (i, 0)), pl.BlockSpec((1, D), lambda i: (0, 0)), wspec, wspec,
                  wspec] + extra.in_specs,
        out_specs=[pl.BlockSpec((tm, D), lambda i: (i, 0)), pl.BlockSpec((tm, Fd), lambda i: (i, 0)),
                   pl.BlockSpec((tm, Fd), lambda i: (i, 0))] + extra.out_specs,
        out_shape=[jax.ShapeDtypeStruct((S, D), F32), jax.ShapeDtypeStruct((S, Fd), BF16),
                   jax.ShapeDtypeStruct((S, Fd), BF16)] + extra.out_shapes,
        scratch_shapes=extra.scratch,
        compiler_params=_cparams("arbitrary" if rider else "parallel"),
    )(x, _row(gain), wg_t, wu_t, wd, *extra.arrays)


def _ffn_bwd(g, x, gain, a, b, wg_t, wu_t, wd, name, rider=None):
    S, D = x.shape
    Fd = wd.shape[0]
    tm = _tile(S, TILES["ffn_bwd"])
    fc = Fd

    def body(*refs):
        i = pl.program_id(0)
        own, finish = _ride(rider, refs, 8, 5, i == 0, i == pl.num_programs(0) - 1)
        g_ref, x_ref, gain_ref, a_ref, b_ref, wg_ref, wu_ref, wd_ref, dx_ref, dz_ref, h_ref, dy_ref, dg_ref = own

        @pl.when(i == 0)
        def _():
            dg_ref[...] = jnp.zeros_like(dg_ref)

        gv = g_ref[...]
        xv = x_ref[...]
        r = lax.rsqrt(jnp.mean(xv * xv, axis=-1, keepdims=True) + NORM_EPS)
        xhat = xv * r
        h_ref[...] = (xhat * gain_ref[...]).astype(BF16)
        dy = (0.5 * gv).astype(BF16)
        dy_ref[...] = dy
        dh = jnp.zeros((tm, D), F32)
        for c in range(Fd // fc):
            sl = slice(c * fc, (c + 1) * fc)
            av = a_ref[:, sl].astype(F32)
            bv = b_ref[:, sl].astype(F32)
            dz = _dot(dy, wd_ref[sl, :], NT).astype(BF16)
            dz_ref[:, sl] = dz
            dzf = dz.astype(F32)
            silu, dsilu = _silu_parts(av)
            da = (dzf * bv * dsilu).astype(BF16)
            db = (dzf * silu).astype(BF16)
            dh = dh + _dot(da, wg_ref[sl, :], NN) + _dot(db, wu_ref[sl, :], NN)
        dg_ref[...] += jnp.sum(dh * xhat, axis=0, keepdims=True)
        dxhat = dh * gain_ref[...]
        dx_ref[...] = gv + r * (dxhat - xhat * jnp.mean(dxhat * xhat, axis=-1, keepdims=True))
        finish()

    wspec = pl.BlockSpec((Fd, D), lambda i: (0, 0), pipeline_mode=pl.Buffered(1))
    row = pl.BlockSpec((tm, D), lambda i: (i, 0))
    wide = pl.BlockSpec((tm, Fd), lambda i: (i, 0))
    extra = rider or _NO_RIDER
    return pl.pallas_call(
        body, name=name, grid=(S // tm,),
        in_specs=[row, row, pl.BlockSpec((1, D), lambda i: (0, 0)), wide, wide, wspec, wspec, wspec] + extra.in_specs,
        out_specs=[row, wide, row, row, pl.BlockSpec((1, D), lambda i: (0, 0))] + extra.out_specs,
        out_shape=[jax.ShapeDtypeStruct((S, D), F32), jax.ShapeDtypeStruct((S, Fd), BF16),
                   jax.ShapeDtypeStruct((S, D), BF16), jax.ShapeDtypeStruct((S, D), BF16),
                   jax.ShapeDtypeStruct((1, D), F32)] + extra.out_shapes,
        scratch_shapes=extra.scratch,
        compiler_params=_cparams("arbitrary"),
    )(g, x, _row(gain), a, b, wg_t, wu_t, wd, *extra.arrays)


def _ffn_dw(a, b, dz, h, dy, name):
    S, Fd = a.shape
    D = h.shape[1]
    ts = _tile(S, TILES["ffn_dw"])
    tf = _lane_tile(Fd, 256)

    def body(a_ref, b_ref, dz_ref, h_ref, dy_ref, og_ref, ou_ref, od_ref, accg, accu, accd):
        s = pl.program_id(1)

        @pl.when(s == 0)
        def _():
            accg[...] = jnp.zeros_like(accg)
            accu[...] = jnp.zeros_like(accu)
            accd[...] = jnp.zeros_like(accd)

        av = a_ref[...].astype(F32)
        bv = b_ref[...].astype(F32)
        dzf = dz_ref[...].astype(F32)
        silu, dsilu = _silu_parts(av)
        da = (dzf * bv * dsilu).astype(BF16)
        db = (dzf * silu).astype(BF16)
        z = (silu * bv).astype(BF16)
        hv = h_ref[...]
        accg[...] += _dot(da, hv, TN)
        accu[...] += _dot(db, hv, TN)
        accd[...] += _dot(z, dy_ref[...], TN)

        @pl.when(s == pl.num_programs(1) - 1)
        def _():
            og_ref[...] = accg[...].astype(BF16)
            ou_ref[...] = accu[...].astype(BF16)
            od_ref[...] = accd[...].astype(BF16)

    wide = pl.BlockSpec((ts, tf), lambda f, s: (s, f))
    row = pl.BlockSpec((ts, D), lambda f, s: (s, 0))
    out = pl.BlockSpec((tf, D), lambda f, s: (f, 0))
    return pl.pallas_call(
        body, name=name, grid=(Fd // tf, S // ts),
        in_specs=[wide, wide, wide, row, row], out_specs=[out, out, out],
        out_shape=[jax.ShapeDtypeStruct((Fd, D), BF16)] * 3,
        scratch_shapes=[pltpu.VMEM((tf, D), F32)] * 3,
        compiler_params=_cparams("parallel", "arbitrary"),
    )(a, b, dz, h, dy)


def _loss_head(x, target, gain, name):
    S, D = x.shape
    tm = _tile(S, TILES["ew"])

    def body(x_ref, t_ref, g_ref, dx_ref, dg_ref, loss_ref):
        @pl.when(pl.program_id(0) == 0)
        def _():
            dg_ref[...] = jnp.zeros_like(dg_ref)
            loss_ref[...] = jnp.zeros_like(loss_ref)

        xv = x_ref[...]
        r = lax.rsqrt(jnp.mean(xv * xv, axis=-1, keepdims=True) + NORM_EPS)
        xhat = xv * r
        e = xhat * g_ref[...] - t_ref[...]
        per_tok = jnp.mean(e * e, axis=-1, keepdims=True)
        loss_ref[...] += jnp.broadcast_to(0.5 * jnp.sum(per_tok, axis=0, keepdims=True), (1, LANE))
        dy = e * (1.0 / D)
        dg_ref[...] += jnp.sum(dy * xhat, axis=0, keepdims=True)
        dxhat = dy * g_ref[...]
        dx_ref[...] = r * (dxhat - xhat * jnp.mean(dxhat * xhat, axis=-1, keepdims=True))

    row = pl.BlockSpec((tm, D), lambda i: (i, 0))
    return pl.pallas_call(
        body, name=name, grid=(S // tm,),
        in_specs=[row, row, pl.BlockSpec((1, D), lambda i: (0, 0))],
        out_specs=[row, pl.BlockSpec((1, D), lambda i: (0, 0)), pl.BlockSpec((1, LANE), lambda i: (0, 0))],
        out_shape=[jax.ShapeDtypeStruct((S, D), F32), jax.ShapeDtypeStruct((1, D), F32),
                   jax.ShapeDtypeStruct((1, LANE), F32)],
        compiler_params=_cparams("arbitrary"),
    )(x, target, _row(gain))


def _shift_down(u, halo, k, rows):
    out = pltpu.roll(u, k, 0)
    for j in range(k):
        out = jnp.where(rows == j, halo[8 - k + j:8 - k + j + 1, :], out)
    return out


def _shift_up(u, halo, k, rows, n):
    out = pltpu.roll(u, n - k, 0)
    for j in range(k):
        out = jnp.where(rows == n - k + j, halo[j:j + 1, :], out)
    return out


def _conv_fwd(p, cw, name):
    S, W3 = p.shape
    W = W3 // 3
    tm = _tile(S, TILES["ew"])
    hb = tm // 8

    def body(p_ref, ph_ref, w_ref, v_ref):
        i = pl.program_id(0)
        bg = p_ref[:, 0:W].astype(F32)
        u = p_ref[:, W:2 * W].astype(F32) * p_ref[:, 2 * W:3 * W].astype(F32)
        uh = ph_ref[:, W:2 * W].astype(F32) * ph_ref[:, 2 * W:3 * W].astype(F32)
        uh = jnp.where(i > 0, uh, 0.0)
        rows = lax.broadcasted_iota(jnp.int32, (tm, 1), 0)
        u1 = _shift_down(u, uh, 1, rows)
        u2 = _shift_down(u, uh, 2, rows)
        y = w_ref[0:1, :] * u2 + w_ref[1:2, :] * u1 + w_ref[2:3, :] * u
        v_ref[...] = (bg * y).astype(BF16)

    return pl.pallas_call(
        body, name=name, grid=(S // tm,),
        in_specs=[pl.BlockSpec((tm, W3), lambda i: (i, 0)),
                  pl.BlockSpec((8, W3), lambda i: (jnp.maximum(i * hb - 1, 0), 0)),
                  pl.BlockSpec((8, W), lambda i: (0, 0))],
        out_specs=pl.BlockSpec((tm, W), lambda i: (i, 0)),
        out_shape=jax.ShapeDtypeStruct((S, W), BF16),
        compiler_params=_cparams("parallel"),
    )(p, p, cw)


def _conv_bwd(dv, p, cw, name):
    S, W3 = p.shape
    W = W3 // 3
    tm = _tile(S, TILES["ew"])
    hb = tm // 8
    last = S // 8 - 1

    def body(dv_ref, dvn_ref, p_ref, pp_ref, pn_ref, w_ref, dp_ref, dw_ref):
        i = pl.program_id(0)
        n = pl.num_programs(0)

        @pl.when(i == 0)
        def _():
            dw_ref[...] = jnp.zeros_like(dw_ref)

        bg = p_ref[:, 0:W].astype(F32)
        cg = p_ref[:, W:2 * W].astype(F32)
        zz = p_ref[:, 2 * W:3 * W].astype(F32)
        u = cg * zz
        uh = pp_ref[:, W:2 * W].astype(F32) * pp_ref[:, 2 * W:3 * W].astype(F32)
        uh = jnp.where(i > 0, uh, 0.0)
        rows = lax.broadcasted_iota(jnp.int32, (tm, 1), 0)
        u1 = _shift_down(u, uh, 1, rows)
        u2 = _shift_down(u, uh, 2, rows)
        w0, w1, w2 = w_ref[0:1, :], w_ref[1:2, :], w_ref[2:3, :]
        y = w0 * u2 + w1 * u1 + w2 * u
        dvv = dv_ref[...].astype(F32)
        dy = dvv * bg
        dyh = dvn_ref[...].astype(F32) * pn_ref[:, 0:W].astype(F32)
        dyh = jnp.where(i < n - 1, dyh, 0.0)
        d1 = _shift_up(dy, dyh, 1, rows, tm)
        d2 = _shift_up(dy, dyh, 2, rows, tm)
        du = w2 * dy + w1 * d1 + w0 * d2
        dp_ref[:, 0:W] = (dvv * y).astype(BF16)
        dp_ref[:, W:2 * W] = (du * zz).astype(BF16)
        dp_ref[:, 2 * W:3 * W] = (du * cg).astype(BF16)
        dw_ref[0:1, :] += jnp.sum(dy * u2, axis=0, keepdims=True)
        dw_ref[1:2, :] += jnp.sum(dy * u1, axis=0, keepdims=True)
        dw_ref[2:3, :] += jnp.sum(dy * u, axis=0, keepdims=True)

    return pl.pallas_call(
        body, name=name, grid=(S // tm,),
        in_specs=[pl.BlockSpec((tm, W), lambda i: (i, 0)),
                  pl.BlockSpec((8, W), lambda i: (jnp.minimum((i + 1) * hb, last), 0)),
                  pl.BlockSpec((tm, W3), lambda i: (i, 0)),
                  pl.BlockSpec((8, W3), lambda i: (jnp.maximum(i * hb - 1, 0), 0)),
                  pl.BlockSpec((8, W3), lambda i: (jnp.minimum((i + 1) * hb, last), 0)),
                  pl.BlockSpec((8, W), lambda i: (0, 0))],
        out_specs=[pl.BlockSpec((tm, W3), lambda i: (i, 0)), pl.BlockSpec((8, W), lambda i: (0, 0))],
        out_shape=[jax.ShapeDtypeStruct((S, W3), BF16), jax.ShapeDtypeStruct((8, W), F32)],
        compiler_params=_cparams("arbitrary"),
    )(dv, dv, p, p, p, cw)


def _rope_swap(r, lane):
    mid = NOPE + ROPE // 2
    first = (lane >= NOPE) & (lane < mid)
    second = (lane >= mid) & (lane < QK_DIM)
    return jnp.where(first, pltpu.roll(r, HEAD_PAD - ROPE // 2, 1), jnp.where(second, pltpu.roll(r, ROPE // 2, 1), 0.0))


def _rope_fwd(q_big, kv_big, proj, kr_col, ct, st, name):
    S = q_big.shape[0]
    HW = HEADS * HEAD_PAD
    tm = _tile(S, TILES["ew"])

    def body(q_ref, k_ref, v_ref, kr_ref, ct_ref, st_ref, qo_ref, ko_ref, vo_ref):
        lane = lax.broadcasted_iota(jnp.int32, (1, HEAD_PAD), 1)
        ctv, stv = ct_ref[...], st_ref[...]
        krr = pltpu.roll(kr_ref[...].astype(F32), NOPE, 1)
        kro = krr * ctv + _rope_swap(krr, lane) * stv
        for h in range(HEADS):
            sl = slice(h * HEAD_PAD, (h + 1) * HEAD_PAD)
            qh = q_ref[:, sl].astype(F32)
            qo_ref[:, sl] = (qh * ctv + _rope_swap(qh, lane) * stv).astype(BF16)
            ko_ref[:, sl] = (k_ref[:, sl].astype(F32) + kro).astype(BF16)
            vo_ref[:, sl] = jnp.where(lane == VDIM, 1.0, v_ref[:, sl].astype(F32)).astype(BF16)

    wide = pl.BlockSpec((tm, HW), lambda i: (i, 0))
    narrow = pl.BlockSpec((tm, HEAD_PAD), lambda i: (i, 0))
    return pl.pallas_call(
        body, name=name, grid=(S // tm,),
        in_specs=[wide, wide, pl.BlockSpec((tm, HW), lambda i: (i, 1)),
                  pl.BlockSpec((tm, HEAD_PAD), lambda i: (i, kr_col)), narrow, narrow],
        out_specs=[wide, wide, wide],
        out_shape=[jax.ShapeDtypeStruct((S, HW), BF16)] * 3,
        compiler_params=_cparams("parallel"),
    )(q_big, kv_big, kv_big, proj, ct, st)


def _rope_bwd(dq, dk, dv, ct, st, name):
    S = dq.shape[0]
    HW = HEADS * HEAD_PAD
    tm = _tile(S, TILES["ew"])

    def body(dq_ref, dk_ref, dv_ref, ct_ref, st_ref, oq_ref, okv_ref, okr_ref):
        lane = lax.broadcasted_iota(jnp.int32, (1, HEAD_PAD), 1)
        ctv, stv = ct_ref[...], st_ref[...]
        acc = jnp.zeros((tm, HEAD_PAD), F32)
        for h in range(HEADS):
            sl = slice(h * HEAD_PAD, (h + 1) * HEAD_PAD)
            d = dq_ref[:, sl].astype(F32)
            oq_ref[:, sl] = (d * ctv + _rope_swap(d * stv, lane)).astype(BF16)
            d = dk_ref[:, sl].astype(F32)
            okv_ref[:, sl] = jnp.where(lane < NOPE, d, 0.0).astype(BF16)
            acc = acc + jnp.where(lane >= NOPE, d * ctv + _rope_swap(d * stv, lane), 0.0)
        okv_ref[:, HW:2 * HW] = dv_ref[...].astype(BF16)
        okr_ref[...] = pltpu.roll(acc, HEAD_PAD - NOPE, 1).astype(BF16)

    wide = pl.BlockSpec((tm, HW), lambda i: (i, 0))
    narrow = pl.BlockSpec((tm, HEAD_PAD), lambda i: (i, 0))
    return pl.pallas_call(
        body, name=name, grid=(S // tm,),
        in_specs=[wide, wide, wide, narrow, narrow],
        out_specs=[wide, pl.BlockSpec((tm, 2 * HW), lambda i: (i, 0)), narrow],
        out_shape=[jax.ShapeDtypeStruct((S, HW), BF16), jax.ShapeDtypeStruct((S, 2 * HW), BF16),
                   jax.ShapeDtypeStruct((S, HEAD_PAD), BF16)],
        compiler_params=_cparams("parallel"),
    )(dq, dk, dv, ct, st)


def _pairs(n, by_key):
    if by_key:
        pr = [(i, j) for j in range(n) for i in range(j, n)]
    else:
        pr = [(i, j) for i in range(n) for j in range(i + 1)]
    qi = np.array([p[0] for p in pr], np.int32)
    kj = np.array([p[1] for p in pr], np.int32)
    return jnp.asarray(qi), jnp.asarray(kj)


_LOG2E = 1.4426950408889634
_LN2 = 0.6931471805599453


def _tile_mask(t):
    return lax.broadcasted_iota(jnp.int32, (t, t), 1) <= lax.broadcasted_iota(jnp.int32, (t, t), 0)


def _attn_fwd(q, k, v, name):
    S = q.shape[0]
    HW = HEADS * HEAD_PAD
    t = _tile(S, TILES["attn_fwd"])
    n = S // t
    qi, kj = _pairs(n, by_key=False)
    c = (QK_DIM ** -0.5) * _LOG2E

    def body(qi_ref, kj_ref, q_ref, k_ref, v_ref, o_ref, lse_ref, m_s, acc_s):
        p_id = pl.program_id(1)
        i, j = qi_ref[p_id], kj_ref[p_id]

        @pl.when(j == 0)
        def _():
            m_s[...] = jnp.full_like(m_s, -jnp.inf)
            acc_s[...] = jnp.zeros_like(acc_s)

        def step(on_diagonal):
            s = _dot(q_ref[...], k_ref[...], NT)
            if on_diagonal:
                s = jnp.where(_tile_mask(t), s, -jnp.inf)
            m_old = m_s[...]
            m_new = jnp.maximum(m_old, jnp.max(s, axis=-1, keepdims=True))
            p = jnp.exp2((s - m_new) * c).astype(BF16)
            acc_s[...] = jnp.exp2((m_old - m_new) * c) * acc_s[...] + _dot(p, v_ref[...], NN)
            m_s[...] = m_new

        @pl.when(i == j)
        def _():
            step(True)

        @pl.when(i != j)
        def _():
            step(False)

        @pl.when(j == i)
        def _():
            acc = acc_s[...]
            l = acc[:, VDIM:VDIM + 1]
            o_ref[...] = (acc * (1.0 / l)).astype(BF16)
            lse_ref[...] = jnp.broadcast_to(m_s[...] * c + jnp.log2(l), (t, HEAD_PAD))

    qspec = pl.BlockSpec((t, HEAD_PAD), lambda h, p, qi, kj: (qi[p], h))
    kspec = pl.BlockSpec((t, HEAD_PAD), lambda h, p, qi, kj: (kj[p], h))
    grid_spec = pltpu.PrefetchScalarGridSpec(
        num_scalar_prefetch=2, grid=(HEADS, int(qi.shape[0])),
        in_specs=[qspec, kspec, kspec], out_specs=[qspec, qspec],
        scratch_shapes=[pltpu.VMEM((t, 1), F32), pltpu.VMEM((t, HEAD_PAD), F32)])
    return pl.pallas_call(
        body, name=name, grid_spec=grid_spec,
        out_shape=[jax.ShapeDtypeStruct((S, HW), BF16), jax.ShapeDtypeStruct((S, HW), F32)],
        compiler_params=_cparams("parallel", "arbitrary"),
    )(qi, kj, q, k, v)


def _attn_bwd(q, k, v, o, do, lse2, name):
    S = q.shape[0]
    HW = HEADS * HEAD_PAD
    t = _tile(S, TILES["attn_bwd"])
    n = S // t
    qi, kj = _pairs(n, by_key=True)
    scale = QK_DIM ** -0.5

    def body(qi_ref, kj_ref, q_ref, k_ref, v_ref, o_ref, do_ref, lse_ref, dq_ref, dk_ref, dv_ref, dk_s, dv_s):
        p_id = pl.program_id(1)
        i, j = qi_ref[p_id], kj_ref[p_id]

        @pl.when(p_id == 0)
        def _():
            dq_ref[...] = jnp.zeros_like(dq_ref)

        @pl.when(i == j)
        def _():
            dk_s[...] = jnp.zeros_like(dk_s)
            dv_s[...] = jnp.zeros_like(dv_s)

        def step(on_diagonal):
            qv, kv, vv = q_ref[...], k_ref[...], v_ref[...]
            dov = do_ref[...]
            p = jnp.exp(_dot(qv, kv, NT) * scale - lse_ref[:, 0:1] * _LN2)
            if on_diagonal:
                p = jnp.where(_tile_mask(t), p, 0.0)
            delta = jnp.sum(dov.astype(F32) * o_ref[...].astype(F32), axis=-1, keepdims=True)
            dv_s[...] += _dot(p.astype(BF16), dov, TN)
            ds = (p * (_dot(dov, vv, NT) - delta) * scale).astype(BF16)
            dk_s[...] += _dot(ds, qv, TN)
            rows = pl.ds(pl.multiple_of(i * t, t), t)
            dq_ref[rows, :] += _dot(ds, kv, NN)

        @pl.when(i == j)
        def _():
            step(True)

        @pl.when(i != j)
        def _():
            step(False)

        @pl.when(i == n - 1)
        def _():
            dk_ref[...] = dk_s[...]
            dv_ref[...] = dv_s[...]

    qspec = pl.BlockSpec((t, HEAD_PAD), lambda h, p, qi, kj: (qi[p], h))
    kspec = pl.BlockSpec((t, HEAD_PAD), lambda h, p, qi, kj: (kj[p], h))
    grid_spec = pltpu.PrefetchScalarGridSpec(
        num_scalar_prefetch=2, grid=(HEADS, int(qi.shape[0])),
        in_specs=[qspec, kspec, kspec, qspec, qspec, qspec],
        out_specs=[pl.BlockSpec((S, HEAD_PAD), lambda h, p, qi, kj: (0, h)), kspec, kspec],
        scratch_shapes=[pltpu.VMEM((t, HEAD_PAD), F32), pltpu.VMEM((t, HEAD_PAD), F32)])
    return pl.pallas_call(
        body, name=name, grid_spec=grid_spec,
        out_shape=[jax.ShapeDtypeStruct((S, HW), F32)] * 3,
        compiler_params=_cparams("parallel", "arbitrary"),
    )(qi, kj, q, k, v, o, do, lse2)


_SQRT_HALF = 0.7071067811865476
_INV_SQRT_2PI = 0.3989422804014327


def _sg_select(r, grp):
    out = jnp.where(grp == 0, r[0:SG_CHUNK, :], 0.0)
    for g in range(1, SG_GROUPS):
        out = out + jnp.where(grp == g, r[g * SG_CHUNK:(g + 1) * SG_CHUNK, :], 0.0)
    return out


def _sgu_fwd(proj, gain, wstack, bmat, name):
    S = proj.shape[0]
    W = SG_WIDTH
    tm = _tile(S, TILES["sgu"])

    def body(z_ref, g_ref, w_ref, b_ref, o_ref):
        z = z_ref[...].astype(F32)
        zg = 0.5 * z * (1.0 + lax.erf(z * _SQRT_HALF))
        u, vv = zg[:, 0:W], zg[:, W:2 * W]
        r = lax.rsqrt(jnp.mean(vv * vv, axis=-1, keepdims=True) + NORM_EPS)
        vn = (vv * r * g_ref[...]).astype(BF16)
        grp = lax.broadcasted_iota(jnp.int32, (1, W), 1) // SG_GROUP_DIM
        for c in range(tm // SG_CHUNK):
            sl = slice(c * SG_CHUNK, (c + 1) * SG_CHUNK)
            mixed = _sg_select(_dot(w_ref[...], vn[sl, :], NN), grp) + b_ref[...]
            o_ref[sl, :] = (u[sl, :] * mixed).astype(BF16)

    return pl.pallas_call(
        body, name=name, grid=(S // tm,),
        in_specs=[pl.BlockSpec((tm, 2 * W), lambda i: (i, 0)), pl.BlockSpec((1, W), lambda i: (0, 0)),
                  pl.BlockSpec(wstack.shape, lambda i: (0, 0)), pl.BlockSpec(bmat.shape, lambda i: (0, 0))],
        out_specs=pl.BlockSpec((tm, W), lambda i: (i, 0)),
        out_shape=jax.ShapeDtypeStruct((S, W), BF16),
        compiler_params=_cparams("parallel"),
    )(proj, _row(gain), wstack, bmat)


def _sgu_bwd(dsg, proj, gain, wstack, wtstack, bmat, gsum, name):
    S = proj.shape[0]
    W = SG_WIDTH
    tm = _tile(S, TILES["sgu"])
    GS = SG_GROUPS * SG_CHUNK

    def body(d_ref, z_ref, g_ref, w_ref, wt_ref, b_ref, e_ref, dz_ref, dw_ref, db_ref, dg_ref, dw_s, db_s):
        i = pl.program_id(0)

        @pl.when(i == 0)
        def _():
            dw_s[...] = jnp.zeros_like(dw_s)
            db_s[...] = jnp.zeros_like(db_s)
            dg_ref[...] = jnp.zeros_like(dg_ref)

        z = z_ref[...].astype(F32)
        cdf = 0.5 * (1.0 + lax.erf(z * _SQRT_HALF))
        zg = z * cdf
        u, vv = zg[:, 0:W], zg[:, W:2 * W]
        r = lax.rsqrt(jnp.mean(vv * vv, axis=-1, keepdims=True) + NORM_EPS)
        vhat = vv * r
        vn = (vhat * g_ref[...]).astype(BF16)
        grp = lax.broadcasted_iota(jnp.int32, (1, W), 1) // SG_GROUP_DIM
        d = d_ref[...].astype(F32)
        du_parts, dvn_parts = [], []
        for c in range(tm // SG_CHUNK):
            sl = slice(c * SG_CHUNK, (c + 1) * SG_CHUNK)
            vc = vn[sl, :]
            mixed = _sg_select(_dot(w_ref[...], vc, NN), grp) + b_ref[...]
            dc = d[sl, :]
            du_parts.append(dc * mixed)
            dmix = dc * u[sl, :]
            db_s[...] += dmix
            dmb = dmix.astype(BF16)
            dvn_parts.append(_sg_select(_dot(wt_ref[...], dmb, NN), grp))
            astack = jnp.concatenate([jnp.where(grp == g, dmb, jnp.zeros_like(dmb)) for g in range(SG_GROUPS)], axis=0)
            dw_s[...] += _dot(astack, vc, NT)
        du = jnp.concatenate(du_parts, axis=0)
        dvn = jnp.concatenate(dvn_parts, axis=0)
        dg_ref[...] += jnp.sum(dvn * vhat, axis=0, keepdims=True)
        dvhat = dvn * g_ref[...]
        dvv = r * (dvhat - vhat * jnp.mean(dvhat * vhat, axis=-1, keepdims=True))
        dgelu = cdf + z * (_INV_SQRT_2PI * jnp.exp(-0.5 * z * z))
        dz_ref[:, 0:W] = (du * dgelu[:, 0:W]).astype(BF16)
        dz_ref[:, W:2 * W] = (dvv * dgelu[:, W:2 * W]).astype(BF16)

        @pl.when(i == pl.num_programs(0) - 1)
        def _():
            dw_ref[...] = dw_s[...]
            db_ref[...] = lax.dot_general(db_s[...], e_ref[...], NN, precision=lax.Precision.HIGHEST,
                                          preferred_element_type=F32)

    full = lambda a: pl.BlockSpec(a.shape, lambda i: (0, 0))
    return pl.pallas_call(
        body, name=name, grid=(S // tm,),
        in_specs=[pl.BlockSpec((tm, W), lambda i: (i, 0)), pl.BlockSpec((tm, 2 * W), lambda i: (i, 0)),
                  pl.BlockSpec((1, W), lambda i: (0, 0)), full(wstack), full(wtstack), full(bmat), full(gsum)],
        out_specs=[pl.BlockSpec((tm, 2 * W), lambda i: (i, 0)), pl.BlockSpec((GS, SG_CHUNK), lambda i: (0, 0)),
                   pl.BlockSpec((SG_CHUNK, LANE), lambda i: (0, 0)), pl.BlockSpec((1, W), lambda i: (0, 0))],
        out_shape=[jax.ShapeDtypeStruct((S, 2 * W), BF16), jax.ShapeDtypeStruct((GS, SG_CHUNK), F32),
                   jax.ShapeDtypeStruct((SG_CHUNK, LANE), F32), jax.ShapeDtypeStruct((1, W), F32)],
        scratch_shapes=[pltpu.VMEM((GS, SG_CHUNK), F32), pltpu.VMEM((SG_CHUNK, W), F32)],
        compiler_params=_cparams("arbitrary"),
    )(dsg, proj, _row(gain), wstack, wtstack, bmat, gsum)


WEIGHTS = ['ffn_pre_norm', 'ffn_pre_w_gate', 'ffn_pre_w_up', 'ffn_pre_w_down', 'mix_norm', 'ffn_post_norm',
           'ffn_post_w_gate', 'ffn_post_w_up', 'ffn_post_w_down', 'even_w_in', 'q_norm', 'w_uq', 'kv_norm', 'w_ukv',
           'sg_norm', 'sg_w', 'sg_b', 'even_w_out', 'conv_w_in', 'conv_w', 'conv_w_out', 'final_norm']
SHARD_AXIS = dict(ffn_pre_w_gate=2, ffn_pre_w_up=2, ffn_pre_w_down=1, ffn_post_w_gate=2, ffn_post_w_up=2,
                  ffn_post_w_down=1, even_w_in=2, w_uq=2, w_ukv=2, even_w_out=1, conv_w_in=2, conv_w=2, conv_w_out=1)
SHARDED = [n for n in WEIGHTS if n in SHARD_AXIS]
REPLICATED = [n for n in WEIGHTS if n not in SHARD_AXIS]


def _to_t(name, w):
    return jnp.swapaxes(w, 1, 2) if SHARD_AXIS[name] == 2 else w


def _rows_of(n):
    return -(-n // PACK_W)


def _pad_rows(a, mult, axis):
    r = a.shape[axis]
    extra = (-r) % mult
    if extra == 0:
        return a
    pad = [(0, 0)] * a.ndim
    pad[axis] = (0, extra)
    return jnp.pad(a, pad)


def _flat_rows(a, lead):
    flat = a.reshape(a.shape[:lead] + (-1,))
    n = flat.shape[-1]
    flat = _pad_rows(flat, PACK_W, lead)
    return flat.reshape(a.shape[:lead] + (_rows_of(n), PACK_W))


def _pack(pieces, lead, mult, piece_mult=1):
    rows, offs, off = [], [], 0
    for p in pieces:
        r = _pad_rows(_flat_rows(p, lead), piece_mult, lead)
        rows.append(r)
        offs.append(off)
        off += r.shape[lead]
    return _pad_rows(jnp.concatenate(rows, axis=lead), mult, lead), offs


def _unpack(buf, off, shape, lead):
    n = math.prod(shape)
    r = _rows_of(n)
    piece = lax.slice_in_dim(buf, off, off + r, axis=lead)
    piece = piece.reshape(buf.shape[:lead] + (r * PACK_W,))
    piece = lax.slice_in_dim(piece, 0, n, axis=lead)
    return piece.reshape(buf.shape[:lead] + tuple(shape))


def _head_pad(w, per_head, keep):
    k = w.shape[-1]
    w = w.reshape(HEADS, per_head, k)[:, keep[0]:keep[1]]
    w = jnp.pad(w, ((0, 0), (0, HEAD_PAD - (keep[1] - keep[0])), (0, 0)))
    return w.reshape(HEADS * HEAD_PAD, k)


def _head_unpad(w, n):
    return w.reshape(HEADS, HEAD_PAD, w.shape[-1])[:, :n]


def kernel(x, positions, ffn_pre_norm, ffn_pre_w_gate, ffn_pre_w_up, ffn_pre_w_down, mix_norm, ffn_post_norm, ffn_post_w_gate, ffn_post_w_up, ffn_post_w_down, even_w_in, q_norm, w_uq, kv_norm, w_ukv, sg_norm, sg_w, sg_b, even_w_out, conv_w_in, conv_w, conv_w_out, final_norm, loss_target, m_ffn_pre_norm, m_ffn_pre_w_gate, m_ffn_pre_w_up, m_ffn_pre_w_down, m_mix_norm, m_ffn_post_norm, m_ffn_post_w_gate, m_ffn_post_w_up, m_ffn_post_w_down, m_even_w_in, m_q_norm, m_w_uq, m_kv_norm, m_w_ukv, m_sg_norm, m_sg_w, m_sg_b, m_even_w_out, m_conv_w_in, m_conv_w, m_conv_w_out, m_final_norm, v_ffn_pre_norm, v_ffn_pre_w_gate, v_ffn_pre_w_up, v_ffn_pre_w_down, v_mix_norm, v_ffn_post_norm, v_ffn_post_w_gate, v_ffn_post_w_up, v_ffn_post_w_down, v_even_w_in, v_q_norm, v_w_uq, v_kv_norm, v_w_ukv, v_sg_norm, v_sg_w, v_sg_b, v_even_w_out, v_conv_w_in, v_conv_w, v_conv_w_out, v_final_norm):
    given = dict(locals())
    w_loc = {n: given[n] for n in WEIGHTS}
    m_loc = {n: given["m_" + n] for n in WEIGHTS}
    v_loc = {n: given["v_" + n] for n in WEIGHTS}

    S, D = x.shape[1], x.shape[2]
    depth = ffn_pre_norm.shape[0]
    QL, KVL = q_norm.shape[1], kv_norm.shape[1]
    ZW = 2 * SG_WIDTH
    assert x.shape[0] == 1 and ZW % KVL == 0 and (ZW + KVL) % HEAD_PAD == 0 and (ZW + KVL + 2 * HEAD_PAD) % QL == 0
    col_ckv = ZW // KVL
    col_kr = (ZW + KVL) // HEAD_PAD
    col_cq = (ZW + KVL + 2 * HEAD_PAD) // QL

    t_loc = {n: _to_t(n, w_loc[n]) for n in SHARDED}
    full = {n: {} for n in SHARDED}

    def ffn_keys(kind, l):
        return [("ffn_%s_w_%s" % (kind, part), l) for part in ("gate", "up", "down")]

    def mixer_keys(l):
        names = ("even_w_in", "w_uq", "w_ukv", "even_w_out") if l % 2 == 0 else ("conv_w_in", "conv_w", "conv_w_out")
        return [(n, l // 2) for n in names]

    def local_pack(keys):
        return _pack([t_loc[n][l].astype(BF16) for n, l in keys], 0, 16, piece_mult=16)

    def take_gathered(gathered, keys, offs):
        for (n, l), off in zip(keys, offs):
            piece = _unpack(gathered, off, t_loc[n].shape[1:], 1)
            full[n][l] = piece.reshape(N_DEV * piece.shape[1], piece.shape[2])

    def gather_rider(keys):
        pack, offs = local_pack(keys)
        return _Exchange("gather", [pack]), offs

    first_keys = ffn_keys("pre", 0)
    pack0, offs0 = local_pack(first_keys)
    take_gathered(_all_gather(pack0, "gather_weights"), first_keys, offs0)

    tril = jnp.tril(jnp.ones((SG_CHUNK, SG_CHUNK), F32))
    even_ops = {}

    def even_operands(e):
        if e not in even_ops:
            wi = full["even_w_in"][e]
            zrow = lambda k: jnp.zeros((k, D), BF16)
            ops = dict(win_pad=jnp.concatenate(
                [wi[QL + KVL + ROPE:], wi[QL:QL + KVL], wi[QL + KVL:QL + KVL + ROPE], zrow(HEAD_PAD - ROPE),
                 zrow(HEAD_PAD), wi[:QL]], axis=0))
            ops["wq_big"] = _head_pad(full["w_uq"][e], QK_DIM, (0, QK_DIM))
            wkv = full["w_ukv"][e]
            ops["wkv_big"] = jnp.concatenate([_head_pad(wkv, NOPE + VDIM, (0, NOPE)),
                                              _head_pad(wkv, NOPE + VDIM, (NOPE, NOPE + VDIM))], axis=0)
            wo = full["even_w_out"][e]
            ops["wo_attn"] = _head_pad(wo[:HEADS * VDIM], VDIM, (0, VDIM))
            ops["wo_sg"] = wo[HEADS * VDIM:]
            wt = sg_w[e] * tril
            ops["wstack"] = wt.reshape(SG_GROUPS * SG_CHUNK, SG_CHUNK).astype(BF16)
            ops["wtstack"] = jnp.swapaxes(wt, 1, 2).reshape(SG_GROUPS * SG_CHUNK, SG_CHUNK).astype(BF16)
            ops["bmat"] = jnp.repeat(sg_b[e].T, SG_GROUP_DIM, axis=1)
            even_ops[e] = ops
        return even_ops[e]

    gsum = (jnp.arange(SG_WIDTH)[:, None] // SG_GROUP_DIM == jnp.arange(LANE)[None, :]).astype(F32)

    def conv_taps(o):
        return jnp.pad(jnp.swapaxes(full["conv_w"][o], 0, 1).astype(F32), ((0, 8 - CONV_K), (0, 0)))

    inv_freq = ROPE_THETA ** (-jnp.arange(0, ROPE, 2, dtype=F32) / ROPE)
    ang = positions[0].astype(F32)[:, None] * inv_freq
    cos, sin = jnp.cos(ang), jnp.sin(ang)
    ones, zeros = jnp.ones((S, NOPE), F32), jnp.zeros((S, HEAD_PAD - QK_DIM), F32)
    ct = jnp.concatenate([ones, cos, cos, zeros], axis=1)
    st = jnp.concatenate([0.0 * ones, -sin, sin, zeros], axis=1)

    xs = x[0]
    saved = []
    def ffn_forward(xin, kind, l, next_keys):
        gain = (ffn_pre_norm if kind == "pre" else ffn_post_norm)[l]
        wg, wu, wd = (full[n][l] for n, _ in ffn_keys(kind, l))
        if not next_keys:
            return _ffn_fwd(xin, gain, wg, wu, wd, "ffn_fwd")
        rider, offs = gather_rider(next_keys)
        xo, a, b, gathered = _ffn_fwd(xin, gain, wg, wu, wd, "ffn_fwd_gather", rider=rider)
        take_gathered(gathered, next_keys, offs)
        return xo, a, b

    for l in range(depth):
        sv = dict(x0=xs)
        x1, sv["a1"], sv["b1"] = ffn_forward(xs, "pre", l, (mixer_keys(0) if l == 0 else []) + ffn_keys("post", l))
        h = _rms_fwd(x1, mix_norm[l], "mix_norm_fwd")
        sv.update(x1=x1, h=h)
        if l % 2 == 0:
            e = l // 2
            ops = even_operands(e)
            proj = _mm(h, ops["win_pad"], "nt", "even_in_proj", out_dtype=F32)
            qn = _rms_fwd(proj, q_norm[e], "q_norm_fwd", col=col_cq, width=QL)
            kvn = _rms_fwd(proj, kv_norm[e], "kv_norm_fwd", col=col_ckv, width=KVL)
            q_big = _mm(qn, ops["wq_big"], "nt", "q_up_proj", out_dtype=F32)
            kv_big = _mm(kvn, ops["wkv_big"], "nt", "kv_up_proj", out_dtype=BF16)
            q_r, k_r, v_r = _rope_fwd(q_big, kv_big, proj, col_kr, ct, st, "rope_fwd")
            o_att, lse = _attn_fwd(q_r, k_r, v_r, "attn_fwd")
            sg = _sgu_fwd(proj, sg_norm[e], ops["wstack"], ops["bmat"], "sgu_fwd")
            tmp = _mm(o_att, ops["wo_attn"], "nn", "even_out_attn", out_dtype=F32, res=x1)
            x2 = _mm(sg, ops["wo_sg"], "nn", "even_out_sg", out_dtype=F32, res=tmp)
            sv.update(proj=proj, qn=qn, kvn=kvn, q=q_r, k=k_r, v=v_r, o=o_att, lse=lse, sg=sg)
        else:
            o = l // 2
            p = _mm(h, full["conv_w_in"][o], "nt", "conv_in_proj", out_dtype=BF16)
            cv = _conv_fwd(p, conv_taps(o), "conv_fwd")
            x2 = _mm(cv, full["conv_w_out"][o], "nn", "conv_out_proj", out_dtype=F32, res=x1)
            sv.update(p=p, cv=cv)
        sv["x2"] = x2
        next_keys = ffn_keys("pre", l + 1) + mixer_keys(l + 1) if l + 1 < depth else []
        xs, sv["a2"], sv["b2"] = ffn_forward(x2, "post", l, next_keys)
        saved.append(sv)

    gr = {n: [None] * w_loc[n].shape[0] for n in REPLICATED if n != "final_norm"}
    per_layer = {n: [None] * w_loc[n].shape[0] for n in SHARDED}
    pending = []

    def scatter_rider():
        pieces, where, off = [], [], 0
        for n, l, g in pending:
            piece = _pad_rows(_flat_rows(g.astype(BF16).reshape(N_DEV, -1), 1), 16, 1)
            pieces.append(piece)
            where.append((n, l, off))
            off += piece.shape[1]
        if off % GRAD_ROWS_MULT:
            pieces.append(jnp.zeros((N_DEV, (-off) % GRAD_ROWS_MULT, PACK_W), BF16))
        pending.clear()
        return _Exchange("scatter", pieces), where

    def take_scattered(received, where):
        owned = _sum_slots(received, "sum_grad_shards")
        for n, l, off in where:
            per_layer[n][l] = _unpack(owned, off, t_loc[n].shape[1:], 0)

    def ffn_backward(dxin, xin, kind, l, a, b):
        gain = (ffn_pre_norm if kind == "pre" else ffn_post_norm)[l]
        keys = ffn_keys(kind, l)
        wg, wu, wd = (full[n][l] for n, _ in keys)
        if pending:
            rider, where = scatter_rider()
            dxo, dz, hh, dy, dgain, received = _ffn_bwd(dxin, xin, gain, a, b, wg, wu, wd, "ffn_bwd_scatter", rider=rider)
            take_scattered(received, where)
        else:
            dxo, dz, hh, dy, dgain = _ffn_bwd(dxin, xin, gain, a, b, wg, wu, wd, "ffn_bwd")
        gr["ffn_%s_norm" % kind][l] = dgain[0]
        for (n, _), g in zip(keys, _ffn_dw(a, b, dz, hh, dy, "ffn_dw")):
            pending.append((n, l, g))
        return dxo

    dx, g_final, loss_part = _loss_head(xs, loss_target[0], final_norm, "loss_head")
    for l in reversed(range(depth)):
        sv = saved[l]
        dx = ffn_backward(dx, sv["x2"], "post", l, sv["a2"], sv["b2"])
        h = sv["h"]
        if l % 2 == 0:
            e = l // 2
            ops = even_operands(e)
            d_o = _mm(dx, ops["wo_attn"], "nt", "even_out_attn_bwd", out_dtype=BF16)
            d_sg = _mm(dx, ops["wo_sg"], "nt", "even_out_sg_bwd", out_dtype=BF16)
            g_wo_attn = _mm(sv["o"], dx, "tn", "even_out_attn_dw", out_dtype=F32)
            g_wo_sg = _mm(sv["sg"], dx, "tn", "even_out_sg_dw", out_dtype=F32)
            dq, dk, dv = _attn_bwd(sv["q"], sv["k"], sv["v"], sv["o"], d_o, sv["lse"], "attn_bwd")
            dq_big, dkv_big, dkr = _rope_bwd(dq, dk, dv, ct, st, "rope_bwd")
            dz_sg, g_wstack, g_bias, g_sgn = _sgu_bwd(d_sg, sv["proj"], sg_norm[e], ops["wstack"], ops["wtstack"],
                                                      ops["bmat"], gsum, "sgu_bwd")
            dqn = _mm(dq_big, ops["wq_big"], "nn", "q_up_proj_bwd", out_dtype=F32)
            g_wq_big = _mm(dq_big, sv["qn"], "tn", "q_up_proj_dw", out_dtype=F32)
            dkvn = _mm(dkv_big, ops["wkv_big"], "nn", "kv_up_proj_bwd", out_dtype=F32)
            g_wkv_big = _mm(dkv_big, sv["kvn"], "tn", "kv_up_proj_dw", out_dtype=F32)
            dcq, g_qn = _rms_bwd(dqn, sv["proj"], q_norm[e], "q_norm_bwd", col=col_cq, out_dtype=BF16)
            dckv, g_kvn = _rms_bwd(dkvn, sv["proj"], kv_norm[e], "kv_norm_bwd", col=col_ckv, out_dtype=BF16)
            dproj = jnp.concatenate([dz_sg, dckv, dkr, jnp.zeros((S, HEAD_PAD), BF16), dcq], axis=1)
            dh = _mm(dproj, ops["win_pad"], "nn", "even_in_proj_bwd", out_dtype=F32)
            g_win = _mm(dproj, h, "tn", "even_in_proj_dw", out_dtype=F32)
            o_cq, o_ckv, o_kr = col_cq * QL, col_ckv * KVL, col_kr * HEAD_PAD
            hw = HEADS * HEAD_PAD
            pending.append(("even_w_in", e, jnp.concatenate(
                [g_win[o_cq:o_cq + QL], g_win[o_ckv:o_ckv + KVL], g_win[o_kr:o_kr + ROPE], g_win[:ZW]], axis=0)))
            pending.append(("w_uq", e, _head_unpad(g_wq_big, QK_DIM).reshape(HEADS * QK_DIM, QL)))
            pending.append(("w_ukv", e, jnp.concatenate(
                [_head_unpad(g_wkv_big[:hw], NOPE), _head_unpad(g_wkv_big[hw:], VDIM)],
                axis=1).reshape(HEADS * (NOPE + VDIM), KVL)))
            pending.append(("even_w_out", e, jnp.concatenate(
                [_head_unpad(g_wo_attn, VDIM).reshape(HEADS * VDIM, D), g_wo_sg], axis=0)))
            gr["q_norm"][e], gr["kv_norm"][e], gr["sg_norm"][e] = g_qn[0], g_kvn[0], g_sgn[0]
            gr["sg_w"][e] = g_wstack.reshape(SG_GROUPS, SG_CHUNK, SG_CHUNK) * tril
            gr["sg_b"][e] = g_bias[:, :SG_GROUPS].T
        else:
            o = l // 2
            dcv = _mm(dx, full["conv_w_out"][o], "nt", "conv_out_proj_bwd", out_dtype=BF16)
            pending.append(("conv_w_out", o, _mm(sv["cv"], dx, "tn", "conv_out_proj_dw", out_dtype=BF16)))
            dp, dcw = _conv_bwd(dcv, sv["p"], conv_taps(o), "conv_bwd")
            dh = _mm(dp, full["conv_w_in"][o], "nn", "conv_in_proj_bwd", out_dtype=F32)
            pending.append(("conv_w_in", o, _mm(dp, h, "tn", "conv_in_proj_dw", out_dtype=BF16)))
            pending.append(("conv_w", o, jnp.swapaxes(dcw[:CONV_K], 0, 1)))
        dx, dgain = _rms_bwd(dh, sv["x1"], mix_norm[l], "mix_norm_bwd", res=dx)
        gr["mix_norm"][l] = dgain[0]
        dx = ffn_backward(dx, sv["x0"], "pre", l, sv["a1"], sv["b1"])
    grad_x = dx[None]

    rider, where = scatter_rider()
    take_scattered(_exchange("scatter", rider.arrays, "scatter_grads"), where)
    grads = {n: _to_t(n, jnp.stack(per_layer[n])) for n in SHARDED}

    small = [jnp.stack(gr[n]) for n in REPLICATED if n != "final_norm"] + [g_final[0], loss_part[0, :1]]
    spack, soffs = _pack(small, 0, SMALL_ROWS_MULT)
    sgath = _all_gather(spack, "gather_small_grads")
    ssum = _sum_slots(sgath, "sum_small_grads")
    names_small = [n for n in REPLICATED if n != "final_norm"] + ["final_norm", "loss"]
    for n, off, piece in zip(names_small, soffs, small):
        val = _unpack(ssum, off, piece.shape, 0)
        if n == "loss":
            loss = val[0]
        else:
            grads[n] = val

    delta, new_m, new_v = {}, {}, {}
    for n in SHARDED:
        two_d = lambda a: a.reshape(-1, a.shape[-1])
        d, nm, nv = _adamw(two_d(w_loc[n]), two_d(grads[n]), two_d(m_loc[n]), two_d(v_loc[n]), "adamw")
        delta[n], new_m[n], new_v[n] = (a.reshape(w_loc[n].shape) for a in (d, nm, nv))
    flat = lambda d: _pack([d[n] for n in REPLICATED], 0, SMALL_ROWS_MULT)
    (wf, aoffs), (gf, _), (mf, _), (vf, _) = flat(w_loc), flat(grads), flat(m_loc), flat(v_loc)
    for res, buf in zip((delta, new_m, new_v), _adamw(wf, gf, mf, vf, "adamw_replicated")):
        for n, off in zip(REPLICATED, aoffs):
            res[n] = _unpack(buf, off, w_loc[n].shape, 0)
    outs = [loss, grad_x] + [grads[n] for n in WEIGHTS]
    for res in (delta, new_m, new_v):
        outs += [res[n] for n in WEIGHTS]
    return tuple(outs)
```

```python
import math

import numpy as np
import jax
import jax.numpy as jnp
from jax import lax
from jax.experimental import pallas as pl
from jax.experimental.pallas import tpu as pltpu

F32 = jnp.float32
BF16 = jnp.bfloat16

N_DEV = 8
NORM_EPS = 1e-6
HEADS = 8
NOPE = 64
ROPE = 32
VDIM = 64
HEAD_PAD = 128
QK_DIM = NOPE + ROPE
ROPE_THETA = 10000.0
SG_GROUPS = 8
SG_GROUP_DIM = 64
SG_WIDTH = SG_GROUPS * SG_GROUP_DIM
SG_CHUNK = 128
CONV_K = 3
ADAM_LR, ADAM_B1, ADAM_B2, ADAM_EPS, ADAM_WD, ADAM_STEP = 0.001, 0.9, 0.999, 1e-08, 0.01, 10

LANE = 128
PACK_W = 1024
GRAD_ROWS_MULT = 512
SMALL_ROWS_MULT = 64
VMEM_LIMIT = 60 * 1024 * 1024

TILES = dict(ffn_fwd=512, ffn_bwd=256, ffn_dw=2048, mm=1024, mm_norm_bwd=512, rms=1024, mm_tn=2048, ew=1024,
             attn_fwd=2048, attn_bwd=1024, sgu=1024, adam=512)

NT = (((1,), (1,)), ((), ()))
NN = (((1,), (0,)), ((), ()))
TN = (((0,), (0,)), ((), ()))


def _dot(a, b, dims):
    return lax.dot_general(a, b, dims, preferred_element_type=F32)


def _cparams(*sem):
    return pltpu.CompilerParams(dimension_semantics=sem if sem else None, vmem_limit_bytes=VMEM_LIMIT)


def _tile(n, want):
    t = min(want, n)
    while n % t:
        t //= 2
    return t if t % 8 == 0 else n


def _lane_tile(n, cap):
    best = None
    for k in range(1, n // LANE + 1):
        t = k * LANE
        if n % t == 0 and t <= cap:
            best = t
    return best or n


def _row(v):
    return v.reshape(1, -1).astype(F32)


def _all_gather(block, name):
    R, W = block.shape

    def body(x_ref, out_ref, send_sems, recv_sems, local_sem):
        x, y, c = lax.axis_index("x"), lax.axis_index("y"), lax.axis_index("c")
        me, sibling = (x, y, c), (x, y, 1 - c)
        chips = [(1 - x, y), (x, 1 - y), (1 - x, 1 - y)]

        def slot(px, py, pc):
            return out_ref.at[4 * px + 2 * py + pc]

        def copy(k, blk, to, src=None):
            return pltpu.make_async_remote_copy(
                src_ref=slot(*blk) if src is None else src, dst_ref=slot(*blk),
                send_sem=send_sems.at[k], recv_sem=recv_sems.at[k],
                device_id=to, device_id_type=pl.DeviceIdType.MESH)

        mine = pltpu.make_async_copy(x_ref, slot(*me), local_sem)
        mine.start()
        first = [copy(0, me, sibling, src=x_ref)]
        first += [copy(1 + j, me, (*chip, c), src=x_ref) for j, chip in enumerate(chips)]
        for cp in first:
            cp.start()
        passed = [copy(4 + j, (*chip, c), sibling) for j, chip in enumerate(chips)]
        for j, chip in enumerate(chips):
            copy(1 + j, (*chip, c), me).wait_recv()
            passed[j].start()
        copy(0, sibling, me).wait_recv()
        for j, chip in enumerate(chips):
            copy(4 + j, (*chip, 1 - c), me).wait_recv()
        for cp in first + passed:
            cp.wait_send()
        mine.wait()

    return pl.pallas_call(
        body, name=name,
        out_shape=jax.ShapeDtypeStruct((N_DEV, R, W), block.dtype),
        in_specs=[pl.BlockSpec(memory_space=pl.ANY)],
        out_specs=pl.BlockSpec(memory_space=pl.ANY),
        scratch_shapes=[pltpu.SemaphoreType.DMA((7,)), pltpu.SemaphoreType.DMA((7,)), pltpu.SemaphoreType.DMA],
    )(block)


class _Exchange:
    def __init__(self, kind, arrays):
        self.kind, self.arrays = kind, list(arrays)
        if kind == "gather":
            (r, w), = [a.shape for a in self.arrays]
            self.rows = [r]
        else:
            self.rows = [a.shape[1] for a in self.arrays]
            w = self.arrays[0].shape[2]
        self.offs = [sum(self.rows[:i]) for i in range(len(self.rows))]
        self.n_in = len(self.arrays)
        self.out_shape = jax.ShapeDtypeStruct((N_DEV, sum(self.rows), w), self.arrays[0].dtype)
        self.in_specs = [pl.BlockSpec(memory_space=pl.ANY)] * self.n_in
        self.out_spec = pl.BlockSpec(memory_space=pl.ANY)
        self.out_specs, self.out_shapes = [self.out_spec], [self.out_shape]
        self.scratch = [pltpu.SemaphoreType.DMA((7,)), pltpu.SemaphoreType.DMA((7,)), pltpu.SemaphoreType.DMA]

    def _peers(self):
        x, y, c = lax.axis_index("x"), lax.axis_index("y"), lax.axis_index("c")
        me = 4 * x + 2 * y + c
        return me, [(k, (x ^ (k >> 2), y ^ ((k >> 1) & 1), c ^ (k & 1))) for k in range(1, N_DEV)]

    @staticmethod
    def _remote(src, dst, k, to, send_sems, recv_sems):
        return pltpu.make_async_remote_copy(
            src_ref=src, dst_ref=dst, send_sem=send_sems.at[k - 1], recv_sem=recv_sems.at[k - 1],
            device_id=to, device_id_type=pl.DeviceIdType.MESH)

    def start(self, s_refs, r_ref, send_sems, recv_sems, local_sem):
        me, peers = self._peers()
        for s_ref, off, r in zip(s_refs, self.offs, self.rows):
            src = s_ref if self.kind == "gather" else s_ref.at[me]
            pltpu.make_async_copy(src, r_ref.at[me, pl.ds(off, r)], local_sem).start()
        for k, to in peers:
            peer = 4 * to[0] + 2 * to[1] + to[2]
            for s_ref, off, r in zip(s_refs, self.offs, self.rows):
                src = s_ref if self.kind == "gather" else s_ref.at[peer]
                self._remote(src, r_ref.at[me, pl.ds(off, r)], k, to, send_sems, recv_sems).start()

    def wait(self, s_refs, r_ref, send_sems, recv_sems, local_sem):
        me, peers = self._peers()
        whole = r_ref.at[me]
        totals = [self._remote(whole, whole, k, to, send_sems, recv_sems) for k, to in peers]
        for cp in totals:
            cp.wait_recv()
        for cp in totals:
            cp.wait_send()
        pltpu.make_async_copy(whole, whole, local_sem).wait()


class _NoRider:
    arrays, in_specs, out_specs, out_shapes, scratch = [], [], [], [], []


_NO_RIDER = _NoRider()


def _ride(rider, refs, n_in, n_out, first, last):
    if rider is None:
        return refs, lambda: None
    k = rider.n_in
    s_refs = refs[n_in:n_in + k]
    r_ref = refs[n_in + k + n_out]
    sems = refs[-3:]
    own = refs[:n_in] + refs[n_in + k:n_in + k + n_out] + refs[n_in + k + n_out + 1:-3]

    @pl.when(first)
    def _():
        rider.start(s_refs, r_ref, *sems)

    def finish():
        @pl.when(last)
        def _():
            rider.wait(s_refs, r_ref, *sems)

    return own, finish


def _exchange(kind, arrays, name):
    ex = _Exchange(kind, arrays)

    def body(*refs):
        s_refs, r_ref, sems = refs[:ex.n_in], refs[ex.n_in], refs[ex.n_in + 1:]
        ex.start(s_refs, r_ref, *sems)
        ex.wait(s_refs, r_ref, *sems)

    return pl.pallas_call(
        body, name=name, out_shape=ex.out_shape, in_specs=ex.in_specs, out_specs=ex.out_spec,
        scratch_shapes=ex.scratch,
    )(*ex.arrays)


def _sum_slots(parts, name):
    _, R, W = parts.shape
    tr = _tile(R, TILES["adam"])

    def body(p_ref, o_ref):
        acc = p_ref[0].astype(F32)
        for s in range(1, N_DEV):
            acc = acc + p_ref[s].astype(F32)
        o_ref[...] = acc

    return pl.pallas_call(
        body, name=name, grid=(R // tr,),
        in_specs=[pl.BlockSpec((N_DEV, tr, W), lambda i: (0, i, 0))],
        out_specs=pl.BlockSpec((tr, W), lambda i: (i, 0)),
        out_shape=jax.ShapeDtypeStruct((R, W), F32),
        compiler_params=_cparams("parallel"),
    )(parts)


def _adamw(w, g, m, v, name):
    R, W = w.shape
    tr = _tile(R, TILES["adam"])
    c1 = 1.0 - ADAM_B1 ** ADAM_STEP
    c2 = 1.0 - ADAM_B2 ** ADAM_STEP

    def body(w_ref, g_ref, m_ref, v_ref, d_ref, nm_ref, nv_ref):
        g = g_ref[...]
        nm = ADAM_B1 * m_ref[...] + (1.0 - ADAM_B1) * g
        nv = ADAM_B2 * v_ref[...] + (1.0 - ADAM_B2) * (g * g)
        d_ref[...] = -ADAM_LR * ((nm / c1) / (jnp.sqrt(nv / c2) + ADAM_EPS) + ADAM_WD * w_ref[...])
        nm_ref[...] = nm
        nv_ref[...] = nv

    spec = pl.BlockSpec((tr, W), lambda i: (i, 0))
    return pl.pallas_call(
        body, name=name, grid=(R // tr,),
        in_specs=[spec] * 4, out_specs=[spec] * 3,
        out_shape=[jax.ShapeDtypeStruct((R, W), F32)] * 3,
        compiler_params=_cparams("parallel"),
    )(w, g, m, v)


def _mm(a, b, mode, name, out_dtype=BF16, res=None, scale=1.0, acol=None, kdim=None):
    if mode == "tn":
        S, M = a.shape
        N = b.shape[1]
        ts = _tile(S, TILES["mm_tn"])
        tmo = _lane_tile(M, 1024)

        def body(a_ref, b_ref, o_ref, acc):
            s = pl.program_id(1)

            @pl.when(s == 0)
            def _():
                acc[...] = jnp.zeros_like(acc)

            acc[...] += _dot(a_ref[...].astype(BF16), b_ref[...].astype(BF16), TN)

            @pl.when(s == pl.num_programs(1) - 1)
            def _():
                o_ref[...] = acc[...].astype(out_dtype)

        return pl.pallas_call(
            body, name=name, grid=(M // tmo, S // ts),
            in_specs=[pl.BlockSpec((ts, tmo), lambda i, s: (s, i)), pl.BlockSpec((ts, N), lambda i, s: (s, 0))],
            out_specs=pl.BlockSpec((tmo, N), lambda i, s: (i, 0)),
            out_shape=jax.ShapeDtypeStruct((M, N), out_dtype),
            scratch_shapes=[pltpu.VMEM((tmo, N), F32)],
            compiler_params=_cparams("parallel", "arbitrary"),
        )(a, b)

    M = a.shape[0]
    K = kdim if kdim is not None else a.shape[1]
    ac = 0 if acol is None else acol
    N = b.shape[1] if mode == "nn" else b.shape[0]
    tm = _tile(M, TILES["mm"])
    dims = NN if mode == "nn" else NT

    def body(*refs):
        if res is None:
            a_ref, b_ref, o_ref = refs
        else:
            a_ref, b_ref, r_ref, o_ref = refs
        acc = _dot(a_ref[...].astype(BF16), b_ref[...].astype(BF16), dims)
        if res is not None:
            acc = r_ref[...] + scale * acc
        o_ref[...] = acc.astype(out_dtype)

    in_specs = [pl.BlockSpec((tm, K), lambda i: (i, ac)), pl.BlockSpec(b.shape, lambda i: (0, 0))]
    args = [a, b]
    if res is not None:
        in_specs.append(pl.BlockSpec((tm, N), lambda i: (i, 0)))
        args.append(res)
    return pl.pallas_call(
        body, name=name, grid=(M // tm,),
        in_specs=in_specs, out_specs=pl.BlockSpec((tm, N), lambda i: (i, 0)),
        out_shape=jax.ShapeDtypeStruct((M, N), out_dtype),
        compiler_params=_cparams("parallel"),
    )(*args)


def _norm_mm(x, gain, w_t, name, out_dtype):
    S, D = x.shape
    N = w_t.shape[0]
    tm = _tile(S, TILES["mm"])

    def body(x_ref, g_ref, w_ref, h_ref, o_ref):
        xv = x_ref[...]
        r = lax.rsqrt(jnp.mean(xv * xv, axis=-1, keepdims=True) + NORM_EPS)
        h = (xv * r * g_ref[...]).astype(BF16)
        h_ref[...] = h
        o_ref[...] = _dot(h, w_ref[...], NT).astype(out_dtype)

    return pl.pallas_call(
        body, name=name, grid=(S // tm,),
        in_specs=[pl.BlockSpec((tm, D), lambda i: (i, 0)), pl.BlockSpec((1, D), lambda i: (0, 0)),
                  pl.BlockSpec((N, D), lambda i: (0, 0))],
        out_specs=[pl.BlockSpec((tm, D), lambda i: (i, 0)), pl.BlockSpec((tm, N), lambda i: (i, 0))],
        out_shape=[jax.ShapeDtypeStruct((S, D), BF16), jax.ShapeDtypeStruct((S, N), out_dtype)],
        compiler_params=_cparams("parallel"),
    )(x, _row(gain), w_t)


def _mm_norm_bwd(a, w, x, gain, res, name):
    S, K = a.shape
    D = w.shape[1]
    tm = _tile(S, TILES["mm_norm_bwd"])

    def body(a_ref, w_ref, x_ref, g_ref, r_ref, dx_ref, dg_ref):
        @pl.when(pl.program_id(0) == 0)
        def _():
            dg_ref[...] = jnp.zeros_like(dg_ref)

        dh = _dot(a_ref[...].astype(BF16), w_ref[...], NN)
        xv = x_ref[...]
        r = lax.rsqrt(jnp.mean(xv * xv, axis=-1, keepdims=True) + NORM_EPS)
        xhat = xv * r
        dg_ref[...] += jnp.sum(dh * xhat, axis=0, keepdims=True)
        dxhat = dh * g_ref[...]
        dx_ref[...] = r_ref[...] + r * (dxhat - xhat * jnp.mean(dxhat * xhat, axis=-1, keepdims=True))

    row = pl.BlockSpec((tm, D), lambda i: (i, 0))
    return pl.pallas_call(
        body, name=name, grid=(S // tm,),
        in_specs=[pl.BlockSpec((tm, K), lambda i: (i, 0)), pl.BlockSpec((K, D), lambda i: (0, 0)), row,
                  pl.BlockSpec((1, D), lambda i: (0, 0)), row],
        out_specs=[row, pl.BlockSpec((1, D), lambda i: (0, 0))],
        out_shape=[jax.ShapeDtypeStruct((S, D), F32), jax.ShapeDtypeStruct((1, D), F32)],
        compiler_params=_cparams("arbitrary"),
    )(a, w, x, _row(gain), res)


def _rms_fwd(x, gain, name, col=0, width=None):
    S = x.shape[0]
    W = width if width is not None else x.shape[1]
    tm = _tile(S, TILES["rms"])

    def body(x_ref, g_ref, o_ref):
        xv = x_ref[...].astype(F32)
        r = lax.rsqrt(jnp.mean(xv * xv, axis=-1, keepdims=True) + NORM_EPS)
        o_ref[...] = (xv * r * g_ref[...]).astype(BF16)

    return pl.pallas_call(
        body, name=name, grid=(S // tm,),
        in_specs=[pl.BlockSpec((tm, W), lambda i: (i, col)), pl.BlockSpec((1, W), lambda i: (0, 0))],
        out_specs=pl.BlockSpec((tm, W), lambda i: (i, 0)),
        out_shape=jax.ShapeDtypeStruct((S, W), BF16),
        compiler_params=_cparams("parallel"),
    )(x, _row(gain))


def _rms_bwd(dy, x, gain, name, col=0, res=None, out_dtype=F32):
    S, W = dy.shape
    tm = _tile(S, TILES["rms"])

    def body(*refs):
        if res is None:
            dy_ref, x_ref, g_ref, dx_ref, dg_ref = refs
        else:
            dy_ref, x_ref, g_ref, r_ref, dx_ref, dg_ref = refs

        @pl.when(pl.program_id(0) == 0)
        def _():
            dg_ref[...] = jnp.zeros_like(dg_ref)

        xv = x_ref[...].astype(F32)
        d = dy_ref[...].astype(F32)
        r = lax.rsqrt(jnp.mean(xv * xv, axis=-1, keepdims=True) + NORM_EPS)
        xhat = xv * r
        dg_ref[...] += jnp.sum(d * xhat, axis=0, keepdims=True)
        dxhat = d * g_ref[...]
        dx = r * (dxhat - xhat * jnp.mean(dxhat * xhat, axis=-1, keepdims=True))
        if res is not None:
            dx = dx + r_ref[...]
        dx_ref[...] = dx.astype(out_dtype)

    in_specs = [pl.BlockSpec((tm, W), lambda i: (i, 0)), pl.BlockSpec((tm, W), lambda i: (i, col)),
                pl.BlockSpec((1, W), lambda i: (0, 0))]
    args = [dy, x, _row(gain)]
    if res is not None:
        in_specs.append(pl.BlockSpec((tm, W), lambda i: (i, 0)))
        args.append(res)
    return pl.pallas_call(
        body, name=name, grid=(S // tm,),
        in_specs=in_specs,
        out_specs=[pl.BlockSpec((tm, W), lambda i: (i, 0)), pl.BlockSpec((1, W), lambda i: (0, 0))],
        out_shape=[jax.ShapeDtypeStruct((S, W), out_dtype), jax.ShapeDtypeStruct((1, W), F32)],
        compiler_params=_cparams("arbitrary"),
    )(*args)


def _silu_parts(a):
    s = jax.nn.sigmoid(a)
    return a * s, s * (1.0 + a * (1.0 - s))


def _ffn_fwd(x, gain, wg_t, wu_t, wd, name, rider=None):
    S, D = x.shape
    Fd = wd.shape[0]
    tm = _tile(S, TILES["ffn_fwd"])
    fc = _lane_tile(Fd, 512)

    def body(*refs):
        i = pl.program_id(0)
        own, finish = _ride(rider, refs, 5, 3, i == 0, i == pl.num_programs(0) - 1)
        x_ref, g_ref, wg_ref, wu_ref, wd_ref, o_ref, a_ref, b_ref = own
        xv = x_ref[...]
        r = lax.rsqrt(jnp.mean(xv * xv, axis=-1, keepdims=True) + NORM_EPS)
        h = (xv * r * g_ref[...]).astype(BF16)
        acc = jnp.zeros((tm, D), F32)
        for c in range(Fd // fc):
            sl = slice(c * fc, (c + 1) * fc)
            a = _dot(h, wg_ref[sl, :], NT)
            b = _dot(h, wu_ref[sl, :], NT)
            a_ref[:, sl] = a.astype(BF16)
            b_ref[:, sl] = b.astype(BF16)
            z = (a * jax.nn.sigmoid(a) * b).astype(BF16)
            acc = acc + _dot(z, wd_ref[sl, :], NN)
        o_ref[...] = xv + 0.5 * acc
        finish()

    wspec = pl.BlockSpec((Fd, D), lambda i: (0, 0), pipeline_mode=pl.Buffered(1))
    extra = rider or _NO_RIDER
    return pl.pallas_call(
        body, name=name, grid=(S // tm,),
        in_specs=[pl.BlockSpec((tm, D), lambda i: (i, 0)), pl.BlockSpec((1, D), lambda i: (0, 0)), wspec, wspec,
                  wspec] + extra.in_specs,
        out_specs=[pl.BlockSpec((tm, D), lambda i: (i, 0)), pl.BlockSpec((tm, Fd), lambda i: (i, 0)),
                   pl.BlockSpec((tm, Fd), lambda i: (i, 0))] + extra.out_specs,
        out_shape=[jax.ShapeDtypeStruct((S, D), F32), jax.ShapeDtypeStruct((S, Fd), BF16),
                   jax.ShapeDtypeStruct((S, Fd), BF16)] + extra.out_shapes,
        scratch_shapes=extra.scratch,
        compiler_params=_cparams("arbitrary" if rider else "parallel"),
    )(x, _row(gain), wg_t, wu_t, wd, *extra.arrays)


def _ffn_bwd(g, x, gain, a, b, wg_t, wu_t, wd, name, rider=None):
    S, D = x.shape
    Fd = wd.shape[0]
    tm = _tile(S, TILES["ffn_bwd"])
    fc = Fd

    def body(*refs):
        i = pl.program_id(0)
        own, finish = _ride(rider, refs, 8, 5, i == 0, i == pl.num_programs(0) - 1)
        g_ref, x_ref, gain_ref, a_ref, b_ref, wg_ref, wu_ref, wd_ref, dx_ref, dz_ref, h_ref, dy_ref, dg_ref = own

        @pl.when(i == 0)
        def _():
            dg_ref[...] = jnp.zeros_like(dg_ref)

        gv = g_ref[...]
        xv = x_ref[...]
        r = lax.rsqrt(jnp.mean(xv * xv, axis=-1, keepdims=True) + NORM_EPS)
        xhat = xv * r
        h_ref[...] = (xhat * gain_ref[...]).astype(BF16)
        dy = (0.5 * gv).astype(BF16)
        dy_ref[...] = dy
        dh = jnp.zeros((tm, D), F32)
        for c in range(Fd // fc):
            sl = slice(c * fc, (c + 1) * fc)
            av = a_ref[:, sl].astype(F32)
            bv = b_ref[:, sl].astype(F32)
            dz = _dot(dy, wd_ref[sl, :], NT).astype(BF16)
            dz_ref[:, sl] = dz
            dzf = dz.astype(F32)
            silu, dsilu = _silu_parts(av)
            da = (dzf * bv * dsilu).astype(BF16)
            db = (dzf * silu).astype(BF16)
            dh = dh + _dot(da, wg_ref[sl, :], NN) + _dot(db, wu_ref[sl, :], NN)
        dg_ref[...] += jnp.sum(dh * xhat, axis=0, keepdims=True)
        dxhat = dh * gain_ref[...]
        dx_ref[...] = gv + r * (dxhat - xhat * jnp.mean(dxhat * xhat, axis=-1, keepdims=True))
        finish()

    wspec = pl.BlockSpec((Fd, D), lambda i: (0, 0), pipeline_mode=pl.Buffered(1))
    row = pl.BlockSpec((tm, D), lambda i: (i, 0))
    wide = pl.BlockSpec((tm, Fd), lambda i: (i, 0))
    extra = rider or _NO_RIDER
    return pl.pallas_call(
        body, name=name, grid=(S // tm,),
        in_specs=[row, row, pl.BlockSpec((1, D), lambda i: (0, 0)), wide, wide, wspec, wspec, wspec] + extra.in_specs,
        out_specs=[row, wide, row, row, pl.BlockSpec((1, D), lambda i: (0, 0))] + extra.out_specs,
        out_shape=[jax.ShapeDtypeStruct((S, D), F32), jax.ShapeDtypeStruct((S, Fd), BF16),
                   jax.ShapeDtypeStruct((S, D), BF16), jax.ShapeDtypeStruct((S, D), BF16),
                   jax.ShapeDtypeStruct((1, D), F32)] + extra.out_shapes,
        scratch_shapes=extra.scratch,
        compiler_params=_cparams("arbitrary"),
    )(g, x, _row(gain), a, b, wg_t, wu_t, wd, *extra.arrays)


def _ffn_dw(a, b, dz, h, dy, name):
    S, Fd = a.shape
    D = h.shape[1]
    ts = _tile(S, TILES["ffn_dw"])
    tf = _lane_tile(Fd, 256)

    def body(a_ref, b_ref, dz_ref, h_ref, dy_ref, og_ref, ou_ref, od_ref, accg, accu, accd):
        s = pl.program_id(1)

        @pl.when(s == 0)
        def _():
            accg[...] = jnp.zeros_like(accg)
            accu[...] = jnp.zeros_like(accu)
            accd[...] = jnp.zeros_like(accd)

        av = a_ref[...].astype(F32)
        bv = b_ref[...].astype(F32)
        dzf = dz_ref[...].astype(F32)
        silu, dsilu = _silu_parts(av)
        da = (dzf * bv * dsilu).astype(BF16)
        db = (dzf * silu).astype(BF16)
        z = (silu * bv).astype(BF16)
        hv = h_ref[...]
        accg[...] += _dot(da, hv, TN)
        accu[...] += _dot(db, hv, TN)
        accd[...] += _dot(z, dy_ref[...], TN)

        @pl.when(s == pl.num_programs(1) - 1)
        def _():
            og_ref[...] = accg[...].astype(BF16)
            ou_ref[...] = accu[...].astype(BF16)
            od_ref[...] = accd[...].astype(BF16)

    wide = pl.BlockSpec((ts, tf), lambda f, s: (s, f))
    row = pl.BlockSpec((ts, D), lambda f, s: (s, 0))
    out = pl.BlockSpec((tf, D), lambda f, s: (f, 0))
    return pl.pallas_call(
        body, name=name, grid=(Fd // tf, S // ts),
        in_specs=[wide, wide, wide, row, row], out_specs=[out, out, out],
        out_shape=[jax.ShapeDtypeStruct((Fd, D), BF16)] * 3,
        scratch_shapes=[pltpu.VMEM((tf, D), F32)] * 3,
        compiler_params=_cparams("parallel", "arbitrary"),
    )(a, b, dz, h, dy)


def _loss_head(x, target, gain, name):
    S, D = x.shape
    tm = _tile(S, TILES["ew"])

    def body(x_ref, t_ref, g_ref, dx_ref, dg_ref, loss_ref):
        @pl.when(pl.program_id(0) == 0)
        def _():
            dg_ref[...] = jnp.zeros_like(dg_ref)
            loss_ref[...] = jnp.zeros_like(loss_ref)

        xv = x_ref[...]
        r = lax.rsqrt(jnp.mean(xv * xv, axis=-1, keepdims=True) + NORM_EPS)
        xhat = xv * r
        e = xhat * g_ref[...] - t_ref[...]
        per_tok = jnp.mean(e * e, axis=-1, keepdims=True)
        loss_ref[...] += jnp.broadcast_to(0.5 * jnp.sum(per_tok, axis=0, keepdims=True), (1, LANE))
        dy = e * (1.0 / D)
        dg_ref[...] += jnp.sum(dy * xhat, axis=0, keepdims=True)
        dxhat = dy * g_ref[...]
        dx_ref[...] = r * (dxhat - xhat * jnp.mean(dxhat * xhat, axis=-1, keepdims=True))

    row = pl.BlockSpec((tm, D), lambda i: (i, 0))
    return pl.pallas_call(
        body, name=name, grid=(S // tm,),
        in_specs=[row, row, pl.BlockSpec((1, D), lambda i: (0, 0))],
        out_specs=[row, pl.BlockSpec((1, D), lambda i: (0, 0)), pl.BlockSpec((1, LANE), lambda i: (0, 0))],
        out_shape=[jax.ShapeDtypeStruct((S, D), F32), jax.ShapeDtypeStruct((1, D), F32),
                   jax.ShapeDtypeStruct((1, LANE), F32)],
        compiler_params=_cparams("arbitrary"),
    )(x, target, _row(gain))


def _shift_down(u, halo, k, rows):
    out = pltpu.roll(u, k, 0)
    for j in range(k):
        out = jnp.where(rows == j, halo[8 - k + j:8 - k + j + 1, :], out)
    return out


def _shift_up(u, halo, k, rows, n):
    out = pltpu.roll(u, n - k, 0)
    for j in range(k):
        out = jnp.where(rows == n - k + j, halo[j:j + 1, :], out)
    return out


def _conv_fwd(p, cw, name):
    S, W3 = p.shape
    W = W3 // 3
    tm = _tile(S, TILES["ew"])
    hb = tm // 8

    def body(p_ref, ph_ref, w_ref, v_ref):
        i = pl.program_id(0)
        bg = p_ref[:, 0:W].astype(F32)
        u = p_ref[:, W:2 * W].astype(F32) * p_ref[:, 2 * W:3 * W].astype(F32)
        uh = ph_ref[:, W:2 * W].astype(F32) * ph_ref[:, 2 * W:3 * W].astype(F32)
        uh = jnp.where(i > 0, uh, 0.0)
        rows = lax.broadcasted_iota(jnp.int32, (tm, 1), 0)
        u1 = _shift_down(u, uh, 1, rows)
        u2 = _shift_down(u, uh, 2, rows)
        y = w_ref[0:1, :] * u2 + w_ref[1:2, :] * u1 + w_ref[2:3, :] * u
        v_ref[...] = (bg * y).astype(BF16)

    return pl.pallas_call(
        body, name=name, grid=(S // tm,),
        in_specs=[pl.BlockSpec((tm, W3), lambda i: (i, 0)),
                  pl.BlockSpec((8, W3), lambda i: (jnp.maximum(i * hb - 1, 0), 0)),
                  pl.BlockSpec((8, W), lambda i: (0, 0))],
        out_specs=pl.BlockSpec((tm, W), lambda i: (i, 0)),
        out_shape=jax.ShapeDtypeStruct((S, W), BF16),
        compiler_params=_cparams("parallel"),
    )(p, p, cw)


def _conv_bwd(dv, p, cw, name):
    S, W3 = p.shape
    W = W3 // 3
    tm = _tile(S, TILES["ew"])
    hb = tm // 8
    last = S // 8 - 1

    def body(dv_ref, dvn_ref, p_ref, pp_ref, pn_ref, w_ref, dp_ref, dw_ref):
        i = pl.program_id(0)
        n = pl.num_programs(0)

        @pl.when(i == 0)
        def _():
            dw_ref[...] = jnp.zeros_like(dw_ref)

        bg = p_ref[:, 0:W].astype(F32)
        cg = p_ref[:, W:2 * W].astype(F32)
        zz = p_ref[:, 2 * W:3 * W].astype(F32)
        u = cg * zz
        uh = pp_ref[:, W:2 * W].astype(F32) * pp_ref[:, 2 * W:3 * W].astype(F32)
        uh = jnp.where(i > 0, uh, 0.0)
        rows = lax.broadcasted_iota(jnp.int32, (tm, 1), 0)
        u1 = _shift_down(u, uh, 1, rows)
        u2 = _shift_down(u, uh, 2, rows)
        w0, w1, w2 = w_ref[0:1, :], w_ref[1:2, :], w_ref[2:3, :]
        y = w0 * u2 + w1 * u1 + w2 * u
        dvv = dv_ref[...].astype(F32)
        dy = dvv * bg
        dyh = dvn_ref[...].astype(F32) * pn_ref[:, 0:W].astype(F32)
        dyh = jnp.where(i < n - 1, dyh, 0.0)
        d1 = _shift_up(dy, dyh, 1, rows, tm)
        d2 = _shift_up(dy, dyh, 2, rows, tm)
        du = w2 * dy + w1 * d1 + w0 * d2
        dp_ref[:, 0:W] = (dvv * y).astype(BF16)
        dp_ref[:, W:2 * W] = (du * zz).astype(BF16)
        dp_ref[:, 2 * W:3 * W] = (du * cg).astype(BF16)
        dw_ref[0:1, :] += jnp.sum(dy * u2, axis=0, keepdims=True)
        dw_ref[1:2, :] += jnp.sum(dy * u1, axis=0, keepdims=True)
        dw_ref[2:3, :] += jnp.sum(dy * u, axis=0, keepdims=True)

    return pl.pallas_call(
        body, name=name, grid=(S // tm,),
        in_specs=[pl.BlockSpec((tm, W), lambda i: (i, 0)),
                  pl.BlockSpec((8, W), lambda i: (jnp.minimum((i + 1) * hb, last), 0)),
                  pl.BlockSpec((tm, W3), lambda i: (i, 0)),
                  pl.BlockSpec((8, W3), lambda i: (jnp.maximum(i * hb - 1, 0), 0)),
                  pl.BlockSpec((8, W3), lambda i: (jnp.minimum((i + 1) * hb, last), 0)),
                  pl.BlockSpec((8, W), lambda i: (0, 0))],
        out_specs=[pl.BlockSpec((tm, W3), lambda i: (i, 0)), pl.BlockSpec((8, W), lambda i: (0, 0))],
        out_shape=[jax.ShapeDtypeStruct((S, W3), BF16), jax.ShapeDtypeStruct((8, W), F32)],
        compiler_params=_cparams("arbitrary"),
    )(dv, dv, p, p, p, cw)


def _rope_swap(r, lane):
    mid = NOPE + ROPE // 2
    first = (lane >= NOPE) & (lane < mid)
    second = (lane >= mid) & (lane < QK_DIM)
    return jnp.where(first, pltpu.roll(r, HEAD_PAD - ROPE // 2, 1), jnp.where(second, pltpu.roll(r, ROPE // 2, 1), 0.0))


def _rope_fwd(q_big, kv_big, proj, kr_col, ct, st, name):
    S = q_big.shape[0]
    HW = HEADS * HEAD_PAD
    tm = _tile(S, TILES["ew"])

    def body(q_ref, k_ref, v_ref, kr_ref, ct_ref, st_ref, qo_ref, ko_ref, vo_ref):
        lane = lax.broadcasted_iota(jnp.int32, (1, HEAD_PAD), 1)
        ctv, stv = ct_ref[...], st_ref[...]
        krr = pltpu.roll(kr_ref[...].astype(F32), NOPE, 1)
        kro = krr * ctv + _rope_swap(krr, lane) * stv
        for h in range(HEADS):
            sl = slice(h * HEAD_PAD, (h + 1) * HEAD_PAD)
            qh = q_ref[:, sl].astype(F32)
            qo_ref[:, sl] = (qh * ctv + _rope_swap(qh, lane) * stv).astype(BF16)
            ko_ref[:, sl] = (k_ref[:, sl].astype(F32) + kro).astype(BF16)
            vo_ref[:, sl] = jnp.where(lane == VDIM, 1.0, v_ref[:, sl].astype(F32)).astype(BF16)

    wide = pl.BlockSpec((tm, HW), lambda i: (i, 0))
    narrow = pl.BlockSpec((tm, HEAD_PAD), lambda i: (i, 0))
    return pl.pallas_call(
        body, name=name, grid=(S // tm,),
        in_specs=[wide, wide, pl.BlockSpec((tm, HW), lambda i: (i, 1)),
                  pl.BlockSpec((tm, HEAD_PAD), lambda i: (i, kr_col)), narrow, narrow],
        out_specs=[wide, wide, wide],
        out_shape=[jax.ShapeDtypeStruct((S, HW), BF16)] * 3,
        compiler_params=_cparams("parallel"),
    )(q_big, kv_big, kv_big, proj, ct, st)


def _rope_bwd(dq, dk, dv, ct, st, name):
    S = dq.shape[0]
    HW = HEADS * HEAD_PAD
    tm = _tile(S, TILES["ew"])

    def body(dq_ref, dk_ref, dv_ref, ct_ref, st_ref, oq_ref, okv_ref, okr_ref):
        lane = lax.broadcasted_iota(jnp.int32, (1, HEAD_PAD), 1)
        ctv, stv = ct_ref[...], st_ref[...]
        acc = jnp.zeros((tm, HEAD_PAD), F32)
        for h in range(HEADS):
            sl = slice(h * HEAD_PAD, (h + 1) * HEAD_PAD)
            d = dq_ref[:, sl].astype(F32)
            oq_ref[:, sl] = (d * ctv + _rope_swap(d * stv, lane)).astype(BF16)
            d = dk_ref[:, sl].astype(F32)
            okv_ref[:, sl] = jnp.where(lane < NOPE, d, 0.0).astype(BF16)
            acc = acc + jnp.where(lane >= NOPE, d * ctv + _rope_swap(d * stv, lane), 0.0)
        okv_ref[:, HW:2 * HW] = dv_ref[...].astype(BF16)
        okr_ref[...] = pltpu.roll(acc, HEAD_PAD - NOPE, 1).astype(BF16)

    wide = pl.BlockSpec((tm, HW), lambda i: (i, 0))
    narrow = pl.BlockSpec((tm, HEAD_PAD), lambda i: (i, 0))
    return pl.pallas_call(
        body, name=name, grid=(S // tm,),
        in_specs=[wide, wide, wide, narrow, narrow],
        out_specs=[wide, pl.BlockSpec((tm, 2 * HW), lambda i: (i, 0)), narrow],
        out_shape=[jax.ShapeDtypeStruct((S, HW), BF16), jax.ShapeDtypeStruct((S, 2 * HW), BF16),
                   jax.ShapeDtypeStruct((S, HEAD_PAD), BF16)],
        compiler_params=_cparams("parallel"),
    )(dq, dk, dv, ct, st)


def _pairs(n, by_key):
    if by_key:
        pr = [(i, j) for j in range(n) for i in range(j, n)]
    else:
        pr = [(i, j) for i in range(n) for j in range(i + 1)]
    qi = np.array([p[0] for p in pr], np.int32)
    kj = np.array([p[1] for p in pr], np.int32)
    return jnp.asarray(qi), jnp.asarray(kj)


_LOG2E = 1.4426950408889634
_LN2 = 0.6931471805599453


def _tile_mask(t):
    return lax.broadcasted_iota(jnp.int32, (t, t), 1) <= lax.broadcasted_iota(jnp.int32, (t, t), 0)


def _attn_fwd(q, k, v, name):
    S = q.shape[0]
    HW = HEADS * HEAD_PAD
    t = _tile(S, TILES["attn_fwd"])
    n = S // t
    qi, kj = _pairs(n, by_key=False)
    c = (QK_DIM ** -0.5) * _LOG2E

    def body(qi_ref, kj_ref, q_ref, k_ref, v_ref, o_ref, lse_ref, m_s, acc_s):
        p_id = pl.program_id(1)
        i, j = qi_ref[p_id], kj_ref[p_id]

        @pl.when(j == 0)
        def _():
            m_s[...] = jnp.full_like(m_s, -jnp.inf)
            acc_s[...] = jnp.zeros_like(acc_s)

        def step(on_diagonal):
            s = _dot(q_ref[...], k_ref[...], NT)
            if on_diagonal:
                s = jnp.where(_tile_mask(t), s, -jnp.inf)
            m_old = m_s[...]
            m_new = jnp.maximum(m_old, jnp.max(s, axis=-1, keepdims=True))
            p = jnp.exp2((s - m_new) * c).astype(BF16)
            acc_s[...] = jnp.exp2((m_old - m_new) * c) * acc_s[...] + _dot(p, v_ref[...], NN)
            m_s[...] = m_new

        @pl.when(i == j)
        def _():
            step(True)

        @pl.when(i != j)
        def _():
            step(False)

        @pl.when(j == i)
        def _():
            acc = acc_s[...]
            l = acc[:, VDIM:VDIM + 1]
            o_ref[...] = (acc * (1.0 / l)).astype(BF16)
            lse_ref[...] = jnp.broadcast_to(m_s[...] * c + jnp.log2(l), (t, HEAD_PAD))

    qspec = pl.BlockSpec((t, HEAD_PAD), lambda h, p, qi, kj: (qi[p], h))
    kspec = pl.BlockSpec((t, HEAD_PAD), lambda h, p, qi, kj: (kj[p], h))
    grid_spec = pltpu.PrefetchScalarGridSpec(
        num_scalar_prefetch=2, grid=(HEADS, int(qi.shape[0])),
        in_specs=[qspec, kspec, kspec], out_specs=[qspec, qspec],
        scratch_shapes=[pltpu.VMEM((t, 1), F32), pltpu.VMEM((t, HEAD_PAD), F32)])
    return pl.pallas_call(
        body, name=name, grid_spec=grid_spec,
        out_shape=[jax.ShapeDtypeStruct((S, HW), BF16), jax.ShapeDtypeStruct((S, HW), F32)],
        compiler_params=_cparams("parallel", "arbitrary"),
    )(qi, kj, q, k, v)


def _attn_bwd(q, k, v, o, do, lse2, name):
    S = q.shape[0]
    HW = HEADS * HEAD_PAD
    t = _tile(S, TILES["attn_bwd"])
    n = S // t
    qi, kj = _pairs(n, by_key=True)
    scale = QK_DIM ** -0.5

    def body(qi_ref, kj_ref, q_ref, k_ref, v_ref, o_ref, do_ref, lse_ref, dq_ref, dk_ref, dv_ref, dk_s, dv_s):
        p_id = pl.program_id(1)
        i, j = qi_ref[p_id], kj_ref[p_id]

        @pl.when(p_id == 0)
        def _():
            dq_ref[...] = jnp.zeros_like(dq_ref)

        @pl.when(i == j)
        def _():
            dk_s[...] = jnp.zeros_like(dk_s)
            dv_s[...] = jnp.zeros_like(dv_s)

        def step(on_diagonal):
            qv, kv, vv = q_ref[...], k_ref[...], v_ref[...]
            dov = do_ref[...]
            p = jnp.exp(_dot(qv, kv, NT) * scale - lse_ref[:, 0:1] * _LN2)
            if on_diagonal:
                p = jnp.where(_tile_mask(t), p, 0.0)
            delta = jnp.sum(dov.astype(F32) * o_ref[...].astype(F32), axis=-1, keepdims=True)
            dv_s[...] += _dot(p.astype(BF16), dov, TN)
            ds = (p * (_dot(dov, vv, NT) - delta) * scale).astype(BF16)
            dk_s[...] += _dot(ds, qv, TN)
            rows = pl.ds(pl.multiple_of(i * t, t), t)
            dq_ref[rows, :] += _dot(ds, kv, NN)

        @pl.when(i == j)
        def _():
            step(True)

        @pl.when(i != j)
        def _():
            step(False)

        @pl.when(i == n - 1)
        def _():
            dk_ref[...] = dk_s[...]
            dv_ref[...] = dv_s[...]

    qspec = pl.BlockSpec((t, HEAD_PAD), lambda h, p, qi, kj: (qi[p], h))
    kspec = pl.BlockSpec((t, HEAD_PAD), lambda h, p, qi, kj: (kj[p], h))
    grid_spec = pltpu.PrefetchScalarGridSpec(
        num_scalar_prefetch=2, grid=(HEADS, int(qi.shape[0])),
        in_specs=[qspec, kspec, kspec, qspec, qspec, qspec],
        out_specs=[pl.BlockSpec((S, HEAD_PAD), lambda h, p, qi, kj: (0, h)), kspec, kspec],
        scratch_shapes=[pltpu.VMEM((t, HEAD_PAD), F32), pltpu.VMEM((t, HEAD_PAD), F32)])
    return pl.pallas_call(
        body, name=name, grid_spec=grid_spec,
        out_shape=[jax.ShapeDtypeStruct((S, HW), F32)] * 3,
        compiler_params=_cparams("parallel", "arbitrary"),
    )(qi, kj, q, k, v, o, do, lse2)


_SQRT_HALF = 0.7071067811865476
_INV_SQRT_2PI = 0.3989422804014327


def _sg_select(r, grp):
    out = jnp.where(grp == 0, r[0:SG_CHUNK, :], 0.0)
    for g in range(1, SG_GROUPS):
        out = out + jnp.where(grp == g, r[g * SG_CHUNK:(g + 1) * SG_CHUNK, :], 0.0)
    return out


def _sgu_fwd(proj, gain, wstack, bmat, name):
    S = proj.shape[0]
    W = SG_WIDTH
    tm = _tile(S, TILES["sgu"])

    def body(z_ref, g_ref, w_ref, b_ref, o_ref):
        z = z_ref[...].astype(F32)
        zg = 0.5 * z * (1.0 + lax.erf(z * _SQRT_HALF))
        u, vv = zg[:, 0:W], zg[:, W:2 * W]
        r = lax.rsqrt(jnp.mean(vv * vv, axis=-1, keepdims=True) + NORM_EPS)
        vn = (vv * r * g_ref[...]).astype(BF16)
        grp = lax.broadcasted_iota(jnp.int32, (1, W), 1) // SG_GROUP_DIM
        for c in range(tm // SG_CHUNK):
            sl = slice(c * SG_CHUNK, (c + 1) * SG_CHUNK)
            mixed = _sg_select(_dot(w_ref[...], vn[sl, :], NN), grp) + b_ref[...]
            o_ref[sl, :] = (u[sl, :] * mixed).astype(BF16)

    return pl.pallas_call(
        body, name=name, grid=(S // tm,),
        in_specs=[pl.BlockSpec((tm, 2 * W), lambda i: (i, 0)), pl.BlockSpec((1, W), lambda i: (0, 0)),
                  pl.BlockSpec(wstack.shape, lambda i: (0, 0)), pl.BlockSpec(bmat.shape, lambda i: (0, 0))],
        out_specs=pl.BlockSpec((tm, W), lambda i: (i, 0)),
        out_shape=jax.ShapeDtypeStruct((S, W), BF16),
        compiler_params=_cparams("parallel"),
    )(proj, _row(gain), wstack, bmat)


def _sgu_bwd(dsg, proj, gain, wstack, wtstack, bmat, gsum, name):
    S = proj.shape[0]
    W = SG_WIDTH
    tm = _tile(S, TILES["sgu"])
    GS = SG_GROUPS * SG_CHUNK

    def body(d_ref, z_ref, g_ref, w_ref, wt_ref, b_ref, e_ref, dz_ref, dw_ref, db_ref, dg_ref, dw_s, db_s):
        i = pl.program_id(0)

        @pl.when(i == 0)
        def _():
            dw_s[...] = jnp.zeros_like(dw_s)
            db_s[...] = jnp.zeros_like(db_s)
            dg_ref[...] = jnp.zeros_like(dg_ref)

        z = z_ref[...].astype(F32)
        cdf = 0.5 * (1.0 + lax.erf(z * _SQRT_HALF))
        zg = z * cdf
        u, vv = zg[:, 0:W], zg[:, W:2 * W]
        r = lax.rsqrt(jnp.mean(vv * vv, axis=-1, keepdims=True) + NORM_EPS)
        vhat = vv * r
        vn = (vhat * g_ref[...]).astype(BF16)
        grp = lax.broadcasted_iota(jnp.int32, (1, W), 1) // SG_GROUP_DIM
        d = d_ref[...].astype(F32)
        du_parts, dvn_parts = [], []
        for c in range(tm // SG_CHUNK):
            sl = slice(c * SG_CHUNK, (c + 1) * SG_CHUNK)
            vc = vn[sl, :]
            mixed = _sg_select(_dot(w_ref[...], vc, NN), grp) + b_ref[...]
            dc = d[sl, :]
            du_parts.append(dc * mixed)
            dmix = dc * u[sl, :]
            db_s[...] += dmix
            dmb = dmix.astype(BF16)
            dvn_parts.append(_sg_select(_dot(wt_ref[...], dmb, NN), grp))
            astack = jnp.concatenate([jnp.where(grp == g, dmb, jnp.zeros_like(dmb)) for g in range(SG_GROUPS)], axis=0)
            dw_s[...] += _dot(astack, vc, NT)
        du = jnp.concatenate(du_parts, axis=0)
        dvn = jnp.concatenate(dvn_parts, axis=0)
        dg_ref[...] += jnp.sum(dvn * vhat, axis=0, keepdims=True)
        dvhat = dvn * g_ref[...]
        dvv = r * (dvhat - vhat * jnp.mean(dvhat * vhat, axis=-1, keepdims=True))
        dgelu = cdf + z * (_INV_SQRT_2PI * jnp.exp(-0.5 * z * z))
        dz_ref[:, 0:W] = (du * dgelu[:, 0:W]).astype(BF16)
        dz_ref[:, W:2 * W] = (dvv * dgelu[:, W:2 * W]).astype(BF16)

        @pl.when(i == pl.num_programs(0) - 1)
        def _():
            dw_ref[...] = dw_s[...]
            db_ref[...] = lax.dot_general(db_s[...], e_ref[...], NN, precision=lax.Precision.HIGHEST,
                                          preferred_element_type=F32)

    full = lambda a: pl.BlockSpec(a.shape, lambda i: (0, 0))
    return pl.pallas_call(
        body, name=name, grid=(S // tm,),
        in_specs=[pl.BlockSpec((tm, W), lambda i: (i, 0)), pl.BlockSpec((tm, 2 * W), lambda i: (i, 0)),
                  pl.BlockSpec((1, W), lambda i: (0, 0)), full(wstack), full(wtstack), full(bmat), full(gsum)],
        out_specs=[pl.BlockSpec((tm, 2 * W), lambda i: (i, 0)), pl.BlockSpec((GS, SG_CHUNK), lambda i: (0, 0)),
                   pl.BlockSpec((SG_CHUNK, LANE), lambda i: (0, 0)), pl.BlockSpec((1, W), lambda i: (0, 0))],
        out_shape=[jax.ShapeDtypeStruct((S, 2 * W), BF16), jax.ShapeDtypeStruct((GS, SG_CHUNK), F32),
                   jax.ShapeDtypeStruct((SG_CHUNK, LANE), F32), jax.ShapeDtypeStruct((1, W), F32)],
        scratch_shapes=[pltpu.VMEM((GS, SG_CHUNK), F32), pltpu.VMEM((SG_CHUNK, W), F32)],
        compiler_params=_cparams("arbitrary"),
    )(dsg, proj, _row(gain), wstack, wtstack, bmat, gsum)


WEIGHTS = ['ffn_pre_norm', 'ffn_pre_w_gate', 'ffn_pre_w_up', 'ffn_pre_w_down', 'mix_norm', 'ffn_post_norm',
           'ffn_post_w_gate', 'ffn_post_w_up', 'ffn_post_w_down', 'even_w_in', 'q_norm', 'w_uq', 'kv_norm', 'w_ukv',
           'sg_norm', 'sg_w', 'sg_b', 'even_w_out', 'conv_w_in', 'conv_w', 'conv_w_out', 'final_norm']
SHARD_AXIS = dict(ffn_pre_w_gate=2, ffn_pre_w_up=2, ffn_pre_w_down=1, ffn_post_w_gate=2, ffn_post_w_up=2,
                  ffn_post_w_down=1, even_w_in=2, w_uq=2, w_ukv=2, even_w_out=1, conv_w_in=2, conv_w=2, conv_w_out=1)
SHARDED = [n for n in WEIGHTS if n in SHARD_AXIS]
REPLICATED = [n for n in WEIGHTS if n not in SHARD_AXIS]


def _to_t(name, w):
    return jnp.swapaxes(w, 1, 2) if SHARD_AXIS[name] == 2 else w


def _rows_of(n):
    return -(-n // PACK_W)


def _pad_rows(a, mult, axis):
    r = a.shape[axis]
    extra = (-r) % mult
    if extra == 0:
        return a
    pad = [(0, 0)] * a.ndim
    pad[axis] = (0, extra)
    return jnp.pad(a, pad)


def _flat_rows(a, lead):
    flat = a.reshape(a.shape[:lead] + (-1,))
    n = flat.shape[-1]
    flat = _pad_rows(flat, PACK_W, lead)
    return flat.reshape(a.shape[:lead] + (_rows_of(n), PACK_W))


def _pack(pieces, lead, mult, piece_mult=1):
    rows, offs, off = [], [], 0
    for p in pieces:
        r = _pad_rows(_flat_rows(p, lead), piece_mult, lead)
        rows.append(r)
        offs.append(off)
        off += r.shape[lead]
    return _pad_rows(jnp.concatenate(rows, axis=lead), mult, lead), offs


def _unpack(buf, off, shape, lead):
    n = math.prod(shape)
    r = _rows_of(n)
    piece = lax.slice_in_dim(buf, off, off + r, axis=lead)
    piece = piece.reshape(buf.shape[:lead] + (r * PACK_W,))
    piece = lax.slice_in_dim(piece, 0, n, axis=lead)
    return piece.reshape(buf.shape[:lead] + tuple(shape))


def _head_pad(w, per_head, keep):
    k = w.shape[-1]
    w = w.reshape(HEADS, per_head, k)[:, keep[0]:keep[1]]
    w = jnp.pad(w, ((0, 0), (0, HEAD_PAD - (keep[1] - keep[0])), (0, 0)))
    return w.reshape(HEADS * HEAD_PAD, k)


def _head_unpad(w, n):
    return w.reshape(HEADS, HEAD_PAD, w.shape[-1])[:, :n]


def kernel(x, positions, ffn_pre_norm, ffn_pre_w_gate, ffn_pre_w_up, ffn_pre_w_down, mix_norm, ffn_post_norm, ffn_post_w_gate, ffn_post_w_up, ffn_post_w_down, even_w_in, q_norm, w_uq, kv_norm, w_ukv, sg_norm, sg_w, sg_b, even_w_out, conv_w_in, conv_w, conv_w_out, final_norm, loss_target, m_ffn_pre_norm, m_ffn_pre_w_gate, m_ffn_pre_w_up, m_ffn_pre_w_down, m_mix_norm, m_ffn_post_norm, m_ffn_post_w_gate, m_ffn_post_w_up, m_ffn_post_w_down, m_even_w_in, m_q_norm, m_w_uq, m_kv_norm, m_w_ukv, m_sg_norm, m_sg_w, m_sg_b, m_even_w_out, m_conv_w_in, m_conv_w, m_conv_w_out, m_final_norm, v_ffn_pre_norm, v_ffn_pre_w_gate, v_ffn_pre_w_up, v_ffn_pre_w_down, v_mix_norm, v_ffn_post_norm, v_ffn_post_w_gate, v_ffn_post_w_up, v_ffn_post_w_down, v_even_w_in, v_q_norm, v_w_uq, v_kv_norm, v_w_ukv, v_sg_norm, v_sg_w, v_sg_b, v_even_w_out, v_conv_w_in, v_conv_w, v_conv_w_out, v_final_norm):
    given = dict(locals())
    w_loc = {n: given[n] for n in WEIGHTS}
    m_loc = {n: given["m_" + n] for n in WEIGHTS}
    v_loc = {n: given["v_" + n] for n in WEIGHTS}

    S, D = x.shape[1], x.shape[2]
    depth = ffn_pre_norm.shape[0]
    QL, KVL = q_norm.shape[1], kv_norm.shape[1]
    ZW = 2 * SG_WIDTH
    assert x.shape[0] == 1 and ZW % KVL == 0 and (ZW + KVL) % HEAD_PAD == 0 and (ZW + KVL + 2 * HEAD_PAD) % QL == 0
    col_ckv = ZW // KVL
    col_kr = (ZW + KVL) // HEAD_PAD
    col_cq = (ZW + KVL + 2 * HEAD_PAD) // QL

    t_loc = {n: _to_t(n, w_loc[n]) for n in SHARDED}
    full = {n: {} for n in SHARDED}

    def ffn_keys(kind, l):
        return [("ffn_%s_w_%s" % (kind, part), l) for part in ("gate", "up", "down")]

    def mixer_keys(l):
        names = ("even_w_in", "w_uq", "w_ukv", "even_w_out") if l % 2 == 0 else ("conv_w_in", "conv_w", "conv_w_out")
        return [(n, l // 2) for n in names]

    def local_pack(keys):
        return _pack([t_loc[n][l].astype(BF16) for n, l in keys], 0, 16, piece_mult=16)

    def take_gathered(gathered, keys, offs):
        for (n, l), off in zip(keys, offs):
            piece = _unpack(gathered, off, t_loc[n].shape[1:], 1)
            full[n][l] = piece.reshape(N_DEV * piece.shape[1], piece.shape[2])

    def gather_rider(keys):
        pack, offs = local_pack(keys)
        return _Exchange("gather", [pack]), offs

    first_keys = ffn_keys("pre", 0)
    pack0, offs0 = local_pack(first_keys)
    take_gathered(_all_gather(pack0, "gather_weights"), first_keys, offs0)

    tril = jnp.tril(jnp.ones((SG_CHUNK, SG_CHUNK), F32))
    even_ops = {}

    def even_operands(e):
        if e not in even_ops:
            wi = full["even_w_in"][e]
            zrow = lambda k: jnp.zeros((k, D), BF16)
            ops = dict(win_pad=jnp.concatenate(
                [wi[QL + KVL + ROPE:], wi[QL:QL + KVL], wi[QL + KVL:QL + KVL + ROPE], zrow(HEAD_PAD - ROPE),
                 zrow(HEAD_PAD), wi[:QL]], axis=0))
            ops["wq_big"] = _head_pad(full["w_uq"][e], QK_DIM, (0, QK_DIM))
            wkv = full["w_ukv"][e]
            ops["wkv_big"] = jnp.concatenate([_head_pad(wkv, NOPE + VDIM, (0, NOPE)),
                                              _head_pad(wkv, NOPE + VDIM, (NOPE, NOPE + VDIM))], axis=0)
            wo = full["even_w_out"][e]
            ops["wo_attn"] = _head_pad(wo[:HEADS * VDIM], VDIM, (0, VDIM))
            ops["wo_sg"] = wo[HEADS * VDIM:]
            wt = sg_w[e] * tril
            ops["wstack"] = wt.reshape(SG_GROUPS * SG_CHUNK, SG_CHUNK).astype(BF16)
            ops["wtstack"] = jnp.swapaxes(wt, 1, 2).reshape(SG_GROUPS * SG_CHUNK, SG_CHUNK).astype(BF16)
            ops["bmat"] = jnp.repeat(sg_b[e].T, SG_GROUP_DIM, axis=1)
            even_ops[e] = ops
        return even_ops[e]

    gsum = (jnp.arange(SG_WIDTH)[:, None] // SG_GROUP_DIM == jnp.arange(LANE)[None, :]).astype(F32)

    def conv_taps(o):
        return jnp.pad(jnp.swapaxes(full["conv_w"][o], 0, 1).astype(F32), ((0, 8 - CONV_K), (0, 0)))

    inv_freq = ROPE_THETA ** (-jnp.arange(0, ROPE, 2, dtype=F32) / ROPE)
    ang = positions[0].astype(F32)[:, None] * inv_freq
    cos, sin = jnp.cos(ang), jnp.sin(ang)
    ones, zeros = jnp.ones((S, NOPE), F32), jnp.zeros((S, HEAD_PAD - QK_DIM), F32)
    ct = jnp.concatenate([ones, cos, cos, zeros], axis=1)
    st = jnp.concatenate([0.0 * ones, -sin, sin, zeros], axis=1)

    xs = x[0]
    saved = []
    def ffn_forward(xin, kind, l, next_keys):
        gain = (ffn_pre_norm if kind == "pre" else ffn_post_norm)[l]
        wg, wu, wd = (full[n][l] for n, _ in ffn_keys(kind, l))
        if not next_keys:
            return _ffn_fwd(xin, gain, wg, wu, wd, "ffn_fwd")
        rider, offs = gather_rider(next_keys)
        xo, a, b, gathered = _ffn_fwd(xin, gain, wg, wu, wd, "ffn_fwd_gather", rider=rider)
        take_gathered(gathered, next_keys, offs)
        return xo, a, b

    for l in range(depth):
        sv = dict(x0=xs)
        x1, sv["a1"], sv["b1"] = ffn_forward(xs, "pre", l, (mixer_keys(0) if l == 0 else []) + ffn_keys("post", l))
        sv["x1"] = x1
        if l % 2 == 0:
            e = l // 2
            ops = even_operands(e)
            sv["h"], proj = _norm_mm(x1, mix_norm[l], ops["win_pad"], "even_in_proj", F32)
            qn = _rms_fwd(proj, q_norm[e], "q_norm_fwd", col=col_cq, width=QL)
            kvn = _rms_fwd(proj, kv_norm[e], "kv_norm_fwd", col=col_ckv, width=KVL)
            q_big = _mm(qn, ops["wq_big"], "nt", "q_up_proj", out_dtype=F32)
            kv_big = _mm(kvn, ops["wkv_big"], "nt", "kv_up_proj", out_dtype=BF16)
            q_r, k_r, v_r = _rope_fwd(q_big, kv_big, proj, col_kr, ct, st, "rope_fwd")
            o_att, lse = _attn_fwd(q_r, k_r, v_r, "attn_fwd")
            sg = _sgu_fwd(proj, sg_norm[e], ops["wstack"], ops["bmat"], "sgu_fwd")
            tmp = _mm(o_att, ops["wo_attn"], "nn", "even_out_attn", out_dtype=F32, res=x1)
            x2 = _mm(sg, ops["wo_sg"], "nn", "even_out_sg", out_dtype=F32, res=tmp)
            sv.update(proj=proj, qn=qn, kvn=kvn, q=q_r, k=k_r, v=v_r, o=o_att, lse=lse, sg=sg)
        else:
            o = l // 2
            sv["h"], p = _norm_mm(x1, mix_norm[l], full["conv_w_in"][o], "conv_in_proj", BF16)
            cv = _conv_fwd(p, conv_taps(o), "conv_fwd")
            x2 = _mm(cv, full["conv_w_out"][o], "nn", "conv_out_proj", out_dtype=F32, res=x1)
            sv.update(p=p, cv=cv)
        sv["x2"] = x2
        next_keys = ffn_keys("pre", l + 1) + mixer_keys(l + 1) if l + 1 < depth else []
        xs, sv["a2"], sv["b2"] = ffn_forward(x2, "post", l, next_keys)
        saved.append(sv)

    gr = {n: [None] * w_loc[n].shape[0] for n in REPLICATED if n != "final_norm"}
    per_layer = {n: [None] * w_loc[n].shape[0] for n in SHARDED}
    pending = []

    def scatter_rider():
        pieces, where, off = [], [], 0
        for n, l, g in pending:
            piece = _pad_rows(_flat_rows(g.astype(BF16).reshape(N_DEV, -1), 1), 16, 1)
            pieces.append(piece)
            where.append((n, l, off))
            off += piece.shape[1]
        if off % GRAD_ROWS_MULT:
            pieces.append(jnp.zeros((N_DEV, (-off) % GRAD_ROWS_MULT, PACK_W), BF16))
        pending.clear()
        return _Exchange("scatter", pieces), where

    def take_scattered(received, where):
        owned = _sum_slots(received, "sum_grad_shards")
        for n, l, off in where:
            per_layer[n][l] = _unpack(owned, off, t_loc[n].shape[1:], 0)

    def ffn_backward(dxin, xin, kind, l, a, b):
        gain = (ffn_pre_norm if kind == "pre" else ffn_post_norm)[l]
        keys = ffn_keys(kind, l)
        wg, wu, wd = (full[n][l] for n, _ in keys)
        if pending:
            rider, where = scatter_rider()
            dxo, dz, hh, dy, dgain, received = _ffn_bwd(dxin, xin, gain, a, b, wg, wu, wd, "ffn_bwd_scatter", rider=rider)
            take_scattered(received, where)
        else:
            dxo, dz, hh, dy, dgain = _ffn_bwd(dxin, xin, gain, a, b, wg, wu, wd, "ffn_bwd")
        gr["ffn_%s_norm" % kind][l] = dgain[0]
        for (n, _), g in zip(keys, _ffn_dw(a, b, dz, hh, dy, "ffn_dw")):
            pending.append((n, l, g))
        return dxo

    dx, g_final, loss_part = _loss_head(xs, loss_target[0], final_norm, "loss_head")
    for l in reversed(range(depth)):
        sv = saved[l]
        dx = ffn_backward(dx, sv["x2"], "post", l, sv["a2"], sv["b2"])
        h = sv["h"]
        if l % 2 == 0:
            e = l // 2
            ops = even_operands(e)
            d_o = _mm(dx, ops["wo_attn"], "nt", "even_out_attn_bwd", out_dtype=BF16)
            d_sg = _mm(dx, ops["wo_sg"], "nt", "even_out_sg_bwd", out_dtype=BF16)
            g_wo_attn = _mm(sv["o"], dx, "tn", "even_out_attn_dw", out_dtype=F32)
            g_wo_sg = _mm(sv["sg"], dx, "tn", "even_out_sg_dw", out_dtype=F32)
            dq, dk, dv = _attn_bwd(sv["q"], sv["k"], sv["v"], sv["o"], d_o, sv["lse"], "attn_bwd")
            dq_big, dkv_big, dkr = _rope_bwd(dq, dk, dv, ct, st, "rope_bwd")
            dz_sg, g_wstack, g_bias, g_sgn = _sgu_bwd(d_sg, sv["proj"], sg_norm[e], ops["wstack"], ops["wtstack"],
                                                      ops["bmat"], gsum, "sgu_bwd")
            dqn = _mm(dq_big, ops["wq_big"], "nn", "q_up_proj_bwd", out_dtype=F32)
            g_wq_big = _mm(dq_big, sv["qn"], "tn", "q_up_proj_dw", out_dtype=F32)
            dkvn = _mm(dkv_big, ops["wkv_big"], "nn", "kv_up_proj_bwd", out_dtype=F32)
            g_wkv_big = _mm(dkv_big, sv["kvn"], "tn", "kv_up_proj_dw", out_dtype=F32)
            dcq, g_qn = _rms_bwd(dqn, sv["proj"], q_norm[e], "q_norm_bwd", col=col_cq, out_dtype=BF16)
            dckv, g_kvn = _rms_bwd(dkvn, sv["proj"], kv_norm[e], "kv_norm_bwd", col=col_ckv, out_dtype=BF16)
            dproj = jnp.concatenate([dz_sg, dckv, dkr, jnp.zeros((S, HEAD_PAD), BF16), dcq], axis=1)
            dx, dgain = _mm_norm_bwd(dproj, ops["win_pad"], sv["x1"], mix_norm[l], dx, "even_in_proj_bwd")
            g_win = _mm(dproj, h, "tn", "even_in_proj_dw", out_dtype=F32)
            o_cq, o_ckv, o_kr = col_cq * QL, col_ckv * KVL, col_kr * HEAD_PAD
            hw = HEADS * HEAD_PAD
            pending.append(("even_w_in", e, jnp.concatenate(
                [g_win[o_cq:o_cq + QL], g_win[o_ckv:o_ckv + KVL], g_win[o_kr:o_kr + ROPE], g_win[:ZW]], axis=0)))
            pending.append(("w_uq", e, _head_unpad(g_wq_big, QK_DIM).reshape(HEADS * QK_DIM, QL)))
            pending.append(("w_ukv", e, jnp.concatenate(
                [_head_unpad(g_wkv_big[:hw], NOPE), _head_unpad(g_wkv_big[hw:], VDIM)],
                axis=1).reshape(HEADS * (NOPE + VDIM), KVL)))
            pending.append(("even_w_out", e, jnp.concatenate(
                [_head_unpad(g_wo_attn, VDIM).reshape(HEADS * VDIM, D), g_wo_sg], axis=0)))
            gr["q_norm"][e], gr["kv_norm"][e], gr["sg_norm"][e] = g_qn[0], g_kvn[0], g_sgn[0]
            gr["sg_w"][e] = g_wstack.reshape(SG_GROUPS, SG_CHUNK, SG_CHUNK) * tril
            gr["sg_b"][e] = g_bias[:, :SG_GROUPS].T
        else:
            o = l // 2
            dcv = _mm(dx, full["conv_w_out"][o], "nt", "conv_out_proj_bwd", out_dtype=BF16)
            pending.append(("conv_w_out", o, _mm(sv["cv"], dx, "tn", "conv_out_proj_dw", out_dtype=BF16)))
            dp, dcw = _conv_bwd(dcv, sv["p"], conv_taps(o), "conv_bwd")
            dx, dgain = _mm_norm_bwd(dp, full["conv_w_in"][o], sv["x1"], mix_norm[l], dx, "conv_in_proj_bwd")
            pending.append(("conv_w_in", o, _mm(dp, h, "tn", "conv_in_proj_dw", out_dtype=BF16)))
            pending.append(("conv_w", o, jnp.swapaxes(dcw[:CONV_K], 0, 1)))
        gr["mix_norm"][l] = dgain[0]
        dx = ffn_backward(dx, sv["x0"], "pre", l, sv["a1"], sv["b1"])
    grad_x = dx[None]

    rider, where = scatter_rider()
    take_scattered(_exchange("scatter", rider.arrays, "scatter_grads"), where)
    grads = {n: _to_t(n, jnp.stack(per_layer[n])) for n in SHARDED}

    small = [jnp.stack(gr[n]) for n in REPLICATED if n != "final_norm"] + [g_final[0], loss_part[0, :1]]
    spack, soffs = _pack(small, 0, SMALL_ROWS_MULT)
    sgath = _all_gather(spack, "gather_small_grads")
    ssum = _sum_slots(sgath, "sum_small_grads")
    names_small = [n for n in REPLICATED if n != "final_norm"] + ["final_norm", "loss"]
    for n, off, piece in zip(names_small, soffs, small):
        val = _unpack(ssum, off, piece.shape, 0)
        if n == "loss":
            loss = val[0]
        else:
            grads[n] = val

    delta, new_m, new_v = {}, {}, {}
    for n in SHARDED:
        two_d = lambda a: a.reshape(-1, a.shape[-1])
        d, nm, nv = _adamw(two_d(w_loc[n]), two_d(grads[n]), two_d(m_loc[n]), two_d(v_loc[n]), "adamw")
        delta[n], new_m[n], new_v[n] = (a.reshape(w_loc[n].shape) for a in (d, nm, nv))
    flat = lambda d: _pack([d[n] for n in REPLICATED], 0, SMALL_ROWS_MULT)
    (wf, aoffs), (gf, _), (mf, _), (vf, _) = flat(w_loc), flat(grads), flat(m_loc), flat(v_loc)
    for res, buf in zip((delta, new_m, new_v), _adamw(wf, gf, mf, vf, "adamw_replicated")):
        for n, off in zip(REPLICATED, aoffs):
            res[n] = _unpack(buf, off, w_loc[n].shape, 0)
    outs = [loss, grad_x] + [grads[n] for n in WEIGHTS]
    for res in (delta, new_m, new_v):
        outs += [res[n] for n in WEIGHTS]
    return tuple(outs)
```

```python
import math

import numpy as np
import jax
import jax.numpy as jnp
from jax import lax
from jax.experimental import pallas as pl
from jax.experimental.pallas import tpu as pltpu

F32 = jnp.float32
BF16 = jnp.bfloat16

N_DEV = 8
NORM_EPS = 1e-6
HEADS = 8
NOPE = 64
ROPE = 32
VDIM = 64
HEAD_PAD = 128
QK_DIM = NOPE + ROPE
ROPE_THETA = 10000.0
SG_GROUPS = 8
SG_GROUP_DIM = 64
SG_WIDTH = SG_GROUPS * SG_GROUP_DIM
SG_CHUNK = 128
CONV_K = 3
ADAM_LR, ADAM_B1, ADAM_B2, ADAM_EPS, ADAM_WD, ADAM_STEP = 0.001, 0.9, 0.999, 1e-08, 0.01, 10

LANE = 128
PACK_W = 1024
GRAD_ROWS_MULT = 512
SMALL_ROWS_MULT = 64
VMEM_LIMIT = 60 * 1024 * 1024

TILES = dict(ffn_fwd=512, ffn_bwd=256, ffn_dw=2048, mm=1024, mm_norm_bwd=512, mm_tn=2048, ew=1024,
             attn_fwd=2048, attn_bwd=1024, sgu=1024, adam=512)

NT = (((1,), (1,)), ((), ()))
NN = (((1,), (0,)), ((), ()))
TN = (((0,), (0,)), ((), ()))


def _dot(a, b, dims):
    return lax.dot_general(a, b, dims, preferred_element_type=F32)


def _cparams(*sem):
    return pltpu.CompilerParams(dimension_semantics=sem if sem else None, vmem_limit_bytes=VMEM_LIMIT)


def _tile(n, want):
    t = min(want, n)
    while n % t:
        t //= 2
    return t if t % 8 == 0 else n


def _lane_tile(n, cap):
    best = None
    for k in range(1, n // LANE + 1):
        t = k * LANE
        if n % t == 0 and t <= cap:
            best = t
    return best or n


def _row(v):
    return v.reshape(1, -1).astype(F32)


def _all_gather(block, name):
    R, W = block.shape

    def body(x_ref, out_ref, send_sems, recv_sems, local_sem):
        x, y, c = lax.axis_index("x"), lax.axis_index("y"), lax.axis_index("c")
        me, sibling = (x, y, c), (x, y, 1 - c)
        chips = [(1 - x, y), (x, 1 - y), (1 - x, 1 - y)]

        def slot(px, py, pc):
            return out_ref.at[4 * px + 2 * py + pc]

        def copy(k, blk, to, src=None):
            return pltpu.make_async_remote_copy(
                src_ref=slot(*blk) if src is None else src, dst_ref=slot(*blk),
                send_sem=send_sems.at[k], recv_sem=recv_sems.at[k],
                device_id=to, device_id_type=pl.DeviceIdType.MESH)

        mine = pltpu.make_async_copy(x_ref, slot(*me), local_sem)
        mine.start()
        first = [copy(0, me, sibling, src=x_ref)]
        first += [copy(1 + j, me, (*chip, c), src=x_ref) for j, chip in enumerate(chips)]
        for cp in first:
            cp.start()
        passed = [copy(4 + j, (*chip, c), sibling) for j, chip in enumerate(chips)]
        for j, chip in enumerate(chips):
            copy(1 + j, (*chip, c), me).wait_recv()
            passed[j].start()
        copy(0, sibling, me).wait_recv()
        for j, chip in enumerate(chips):
            copy(4 + j, (*chip, 1 - c), me).wait_recv()
        for cp in first + passed:
            cp.wait_send()
        mine.wait()

    return pl.pallas_call(
        body, name=name,
        out_shape=jax.ShapeDtypeStruct((N_DEV, R, W), block.dtype),
        in_specs=[pl.BlockSpec(memory_space=pl.ANY)],
        out_specs=pl.BlockSpec(memory_space=pl.ANY),
        scratch_shapes=[pltpu.SemaphoreType.DMA((7,)), pltpu.SemaphoreType.DMA((7,)), pltpu.SemaphoreType.DMA],
    )(block)


class _Exchange:
    def __init__(self, kind, arrays):
        self.kind, self.arrays = kind, list(arrays)
        if kind == "gather":
            (r, w), = [a.shape for a in self.arrays]
            self.rows = [r]
        else:
            self.rows = [a.shape[1] for a in self.arrays]
            w = self.arrays[0].shape[2]
        self.offs = [sum(self.rows[:i]) for i in range(len(self.rows))]
        self.n_in = len(self.arrays)
        self.out_shape = jax.ShapeDtypeStruct((N_DEV, sum(self.rows), w), self.arrays[0].dtype)
        self.in_specs = [pl.BlockSpec(memory_space=pl.ANY)] * self.n_in
        self.out_spec = pl.BlockSpec(memory_space=pl.ANY)
        self.out_specs, self.out_shapes = [self.out_spec], [self.out_shape]
        self.scratch = [pltpu.SemaphoreType.DMA((7,)), pltpu.SemaphoreType.DMA((7,)), pltpu.SemaphoreType.DMA]

    def _peers(self):
        x, y, c = lax.axis_index("x"), lax.axis_index("y"), lax.axis_index("c")
        me = 4 * x + 2 * y + c
        return me, [(k, (x ^ (k >> 2), y ^ ((k >> 1) & 1), c ^ (k & 1))) for k in range(1, N_DEV)]

    @staticmethod
    def _remote(src, dst, k, to, send_sems, recv_sems):
        return pltpu.make_async_remote_copy(
            src_ref=src, dst_ref=dst, send_sem=send_sems.at[k - 1], recv_sem=recv_sems.at[k - 1],
            device_id=to, device_id_type=pl.DeviceIdType.MESH)

    def start(self, s_refs, r_ref, send_sems, recv_sems, local_sem):
        me, peers = self._peers()
        for s_ref, off, r in zip(s_refs, self.offs, self.rows):
            src = s_ref if self.kind == "gather" else s_ref.at[me]
            pltpu.make_async_copy(src, r_ref.at[me, pl.ds(off, r)], local_sem).start()
        for k, to in peers:
            peer = 4 * to[0] + 2 * to[1] + to[2]
            for s_ref, off, r in zip(s_refs, self.offs, self.rows):
                src = s_ref if self.kind == "gather" else s_ref.at[peer]
                self._remote(src, r_ref.at[me, pl.ds(off, r)], k, to, send_sems, recv_sems).start()

    def wait(self, s_refs, r_ref, send_sems, recv_sems, local_sem):
        me, peers = self._peers()
        whole = r_ref.at[me]
        totals = [self._remote(whole, whole, k, to, send_sems, recv_sems) for k, to in peers]
        for cp in totals:
            cp.wait_recv()
        for cp in totals:
            cp.wait_send()
        pltpu.make_async_copy(whole, whole, local_sem).wait()


class _NoRider:
    arrays, in_specs, out_specs, out_shapes, scratch = [], [], [], [], []


_NO_RIDER = _NoRider()


def _ride(rider, refs, n_in, n_out, first, last):
    if rider is None:
        return refs, lambda: None
    k = rider.n_in
    s_refs = refs[n_in:n_in + k]
    r_ref = refs[n_in + k + n_out]
    sems = refs[-3:]
    own = refs[:n_in] + refs[n_in + k:n_in + k + n_out] + refs[n_in + k + n_out + 1:-3]

    @pl.when(first)
    def _():
        rider.start(s_refs, r_ref, *sems)

    def finish():
        @pl.when(last)
        def _():
            rider.wait(s_refs, r_ref, *sems)

    return own, finish


def _exchange(kind, arrays, name):
    ex = _Exchange(kind, arrays)

    def body(*refs):
        s_refs, r_ref, sems = refs[:ex.n_in], refs[ex.n_in], refs[ex.n_in + 1:]
        ex.start(s_refs, r_ref, *sems)
        ex.wait(s_refs, r_ref, *sems)

    return pl.pallas_call(
        body, name=name, out_shape=ex.out_shape, in_specs=ex.in_specs, out_specs=ex.out_spec,
        scratch_shapes=ex.scratch,
    )(*ex.arrays)


def _sum_slots(parts, name):
    _, R, W = parts.shape
    tr = _tile(R, TILES["adam"])

    def body(p_ref, o_ref):
        acc = p_ref[0].astype(F32)
        for s in range(1, N_DEV):
            acc = acc + p_ref[s].astype(F32)
        o_ref[...] = acc

    return pl.pallas_call(
        body, name=name, grid=(R // tr,),
        in_specs=[pl.BlockSpec((N_DEV, tr, W), lambda i: (0, i, 0))],
        out_specs=pl.BlockSpec((tr, W), lambda i: (i, 0)),
        out_shape=jax.ShapeDtypeStruct((R, W), F32),
        compiler_params=_cparams("parallel"),
    )(parts)


def _adamw(w, g, m, v, name):
    R, W = w.shape
    tr = _tile(R, TILES["adam"])
    c1 = 1.0 - ADAM_B1 ** ADAM_STEP
    c2 = 1.0 - ADAM_B2 ** ADAM_STEP

    def body(w_ref, g_ref, m_ref, v_ref, d_ref, nm_ref, nv_ref):
        g = g_ref[...]
        nm = ADAM_B1 * m_ref[...] + (1.0 - ADAM_B1) * g
        nv = ADAM_B2 * v_ref[...] + (1.0 - ADAM_B2) * (g * g)
        d_ref[...] = -ADAM_LR * ((nm / c1) / (jnp.sqrt(nv / c2) + ADAM_EPS) + ADAM_WD * w_ref[...])
        nm_ref[...] = nm
        nv_ref[...] = nv

    spec = pl.BlockSpec((tr, W), lambda i: (i, 0))
    return pl.pallas_call(
        body, name=name, grid=(R // tr,),
        in_specs=[spec] * 4, out_specs=[spec] * 3,
        out_shape=[jax.ShapeDtypeStruct((R, W), F32)] * 3,
        compiler_params=_cparams("parallel"),
    )(w, g, m, v)


def _mm(a, b, mode, name, out_dtype=BF16, res=None, scale=1.0, acol=None, kdim=None):
    if mode == "tn":
        S, M = a.shape
        N = b.shape[1]
        ts = _tile(S, TILES["mm_tn"])
        tmo = _lane_tile(M, 1024)

        def body(a_ref, b_ref, o_ref, acc):
            s = pl.program_id(1)

            @pl.when(s == 0)
            def _():
                acc[...] = jnp.zeros_like(acc)

            acc[...] += _dot(a_ref[...].astype(BF16), b_ref[...].astype(BF16), TN)

            @pl.when(s == pl.num_programs(1) - 1)
            def _():
                o_ref[...] = acc[...].astype(out_dtype)

        return pl.pallas_call(
            body, name=name, grid=(M // tmo, S // ts),
            in_specs=[pl.BlockSpec((ts, tmo), lambda i, s: (s, i)), pl.BlockSpec((ts, N), lambda i, s: (s, 0))],
            out_specs=pl.BlockSpec((tmo, N), lambda i, s: (i, 0)),
            out_shape=jax.ShapeDtypeStruct((M, N), out_dtype),
            scratch_shapes=[pltpu.VMEM((tmo, N), F32)],
            compiler_params=_cparams("parallel", "arbitrary"),
        )(a, b)

    M = a.shape[0]
    K = kdim if kdim is not None else a.shape[1]
    ac = 0 if acol is None else acol
    N = b.shape[1] if mode == "nn" else b.shape[0]
    tm = _tile(M, TILES["mm"])
    dims = NN if mode == "nn" else NT

    def body(*refs):
        if res is None:
            a_ref, b_ref, o_ref = refs
        else:
            a_ref, b_ref, r_ref, o_ref = refs
        acc = _dot(a_ref[...].astype(BF16), b_ref[...].astype(BF16), dims)
        if res is not None:
            acc = r_ref[...] + scale * acc
        o_ref[...] = acc.astype(out_dtype)

    in_specs = [pl.BlockSpec((tm, K), lambda i: (i, ac)), pl.BlockSpec(b.shape, lambda i: (0, 0))]
    args = [a, b]
    if res is not None:
        in_specs.append(pl.BlockSpec((tm, N), lambda i: (i, 0)))
        args.append(res)
    return pl.pallas_call(
        body, name=name, grid=(M // tm,),
        in_specs=in_specs, out_specs=pl.BlockSpec((tm, N), lambda i: (i, 0)),
        out_shape=jax.ShapeDtypeStruct((M, N), out_dtype),
        compiler_params=_cparams("parallel"),
    )(*args)


def _norm_mm(x, gain, w_t, name, out_dtype, col=0):
    S = x.shape[0]
    N, D = w_t.shape
    tm = _tile(S, TILES["mm"])

    def body(x_ref, g_ref, w_ref, h_ref, o_ref):
        xv = x_ref[...].astype(F32)
        r = lax.rsqrt(jnp.mean(xv * xv, axis=-1, keepdims=True) + NORM_EPS)
        h = (xv * r * g_ref[...]).astype(BF16)
        h_ref[...] = h
        o_ref[...] = _dot(h, w_ref[...], NT).astype(out_dtype)

    return pl.pallas_call(
        body, name=name, grid=(S // tm,),
        in_specs=[pl.BlockSpec((tm, D), lambda i: (i, col)), pl.BlockSpec((1, D), lambda i: (0, 0)),
                  pl.BlockSpec((N, D), lambda i: (0, 0))],
        out_specs=[pl.BlockSpec((tm, D), lambda i: (i, 0)), pl.BlockSpec((tm, N), lambda i: (i, 0))],
        out_shape=[jax.ShapeDtypeStruct((S, D), BF16), jax.ShapeDtypeStruct((S, N), out_dtype)],
        compiler_params=_cparams("parallel"),
    )(x, _row(gain), w_t)


def _mm_norm_bwd(a, w, x, gain, res, name, col=0, out_dtype=F32):
    S, K = a.shape
    D = w.shape[1]
    tm = _tile(S, TILES["mm_norm_bwd"])

    def body(*refs):
        if res is None:
            a_ref, w_ref, x_ref, g_ref, dx_ref, dg_ref = refs
        else:
            a_ref, w_ref, x_ref, g_ref, r_ref, dx_ref, dg_ref = refs

        @pl.when(pl.program_id(0) == 0)
        def _():
            dg_ref[...] = jnp.zeros_like(dg_ref)

        dh = _dot(a_ref[...].astype(BF16), w_ref[...], NN)
        xv = x_ref[...].astype(F32)
        r = lax.rsqrt(jnp.mean(xv * xv, axis=-1, keepdims=True) + NORM_EPS)
        xhat = xv * r
        dg_ref[...] += jnp.sum(dh * xhat, axis=0, keepdims=True)
        dxhat = dh * g_ref[...]
        dx = r * (dxhat - xhat * jnp.mean(dxhat * xhat, axis=-1, keepdims=True))
        if res is not None:
            dx = dx + r_ref[...]
        dx_ref[...] = dx.astype(out_dtype)

    row = pl.BlockSpec((tm, D), lambda i: (i, 0))
    in_specs = [pl.BlockSpec((tm, K), lambda i: (i, 0)), pl.BlockSpec((K, D), lambda i: (0, 0)),
                pl.BlockSpec((tm, D), lambda i: (i, col)), pl.BlockSpec((1, D), lambda i: (0, 0))]
    args = [a, w, x, _row(gain)]
    if res is not None:
        in_specs.append(row)
        args.append(res)
    return pl.pallas_call(
        body, name=name, grid=(S // tm,),
        in_specs=in_specs, out_specs=[row, pl.BlockSpec((1, D), lambda i: (0, 0))],
        out_shape=[jax.ShapeDtypeStruct((S, D), out_dtype), jax.ShapeDtypeStruct((1, D), F32)],
        compiler_params=_cparams("arbitrary"),
    )(*args)


def _silu_parts(a):
    s = jax.nn.sigmoid(a)
    return a * s, s * (1.0 + a * (1.0 - s))


def _ffn_fwd(x, gain, wg_t, wu_t, wd, name, rider=None):
    S, D = x.shape
    Fd = wd.shape[0]
    tm = _tile(S, TILES["ffn_fwd"])
    fc = _lane_tile(Fd, 512)

    def body(*refs):
        i = pl.program_id(0)
        own, finish = _ride(rider, refs, 5, 3, i == 0, i == pl.num_programs(0) - 1)
        x_ref, g_ref, wg_ref, wu_ref, wd_ref, o_ref, a_ref, b_ref = own
        xv = x_ref[...]
        r = lax.rsqrt(jnp.mean(xv * xv, axis=-1, keepdims=True) + NORM_EPS)
        h = (xv * r * g_ref[...]).astype(BF16)
        acc = jnp.zeros((tm, D), F32)
        for c in range(Fd // fc):
            sl = slice(c * fc, (c + 1) * fc)
            a = _dot(h, wg_ref[sl, :], NT)
            b = _dot(h, wu_ref[sl, :], NT)
            a_ref[:, sl] = a.astype(BF16)
            b_ref[:, sl] = b.astype(BF16)
            z = (a * jax.nn.sigmoid(a) * b).astype(BF16)
            acc = acc + _dot(z, wd_ref[sl, :], NN)
        o_ref[...] = xv + 0.5 * acc
        finish()

    wspec = pl.BlockSpec((Fd, D), lambda i: (0, 0), pipeline_mode=pl.Buffered(1))
    extra = rider or _NO_RIDER
    return pl.pallas_call(
        body, name=name, grid=(S // tm,),
        in_specs=[pl.BlockSpec((tm, D), lambda i: (i, 0)), pl.BlockSpec((1, D), lambda i: (0, 0)), wspec, wspec,
                  wspec] + extra.in_specs,
        out_specs=[pl.BlockSpec((tm, D), lambda i: (i, 0)), pl.BlockSpec((tm, Fd), lambda i: (i, 0)),
                   pl.BlockSpec((tm, Fd), lambda i: (i, 0))] + extra.out_specs,
        out_shape=[jax.ShapeDtypeStruct((S, D), F32), jax.ShapeDtypeStruct((S, Fd), BF16),
                   jax.ShapeDtypeStruct((S, Fd), BF16)] + extra.out_shapes,
        scratch_shapes=extra.scratch,
        compiler_params=_cparams("arbitrary" if rider else "parallel"),
    )(x, _row(gain), wg_t, wu_t, wd, *extra.arrays)


def _ffn_bwd(g, x, gain, a, b, wg_t, wu_t, wd, name, rider=None):
    S, D = x.shape
    Fd = wd.shape[0]
    tm = _tile(S, TILES["ffn_bwd"])
    fc = Fd

    def body(*refs):
        i = pl.program_id(0)
        own, finish = _ride(rider, refs, 8, 5, i == 0, i == pl.num_programs(0) - 1)
        g_ref, x_ref, gain_ref, a_ref, b_ref, wg_ref, wu_ref, wd_ref, dx_ref, dz_ref, h_ref, dy_ref, dg_ref = own

        @pl.when(i == 0)
        def _():
            dg_ref[...] = jnp.zeros_like(dg_ref)

        gv = g_ref[...]
        xv = x_ref[...]
        r = lax.rsqrt(jnp.mean(xv * xv, axis=-1, keepdims=True) + NORM_EPS)
        xhat = xv * r
        h_ref[...] = (xhat * gain_ref[...]).astype(BF16)
        dy = (0.5 * gv).astype(BF16)
        dy_ref[...] = dy
        dh = jnp.zeros((tm, D), F32)
        for c in range(Fd // fc):
            sl = slice(c * fc, (c + 1) * fc)
            av = a_ref[:, sl].astype(F32)
            bv = b_ref[:, sl].astype(F32)
            dz = _dot(dy, wd_ref[sl, :], NT).astype(BF16)
            dz_ref[:, sl] = dz
            dzf = dz.astype(F32)
            silu, dsilu = _silu_parts(av)
            da = (dzf * bv * dsilu).astype(BF16)
            db = (dzf * silu).astype(BF16)
            dh = dh + _dot(da, wg_ref[sl, :], NN) + _dot(db, wu_ref[sl, :], NN)
        dg_ref[...] += jnp.sum(dh * xhat, axis=0, keepdims=True)
        dxhat = dh * gain_ref[...]
        dx_ref[...] = gv + r * (dxhat - xhat * jnp.mean(dxhat * xhat, axis=-1, keepdims=True))
        finish()

    wspec = pl.BlockSpec((Fd, D), lambda i: (0, 0), pipeline_mode=pl.Buffered(1))
    row = pl.BlockSpec((tm, D), lambda i: (i, 0))
    wide = pl.BlockSpec((tm, Fd), lambda i: (i, 0))
    extra = rider or _NO_RIDER
    return pl.pallas_call(
        body, name=name, grid=(S // tm,),
        in_specs=[row, row, pl.BlockSpec((1, D), lambda i: (0, 0)), wide, wide, wspec, wspec, wspec] + extra.in_specs,
        out_specs=[row, wide, row, row, pl.BlockSpec((1, D), lambda i: (0, 0))] + extra.out_specs,
        out_shape=[jax.ShapeDtypeStruct((S, D), F32), jax.ShapeDtypeStruct((S, Fd), BF16),
                   jax.ShapeDtypeStruct((S, D), BF16), jax.ShapeDtypeStruct((S, D), BF16),
                   jax.ShapeDtypeStruct((1, D), F32)] + extra.out_shapes,
        scratch_shapes=extra.scratch,
        compiler_params=_cparams("arbitrary"),
    )(g, x, _row(gain), a, b, wg_t, wu_t, wd, *extra.arrays)


def _ffn_dw(a, b, dz, h, dy, name):
    S, Fd = a.shape
    D = h.shape[1]
    ts = _tile(S, TILES["ffn_dw"])
    tf = _lane_tile(Fd, 256)

    def body(a_ref, b_ref, dz_ref, h_ref, dy_ref, og_ref, ou_ref, od_ref, accg, accu, accd):
        s = pl.program_id(1)

        @pl.when(s == 0)
        def _():
            accg[...] = jnp.zeros_like(accg)
            accu[...] = jnp.zeros_like(accu)
            accd[...] = jnp.zeros_like(accd)

        av = a_ref[...].astype(F32)
        bv = b_ref[...].astype(F32)
        dzf = dz_ref[...].astype(F32)
        silu, dsilu = _silu_parts(av)
        da = (dzf * bv * dsilu).astype(BF16)
        db = (dzf * silu).astype(BF16)
        z = (silu * bv).astype(BF16)
        hv = h_ref[...]
        accg[...] += _dot(da, hv, TN)
        accu[...] += _dot(db, hv, TN)
        accd[...] += _dot(z, dy_ref[...], TN)

        @pl.when(s == pl.num_programs(1) - 1)
        def _():
            og_ref[...] = accg[...].astype(BF16)
            ou_ref[...] = accu[...].astype(BF16)
            od_ref[...] = accd[...].astype(BF16)

    wide = pl.BlockSpec((ts, tf), lambda f, s: (s, f))
    row = pl.BlockSpec((ts, D), lambda f, s: (s, 0))
    out = pl.BlockSpec((tf, D), lambda f, s: (f, 0))
    return pl.pallas_call(
        body, name=name, grid=(Fd // tf, S // ts),
        in_specs=[wide, wide, wide, row, row], out_specs=[out, out, out],
        out_shape=[jax.ShapeDtypeStruct((Fd, D), BF16)] * 3,
        scratch_shapes=[pltpu.VMEM((tf, D), F32)] * 3,
        compiler_params=_cparams("parallel", "arbitrary"),
    )(a, b, dz, h, dy)


def _loss_head(x, target, gain, name):
    S, D = x.shape
    tm = _tile(S, TILES["ew"])

    def body(x_ref, t_ref, g_ref, dx_ref, dg_ref, loss_ref):
        @pl.when(pl.program_id(0) == 0)
        def _():
            dg_ref[...] = jnp.zeros_like(dg_ref)
            loss_ref[...] = jnp.zeros_like(loss_ref)

        xv = x_ref[...]
        r = lax.rsqrt(jnp.mean(xv * xv, axis=-1, keepdims=True) + NORM_EPS)
        xhat = xv * r
        e = xhat * g_ref[...] - t_ref[...]
        per_tok = jnp.mean(e * e, axis=-1, keepdims=True)
        loss_ref[...] += jnp.broadcast_to(0.5 * jnp.sum(per_tok, axis=0, keepdims=True), (1, LANE))
        dy = e * (1.0 / D)
        dg_ref[...] += jnp.sum(dy * xhat, axis=0, keepdims=True)
        dxhat = dy * g_ref[...]
        dx_ref[...] = r * (dxhat - xhat * jnp.mean(dxhat * xhat, axis=-1, keepdims=True))

    row = pl.BlockSpec((tm, D), lambda i: (i, 0))
    return pl.pallas_call(
        body, name=name, grid=(S // tm,),
        in_specs=[row, row, pl.BlockSpec((1, D), lambda i: (0, 0))],
        out_specs=[row, pl.BlockSpec((1, D), lambda i: (0, 0)), pl.BlockSpec((1, LANE), lambda i: (0, 0))],
        out_shape=[jax.ShapeDtypeStruct((S, D), F32), jax.ShapeDtypeStruct((1, D), F32),
                   jax.ShapeDtypeStruct((1, LANE), F32)],
        compiler_params=_cparams("arbitrary"),
    )(x, target, _row(gain))


def _shift_down(u, halo, k, rows):
    out = pltpu.roll(u, k, 0)
    for j in range(k):
        out = jnp.where(rows == j, halo[8 - k + j:8 - k + j + 1, :], out)
    return out


def _shift_up(u, halo, k, rows, n):
    out = pltpu.roll(u, n - k, 0)
    for j in range(k):
        out = jnp.where(rows == n - k + j, halo[j:j + 1, :], out)
    return out


def _conv_fwd(p, cw, name):
    S, W3 = p.shape
    W = W3 // 3
    tm = _tile(S, TILES["ew"])
    hb = tm // 8

    def body(p_ref, ph_ref, w_ref, v_ref):
        i = pl.program_id(0)
        bg = p_ref[:, 0:W].astype(F32)
        u = p_ref[:, W:2 * W].astype(F32) * p_ref[:, 2 * W:3 * W].astype(F32)
        uh = ph_ref[:, W:2 * W].astype(F32) * ph_ref[:, 2 * W:3 * W].astype(F32)
        uh = jnp.where(i > 0, uh, 0.0)
        rows = lax.broadcasted_iota(jnp.int32, (tm, 1), 0)
        u1 = _shift_down(u, uh, 1, rows)
        u2 = _shift_down(u, uh, 2, rows)
        y = w_ref[0:1, :] * u2 + w_ref[1:2, :] * u1 + w_ref[2:3, :] * u
        v_ref[...] = (bg * y).astype(BF16)

    return pl.pallas_call(
        body, name=name, grid=(S // tm,),
        in_specs=[pl.BlockSpec((tm, W3), lambda i: (i, 0)),
                  pl.BlockSpec((8, W3), lambda i: (jnp.maximum(i * hb - 1, 0), 0)),
                  pl.BlockSpec((8, W), lambda i: (0, 0))],
        out_specs=pl.BlockSpec((tm, W), lambda i: (i, 0)),
        out_shape=jax.ShapeDtypeStruct((S, W), BF16),
        compiler_params=_cparams("parallel"),
    )(p, p, cw)


def _conv_bwd(dv, p, cw, name):
    S, W3 = p.shape
    W = W3 // 3
    tm = _tile(S, TILES["ew"])
    hb = tm // 8
    last = S // 8 - 1

    def body(dv_ref, dvn_ref, p_ref, pp_ref, pn_ref, w_ref, dp_ref, dw_ref):
        i = pl.program_id(0)
        n = pl.num_programs(0)

        @pl.when(i == 0)
        def _():
            dw_ref[...] = jnp.zeros_like(dw_ref)

        bg = p_ref[:, 0:W].astype(F32)
        cg = p_ref[:, W:2 * W].astype(F32)
        zz = p_ref[:, 2 * W:3 * W].astype(F32)
        u = cg * zz
        uh = pp_ref[:, W:2 * W].astype(F32) * pp_ref[:, 2 * W:3 * W].astype(F32)
        uh = jnp.where(i > 0, uh, 0.0)
        rows = lax.broadcasted_iota(jnp.int32, (tm, 1), 0)
        u1 = _shift_down(u, uh, 1, rows)
        u2 = _shift_down(u, uh, 2, rows)
        w0, w1, w2 = w_ref[0:1, :], w_ref[1:2, :], w_ref[2:3, :]
        y = w0 * u2 + w1 * u1 + w2 * u
        dvv = dv_ref[...].astype(F32)
        dy = dvv * bg
        dyh = dvn_ref[...].astype(F32) * pn_ref[:, 0:W].astype(F32)
        dyh = jnp.where(i < n - 1, dyh, 0.0)
        d1 = _shift_up(dy, dyh, 1, rows, tm)
        d2 = _shift_up(dy, dyh, 2, rows, tm)
        du = w2 * dy + w1 * d1 + w0 * d2
        dp_ref[:, 0:W] = (dvv * y).astype(BF16)
        dp_ref[:, W:2 * W] = (du * zz).astype(BF16)
        dp_ref[:, 2 * W:3 * W] = (du * cg).astype(BF16)
        dw_ref[0:1, :] += jnp.sum(dy * u2, axis=0, keepdims=True)
        dw_ref[1:2, :] += jnp.sum(dy * u1, axis=0, keepdims=True)
        dw_ref[2:3, :] += jnp.sum(dy * u, axis=0, keepdims=True)

    return pl.pallas_call(
        body, name=name, grid=(S // tm,),
        in_specs=[pl.BlockSpec((tm, W), lambda i: (i, 0)),
                  pl.BlockSpec((8, W), lambda i: (jnp.minimum((i + 1) * hb, last), 0)),
                  pl.BlockSpec((tm, W3), lambda i: (i, 0)),
                  pl.BlockSpec((8, W3), lambda i: (jnp.maximum(i * hb - 1, 0), 0)),
                  pl.BlockSpec((8, W3), lambda i: (jnp.minimum((i + 1) * hb, last), 0)),
                  pl.BlockSpec((8, W), lambda i: (0, 0))],
        out_specs=[pl.BlockSpec((tm, W3), lambda i: (i, 0)), pl.BlockSpec((8, W), lambda i: (0, 0))],
        out_shape=[jax.ShapeDtypeStruct((S, W3), BF16), jax.ShapeDtypeStruct((8, W), F32)],
        compiler_params=_cparams("arbitrary"),
    )(dv, dv, p, p, p, cw)


def _rope_swap(r, lane):
    mid = NOPE + ROPE // 2
    first = (lane >= NOPE) & (lane < mid)
    second = (lane >= mid) & (lane < QK_DIM)
    return jnp.where(first, pltpu.roll(r, HEAD_PAD - ROPE // 2, 1), jnp.where(second, pltpu.roll(r, ROPE // 2, 1), 0.0))


def _rope_fwd(q_big, kv_big, proj, kr_col, ct, st, name):
    S = q_big.shape[0]
    HW = HEADS * HEAD_PAD
    tm = _tile(S, TILES["ew"])

    def body(q_ref, k_ref, v_ref, kr_ref, ct_ref, st_ref, qo_ref, ko_ref, vo_ref):
        lane = lax.broadcasted_iota(jnp.int32, (1, HEAD_PAD), 1)
        ctv, stv = ct_ref[...], st_ref[...]
        krr = pltpu.roll(kr_ref[...].astype(F32), NOPE, 1)
        kro = krr * ctv + _rope_swap(krr, lane) * stv
        for h in range(HEADS):
            sl = slice(h * HEAD_PAD, (h + 1) * HEAD_PAD)
            qh = q_ref[:, sl].astype(F32)
            qo_ref[:, sl] = (qh * ctv + _rope_swap(qh, lane) * stv).astype(BF16)
            ko_ref[:, sl] = (k_ref[:, sl].astype(F32) + kro).astype(BF16)
            vo_ref[:, sl] = jnp.where(lane == VDIM, 1.0, v_ref[:, sl].astype(F32)).astype(BF16)

    wide = pl.BlockSpec((tm, HW), lambda i: (i, 0))
    narrow = pl.BlockSpec((tm, HEAD_PAD), lambda i: (i, 0))
    return pl.pallas_call(
        body, name=name, grid=(S // tm,),
        in_specs=[wide, wide, pl.BlockSpec((tm, HW), lambda i: (i, 1)),
                  pl.BlockSpec((tm, HEAD_PAD), lambda i: (i, kr_col)), narrow, narrow],
        out_specs=[wide, wide, wide],
        out_shape=[jax.ShapeDtypeStruct((S, HW), BF16)] * 3,
        compiler_params=_cparams("parallel"),
    )(q_big, kv_big, kv_big, proj, ct, st)


def _rope_bwd(dq, dk, dv, ct, st, name):
    S = dq.shape[0]
    HW = HEADS * HEAD_PAD
    tm = _tile(S, TILES["ew"])

    def body(dq_ref, dk_ref, dv_ref, ct_ref, st_ref, oq_ref, okv_ref, okr_ref):
        lane = lax.broadcasted_iota(jnp.int32, (1, HEAD_PAD), 1)
        ctv, stv = ct_ref[...], st_ref[...]
        acc = jnp.zeros((tm, HEAD_PAD), F32)
        for h in range(HEADS):
            sl = slice(h * HEAD_PAD, (h + 1) * HEAD_PAD)
            d = dq_ref[:, sl].astype(F32)
            oq_ref[:, sl] = (d * ctv + _rope_swap(d * stv, lane)).astype(BF16)
            d = dk_ref[:, sl].astype(F32)
            okv_ref[:, sl] = jnp.where(lane < NOPE, d, 0.0).astype(BF16)
            acc = acc + jnp.where(lane >= NOPE, d * ctv + _rope_swap(d * stv, lane), 0.0)
        okv_ref[:, HW:2 * HW] = dv_ref[...].astype(BF16)
        okr_ref[...] = pltpu.roll(acc, HEAD_PAD - NOPE, 1).astype(BF16)

    wide = pl.BlockSpec((tm, HW), lambda i: (i, 0))
    narrow = pl.BlockSpec((tm, HEAD_PAD), lambda i: (i, 0))
    return pl.pallas_call(
        body, name=name, grid=(S // tm,),
        in_specs=[wide, wide, wide, narrow, narrow],
        out_specs=[wide, pl.BlockSpec((tm, 2 * HW), lambda i: (i, 0)), narrow],
        out_shape=[jax.ShapeDtypeStruct((S, HW), BF16), jax.ShapeDtypeStruct((S, 2 * HW), BF16),
                   jax.ShapeDtypeStruct((S, HEAD_PAD), BF16)],
        compiler_params=_cparams("parallel"),
    )(dq, dk, dv, ct, st)


def _pairs(n, by_key):
    if by_key:
        pr = [(i, j) for j in range(n) for i in range(j, n)]
    else:
        pr = [(i, j) for i in range(n) for j in range(i + 1)]
    qi = np.array([p[0] for p in pr], np.int32)
    kj = np.array([p[1] for p in pr], np.int32)
    return jnp.asarray(qi), jnp.asarray(kj)


_LOG2E = 1.4426950408889634
_LN2 = 0.6931471805599453


def _tile_mask(t):
    return lax.broadcasted_iota(jnp.int32, (t, t), 1) <= lax.broadcasted_iota(jnp.int32, (t, t), 0)


def _attn_fwd(q, k, v, name):
    S = q.shape[0]
    HW = HEADS * HEAD_PAD
    t = _tile(S, TILES["attn_fwd"])
    n = S // t
    qi, kj = _pairs(n, by_key=False)
    c = (QK_DIM ** -0.5) * _LOG2E

    def body(qi_ref, kj_ref, q_ref, k_ref, v_ref, o_ref, lse_ref, m_s, acc_s):
        p_id = pl.program_id(1)
        i, j = qi_ref[p_id], kj_ref[p_id]

        @pl.when(j == 0)
        def _():
            m_s[...] = jnp.full_like(m_s, -jnp.inf)
            acc_s[...] = jnp.zeros_like(acc_s)

        def step(on_diagonal):
            s = _dot(q_ref[...], k_ref[...], NT)
            if on_diagonal:
                s = jnp.where(_tile_mask(t), s, -jnp.inf)
            m_old = m_s[...]
            m_new = jnp.maximum(m_old, jnp.max(s, axis=-1, keepdims=True))
            p = jnp.exp2((s - m_new) * c).astype(BF16)
            acc_s[...] = jnp.exp2((m_old - m_new) * c) * acc_s[...] + _dot(p, v_ref[...], NN)
            m_s[...] = m_new

        @pl.when(i == j)
        def _():
            step(True)

        @pl.when(i != j)
        def _():
            step(False)

        @pl.when(j == i)
        def _():
            acc = acc_s[...]
            l = acc[:, VDIM:VDIM + 1]
            o_ref[...] = (acc * (1.0 / l)).astype(BF16)
            lse_ref[...] = jnp.broadcast_to(m_s[...] * c + jnp.log2(l), (t, HEAD_PAD))

    qspec = pl.BlockSpec((t, HEAD_PAD), lambda h, p, qi, kj: (qi[p], h))
    kspec = pl.BlockSpec((t, HEAD_PAD), lambda h, p, qi, kj: (kj[p], h))
    grid_spec = pltpu.PrefetchScalarGridSpec(
        num_scalar_prefetch=2, grid=(HEADS, int(qi.shape[0])),
        in_specs=[qspec, kspec, kspec], out_specs=[qspec, qspec],
        scratch_shapes=[pltpu.VMEM((t, 1), F32), pltpu.VMEM((t, HEAD_PAD), F32)])
    return pl.pallas_call(
        body, name=name, grid_spec=grid_spec,
        out_shape=[jax.ShapeDtypeStruct((S, HW), BF16), jax.ShapeDtypeStruct((S, HW), F32)],
        compiler_params=_cparams("parallel", "arbitrary"),
    )(qi, kj, q, k, v)


def _attn_bwd(q, k, v, o, do, lse2, name):
    S = q.shape[0]
    HW = HEADS * HEAD_PAD
    t = _tile(S, TILES["attn_bwd"])
    n = S // t
    qi, kj = _pairs(n, by_key=True)
    scale = QK_DIM ** -0.5

    def body(qi_ref, kj_ref, q_ref, k_ref, v_ref, o_ref, do_ref, lse_ref, dq_ref, dk_ref, dv_ref, dk_s, dv_s):
        p_id = pl.program_id(1)
        i, j = qi_ref[p_id], kj_ref[p_id]

        @pl.when(p_id == 0)
        def _():
            dq_ref[...] = jnp.zeros_like(dq_ref)

        @pl.when(i == j)
        def _():
            dk_s[...] = jnp.zeros_like(dk_s)
            dv_s[...] = jnp.zeros_like(dv_s)

        def step(on_diagonal):
            qv, kv, vv = q_ref[...], k_ref[...], v_ref[...]
            dov = do_ref[...]
            p = jnp.exp(_dot(qv, kv, NT) * scale - lse_ref[:, 0:1] * _LN2)
            if on_diagonal:
                p = jnp.where(_tile_mask(t), p, 0.0)
            delta = jnp.sum(dov.astype(F32) * o_ref[...].astype(F32), axis=-1, keepdims=True)
            dv_s[...] += _dot(p.astype(BF16), dov, TN)
            ds = (p * (_dot(dov, vv, NT) - delta) * scale).astype(BF16)
            dk_s[...] += _dot(ds, qv, TN)
            rows = pl.ds(pl.multiple_of(i * t, t), t)
            dq_ref[rows, :] += _dot(ds, kv, NN)

        @pl.when(i == j)
        def _():
            step(True)

        @pl.when(i != j)
        def _():
            step(False)

        @pl.when(i == n - 1)
        def _():
            dk_ref[...] = dk_s[...]
            dv_ref[...] = dv_s[...]

    qspec = pl.BlockSpec((t, HEAD_PAD), lambda h, p, qi, kj: (qi[p], h))
    kspec = pl.BlockSpec((t, HEAD_PAD), lambda h, p, qi, kj: (kj[p], h))
    grid_spec = pltpu.PrefetchScalarGridSpec(
        num_scalar_prefetch=2, grid=(HEADS, int(qi.shape[0])),
        in_specs=[qspec, kspec, kspec, qspec, qspec, qspec],
        out_specs=[pl.BlockSpec((S, HEAD_PAD), lambda h, p, qi, kj: (0, h)), kspec, kspec],
        scratch_shapes=[pltpu.VMEM((t, HEAD_PAD), F32), pltpu.VMEM((t, HEAD_PAD), F32)])
    return pl.pallas_call(
        body, name=name, grid_spec=grid_spec,
        out_shape=[jax.ShapeDtypeStruct((S, HW), F32)] * 3,
        compiler_params=_cparams("parallel", "arbitrary"),
    )(qi, kj, q, k, v, o, do, lse2)


_SQRT_HALF = 0.7071067811865476
_INV_SQRT_2PI = 0.3989422804014327


def _sg_select(r, grp):
    out = jnp.where(grp == 0, r[0:SG_CHUNK, :], 0.0)
    for g in range(1, SG_GROUPS):
        out = out + jnp.where(grp == g, r[g * SG_CHUNK:(g + 1) * SG_CHUNK, :], 0.0)
    return out


def _sgu_fwd(proj, gain, wstack, bmat, name):
    S = proj.shape[0]
    W = SG_WIDTH
    tm = _tile(S, TILES["sgu"])

    def body(z_ref, g_ref, w_ref, b_ref, o_ref):
        z = z_ref[...].astype(F32)
        zg = 0.5 * z * (1.0 + lax.erf(z * _SQRT_HALF))
        u, vv = zg[:, 0:W], zg[:, W:2 * W]
        r = lax.rsqrt(jnp.mean(vv * vv, axis=-1, keepdims=True) + NORM_EPS)
        vn = (vv * r * g_ref[...]).astype(BF16)
        grp = lax.broadcasted_iota(jnp.int32, (1, W), 1) // SG_GROUP_DIM
        for c in range(tm // SG_CHUNK):
            sl = slice(c * SG_CHUNK, (c + 1) * SG_CHUNK)
            mixed = _sg_select(_dot(w_ref[...], vn[sl, :], NN), grp) + b_ref[...]
            o_ref[sl, :] = (u[sl, :] * mixed).astype(BF16)

    return pl.pallas_call(
        body, name=name, grid=(S // tm,),
        in_specs=[pl.BlockSpec((tm, 2 * W), lambda i: (i, 0)), pl.BlockSpec((1, W), lambda i: (0, 0)),
                  pl.BlockSpec(wstack.shape, lambda i: (0, 0)), pl.BlockSpec(bmat.shape, lambda i: (0, 0))],
        out_specs=pl.BlockSpec((tm, W), lambda i: (i, 0)),
        out_shape=jax.ShapeDtypeStruct((S, W), BF16),
        compiler_params=_cparams("parallel"),
    )(proj, _row(gain), wstack, bmat)


def _sgu_bwd(dsg, proj, gain, wstack, wtstack, bmat, gsum, name):
    S = proj.shape[0]
    W = SG_WIDTH
    tm = _tile(S, TILES["sgu"])
    GS = SG_GROUPS * SG_CHUNK

    def body(d_ref, z_ref, g_ref, w_ref, wt_ref, b_ref, e_ref, dz_ref, dw_ref, db_ref, dg_ref, dw_s, db_s):
        i = pl.program_id(0)

        @pl.when(i == 0)
        def _():
            dw_s[...] = jnp.zeros_like(dw_s)
            db_s[...] = jnp.zeros_like(db_s)
            dg_ref[...] = jnp.zeros_like(dg_ref)

        z = z_ref[...].astype(F32)
        cdf = 0.5 * (1.0 + lax.erf(z * _SQRT_HALF))
        zg = z * cdf
        u, vv = zg[:, 0:W], zg[:, W:2 * W]
        r = lax.rsqrt(jnp.mean(vv * vv, axis=-1, keepdims=True) + NORM_EPS)
        vhat = vv * r
        vn = (vhat * g_ref[...]).astype(BF16)
        grp = lax.broadcasted_iota(jnp.int32, (1, W), 1) // SG_GROUP_DIM
        d = d_ref[...].astype(F32)
        du_parts, dvn_parts = [], []
        for c in range(tm // SG_CHUNK):
            sl = slice(c * SG_CHUNK, (c + 1) * SG_CHUNK)
            vc = vn[sl, :]
            mixed = _sg_select(_dot(w_ref[...], vc, NN), grp) + b_ref[...]
            dc = d[sl, :]
            du_parts.append(dc * mixed)
            dmix = dc * u[sl, :]
            db_s[...] += dmix
            dmb = dmix.astype(BF16)
            dvn_parts.append(_sg_select(_dot(wt_ref[...], dmb, NN), grp))
            astack = jnp.concatenate([jnp.where(grp == g, dmb, jnp.zeros_like(dmb)) for g in range(SG_GROUPS)], axis=0)
            dw_s[...] += _dot(astack, vc, NT)
        du = jnp.concatenate(du_parts, axis=0)
        dvn = jnp.concatenate(dvn_parts, axis=0)
        dg_ref[...] += jnp.sum(dvn * vhat, axis=0, keepdims=True)
        dvhat = dvn * g_ref[...]
        dvv = r * (dvhat - vhat * jnp.mean(dvhat * vhat, axis=-1, keepdims=True))
        dgelu = cdf + z * (_INV_SQRT_2PI * jnp.exp(-0.5 * z * z))
        dz_ref[:, 0:W] = (du * dgelu[:, 0:W]).astype(BF16)
        dz_ref[:, W:2 * W] = (dvv * dgelu[:, W:2 * W]).astype(BF16)

        @pl.when(i == pl.num_programs(0) - 1)
        def _():
            dw_ref[...] = dw_s[...]
            db_ref[...] = lax.dot_general(db_s[...], e_ref[...], NN, precision=lax.Precision.HIGHEST,
                                          preferred_element_type=F32)

    full = lambda a: pl.BlockSpec(a.shape, lambda i: (0, 0))
    return pl.pallas_call(
        body, name=name, grid=(S // tm,),
        in_specs=[pl.BlockSpec((tm, W), lambda i: (i, 0)), pl.BlockSpec((tm, 2 * W), lambda i: (i, 0)),
                  pl.BlockSpec((1, W), lambda i: (0, 0)), full(wstack), full(wtstack), full(bmat), full(gsum)],
        out_specs=[pl.BlockSpec((tm, 2 * W), lambda i: (i, 0)), pl.BlockSpec((GS, SG_CHUNK), lambda i: (0, 0)),
                   pl.BlockSpec((SG_CHUNK, LANE), lambda i: (0, 0)), pl.BlockSpec((1, W), lambda i: (0, 0))],
        out_shape=[jax.ShapeDtypeStruct((S, 2 * W), BF16), jax.ShapeDtypeStruct((GS, SG_CHUNK), F32),
                   jax.ShapeDtypeStruct((SG_CHUNK, LANE), F32), jax.ShapeDtypeStruct((1, W), F32)],
        scratch_shapes=[pltpu.VMEM((GS, SG_CHUNK), F32), pltpu.VMEM((SG_CHUNK, W), F32)],
        compiler_params=_cparams("arbitrary"),
    )(dsg, proj, _row(gain), wstack, wtstack, bmat, gsum)


WEIGHTS = ['ffn_pre_norm', 'ffn_pre_w_gate', 'ffn_pre_w_up', 'ffn_pre_w_down', 'mix_norm', 'ffn_post_norm',
           'ffn_post_w_gate', 'ffn_post_w_up', 'ffn_post_w_down', 'even_w_in', 'q_norm', 'w_uq', 'kv_norm', 'w_ukv',
           'sg_norm', 'sg_w', 'sg_b', 'even_w_out', 'conv_w_in', 'conv_w', 'conv_w_out', 'final_norm']
SHARD_AXIS = dict(ffn_pre_w_gate=2, ffn_pre_w_up=2, ffn_pre_w_down=1, ffn_post_w_gate=2, ffn_post_w_up=2,
                  ffn_post_w_down=1, even_w_in=2, w_uq=2, w_ukv=2, even_w_out=1, conv_w_in=2, conv_w=2, conv_w_out=1)
SHARDED = [n for n in WEIGHTS if n in SHARD_AXIS]
REPLICATED = [n for n in WEIGHTS if n not in SHARD_AXIS]


def _to_t(name, w):
    return jnp.swapaxes(w, 1, 2) if SHARD_AXIS[name] == 2 else w


def _rows_of(n):
    return -(-n // PACK_W)


def _pad_rows(a, mult, axis):
    r = a.shape[axis]
    extra = (-r) % mult
    if extra == 0:
        return a
    pad = [(0, 0)] * a.ndim
    pad[axis] = (0, extra)
    return jnp.pad(a, pad)


def _flat_rows(a, lead):
    flat = a.reshape(a.shape[:lead] + (-1,))
    n = flat.shape[-1]
    flat = _pad_rows(flat, PACK_W, lead)
    return flat.reshape(a.shape[:lead] + (_rows_of(n), PACK_W))


def _pack(pieces, lead, mult, piece_mult=1):
    rows, offs, off = [], [], 0
    for p in pieces:
        r = _pad_rows(_flat_rows(p, lead), piece_mult, lead)
        rows.append(r)
        offs.append(off)
        off += r.shape[lead]
    return _pad_rows(jnp.concatenate(rows, axis=lead), mult, lead), offs


def _unpack(buf, off, shape, lead):
    n = math.prod(shape)
    r = _rows_of(n)
    piece = lax.slice_in_dim(buf, off, off + r, axis=lead)
    piece = piece.reshape(buf.shape[:lead] + (r * PACK_W,))
    piece = lax.slice_in_dim(piece, 0, n, axis=lead)
    return piece.reshape(buf.shape[:lead] + tuple(shape))


def _head_pad(w, per_head, keep):
    k = w.shape[-1]
    w = w.reshape(HEADS, per_head, k)[:, keep[0]:keep[1]]
    w = jnp.pad(w, ((0, 0), (0, HEAD_PAD - (keep[1] - keep[0])), (0, 0)))
    return w.reshape(HEADS * HEAD_PAD, k)


def _head_unpad(w, n):
    return w.reshape(HEADS, HEAD_PAD, w.shape[-1])[:, :n]


def kernel(x, positions, ffn_pre_norm, ffn_pre_w_gate, ffn_pre_w_up, ffn_pre_w_down, mix_norm, ffn_post_norm, ffn_post_w_gate, ffn_post_w_up, ffn_post_w_down, even_w_in, q_norm, w_uq, kv_norm, w_ukv, sg_norm, sg_w, sg_b, even_w_out, conv_w_in, conv_w, conv_w_out, final_norm, loss_target, m_ffn_pre_norm, m_ffn_pre_w_gate, m_ffn_pre_w_up, m_ffn_pre_w_down, m_mix_norm, m_ffn_post_norm, m_ffn_post_w_gate, m_ffn_post_w_up, m_ffn_post_w_down, m_even_w_in, m_q_norm, m_w_uq, m_kv_norm, m_w_ukv, m_sg_norm, m_sg_w, m_sg_b, m_even_w_out, m_conv_w_in, m_conv_w, m_conv_w_out, m_final_norm, v_ffn_pre_norm, v_ffn_pre_w_gate, v_ffn_pre_w_up, v_ffn_pre_w_down, v_mix_norm, v_ffn_post_norm, v_ffn_post_w_gate, v_ffn_post_w_up, v_ffn_post_w_down, v_even_w_in, v_q_norm, v_w_uq, v_kv_norm, v_w_ukv, v_sg_norm, v_sg_w, v_sg_b, v_even_w_out, v_conv_w_in, v_conv_w, v_conv_w_out, v_final_norm):
    given = dict(locals())
    w_loc = {n: given[n] for n in WEIGHTS}
    m_loc = {n: given["m_" + n] for n in WEIGHTS}
    v_loc = {n: given["v_" + n] for n in WEIGHTS}

    S, D = x.shape[1], x.shape[2]
    depth = ffn_pre_norm.shape[0]
    QL, KVL = q_norm.shape[1], kv_norm.shape[1]
    ZW = 2 * SG_WIDTH
    assert x.shape[0] == 1 and ZW % KVL == 0 and (ZW + KVL) % HEAD_PAD == 0 and (ZW + KVL + 2 * HEAD_PAD) % QL == 0
    col_ckv = ZW // KVL
    col_kr = (ZW + KVL) // HEAD_PAD
    col_cq = (ZW + KVL + 2 * HEAD_PAD) // QL

    t_loc = {n: _to_t(n, w_loc[n]) for n in SHARDED}
    full = {n: {} for n in SHARDED}

    def ffn_keys(kind, l):
        return [("ffn_%s_w_%s" % (kind, part), l) for part in ("gate", "up", "down")]

    def mixer_keys(l):
        names = ("even_w_in", "w_uq", "w_ukv", "even_w_out") if l % 2 == 0 else ("conv_w_in", "conv_w", "conv_w_out")
        return [(n, l // 2) for n in names]

    def local_pack(keys):
        return _pack([t_loc[n][l].astype(BF16) for n, l in keys], 0, 16, piece_mult=16)

    def take_gathered(gathered, keys, offs):
        for (n, l), off in zip(keys, offs):
            piece = _unpack(gathered, off, t_loc[n].shape[1:], 1)
            full[n][l] = piece.reshape(N_DEV * piece.shape[1], piece.shape[2])

    def gather_rider(keys):
        pack, offs = local_pack(keys)
        return _Exchange("gather", [pack]), offs

    first_keys = ffn_keys("pre", 0)
    pack0, offs0 = local_pack(first_keys)
    take_gathered(_all_gather(pack0, "gather_weights"), first_keys, offs0)

    tril = jnp.tril(jnp.ones((SG_CHUNK, SG_CHUNK), F32))
    even_ops = {}

    def even_operands(e):
        if e not in even_ops:
            wi = full["even_w_in"][e]
            zrow = lambda k: jnp.zeros((k, D), BF16)
            ops = dict(win_pad=jnp.concatenate(
                [wi[QL + KVL + ROPE:], wi[QL:QL + KVL], wi[QL + KVL:QL + KVL + ROPE], zrow(HEAD_PAD - ROPE),
                 zrow(HEAD_PAD), wi[:QL]], axis=0))
            ops["wq_big"] = _head_pad(full["w_uq"][e], QK_DIM, (0, QK_DIM))
            wkv = full["w_ukv"][e]
            ops["wkv_big"] = jnp.concatenate([_head_pad(wkv, NOPE + VDIM, (0, NOPE)),
                                              _head_pad(wkv, NOPE + VDIM, (NOPE, NOPE + VDIM))], axis=0)
            wo = full["even_w_out"][e]
            ops["wo_attn"] = _head_pad(wo[:HEADS * VDIM], VDIM, (0, VDIM))
            ops["wo_sg"] = wo[HEADS * VDIM:]
            wt = sg_w[e] * tril
            ops["wstack"] = wt.reshape(SG_GROUPS * SG_CHUNK, SG_CHUNK).astype(BF16)
            ops["wtstack"] = jnp.swapaxes(wt, 1, 2).reshape(SG_GROUPS * SG_CHUNK, SG_CHUNK).astype(BF16)
            ops["bmat"] = jnp.repeat(sg_b[e].T, SG_GROUP_DIM, axis=1)
            even_ops[e] = ops
        return even_ops[e]

    gsum = (jnp.arange(SG_WIDTH)[:, None] // SG_GROUP_DIM == jnp.arange(LANE)[None, :]).astype(F32)

    def conv_taps(o):
        return jnp.pad(jnp.swapaxes(full["conv_w"][o], 0, 1).astype(F32), ((0, 8 - CONV_K), (0, 0)))

    inv_freq = ROPE_THETA ** (-jnp.arange(0, ROPE, 2, dtype=F32) / ROPE)
    ang = positions[0].astype(F32)[:, None] * inv_freq
    cos, sin = jnp.cos(ang), jnp.sin(ang)
    ones, zeros = jnp.ones((S, NOPE), F32), jnp.zeros((S, HEAD_PAD - QK_DIM), F32)
    ct = jnp.concatenate([ones, cos, cos, zeros], axis=1)
    st = jnp.concatenate([0.0 * ones, -sin, sin, zeros], axis=1)

    xs = x[0]
    saved = []
    def ffn_forward(xin, kind, l, next_keys):
        gain = (ffn_pre_norm if kind == "pre" else ffn_post_norm)[l]
        wg, wu, wd = (full[n][l] for n, _ in ffn_keys(kind, l))
        if not next_keys:
            return _ffn_fwd(xin, gain, wg, wu, wd, "ffn_fwd")
        rider, offs = gather_rider(next_keys)
        xo, a, b, gathered = _ffn_fwd(xin, gain, wg, wu, wd, "ffn_fwd_gather", rider=rider)
        take_gathered(gathered, next_keys, offs)
        return xo, a, b

    for l in range(depth):
        sv = dict(x0=xs)
        x1, sv["a1"], sv["b1"] = ffn_forward(xs, "pre", l, (mixer_keys(0) if l == 0 else []) + ffn_keys("post", l))
        sv["x1"] = x1
        if l % 2 == 0:
            e = l // 2
            ops = even_operands(e)
            sv["h"], proj = _norm_mm(x1, mix_norm[l], ops["win_pad"], "even_in_proj", F32)
            qn, q_big = _norm_mm(proj, q_norm[e], ops["wq_big"], "q_up_proj", F32, col=col_cq)
            kvn, kv_big = _norm_mm(proj, kv_norm[e], ops["wkv_big"], "kv_up_proj", BF16, col=col_ckv)
            q_r, k_r, v_r = _rope_fwd(q_big, kv_big, proj, col_kr, ct, st, "rope_fwd")
            o_att, lse = _attn_fwd(q_r, k_r, v_r, "attn_fwd")
            sg = _sgu_fwd(proj, sg_norm[e], ops["wstack"], ops["bmat"], "sgu_fwd")
            tmp = _mm(o_att, ops["wo_attn"], "nn", "even_out_attn", out_dtype=F32, res=x1)
            x2 = _mm(sg, ops["wo_sg"], "nn", "even_out_sg", out_dtype=F32, res=tmp)
            sv.update(proj=proj, qn=qn, kvn=kvn, q=q_r, k=k_r, v=v_r, o=o_att, lse=lse, sg=sg)
        else:
            o = l // 2
            sv["h"], p = _norm_mm(x1, mix_norm[l], full["conv_w_in"][o], "conv_in_proj", BF16)
            cv = _conv_fwd(p, conv_taps(o), "conv_fwd")
            x2 = _mm(cv, full["conv_w_out"][o], "nn", "conv_out_proj", out_dtype=F32, res=x1)
            sv.update(p=p, cv=cv)
        sv["x2"] = x2
        next_keys = ffn_keys("pre", l + 1) + mixer_keys(l + 1) if l + 1 < depth else []
        xs, sv["a2"], sv["b2"] = ffn_forward(x2, "post", l, next_keys)
        saved.append(sv)

    gr = {n: [None] * w_loc[n].shape[0] for n in REPLICATED if n != "final_norm"}
    per_layer = {n: [None] * w_loc[n].shape[0] for n in SHARDED}
    pending = []

    def scatter_rider():
        pieces, where, off = [], [], 0
        for n, l, g in pending:
            piece = _pad_rows(_flat_rows(g.astype(BF16).reshape(N_DEV, -1), 1), 16, 1)
            pieces.append(piece)
            where.append((n, l, off))
            off += piece.shape[1]
        if off % GRAD_ROWS_MULT:
            pieces.append(jnp.zeros((N_DEV, (-off) % GRAD_ROWS_MULT, PACK_W), BF16))
        pending.clear()
        return _Exchange("scatter", pieces), where

    def take_scattered(received, where):
        owned = _sum_slots(received, "sum_grad_shards")
        for n, l, off in where:
            per_layer[n][l] = _unpack(owned, off, t_loc[n].shape[1:], 0)

    def ffn_backward(dxin, xin, kind, l, a, b):
        gain = (ffn_pre_norm if kind == "pre" else ffn_post_norm)[l]
        keys = ffn_keys(kind, l)
        wg, wu, wd = (full[n][l] for n, _ in keys)
        if pending:
            rider, where = scatter_rider()
            dxo, dz, hh, dy, dgain, received = _ffn_bwd(dxin, xin, gain, a, b, wg, wu, wd, "ffn_bwd_scatter", rider=rider)
            take_scattered(received, where)
        else:
            dxo, dz, hh, dy, dgain = _ffn_bwd(dxin, xin, gain, a, b, wg, wu, wd, "ffn_bwd")
        gr["ffn_%s_norm" % kind][l] = dgain[0]
        for (n, _), g in zip(keys, _ffn_dw(a, b, dz, hh, dy, "ffn_dw")):
            pending.append((n, l, g))
        return dxo

    dx, g_final, loss_part = _loss_head(xs, loss_target[0], final_norm, "loss_head")
    for l in reversed(range(depth)):
        sv = saved[l]
        dx = ffn_backward(dx, sv["x2"], "post", l, sv["a2"], sv["b2"])
        h = sv["h"]
        if l % 2 == 0:
            e = l // 2
            ops = even_operands(e)
            d_o = _mm(dx, ops["wo_attn"], "nt", "even_out_attn_bwd", out_dtype=BF16)
            d_sg = _mm(dx, ops["wo_sg"], "nt", "even_out_sg_bwd", out_dtype=BF16)
            g_wo_attn = _mm(sv["o"], dx, "tn", "even_out_attn_dw", out_dtype=F32)
            g_wo_sg = _mm(sv["sg"], dx, "tn", "even_out_sg_dw", out_dtype=F32)
            dq, dk, dv = _attn_bwd(sv["q"], sv["k"], sv["v"], sv["o"], d_o, sv["lse"], "attn_bwd")
            dq_big, dkv_big, dkr = _rope_bwd(dq, dk, dv, ct, st, "rope_bwd")
            dz_sg, g_wstack, g_bias, g_sgn = _sgu_bwd(d_sg, sv["proj"], sg_norm[e], ops["wstack"], ops["wtstack"],
                                                      ops["bmat"], gsum, "sgu_bwd")
            g_wq_big = _mm(dq_big, sv["qn"], "tn", "q_up_proj_dw", out_dtype=F32)
            g_wkv_big = _mm(dkv_big, sv["kvn"], "tn", "kv_up_proj_dw", out_dtype=F32)
            dcq, g_qn = _mm_norm_bwd(dq_big, ops["wq_big"], sv["proj"], q_norm[e], None, "q_up_proj_bwd", col=col_cq,
                                     out_dtype=BF16)
            dckv, g_kvn = _mm_norm_bwd(dkv_big, ops["wkv_big"], sv["proj"], kv_norm[e], None, "kv_up_proj_bwd",
                                       col=col_ckv, out_dtype=BF16)
            dproj = jnp.concatenate([dz_sg, dckv, dkr, jnp.zeros((S, HEAD_PAD), BF16), dcq], axis=1)
            dx, dgain = _mm_norm_bwd(dproj, ops["win_pad"], sv["x1"], mix_norm[l], dx, "even_in_proj_bwd")
            g_win = _mm(dproj, h, "tn", "even_in_proj_dw", out_dtype=F32)
            o_cq, o_ckv, o_kr = col_cq * QL, col_ckv * KVL, col_kr * HEAD_PAD
            hw = HEADS * HEAD_PAD
            pending.append(("even_w_in", e, jnp.concatenate(
                [g_win[o_cq:o_cq + QL], g_win[o_ckv:o_ckv + KVL], g_win[o_kr:o_kr + ROPE], g_win[:ZW]], axis=0)))
            pending.append(("w_uq", e, _head_unpad(g_wq_big, QK_DIM).reshape(HEADS * QK_DIM, QL)))
            pending.append(("w_ukv", e, jnp.concatenate(
                [_head_unpad(g_wkv_big[:hw], NOPE), _head_unpad(g_wkv_big[hw:], VDIM)],
                axis=1).reshape(HEADS * (NOPE + VDIM), KVL)))
            pending.append(("even_w_out", e, jnp.concatenate(
                [_head_unpad(g_wo_attn, VDIM).reshape(HEADS * VDIM, D), g_wo_sg], axis=0)))
            gr["q_norm"][e], gr["kv_norm"][e], gr["sg_norm"][e] = g_qn[0], g_kvn[0], g_sgn[0]
            gr["sg_w"][e] = g_wstack.reshape(SG_GROUPS, SG_CHUNK, SG_CHUNK) * tril
            gr["sg_b"][e] = g_bias[:, :SG_GROUPS].T
        else:
            o = l // 2
            dcv = _mm(dx, full["conv_w_out"][o], "nt", "conv_out_proj_bwd", out_dtype=BF16)
            pending.append(("conv_w_out", o, _mm(sv["cv"], dx, "tn", "conv_out_proj_dw", out_dtype=BF16)))
            dp, dcw = _conv_bwd(dcv, sv["p"], conv_taps(o), "conv_bwd")
            dx, dgain = _mm_norm_bwd(dp, full["conv_w_in"][o], sv["x1"], mix_norm[l], dx, "conv_in_proj_bwd")
            pending.append(("conv_w_in", o, _mm(dp, h, "tn", "conv_in_proj_dw", out_dtype=BF16)))
            pending.append(("conv_w", o, jnp.swapaxes(dcw[:CONV_K], 0, 1)))
        gr["mix_norm"][l] = dgain[0]
        dx = ffn_backward(dx, sv["x0"], "pre", l, sv["a1"], sv["b1"])
    grad_x = dx[None]

    rider, where = scatter_rider()
    take_scattered(_exchange("scatter", rider.arrays, "scatter_grads"), where)
    grads = {n: _to_t(n, jnp.stack(per_layer[n])) for n in SHARDED}

    small = [jnp.stack(gr[n]) for n in REPLICATED if n != "final_norm"] + [g_final[0], loss_part[0, :1]]
    spack, soffs = _pack(small, 0, SMALL_ROWS_MULT)
    sgath = _all_gather(spack, "gather_small_grads")
    ssum = _sum_slots(sgath, "sum_small_grads")
    names_small = [n for n in REPLICATED if n != "final_norm"] + ["final_norm", "loss"]
    for n, off, piece in zip(names_small, soffs, small):
        val = _unpack(ssum, off, piece.shape, 0)
        if n == "loss":
            loss = val[0]
        else:
            grads[n] = val

    delta, new_m, new_v = {}, {}, {}
    for n in SHARDED:
        two_d = lambda a: a.reshape(-1, a.shape[-1])
        d, nm, nv = _adamw(two_d(w_loc[n]), two_d(grads[n]), two_d(m_loc[n]), two_d(v_loc[n]), "adamw")
        delta[n], new_m[n], new_v[n] = (a.reshape(w_loc[n].shape) for a in (d, nm, nv))
    flat = lambda d: _pack([d[n] for n in REPLICATED], 0, SMALL_ROWS_MULT)
    (wf, aoffs), (gf, _), (mf, _), (vf, _) = flat(w_loc), flat(grads), flat(m_loc), flat(v_loc)
    for res, buf in zip((delta, new_m, new_v), _adamw(wf, gf, mf, vf, "adamw_replicated")):
        for n, off in zip(REPLICATED, aoffs):
            res[n] = _unpack(buf, off, w_loc[n].shape, 0)
    outs = [loss, grad_x] + [grads[n] for n in WEIGHTS]
    for res in (delta, new_m, new_v):
        outs += [res[n] for n in WEIGHTS]
    return tuple(outs)
```

```python
import math

import numpy as np
import jax
import jax.numpy as jnp
from jax import lax
from jax.experimental import pallas as pl
from jax.experimental.pallas import tpu as pltpu

F32 = jnp.float32
BF16 = jnp.bfloat16

N_DEV = 8
NORM_EPS = 1e-6
HEADS = 8
NOPE = 64
ROPE = 32
VDIM = 64
HEAD_PAD = 128
QK_DIM = NOPE + ROPE
ROPE_THETA = 10000.0
SG_GROUPS = 8
SG_GROUP_DIM = 64
SG_WIDTH = SG_GROUPS * SG_GROUP_DIM
SG_CHUNK = 128
CONV_K = 3
ADAM_LR, ADAM_B1, ADAM_B2, ADAM_EPS, ADAM_WD, ADAM_STEP = 0.001, 0.9, 0.999, 1e-08, 0.01, 10

LANE = 128
PACK_W = 1024
GRAD_ROWS_MULT = 512
SMALL_ROWS_MULT = 64
VMEM_LIMIT = 60 * 1024 * 1024

TILES = dict(ffn_fwd=512, ffn_bwd=256, ffn_dw=2048, mm=1024, mm_norm_bwd=512, rms=1024, mm_tn=2048, ew=1024,
             attn_fwd=2048, attn_bwd=1024, sgu=1024, adam=512)

NT = (((1,), (1,)), ((), ()))
NN = (((1,), (0,)), ((), ()))
TN = (((0,), (0,)), ((), ()))


def _dot(a, b, dims):
    return lax.dot_general(a, b, dims, preferred_element_type=F32)


def _cparams(*sem):
    return pltpu.CompilerParams(dimension_semantics=sem if sem else None, vmem_limit_bytes=VMEM_LIMIT)


def _tile(n, want):
    t = min(want, n)
    while n % t:
        t //= 2
    return t if t % 8 == 0 else n


def _lane_tile(n, cap):
    best = None
    for k in range(1, n // LANE + 1):
        t = k * LANE
        if n % t == 0 and t <= cap:
            best = t
    return best or n


def _row(v):
    return v.reshape(1, -1).astype(F32)


def _all_gather(block, name):
    R, W = block.shape

    def body(x_ref, out_ref, send_sems, recv_sems, local_sem):
        x, y, c = lax.axis_index("x"), lax.axis_index("y"), lax.axis_index("c")
        me, sibling = (x, y, c), (x, y, 1 - c)
        chips = [(1 - x, y), (x, 1 - y), (1 - x, 1 - y)]

        def slot(px, py, pc):
            return out_ref.at[4 * px + 2 * py + pc]

        def copy(k, blk, to, src=None):
            return pltpu.make_async_remote_copy(
                src_ref=slot(*blk) if src is None else src, dst_ref=slot(*blk),
                send_sem=send_sems.at[k], recv_sem=recv_sems.at[k],
                device_id=to, device_id_type=pl.DeviceIdType.MESH)

        mine = pltpu.make_async_copy(x_ref, slot(*me), local_sem)
        mine.start()
        first = [copy(0, me, sibling, src=x_ref)]
        first += [copy(1 + j, me, (*chip, c), src=x_ref) for j, chip in enumerate(chips)]
        for cp in first:
            cp.start()
        passed = [copy(4 + j, (*chip, c), sibling) for j, chip in enumerate(chips)]
        for j, chip in enumerate(chips):
            copy(1 + j, (*chip, c), me).wait_recv()
            passed[j].start()
        copy(0, sibling, me).wait_recv()
        for j, chip in enumerate(chips):
            copy(4 + j, (*chip, 1 - c), me).wait_recv()
        for cp in first + passed:
            cp.wait_send()
        mine.wait()

    return pl.pallas_call(
        body, name=name,
        out_shape=jax.ShapeDtypeStruct((N_DEV, R, W), block.dtype),
        in_specs=[pl.BlockSpec(memory_space=pl.ANY)],
        out_specs=pl.BlockSpec(memory_space=pl.ANY),
        scratch_shapes=[pltpu.SemaphoreType.DMA((7,)), pltpu.SemaphoreType.DMA((7,)), pltpu.SemaphoreType.DMA],
    )(block)


class _Exchange:
    def __init__(self, kind, arrays):
        self.kind, self.arrays = kind, list(arrays)
        if kind == "gather":
            (r, w), = [a.shape for a in self.arrays]
            self.rows = [r]
        else:
            self.rows = [a.shape[1] for a in self.arrays]
            w = self.arrays[0].shape[2]
        self.offs = [sum(self.rows[:i]) for i in range(len(self.rows))]
        self.n_in = len(self.arrays)
        self.out_shape = jax.ShapeDtypeStruct((N_DEV, sum(self.rows), w), self.arrays[0].dtype)
        self.in_specs = [pl.BlockSpec(memory_space=pl.ANY)] * self.n_in
        self.out_spec = pl.BlockSpec(memory_space=pl.ANY)
        self.out_specs, self.out_shapes = [self.out_spec], [self.out_shape]
        self.scratch = [pltpu.SemaphoreType.DMA((7,)), pltpu.SemaphoreType.DMA((7,)), pltpu.SemaphoreType.DMA]

    def _peers(self):
        x, y, c = lax.axis_index("x"), lax.axis_index("y"), lax.axis_index("c")
        me = 4 * x + 2 * y + c
        return me, [(k, (x ^ (k >> 2), y ^ ((k >> 1) & 1), c ^ (k & 1))) for k in range(1, N_DEV)]

    @staticmethod
    def _remote(src, dst, k, to, send_sems, recv_sems):
        return pltpu.make_async_remote_copy(
            src_ref=src, dst_ref=dst, send_sem=send_sems.at[k - 1], recv_sem=recv_sems.at[k - 1],
            device_id=to, device_id_type=pl.DeviceIdType.MESH)

    def start(self, s_refs, r_ref, send_sems, recv_sems, local_sem):
        me, peers = self._peers()
        for s_ref, off, r in zip(s_refs, self.offs, self.rows):
            src = s_ref if self.kind == "gather" else s_ref.at[me]
            pltpu.make_async_copy(src, r_ref.at[me, pl.ds(off, r)], local_sem).start()
        for k, to in peers:
            peer = 4 * to[0] + 2 * to[1] + to[2]
            for s_ref, off, r in zip(s_refs, self.offs, self.rows):
                src = s_ref if self.kind == "gather" else s_ref.at[peer]
                self._remote(src, r_ref.at[me, pl.ds(off, r)], k, to, send_sems, recv_sems).start()

    def wait(self, s_refs, r_ref, send_sems, recv_sems, local_sem):
        me, peers = self._peers()
        whole = r_ref.at[me]
        totals = [self._remote(whole, whole, k, to, send_sems, recv_sems) for k, to in peers]
        for cp in totals:
            cp.wait_recv()
        for cp in totals:
            cp.wait_send()
        pltpu.make_async_copy(whole, whole, local_sem).wait()


class _NoRider:
    arrays, in_specs, out_specs, out_shapes, scratch = [], [], [], [], []


_NO_RIDER = _NoRider()


def _ride(rider, refs, n_in, n_out, first, last):
    if rider is None:
        return refs, lambda: None
    k = rider.n_in
    s_refs = refs[n_in:n_in + k]
    r_ref = refs[n_in + k + n_out]
    sems = refs[-3:]
    own = refs[:n_in] + refs[n_in + k:n_in + k + n_out] + refs[n_in + k + n_out + 1:-3]

    @pl.when(first)
    def _():
        rider.start(s_refs, r_ref, *sems)

    def finish():
        @pl.when(last)
        def _():
            rider.wait(s_refs, r_ref, *sems)

    return own, finish


def _exchange_pair(ex_a, ex_b, name):
    na, nb = ex_a.n_in, ex_b.n_in

    def body(*refs):
        sa, sb = refs[:na], refs[na:na + nb]
        ra, rb = refs[na + nb], refs[na + nb + 1]
        sems = refs[na + nb + 2:]
        ex_a.start(sa, ra, *sems[:3])
        ex_b.start(sb, rb, *sems[3:])
        ex_a.wait(sa, ra, *sems[:3])
        ex_b.wait(sb, rb, *sems[3:])

    return pl.pallas_call(
        body, name=name, out_shape=[ex_a.out_shape, ex_b.out_shape], in_specs=ex_a.in_specs + ex_b.in_specs,
        out_specs=[ex_a.out_spec, ex_b.out_spec], scratch_shapes=ex_a.scratch + ex_b.scratch,
    )(*ex_a.arrays, *ex_b.arrays)


def _sum_slots(parts, name):
    _, R, W = parts.shape
    tr = _tile(R, TILES["adam"])

    def body(p_ref, o_ref):
        acc = p_ref[0].astype(F32)
        for s in range(1, N_DEV):
            acc = acc + p_ref[s].astype(F32)
        o_ref[...] = acc

    return pl.pallas_call(
        body, name=name, grid=(R // tr,),
        in_specs=[pl.BlockSpec((N_DEV, tr, W), lambda i: (0, i, 0))],
        out_specs=pl.BlockSpec((tr, W), lambda i: (i, 0)),
        out_shape=jax.ShapeDtypeStruct((R, W), F32),
        compiler_params=_cparams("parallel"),
    )(parts)


def _adamw(w, g, m, v, name):
    R, W = w.shape
    tr = _tile(R, TILES["adam"])
    c1 = 1.0 - ADAM_B1 ** ADAM_STEP
    c2 = 1.0 - ADAM_B2 ** ADAM_STEP

    def body(w_ref, g_ref, m_ref, v_ref, d_ref, nm_ref, nv_ref):
        g = g_ref[...]
        nm = ADAM_B1 * m_ref[...] + (1.0 - ADAM_B1) * g
        nv = ADAM_B2 * v_ref[...] + (1.0 - ADAM_B2) * (g * g)
        d_ref[...] = -ADAM_LR * ((nm / c1) / (jnp.sqrt(nv / c2) + ADAM_EPS) + ADAM_WD * w_ref[...])
        nm_ref[...] = nm
        nv_ref[...] = nv

    spec = pl.BlockSpec((tr, W), lambda i: (i, 0))
    return pl.pallas_call(
        body, name=name, grid=(R // tr,),
        in_specs=[spec] * 4, out_specs=[spec] * 3,
        out_shape=[jax.ShapeDtypeStruct((R, W), F32)] * 3,
        compiler_params=_cparams("parallel"),
    )(w, g, m, v)


def _mm(a, b, mode, name, out_dtype=BF16, res=None, scale=1.0, acol=None, kdim=None):
    if mode == "tn":
        S, M = a.shape
        N = b.shape[1]
        ts = _tile(S, TILES["mm_tn"])
        tmo = _lane_tile(M, 1024)

        def body(a_ref, b_ref, o_ref, acc):
            s = pl.program_id(1)

            @pl.when(s == 0)
            def _():
                acc[...] = jnp.zeros_like(acc)

            acc[...] += _dot(a_ref[...].astype(BF16), b_ref[...].astype(BF16), TN)

            @pl.when(s == pl.num_programs(1) - 1)
            def _():
                o_ref[...] = acc[...].astype(out_dtype)

        return pl.pallas_call(
            body, name=name, grid=(M // tmo, S // ts),
            in_specs=[pl.BlockSpec((ts, tmo), lambda i, s: (s, i)), pl.BlockSpec((ts, N), lambda i, s: (s, 0))],
            out_specs=pl.BlockSpec((tmo, N), lambda i, s: (i, 0)),
            out_shape=jax.ShapeDtypeStruct((M, N), out_dtype),
            scratch_shapes=[pltpu.VMEM((tmo, N), F32)],
            compiler_params=_cparams("parallel", "arbitrary"),
        )(a, b)

    M = a.shape[0]
    K = kdim if kdim is not None else a.shape[1]
    ac = 0 if acol is None else acol
    N = b.shape[1] if mode == "nn" else b.shape[0]
    tm = _tile(M, TILES["mm"])
    dims = NN if mode == "nn" else NT

    def body(*refs):
        if res is None:
            a_ref, b_ref, o_ref = refs
        else:
            a_ref, b_ref, r_ref, o_ref = refs
        acc = _dot(a_ref[...].astype(BF16), b_ref[...].astype(BF16), dims)
        if res is not None:
            acc = r_ref[...] + scale * acc
        o_ref[...] = acc.astype(out_dtype)

    in_specs = [pl.BlockSpec((tm, K), lambda i: (i, ac)), pl.BlockSpec(b.shape, lambda i: (0, 0))]
    args = [a, b]
    if res is not None:
        in_specs.append(pl.BlockSpec((tm, N), lambda i: (i, 0)))
        args.append(res)
    return pl.pallas_call(
        body, name=name, grid=(M // tm,),
        in_specs=in_specs, out_specs=pl.BlockSpec((tm, N), lambda i: (i, 0)),
        out_shape=jax.ShapeDtypeStruct((M, N), out_dtype),
        compiler_params=_cparams("parallel"),
    )(*args)


def _norm_mm(x, gain, w_t, name, out_dtype):
    S, D = x.shape
    N = w_t.shape[0]
    tm = _tile(S, TILES["mm"])

    def body(x_ref, g_ref, w_ref, h_ref, o_ref):
        xv = x_ref[...]
        r = lax.rsqrt(jnp.mean(xv * xv, axis=-1, keepdims=True) + NORM_EPS)
        h = (xv * r * g_ref[...]).astype(BF16)
        h_ref[...] = h
        o_ref[...] = _dot(h, w_ref[...], NT).astype(out_dtype)

    return pl.pallas_call(
        body, name=name, grid=(S // tm,),
        in_specs=[pl.BlockSpec((tm, D), lambda i: (i, 0)), pl.BlockSpec((1, D), lambda i: (0, 0)),
                  pl.BlockSpec((N, D), lambda i: (0, 0))],
        out_specs=[pl.BlockSpec((tm, D), lambda i: (i, 0)), pl.BlockSpec((tm, N), lambda i: (i, 0))],
        out_shape=[jax.ShapeDtypeStruct((S, D), BF16), jax.ShapeDtypeStruct((S, N), out_dtype)],
        compiler_params=_cparams("parallel"),
    )(x, _row(gain), w_t)


def _mm_norm_bwd(a, w, x, gain, res, name):
    S, K = a.shape
    D = w.shape[1]
    tm = _tile(S, TILES["mm_norm_bwd"])

    def body(a_ref, w_ref, x_ref, g_ref, r_ref, dx_ref, dg_ref):
        @pl.when(pl.program_id(0) == 0)
        def _():
            dg_ref[...] = jnp.zeros_like(dg_ref)

        dh = _dot(a_ref[...].astype(BF16), w_ref[...], NN)
        xv = x_ref[...]
        r = lax.rsqrt(jnp.mean(xv * xv, axis=-1, keepdims=True) + NORM_EPS)
        xhat = xv * r
        dg_ref[...] += jnp.sum(dh * xhat, axis=0, keepdims=True)
        dxhat = dh * g_ref[...]
        dx_ref[...] = r_ref[...] + r * (dxhat - xhat * jnp.mean(dxhat * xhat, axis=-1, keepdims=True))

    row = pl.BlockSpec((tm, D), lambda i: (i, 0))
    return pl.pallas_call(
        body, name=name, grid=(S // tm,),
        in_specs=[pl.BlockSpec((tm, K), lambda i: (i, 0)), pl.BlockSpec((K, D), lambda i: (0, 0)), row,
                  pl.BlockSpec((1, D), lambda i: (0, 0)), row],
        out_specs=[row, pl.BlockSpec((1, D), lambda i: (0, 0))],
        out_shape=[jax.ShapeDtypeStruct((S, D), F32), jax.ShapeDtypeStruct((1, D), F32)],
        compiler_params=_cparams("arbitrary"),
    )(a, w, x, _row(gain), res)


def _rms_fwd(x, gain, name, col=0, width=None):
    S = x.shape[0]
    W = width if width is not None else x.shape[1]
    tm = _tile(S, TILES["rms"])

    def body(x_ref, g_ref, o_ref):
        xv = x_ref[...].astype(F32)
        r = lax.rsqrt(jnp.mean(xv * xv, axis=-1, keepdims=True) + NORM_EPS)
        o_ref[...] = (xv * r * g_ref[...]).astype(BF16)

    return pl.pallas_call(
        body, name=name, grid=(S // tm,),
        in_specs=[pl.BlockSpec((tm, W), lambda i: (i, col)), pl.BlockSpec((1, W), lambda i: (0, 0))],
        out_specs=pl.BlockSpec((tm, W), lambda i: (i, 0)),
        out_shape=jax.ShapeDtypeStruct((S, W), BF16),
        compiler_params=_cparams("parallel"),
    )(x, _row(gain))


def _rms_bwd(dy, x, gain, name, col=0, res=None, out_dtype=F32):
    S, W = dy.shape
    tm = _tile(S, TILES["rms"])

    def body(*refs):
        if res is None:
            dy_ref, x_ref, g_ref, dx_ref, dg_ref = refs
        else:
            dy_ref, x_ref, g_ref, r_ref, dx_ref, dg_ref = refs

        @pl.when(pl.program_id(0) == 0)
        def _():
            dg_ref[...] = jnp.zeros_like(dg_ref)

        xv = x_ref[...].astype(F32)
        d = dy_ref[...].astype(F32)
        r = lax.rsqrt(jnp.mean(xv * xv, axis=-1, keepdims=True) + NORM_EPS)
        xhat = xv * r
        dg_ref[...] += jnp.sum(d * xhat, axis=0, keepdims=True)
        dxhat = d * g_ref[...]
        dx = r * (dxhat - xhat * jnp.mean(dxhat * xhat, axis=-1, keepdims=True))
        if res is not None:
            dx = dx + r_ref[...]
        dx_ref[...] = dx.astype(out_dtype)

    in_specs = [pl.BlockSpec((tm, W), lambda i: (i, 0)), pl.BlockSpec((tm, W), lambda i: (i, col)),
                pl.BlockSpec((1, W), lambda i: (0, 0))]
    args = [dy, x, _row(gain)]
    if res is not None:
        in_specs.append(pl.BlockSpec((tm, W), lambda i: (i, 0)))
        args.append(res)
    return pl.pallas_call(
        body, name=name, grid=(S // tm,),
        in_specs=in_specs,
        out_specs=[pl.BlockSpec((tm, W), lambda i: (i, 0)), pl.BlockSpec((1, W), lambda i: (0, 0))],
        out_shape=[jax.ShapeDtypeStruct((S, W), out_dtype), jax.ShapeDtypeStruct((1, W), F32)],
        compiler_params=_cparams("arbitrary"),
    )(*args)


def _silu_parts(a):
    s = jax.nn.sigmoid(a)
    return a * s, s * (1.0 + a * (1.0 - s))


def _ffn_fwd(x, gain, wg_t, wu_t, wd, name, rider=None):
    S, D = x.shape
    Fd = wd.shape[0]
    tm = _tile(S, TILES["ffn_fwd"])
    fc = _lane_tile(Fd, 512)

    def body(*refs):
        i = pl.program_id(0)
        own, finish = _ride(rider, refs, 5, 3, i == 0, i == pl.num_programs(0) - 1)
        x_ref, g_ref, wg_ref, wu_ref, wd_ref, o_ref, a_ref, b_ref = own
        xv = x_ref[...]
        r = lax.rsqrt(jnp.mean(xv * xv, axis=-1, keepdims=True) + NORM_EPS)
        h = (xv * r * g_ref[...]).astype(BF16)
        acc = jnp.zeros((tm, D), F32)
        for c in range(Fd // fc):
            sl = slice(c * fc, (c + 1) * fc)
            a = _dot(h, wg_ref[sl, :], NT)
            b = _dot(h, wu_ref[sl, :], NT)
            a_ref[:, sl] = a.astype(BF16)
            b_ref[:, sl] = b.astype(BF16)
            z = (a * jax.nn.sigmoid(a) * b).astype(BF16)
            acc = acc + _dot(z, wd_ref[sl, :], NN)
        o_ref[...] = xv + 0.5 * acc
        finish()

    wspec = pl.BlockSpec((Fd, D), lambda i: (0, 0), pipeline_mode=pl.Buffered(1))
    extra = rider or _NO_RIDER
    return pl.pallas_call(
        body, name=name, grid=(S // tm,),
        in_specs=[pl.BlockSpec((tm, D), lambda i: (i, 0)), pl.BlockSpec((1, D), lambda i: (0, 0)), wspec, wspec,
                  wspec] + extra.in_specs,
        out_specs=[pl.BlockSpec((tm, D), lambda i: (i, 0)), pl.BlockSpec((tm, Fd), lambda i: (i, 0)),
                   pl.BlockSpec((tm, Fd), lambda i: (i, 0))] + extra.out_specs,
        out_shape=[jax.ShapeDtypeStruct((S, D), F32), jax.ShapeDtypeStruct((S, Fd), BF16),
                   jax.ShapeDtypeStruct((S, Fd), BF16)] + extra.out_shapes,
        scratch_shapes=extra.scratch,
        compiler_params=_cparams("arbitrary" if rider else "parallel"),
    )(x, _row(gain), wg_t, wu_t, wd, *extra.arrays)


def _ffn_bwd(g, x, gain, a, b, wg_t, wu_t, wd, name, rider=None):
    S, D = x.shape
    Fd = wd.shape[0]
    tm = _tile(S, TILES["ffn_bwd"])
    fc = Fd

    def body(*refs):
        i = pl.program_id(0)
        own, finish = _ride(rider, refs, 8, 5, i == 0, i == pl.num_programs(0) - 1)
        g_ref, x_ref, gain_ref, a_ref, b_ref, wg_ref, wu_ref, wd_ref, dx_ref, dz_ref, h_ref, dy_ref, dg_ref = own

        @pl.when(i == 0)
        def _():
            dg_ref[...] = jnp.zeros_like(dg_ref)

        gv = g_ref[...]
        xv = x_ref[...]
        r = lax.rsqrt(jnp.mean(xv * xv, axis=-1, keepdims=True) + NORM_EPS)
        xhat = xv * r
        h_ref[...] = (xhat * gain_ref[...]).astype(BF16)
        dy = (0.5 * gv).astype(BF16)
        dy_ref[...] = dy
        dh = jnp.zeros((tm, D), F32)
        for c in range(Fd // fc):
            sl = slice(c * fc, (c + 1) * fc)
            av = a_ref[:, sl].astype(F32)
            bv = b_ref[:, sl].astype(F32)
            dz = _dot(dy, wd_ref[sl, :], NT).astype(BF16)
            dz_ref[:, sl] = dz
            dzf = dz.astype(F32)
            silu, dsilu = _silu_parts(av)
            da = (dzf * bv * dsilu).astype(BF16)
            db = (dzf * silu).astype(BF16)
            dh = dh + _dot(da, wg_ref[sl, :], NN) + _dot(db, wu_ref[sl, :], NN)
        dg_ref[...] += jnp.sum(dh * xhat, axis=0, keepdims=True)
        dxhat = dh * gain_ref[...]
        dx_ref[...] = gv + r * (dxhat - xhat * jnp.mean(dxhat * xhat, axis=-1, keepdims=True))
        finish()

    wspec = pl.BlockSpec((Fd, D), lambda i: (0, 0), pipeline_mode=pl.Buffered(1))
    row = pl.BlockSpec((tm, D), lambda i: (i, 0))
    wide = pl.BlockSpec((tm, Fd), lambda i: (i, 0))
    extra = rider or _NO_RIDER
    return pl.pallas_call(
        body, name=name, grid=(S // tm,),
        in_specs=[row, row, pl.BlockSpec((1, D), lambda i: (0, 0)), wide, wide, wspec, wspec, wspec] + extra.in_specs,
        out_specs=[row, wide, row, row, pl.BlockSpec((1, D), lambda i: (0, 0))] + extra.out_specs,
        out_shape=[jax.ShapeDtypeStruct((S, D), F32), jax.ShapeDtypeStruct((S, Fd), BF16),
                   jax.ShapeDtypeStruct((S, D), BF16), jax.ShapeDtypeStruct((S, D), BF16),
                   jax.ShapeDtypeStruct((1, D), F32)] + extra.out_shapes,
        scratch_shapes=extra.scratch,
        compiler_params=_cparams("arbitrary"),
    )(g, x, _row(gain), a, b, wg_t, wu_t, wd, *extra.arrays)


def _ffn_dw(a, b, dz, h, dy, name):
    S, Fd = a.shape
    D = h.shape[1]
    ts = _tile(S, TILES["ffn_dw"])
    tf = _lane_tile(Fd, 256)

    def body(a_ref, b_ref, dz_ref, h_ref, dy_ref, og_ref, ou_ref, od_ref, accg, accu, accd):
        s = pl.program_id(1)

        @pl.when(s == 0)
        def _():
            accg[...] = jnp.zeros_like(accg)
            accu[...] = jnp.zeros_like(accu)
            accd[...] = jnp.zeros_like(accd)

        av = a_ref[...].astype(F32)
        bv = b_ref[...].astype(F32)
        dzf = dz_ref[...].astype(F32)
        silu, dsilu = _silu_parts(av)
        da = (dzf * bv * dsilu).astype(BF16)
        db = (dzf * silu).astype(BF16)
        z = (silu * bv).astype(BF16)
        hv = h_ref[...]
        accg[...] += _dot(da, hv, TN)
        accu[...] += _dot(db, hv, TN)
        accd[...] += _dot(z, dy_ref[...], TN)

        @pl.when(s == pl.num_programs(1) - 1)
        def _():
            og_ref[...] = accg[...].astype(BF16)
            ou_ref[...] = accu[...].astype(BF16)
            od_ref[...] = accd[...].astype(BF16)

    wide = pl.BlockSpec((ts, tf), lambda f, s: (s, f))
    row = pl.BlockSpec((ts, D), lambda f, s: (s, 0))
    out = pl.BlockSpec((tf, D), lambda f, s: (f, 0))
    return pl.pallas_call(
        body, name=name, grid=(Fd // tf, S // ts),
        in_specs=[wide, wide, wide, row, row], out_specs=[out, out, out],
        out_shape=[jax.ShapeDtypeStruct((Fd, D), BF16)] * 3,
        scratch_shapes=[pltpu.VMEM((tf, D), F32)] * 3,
        compiler_params=_cparams("parallel", "arbitrary"),
    )(a, b, dz, h, dy)


def _loss_head(x, target, gain, name):
    S, D = x.shape
    tm = _tile(S, TILES["ew"])

    def body(x_ref, t_ref, g_ref, dx_ref, dg_ref, loss_ref):
        @pl.when(pl.program_id(0) == 0)
        def _():
            dg_ref[...] = jnp.zeros_like(dg_ref)
            loss_ref[...] = jnp.zeros_like(loss_ref)

        xv = x_ref[...]
        r = lax.rsqrt(jnp.mean(xv * xv, axis=-1, keepdims=True) + NORM_EPS)
        xhat = xv * r
        e = xhat * g_ref[...] - t_ref[...]
        per_tok = jnp.mean(e * e, axis=-1, keepdims=True)
        loss_ref[...] += jnp.broadcast_to(0.5 * jnp.sum(per_tok, axis=0, keepdims=True), (1, LANE))
        dy = e * (1.0 / D)
        dg_ref[...] += jnp.sum(dy * xhat, axis=0, keepdims=True)
        dxhat = dy * g_ref[...]
        dx_ref[...] = r * (dxhat - xhat * jnp.mean(dxhat * xhat, axis=-1, keepdims=True))

    row = pl.BlockSpec((tm, D), lambda i: (i, 0))
    return pl.pallas_call(
        body, name=name, grid=(S // tm,),
        in_specs=[row, row, pl.BlockSpec((1, D), lambda i: (0, 0))],
        out_specs=[row, pl.BlockSpec((1, D), lambda i: (0, 0)), pl.BlockSpec((1, LANE), lambda i: (0, 0))],
        out_shape=[jax.ShapeDtypeStruct((S, D), F32), jax.ShapeDtypeStruct((1, D), F32),
                   jax.ShapeDtypeStruct((1, LANE), F32)],
        compiler_params=_cparams("arbitrary"),
    )(x, target, _row(gain))


def _shift_down(u, halo, k, rows):
    out = pltpu.roll(u, k, 0)
    for j in range(k):
        out = jnp.where(rows == j, halo[8 - k + j:8 - k + j + 1, :], out)
    return out


def _shift_up(u, halo, k, rows, n):
    out = pltpu.roll(u, n - k, 0)
    for j in range(k):
        out = jnp.where(rows == n - k + j, halo[j:j + 1, :], out)
    return out


def _conv_fwd(p, cw, name):
    S, W3 = p.shape
    W = W3 // 3
    tm = _tile(S, TILES["ew"])
    hb = tm // 8

    def body(p_ref, ph_ref, w_ref, v_ref):
        i = pl.program_id(0)
        bg = p_ref[:, 0:W].astype(F32)
        u = p_ref[:, W:2 * W].astype(F32) * p_ref[:, 2 * W:3 * W].astype(F32)
        uh = ph_ref[:, W:2 * W].astype(F32) * ph_ref[:, 2 * W:3 * W].astype(F32)
        uh = jnp.where(i > 0, uh, 0.0)
        rows = lax.broadcasted_iota(jnp.int32, (tm, 1), 0)
        u1 = _shift_down(u, uh, 1, rows)
        u2 = _shift_down(u, uh, 2, rows)
        y = w_ref[0:1, :] * u2 + w_ref[1:2, :] * u1 + w_ref[2:3, :] * u
        v_ref[...] = (bg * y).astype(BF16)

    return pl.pallas_call(
        body, name=name, grid=(S // tm,),
        in_specs=[pl.BlockSpec((tm, W3), lambda i: (i, 0)),
                  pl.BlockSpec((8, W3), lambda i: (jnp.maximum(i * hb - 1, 0), 0)),
                  pl.BlockSpec((8, W), lambda i: (0, 0))],
        out_specs=pl.BlockSpec((tm, W), lambda i: (i, 0)),
        out_shape=jax.ShapeDtypeStruct((S, W), BF16),
        compiler_params=_cparams("parallel"),
    )(p, p, cw)


def _conv_bwd(dv, p, cw, name):
    S, W3 = p.shape
    W = W3 // 3
    tm = _tile(S, TILES["ew"])
    hb = tm // 8
    last = S // 8 - 1

    def body(dv_ref, dvn_ref, p_ref, pp_ref, pn_ref, w_ref, dp_ref, dw_ref):
        i = pl.program_id(0)
        n = pl.num_programs(0)

        @pl.when(i == 0)
        def _():
            dw_ref[...] = jnp.zeros_like(dw_ref)

        bg = p_ref[:, 0:W].astype(F32)
        cg = p_ref[:, W:2 * W].astype(F32)
        zz = p_ref[:, 2 * W:3 * W].astype(F32)
        u = cg * zz
        uh = pp_ref[:, W:2 * W].astype(F32) * pp_ref[:, 2 * W:3 * W].astype(F32)
        uh = jnp.where(i > 0, uh, 0.0)
        rows = lax.broadcasted_iota(jnp.int32, (tm, 1), 0)
        u1 = _shift_down(u, uh, 1, rows)
        u2 = _shift_down(u, uh, 2, rows)
        w0, w1, w2 = w_ref[0:1, :], w_ref[1:2, :], w_ref[2:3, :]
        y = w0 * u2 + w1 * u1 + w2 * u
        dvv = dv_ref[...].astype(F32)
        dy = dvv * bg
        dyh = dvn_ref[...].astype(F32) * pn_ref[:, 0:W].astype(F32)
        dyh = jnp.where(i < n - 1, dyh, 0.0)
        d1 = _shift_up(dy, dyh, 1, rows, tm)
        d2 = _shift_up(dy, dyh, 2, rows, tm)
        du = w2 * dy + w1 * d1 + w0 * d2
        dp_ref[:, 0:W] = (dvv * y).astype(BF16)
        dp_ref[:, W:2 * W] = (du * zz).astype(BF16)
        dp_ref[:, 2 * W:3 * W] = (du * cg).astype(BF16)
        dw_ref[0:1, :] += jnp.sum(dy * u2, axis=0, keepdims=True)
        dw_ref[1:2, :] += jnp.sum(dy * u1, axis=0, keepdims=True)
        dw_ref[2:3, :] += jnp.sum(dy * u, axis=0, keepdims=True)

    return pl.pallas_call(
        body, name=name, grid=(S // tm,),
        in_specs=[pl.BlockSpec((tm, W), lambda i: (i, 0)),
                  pl.BlockSpec((8, W), lambda i: (jnp.minimum((i + 1) * hb, last), 0)),
                  pl.BlockSpec((tm, W3), lambda i: (i, 0)),
                  pl.BlockSpec((8, W3), lambda i: (jnp.maximum(i * hb - 1, 0), 0)),
                  pl.BlockSpec((8, W3), lambda i: (jnp.minimum((i + 1) * hb, last), 0)),
                  pl.BlockSpec((8, W), lambda i: (0, 0))],
        out_specs=[pl.BlockSpec((tm, W3), lambda i: (i, 0)), pl.BlockSpec((8, W), lambda i: (0, 0))],
        out_shape=[jax.ShapeDtypeStruct((S, W3), BF16), jax.ShapeDtypeStruct((8, W), F32)],
        compiler_params=_cparams("arbitrary"),
    )(dv, dv, p, p, p, cw)


def _rope_swap(r, lane):
    mid = NOPE + ROPE // 2
    first = (lane >= NOPE) & (lane < mid)
    second = (lane >= mid) & (lane < QK_DIM)
    return jnp.where(first, pltpu.roll(r, HEAD_PAD - ROPE // 2, 1), jnp.where(second, pltpu.roll(r, ROPE // 2, 1), 0.0))


def _rope_fwd(q_big, kv_big, proj, kr_col, ct, st, name):
    S = q_big.shape[0]
    HW = HEADS * HEAD_PAD
    tm = _tile(S, TILES["ew"])

    def body(q_ref, k_ref, v_ref, kr_ref, ct_ref, st_ref, qo_ref, ko_ref, vo_ref):
        lane = lax.broadcasted_iota(jnp.int32, (1, HEAD_PAD), 1)
        ctv, stv = ct_ref[...], st_ref[...]
        krr = pltpu.roll(kr_ref[...].astype(F32), NOPE, 1)
        kro = krr * ctv + _rope_swap(krr, lane) * stv
        for h in range(HEADS):
            sl = slice(h * HEAD_PAD, (h + 1) * HEAD_PAD)
            qh = q_ref[:, sl].astype(F32)
            qo_ref[:, sl] = (qh * ctv + _rope_swap(qh, lane) * stv).astype(BF16)
            ko_ref[:, sl] = (k_ref[:, sl].astype(F32) + kro).astype(BF16)
            vo_ref[:, sl] = jnp.where(lane == VDIM, 1.0, v_ref[:, sl].astype(F32)).astype(BF16)

    wide = pl.BlockSpec((tm, HW), lambda i: (i, 0))
    narrow = pl.BlockSpec((tm, HEAD_PAD), lambda i: (i, 0))
    return pl.pallas_call(
        body, name=name, grid=(S // tm,),
        in_specs=[wide, wide, pl.BlockSpec((tm, HW), lambda i: (i, 1)),
                  pl.BlockSpec((tm, HEAD_PAD), lambda i: (i, kr_col)), narrow, narrow],
        out_specs=[wide, wide, wide],
        out_shape=[jax.ShapeDtypeStruct((S, HW), BF16)] * 3,
        compiler_params=_cparams("parallel"),
    )(q_big, kv_big, kv_big, proj, ct, st)


def _rope_bwd(dq, dk, dv, ct, st, name):
    S = dq.shape[0]
    HW = HEADS * HEAD_PAD
    tm = _tile(S, TILES["ew"])

    def body(dq_ref, dk_ref, dv_ref, ct_ref, st_ref, oq_ref, okv_ref, okr_ref):
        lane = lax.broadcasted_iota(jnp.int32, (1, HEAD_PAD), 1)
        ctv, stv = ct_ref[...], st_ref[...]
        acc = jnp.zeros((tm, HEAD_PAD), F32)
        for h in range(HEADS):
            sl = slice(h * HEAD_PAD, (h + 1) * HEAD_PAD)
            d = dq_ref[:, sl].astype(F32)
            oq_ref[:, sl] = (d * ctv + _rope_swap(d * stv, lane)).astype(BF16)
            d = dk_ref[:, sl].astype(F32)
            okv_ref[:, sl] = jnp.where(lane < NOPE, d, 0.0).astype(BF16)
            acc = acc + jnp.where(lane >= NOPE, d * ctv + _rope_swap(d * stv, lane), 0.0)
        okv_ref[:, HW:2 * HW] = dv_ref[...].astype(BF16)
        okr_ref[...] = pltpu.roll(acc, HEAD_PAD - NOPE, 1).astype(BF16)

    wide = pl.BlockSpec((tm, HW), lambda i: (i, 0))
    narrow = pl.BlockSpec((tm, HEAD_PAD), lambda i: (i, 0))
    return pl.pallas_call(
        body, name=name, grid=(S // tm,),
        in_specs=[wide, wide, wide, narrow, narrow],
        out_specs=[wide, pl.BlockSpec((tm, 2 * HW), lambda i: (i, 0)), narrow],
        out_shape=[jax.ShapeDtypeStruct((S, HW), BF16), jax.ShapeDtypeStruct((S, 2 * HW), BF16),
                   jax.ShapeDtypeStruct((S, HEAD_PAD), BF16)],
        compiler_params=_cparams("parallel"),
    )(dq, dk, dv, ct, st)


def _pairs(n, by_key):
    if by_key:
        pr = [(i, j) for j in range(n) for i in range(j, n)]
    else:
        pr = [(i, j) for i in range(n) for j in range(i + 1)]
    qi = np.array([p[0] for p in pr], np.int32)
    kj = np.array([p[1] for p in pr], np.int32)
    return jnp.asarray(qi), jnp.asarray(kj)


_LOG2E = 1.4426950408889634
_LN2 = 0.6931471805599453


def _tile_mask(t):
    return lax.broadcasted_iota(jnp.int32, (t, t), 1) <= lax.broadcasted_iota(jnp.int32, (t, t), 0)


def _attn_fwd(q, k, v, name):
    S = q.shape[0]
    HW = HEADS * HEAD_PAD
    t = _tile(S, TILES["attn_fwd"])
    n = S // t
    qi, kj = _pairs(n, by_key=False)
    c = (QK_DIM ** -0.5) * _LOG2E

    def body(qi_ref, kj_ref, q_ref, k_ref, v_ref, o_ref, lse_ref, m_s, acc_s):
        p_id = pl.program_id(1)
        i, j = qi_ref[p_id], kj_ref[p_id]

        @pl.when(j == 0)
        def _():
            m_s[...] = jnp.full_like(m_s, -jnp.inf)
            acc_s[...] = jnp.zeros_like(acc_s)

        def step(on_diagonal):
            s = _dot(q_ref[...], k_ref[...], NT)
            if on_diagonal:
                s = jnp.where(_tile_mask(t), s, -jnp.inf)
            m_old = m_s[...]
            m_new = jnp.maximum(m_old, jnp.max(s, axis=-1, keepdims=True))
            p = jnp.exp2((s - m_new) * c).astype(BF16)
            acc_s[...] = jnp.exp2((m_old - m_new) * c) * acc_s[...] + _dot(p, v_ref[...], NN)
            m_s[...] = m_new

        @pl.when(i == j)
        def _():
            step(True)

        @pl.when(i != j)
        def _():
            step(False)

        @pl.when(j == i)
        def _():
            acc = acc_s[...]
            l = acc[:, VDIM:VDIM + 1]
            o_ref[...] = (acc * (1.0 / l)).astype(BF16)
            lse_ref[...] = jnp.broadcast_to(m_s[...] * c + jnp.log2(l), (t, HEAD_PAD))

    qspec = pl.BlockSpec((t, HEAD_PAD), lambda h, p, qi, kj: (qi[p], h))
    kspec = pl.BlockSpec((t, HEAD_PAD), lambda h, p, qi, kj: (kj[p], h))
    grid_spec = pltpu.PrefetchScalarGridSpec(
        num_scalar_prefetch=2, grid=(HEADS, int(qi.shape[0])),
        in_specs=[qspec, kspec, kspec], out_specs=[qspec, qspec],
        scratch_shapes=[pltpu.VMEM((t, 1), F32), pltpu.VMEM((t, HEAD_PAD), F32)])
    return pl.pallas_call(
        body, name=name, grid_spec=grid_spec,
        out_shape=[jax.ShapeDtypeStruct((S, HW), BF16), jax.ShapeDtypeStruct((S, HW), F32)],
        compiler_params=_cparams("parallel", "arbitrary"),
    )(qi, kj, q, k, v)


def _attn_bwd(q, k, v, o, do, lse2, name):
    S = q.shape[0]
    HW = HEADS * HEAD_PAD
    t = _tile(S, TILES["attn_bwd"])
    n = S // t
    qi, kj = _pairs(n, by_key=True)
    scale = QK_DIM ** -0.5

    def body(qi_ref, kj_ref, q_ref, k_ref, v_ref, o_ref, do_ref, lse_ref, dq_ref, dk_ref, dv_ref, dk_s, dv_s):
        p_id = pl.program_id(1)
        i, j = qi_ref[p_id], kj_ref[p_id]

        @pl.when(p_id == 0)
        def _():
            dq_ref[...] = jnp.zeros_like(dq_ref)

        @pl.when(i == j)
        def _():
            dk_s[...] = jnp.zeros_like(dk_s)
            dv_s[...] = jnp.zeros_like(dv_s)

        def step(on_diagonal):
            qv, kv, vv = q_ref[...], k_ref[...], v_ref[...]
            dov = do_ref[...]
            p = jnp.exp(_dot(qv, kv, NT) * scale - lse_ref[:, 0:1] * _LN2)
            if on_diagonal:
                p = jnp.where(_tile_mask(t), p, 0.0)
            delta = jnp.sum(dov.astype(F32) * o_ref[...].astype(F32), axis=-1, keepdims=True)
            dv_s[...] += _dot(p.astype(BF16), dov, TN)
            ds = (p * (_dot(dov, vv, NT) - delta) * scale).astype(BF16)
            dk_s[...] += _dot(ds, qv, TN)
            rows = pl.ds(pl.multiple_of(i * t, t), t)
            dq_ref[rows, :] += _dot(ds, kv, NN)

        @pl.when(i == j)
        def _():
            step(True)

        @pl.when(i != j)
        def _():
            step(False)

        @pl.when(i == n - 1)
        def _():
            dk_ref[...] = dk_s[...]
            dv_ref[...] = dv_s[...]

    qspec = pl.BlockSpec((t, HEAD_PAD), lambda h, p, qi, kj: (qi[p], h))
    kspec = pl.BlockSpec((t, HEAD_PAD), lambda h, p, qi, kj: (kj[p], h))
    grid_spec = pltpu.PrefetchScalarGridSpec(
        num_scalar_prefetch=2, grid=(HEADS, int(qi.shape[0])),
        in_specs=[qspec, kspec, kspec, qspec, qspec, qspec],
        out_specs=[pl.BlockSpec((S, HEAD_PAD), lambda h, p, qi, kj: (0, h)), kspec, kspec],
        scratch_shapes=[pltpu.VMEM((t, HEAD_PAD), F32), pltpu.VMEM((t, HEAD_PAD), F32)])
    return pl.pallas_call(
        body, name=name, grid_spec=grid_spec,
        out_shape=[jax.ShapeDtypeStruct((S, HW), F32)] * 3,
        compiler_params=_cparams("parallel", "arbitrary"),
    )(qi, kj, q, k, v, o, do, lse2)


_SQRT_HALF = 0.7071067811865476
_INV_SQRT_2PI = 0.3989422804014327


def _sg_select(r, grp):
    out = jnp.where(grp == 0, r[0:SG_CHUNK, :], 0.0)
    for g in range(1, SG_GROUPS):
        out = out + jnp.where(grp == g, r[g * SG_CHUNK:(g + 1) * SG_CHUNK, :], 0.0)
    return out


def _sgu_fwd(proj, gain, wstack, bmat, name):
    S = proj.shape[0]
    W = SG_WIDTH
    tm = _tile(S, TILES["sgu"])

    def body(z_ref, g_ref, w_ref, b_ref, o_ref):
        z = z_ref[...].astype(F32)
        zg = 0.5 * z * (1.0 + lax.erf(z * _SQRT_HALF))
        u, vv = zg[:, 0:W], zg[:, W:2 * W]
        r = lax.rsqrt(jnp.mean(vv * vv, axis=-1, keepdims=True) + NORM_EPS)
        vn = (vv * r * g_ref[...]).astype(BF16)
        grp = lax.broadcasted_iota(jnp.int32, (1, W), 1) // SG_GROUP_DIM
        for c in range(tm // SG_CHUNK):
            sl = slice(c * SG_CHUNK, (c + 1) * SG_CHUNK)
            mixed = _sg_select(_dot(w_ref[...], vn[sl, :], NN), grp) + b_ref[...]
            o_ref[sl, :] = (u[sl, :] * mixed).astype(BF16)

    return pl.pallas_call(
        body, name=name, grid=(S // tm,),
        in_specs=[pl.BlockSpec((tm, 2 * W), lambda i: (i, 0)), pl.BlockSpec((1, W), lambda i: (0, 0)),
                  pl.BlockSpec(wstack.shape, lambda i: (0, 0)), pl.BlockSpec(bmat.shape, lambda i: (0, 0))],
        out_specs=pl.BlockSpec((tm, W), lambda i: (i, 0)),
        out_shape=jax.ShapeDtypeStruct((S, W), BF16),
        compiler_params=_cparams("parallel"),
    )(proj, _row(gain), wstack, bmat)


def _sgu_bwd(dsg, proj, gain, wstack, wtstack, bmat, gsum, name):
    S = proj.shape[0]
    W = SG_WIDTH
    tm = _tile(S, TILES["sgu"])
    GS = SG_GROUPS * SG_CHUNK

    def body(d_ref, z_ref, g_ref, w_ref, wt_ref, b_ref, e_ref, dz_ref, dw_ref, db_ref, dg_ref, dw_s, db_s):
        i = pl.program_id(0)

        @pl.when(i == 0)
        def _():
            dw_s[...] = jnp.zeros_like(dw_s)
            db_s[...] = jnp.zeros_like(db_s)
            dg_ref[...] = jnp.zeros_like(dg_ref)

        z = z_ref[...].astype(F32)
        cdf = 0.5 * (1.0 + lax.erf(z * _SQRT_HALF))
        zg = z * cdf
        u, vv = zg[:, 0:W], zg[:, W:2 * W]
        r = lax.rsqrt(jnp.mean(vv * vv, axis=-1, keepdims=True) + NORM_EPS)
        vhat = vv * r
        vn = (vhat * g_ref[...]).astype(BF16)
        grp = lax.broadcasted_iota(jnp.int32, (1, W), 1) // SG_GROUP_DIM
        d = d_ref[...].astype(F32)
        du_parts, dvn_parts = [], []
        for c in range(tm // SG_CHUNK):
            sl = slice(c * SG_CHUNK, (c + 1) * SG_CHUNK)
            vc = vn[sl, :]
            mixed = _sg_select(_dot(w_ref[...], vc, NN), grp) + b_ref[...]
            dc = d[sl, :]
            du_parts.append(dc * mixed)
            dmix = dc * u[sl, :]
            db_s[...] += dmix
            dmb = dmix.astype(BF16)
            dvn_parts.append(_sg_select(_dot(wt_ref[...], dmb, NN), grp))
            astack = jnp.concatenate([jnp.where(grp == g, dmb, jnp.zeros_like(dmb)) for g in range(SG_GROUPS)], axis=0)
            dw_s[...] += _dot(astack, vc, NT)
        du = jnp.concatenate(du_parts, axis=0)
        dvn = jnp.concatenate(dvn_parts, axis=0)
        dg_ref[...] += jnp.sum(dvn * vhat, axis=0, keepdims=True)
        dvhat = dvn * g_ref[...]
        dvv = r * (dvhat - vhat * jnp.mean(dvhat * vhat, axis=-1, keepdims=True))
        dgelu = cdf + z * (_INV_SQRT_2PI * jnp.exp(-0.5 * z * z))
        dz_ref[:, 0:W] = (du * dgelu[:, 0:W]).astype(BF16)
        dz_ref[:, W:2 * W] = (dvv * dgelu[:, W:2 * W]).astype(BF16)

        @pl.when(i == pl.num_programs(0) - 1)
        def _():
            dw_ref[...] = dw_s[...]
            db_ref[...] = lax.dot_general(db_s[...], e_ref[...], NN, precision=lax.Precision.HIGHEST,
                                          preferred_element_type=F32)

    full = lambda a: pl.BlockSpec(a.shape, lambda i: (0, 0))
    return pl.pallas_call(
        body, name=name, grid=(S // tm,),
        in_specs=[pl.BlockSpec((tm, W), lambda i: (i, 0)), pl.BlockSpec((tm, 2 * W), lambda i: (i, 0)),
                  pl.BlockSpec((1, W), lambda i: (0, 0)), full(wstack), full(wtstack), full(bmat), full(gsum)],
        out_specs=[pl.BlockSpec((tm, 2 * W), lambda i: (i, 0)), pl.BlockSpec((GS, SG_CHUNK), lambda i: (0, 0)),
                   pl.BlockSpec((SG_CHUNK, LANE), lambda i: (0, 0)), pl.BlockSpec((1, W), lambda i: (0, 0))],
        out_shape=[jax.ShapeDtypeStruct((S, 2 * W), BF16), jax.ShapeDtypeStruct((GS, SG_CHUNK), F32),
                   jax.ShapeDtypeStruct((SG_CHUNK, LANE), F32), jax.ShapeDtypeStruct((1, W), F32)],
        scratch_shapes=[pltpu.VMEM((GS, SG_CHUNK), F32), pltpu.VMEM((SG_CHUNK, W), F32)],
        compiler_params=_cparams("arbitrary"),
    )(dsg, proj, _row(gain), wstack, wtstack, bmat, gsum)


WEIGHTS = ['ffn_pre_norm', 'ffn_pre_w_gate', 'ffn_pre_w_up', 'ffn_pre_w_down', 'mix_norm', 'ffn_post_norm',
           'ffn_post_w_gate', 'ffn_post_w_up', 'ffn_post_w_down', 'even_w_in', 'q_norm', 'w_uq', 'kv_norm', 'w_ukv',
           'sg_norm', 'sg_w', 'sg_b', 'even_w_out', 'conv_w_in', 'conv_w', 'conv_w_out', 'final_norm']
SHARD_AXIS = dict(ffn_pre_w_gate=2, ffn_pre_w_up=2, ffn_pre_w_down=1, ffn_post_w_gate=2, ffn_post_w_up=2,
                  ffn_post_w_down=1, even_w_in=2, w_uq=2, w_ukv=2, even_w_out=1, conv_w_in=2, conv_w=2, conv_w_out=1)
SHARDED = [n for n in WEIGHTS if n in SHARD_AXIS]
REPLICATED = [n for n in WEIGHTS if n not in SHARD_AXIS]


def _to_t(name, w):
    return jnp.swapaxes(w, 1, 2) if SHARD_AXIS[name] == 2 else w


def _rows_of(n):
    return -(-n // PACK_W)


def _pad_rows(a, mult, axis):
    r = a.shape[axis]
    extra = (-r) % mult
    if extra == 0:
        return a
    pad = [(0, 0)] * a.ndim
    pad[axis] = (0, extra)
    return jnp.pad(a, pad)


def _flat_rows(a, lead):
    flat = a.reshape(a.shape[:lead] + (-1,))
    n = flat.shape[-1]
    flat = _pad_rows(flat, PACK_W, lead)
    return flat.reshape(a.shape[:lead] + (_rows_of(n), PACK_W))


def _pack(pieces, lead, mult, piece_mult=1):
    rows, offs, off = [], [], 0
    for p in pieces:
        r = _pad_rows(_flat_rows(p, lead), piece_mult, lead)
        rows.append(r)
        offs.append(off)
        off += r.shape[lead]
    return _pad_rows(jnp.concatenate(rows, axis=lead), mult, lead), offs


def _unpack(buf, off, shape, lead):
    n = math.prod(shape)
    r = _rows_of(n)
    piece = lax.slice_in_dim(buf, off, off + r, axis=lead)
    piece = piece.reshape(buf.shape[:lead] + (r * PACK_W,))
    piece = lax.slice_in_dim(piece, 0, n, axis=lead)
    return piece.reshape(buf.shape[:lead] + tuple(shape))


def _head_pad(w, per_head, keep):
    k = w.shape[-1]
    w = w.reshape(HEADS, per_head, k)[:, keep[0]:keep[1]]
    w = jnp.pad(w, ((0, 0), (0, HEAD_PAD - (keep[1] - keep[0])), (0, 0)))
    return w.reshape(HEADS * HEAD_PAD, k)


def _head_unpad(w, n):
    return w.reshape(HEADS, HEAD_PAD, w.shape[-1])[:, :n]


def kernel(x, positions, ffn_pre_norm, ffn_pre_w_gate, ffn_pre_w_up, ffn_pre_w_down, mix_norm, ffn_post_norm, ffn_post_w_gate, ffn_post_w_up, ffn_post_w_down, even_w_in, q_norm, w_uq, kv_norm, w_ukv, sg_norm, sg_w, sg_b, even_w_out, conv_w_in, conv_w, conv_w_out, final_norm, loss_target, m_ffn_pre_norm, m_ffn_pre_w_gate, m_ffn_pre_w_up, m_ffn_pre_w_down, m_mix_norm, m_ffn_post_norm, m_ffn_post_w_gate, m_ffn_post_w_up, m_ffn_post_w_down, m_even_w_in, m_q_norm, m_w_uq, m_kv_norm, m_w_ukv, m_sg_norm, m_sg_w, m_sg_b, m_even_w_out, m_conv_w_in, m_conv_w, m_conv_w_out, m_final_norm, v_ffn_pre_norm, v_ffn_pre_w_gate, v_ffn_pre_w_up, v_ffn_pre_w_down, v_mix_norm, v_ffn_post_norm, v_ffn_post_w_gate, v_ffn_post_w_up, v_ffn_post_w_down, v_even_w_in, v_q_norm, v_w_uq, v_kv_norm, v_w_ukv, v_sg_norm, v_sg_w, v_sg_b, v_even_w_out, v_conv_w_in, v_conv_w, v_conv_w_out, v_final_norm):
    given = dict(locals())
    w_loc = {n: given[n] for n in WEIGHTS}
    m_loc = {n: given["m_" + n] for n in WEIGHTS}
    v_loc = {n: given["v_" + n] for n in WEIGHTS}

    S, D = x.shape[1], x.shape[2]
    depth = ffn_pre_norm.shape[0]
    QL, KVL = q_norm.shape[1], kv_norm.shape[1]
    ZW = 2 * SG_WIDTH
    assert x.shape[0] == 1 and ZW % KVL == 0 and (ZW + KVL) % HEAD_PAD == 0 and (ZW + KVL + 2 * HEAD_PAD) % QL == 0
    col_ckv = ZW // KVL
    col_kr = (ZW + KVL) // HEAD_PAD
    col_cq = (ZW + KVL + 2 * HEAD_PAD) // QL

    t_loc = {n: _to_t(n, w_loc[n]) for n in SHARDED}
    full = {n: {} for n in SHARDED}

    def ffn_keys(kind, l):
        return [("ffn_%s_w_%s" % (kind, part), l) for part in ("gate", "up", "down")]

    def mixer_keys(l):
        names = ("even_w_in", "w_uq", "w_ukv", "even_w_out") if l % 2 == 0 else ("conv_w_in", "conv_w", "conv_w_out")
        return [(n, l // 2) for n in names]

    def local_pack(keys):
        return _pack([t_loc[n][l].astype(BF16) for n, l in keys], 0, 16, piece_mult=16)

    def take_gathered(gathered, keys, offs):
        for (n, l), off in zip(keys, offs):
            piece = _unpack(gathered, off, t_loc[n].shape[1:], 1)
            full[n][l] = piece.reshape(N_DEV * piece.shape[1], piece.shape[2])

    def gather_rider(keys):
        pack, offs = local_pack(keys)
        return _Exchange("gather", [pack]), offs

    first_keys = ffn_keys("pre", 0)
    pack0, offs0 = local_pack(first_keys)
    take_gathered(_all_gather(pack0, "gather_weights"), first_keys, offs0)

    tril = jnp.tril(jnp.ones((SG_CHUNK, SG_CHUNK), F32))
    even_ops = {}

    def even_operands(e):
        if e not in even_ops:
            wi = full["even_w_in"][e]
            zrow = lambda k: jnp.zeros((k, D), BF16)
            ops = dict(win_pad=jnp.concatenate(
                [wi[QL + KVL + ROPE:], wi[QL:QL + KVL], wi[QL + KVL:QL + KVL + ROPE], zrow(HEAD_PAD - ROPE),
                 zrow(HEAD_PAD), wi[:QL]], axis=0))
            ops["wq_big"] = _head_pad(full["w_uq"][e], QK_DIM, (0, QK_DIM))
            wkv = full["w_ukv"][e]
            ops["wkv_big"] = jnp.concatenate([_head_pad(wkv, NOPE + VDIM, (0, NOPE)),
                                              _head_pad(wkv, NOPE + VDIM, (NOPE, NOPE + VDIM))], axis=0)
            wo = full["even_w_out"][e]
            ops["wo_attn"] = _head_pad(wo[:HEADS * VDIM], VDIM, (0, VDIM))
            ops["wo_sg"] = wo[HEADS * VDIM:]
            wt = sg_w[e] * tril
            ops["wstack"] = wt.reshape(SG_GROUPS * SG_CHUNK, SG_CHUNK).astype(BF16)
            ops["wtstack"] = jnp.swapaxes(wt, 1, 2).reshape(SG_GROUPS * SG_CHUNK, SG_CHUNK).astype(BF16)
            ops["bmat"] = jnp.repeat(sg_b[e].T, SG_GROUP_DIM, axis=1)
            even_ops[e] = ops
        return even_ops[e]

    gsum = (jnp.arange(SG_WIDTH)[:, None] // SG_GROUP_DIM == jnp.arange(LANE)[None, :]).astype(F32)

    def conv_taps(o):
        return jnp.pad(jnp.swapaxes(full["conv_w"][o], 0, 1).astype(F32), ((0, 8 - CONV_K), (0, 0)))

    inv_freq = ROPE_THETA ** (-jnp.arange(0, ROPE, 2, dtype=F32) / ROPE)
    ang = positions[0].astype(F32)[:, None] * inv_freq
    cos, sin = jnp.cos(ang), jnp.sin(ang)
    ones, zeros = jnp.ones((S, NOPE), F32), jnp.zeros((S, HEAD_PAD - QK_DIM), F32)
    ct = jnp.concatenate([ones, cos, cos, zeros], axis=1)
    st = jnp.concatenate([0.0 * ones, -sin, sin, zeros], axis=1)

    xs = x[0]
    saved = []
    def ffn_forward(xin, kind, l, next_keys):
        gain = (ffn_pre_norm if kind == "pre" else ffn_post_norm)[l]
        wg, wu, wd = (full[n][l] for n, _ in ffn_keys(kind, l))
        if not next_keys:
            return _ffn_fwd(xin, gain, wg, wu, wd, "ffn_fwd")
        rider, offs = gather_rider(next_keys)
        xo, a, b, gathered = _ffn_fwd(xin, gain, wg, wu, wd, "ffn_fwd_gather", rider=rider)
        take_gathered(gathered, next_keys, offs)
        return xo, a, b

    for l in range(depth):
        sv = dict(x0=xs)
        x1, sv["a1"], sv["b1"] = ffn_forward(xs, "pre", l, (mixer_keys(0) if l == 0 else []) + ffn_keys("post", l))
        sv["x1"] = x1
        if l % 2 == 0:
            e = l // 2
            ops = even_operands(e)
            sv["h"], proj = _norm_mm(x1, mix_norm[l], ops["win_pad"], "even_in_proj", F32)
            qn = _rms_fwd(proj, q_norm[e], "q_norm_fwd", col=col_cq, width=QL)
            kvn = _rms_fwd(proj, kv_norm[e], "kv_norm_fwd", col=col_ckv, width=KVL)
            q_big = _mm(qn, ops["wq_big"], "nt", "q_up_proj", out_dtype=F32)
            kv_big = _mm(kvn, ops["wkv_big"], "nt", "kv_up_proj", out_dtype=BF16)
            q_r, k_r, v_r = _rope_fwd(q_big, kv_big, proj, col_kr, ct, st, "rope_fwd")
            o_att, lse = _attn_fwd(q_r, k_r, v_r, "attn_fwd")
            sg = _sgu_fwd(proj, sg_norm[e], ops["wstack"], ops["bmat"], "sgu_fwd")
            tmp = _mm(o_att, ops["wo_attn"], "nn", "even_out_attn", out_dtype=F32, res=x1)
            x2 = _mm(sg, ops["wo_sg"], "nn", "even_out_sg", out_dtype=F32, res=tmp)
            sv.update(proj=proj, qn=qn, kvn=kvn, q=q_r, k=k_r, v=v_r, o=o_att, lse=lse, sg=sg)
        else:
            o = l // 2
            sv["h"], p = _norm_mm(x1, mix_norm[l], full["conv_w_in"][o], "conv_in_proj", BF16)
            cv = _conv_fwd(p, conv_taps(o), "conv_fwd")
            x2 = _mm(cv, full["conv_w_out"][o], "nn", "conv_out_proj", out_dtype=F32, res=x1)
            sv.update(p=p, cv=cv)
        sv["x2"] = x2
        next_keys = ffn_keys("pre", l + 1) + mixer_keys(l + 1) if l + 1 < depth else []
        xs, sv["a2"], sv["b2"] = ffn_forward(x2, "post", l, next_keys)
        saved.append(sv)

    gr = {n: [None] * w_loc[n].shape[0] for n in REPLICATED if n != "final_norm"}
    per_layer = {n: [None] * w_loc[n].shape[0] for n in SHARDED}
    pending = []

    def scatter_rider():
        pieces, where, off = [], [], 0
        for n, l, g in pending:
            piece = _pad_rows(_flat_rows(g.astype(BF16).reshape(N_DEV, -1), 1), 16, 1)
            pieces.append(piece)
            where.append((n, l, off))
            off += piece.shape[1]
        if off % GRAD_ROWS_MULT:
            pieces.append(jnp.zeros((N_DEV, (-off) % GRAD_ROWS_MULT, PACK_W), BF16))
        pending.clear()
        return _Exchange("scatter", pieces), where

    def take_scattered(received, where):
        owned = _sum_slots(received, "sum_grad_shards")
        for n, l, off in where:
            per_layer[n][l] = _unpack(owned, off, t_loc[n].shape[1:], 0)

    def ffn_backward(dxin, xin, kind, l, a, b):
        gain = (ffn_pre_norm if kind == "pre" else ffn_post_norm)[l]
        keys = ffn_keys(kind, l)
        wg, wu, wd = (full[n][l] for n, _ in keys)
        if pending:
            rider, where = scatter_rider()
            dxo, dz, hh, dy, dgain, received = _ffn_bwd(dxin, xin, gain, a, b, wg, wu, wd, "ffn_bwd_scatter", rider=rider)
            take_scattered(received, where)
        else:
            dxo, dz, hh, dy, dgain = _ffn_bwd(dxin, xin, gain, a, b, wg, wu, wd, "ffn_bwd")
        gr["ffn_%s_norm" % kind][l] = dgain[0]
        for (n, _), g in zip(keys, _ffn_dw(a, b, dz, hh, dy, "ffn_dw")):
            pending.append((n, l, g))
        return dxo

    dx, g_final, loss_part = _loss_head(xs, loss_target[0], final_norm, "loss_head")
    for l in reversed(range(depth)):
        sv = saved[l]
        dx = ffn_backward(dx, sv["x2"], "post", l, sv["a2"], sv["b2"])
        h = sv["h"]
        if l % 2 == 0:
            e = l // 2
            ops = even_operands(e)
            d_o = _mm(dx, ops["wo_attn"], "nt", "even_out_attn_bwd", out_dtype=BF16)
            d_sg = _mm(dx, ops["wo_sg"], "nt", "even_out_sg_bwd", out_dtype=BF16)
            g_wo_attn = _mm(sv["o"], dx, "tn", "even_out_attn_dw", out_dtype=F32)
            g_wo_sg = _mm(sv["sg"], dx, "tn", "even_out_sg_dw", out_dtype=F32)
            dq, dk, dv = _attn_bwd(sv["q"], sv["k"], sv["v"], sv["o"], d_o, sv["lse"], "attn_bwd")
            dq_big, dkv_big, dkr = _rope_bwd(dq, dk, dv, ct, st, "rope_bwd")
            dz_sg, g_wstack, g_bias, g_sgn = _sgu_bwd(d_sg, sv["proj"], sg_norm[e], ops["wstack"], ops["wtstack"],
                                                      ops["bmat"], gsum, "sgu_bwd")
            dqn = _mm(dq_big, ops["wq_big"], "nn", "q_up_proj_bwd", out_dtype=F32)
            g_wq_big = _mm(dq_big, sv["qn"], "tn", "q_up_proj_dw", out_dtype=F32)
            dkvn = _mm(dkv_big, ops["wkv_big"], "nn", "kv_up_proj_bwd", out_dtype=F32)
            g_wkv_big = _mm(dkv_big, sv["kvn"], "tn", "kv_up_proj_dw", out_dtype=F32)
            dcq, g_qn = _rms_bwd(dqn, sv["proj"], q_norm[e], "q_norm_bwd", col=col_cq, out_dtype=BF16)
            dckv, g_kvn = _rms_bwd(dkvn, sv["proj"], kv_norm[e], "kv_norm_bwd", col=col_ckv, out_dtype=BF16)
            dproj = jnp.concatenate([dz_sg, dckv, dkr, jnp.zeros((S, HEAD_PAD), BF16), dcq], axis=1)
            dx, dgain = _mm_norm_bwd(dproj, ops["win_pad"], sv["x1"], mix_norm[l], dx, "even_in_proj_bwd")
            g_win = _mm(dproj, h, "tn", "even_in_proj_dw", out_dtype=F32)
            o_cq, o_ckv, o_kr = col_cq * QL, col_ckv * KVL, col_kr * HEAD_PAD
            hw = HEADS * HEAD_PAD
            pending.append(("even_w_in", e, jnp.concatenate(
                [g_win[o_cq:o_cq + QL], g_win[o_ckv:o_ckv + KVL], g_win[o_kr:o_kr + ROPE], g_win[:ZW]], axis=0)))
            pending.append(("w_uq", e, _head_unpad(g_wq_big, QK_DIM).reshape(HEADS * QK_DIM, QL)))
            pending.append(("w_ukv", e, jnp.concatenate(
                [_head_unpad(g_wkv_big[:hw], NOPE), _head_unpad(g_wkv_big[hw:], VDIM)],
                axis=1).reshape(HEADS * (NOPE + VDIM), KVL)))
            pending.append(("even_w_out", e, jnp.concatenate(
                [_head_unpad(g_wo_attn, VDIM).reshape(HEADS * VDIM, D), g_wo_sg], axis=0)))
            gr["q_norm"][e], gr["kv_norm"][e], gr["sg_norm"][e] = g_qn[0], g_kvn[0], g_sgn[0]
            gr["sg_w"][e] = g_wstack.reshape(SG_GROUPS, SG_CHUNK, SG_CHUNK) * tril
            gr["sg_b"][e] = g_bias[:, :SG_GROUPS].T
        else:
            o = l // 2
            dcv = _mm(dx, full["conv_w_out"][o], "nt", "conv_out_proj_bwd", out_dtype=BF16)
            pending.append(("conv_w_out", o, _mm(sv["cv"], dx, "tn", "conv_out_proj_dw", out_dtype=BF16)))
            dp, dcw = _conv_bwd(dcv, sv["p"], conv_taps(o), "conv_bwd")
            dx, dgain = _mm_norm_bwd(dp, full["conv_w_in"][o], sv["x1"], mix_norm[l], dx, "conv_in_proj_bwd")
            pending.append(("conv_w_in", o, _mm(dp, h, "tn", "conv_in_proj_dw", out_dtype=BF16)))
            pending.append(("conv_w", o, jnp.swapaxes(dcw[:CONV_K], 0, 1)))
        gr["mix_norm"][l] = dgain[0]
        dx = ffn_backward(dx, sv["x0"], "pre", l, sv["a1"], sv["b1"])
    grad_x = dx[None]

    rider, where = scatter_rider()
    small = [jnp.stack(gr[n]) for n in REPLICATED if n != "final_norm"] + [g_final[0], loss_part[0, :1]]
    spack, soffs = _pack(small, 0, SMALL_ROWS_MULT)
    received, sgath = _exchange_pair(rider, _Exchange("gather", [spack]), "scatter_grads_gather_small")
    take_scattered(received, where)
    grads = {n: _to_t(n, jnp.stack(per_layer[n])) for n in SHARDED}
    ssum = _sum_slots(sgath, "sum_small_grads")
    names_small = [n for n in REPLICATED if n != "final_norm"] + ["final_norm", "loss"]
    for n, off, piece in zip(names_small, soffs, small):
        val = _unpack(ssum, off, piece.shape, 0)
        if n == "loss":
            loss = val[0]
        else:
            grads[n] = val

    delta, new_m, new_v = {}, {}, {}
    for n in SHARDED:
        two_d = lambda a: a.reshape(-1, a.shape[-1])
        d, nm, nv = _adamw(two_d(w_loc[n]), two_d(grads[n]), two_d(m_loc[n]), two_d(v_loc[n]), "adamw")
        delta[n], new_m[n], new_v[n] = (a.reshape(w_loc[n].shape) for a in (d, nm, nv))
    flat = lambda d: _pack([d[n] for n in REPLICATED], 0, SMALL_ROWS_MULT)
    (wf, aoffs), (gf, _), (mf, _), (vf, _) = flat(w_loc), flat(grads), flat(m_loc), flat(v_loc)
    for res, buf in zip((delta, new_m, new_v), _adamw(wf, gf, mf, vf, "adamw_replicated")):
        for n, off in zip(REPLICATED, aoffs):
            res[n] = _unpack(buf, off, w_loc[n].shape, 0)
    outs = [loss, grad_x] + [grads[n] for n in WEIGHTS]
    for res in (delta, new_m, new_v):
        outs += [res[n] for n in WEIGHTS]
    return tuple(outs)
```
